```python
import jax
import jax.numpy as jnp
from jax import lax
import numpy as np

D_MODEL = 2048
BATCH = 4
SEQ = 2048
DEPTH = 2
DEC_BATCH = 128
DEC_SEQ = 4
PAST_LEN = 16384
PAGE_SIZE = 128

N_GROUPS = 4
GROUP_W = D_MODEL // N_GROUPS
RWKV_HEAD = 64
RWKV_HEADS = GROUP_W // RWKV_HEAD
W_LORA = 32
A_LORA = 32
G_LORA = 96
RWKV_GN_EPS = 64e-5
SGU_CHUNK = 128
SGU_HEADS = 4
SGU_HEAD = GROUP_W // SGU_HEADS
HGRN_HEADS = 4
HGRN_HEAD = GROUP_W // HGRN_HEADS
HGRN_CHUNK = 64
POOL_WINDOWS = (2, 4, 8, 16)
POOL_GROUPS = len(POOL_WINDOWS)
POOL_CH = GROUP_W // POOL_GROUPS
POOL_HIST = max(POOL_WINDOWS) - 1
D_FF = -(-8 * D_MODEL // (3 * 256)) * 256
PLE_DIM = 256
NORM_EPS = 1e-6
LN_EPS = 1e-5
RWKV_COLS = 3 * GROUP_W + W_LORA + A_LORA + G_LORA
SGU_COLS = 2 * GROUP_W
HGRN_COLS = 4 * GROUP_W
POOL_COLS = GROUP_W
IN_COLS = RWKV_COLS + SGU_COLS + HGRN_COLS + POOL_COLS

kernel_name = "hybrid_rwkv7_sgu_hgrn2_pool_decode_step"

F32 = jnp.float32


def _rmsnorm(x, g):
    xf = x.astype(F32)
    return xf * lax.rsqrt(jnp.mean(xf * xf, axis=-1, keepdims=True) + NORM_EPS) * g.astype(F32)


def _rwkv7(z, shift_prev, wkv0, w):
    B, L, _ = z.shape
    G = GROUP_W
    z_prev = jnp.concatenate([shift_prev[:, None, :].astype(F32), z[:, :-1]], axis=1)
    zm = z + w['rwkv_mu'] * (z_prev - z)
    r, k, v, xw, xa, xg = jnp.split(zm, [G, 2 * G, 3 * G, 3 * G + W_LORA, 3 * G + W_LORA + A_LORA], axis=-1)
    w_log = -jax.nn.softplus(-(w['rwkv_w0'] + jnp.tanh(xw) @ w['rwkv_w_lora'])) - 0.5
    decay = jnp.exp(-jnp.exp(w_log))
    a = jax.nn.sigmoid(w['rwkv_a0'] + xa @ w['rwkv_a_lora'])
    g = jax.nn.sigmoid(xg) @ w['rwkv_g_lora']
    hs = lambda t: t.reshape(B, L, RWKV_HEADS, RWKV_HEAD)
    kk = hs(k * w['rwkv_k_k'])
    kk = kk / jnp.maximum(jnp.linalg.norm(kk, axis=-1, keepdims=True), 1e-12)
    k = k * (1.0 + (a - 1.0) * w['rwkv_k_a'])
    r, k, v, decay, a = hs(r), hs(k), hs(v), hs(decay), hs(a)
    a_vec = -kk
    b_vec = kk * a

    def step(S, inp):
        r_t, w_t, k_t, v_t, av_t, bv_t = inp
        sa = jnp.einsum('bhvk,bhk->bhv', S, av_t)
        S = (S * w_t[:, :, None, :] + sa[..., None] * bv_t[:, :, None, :]
             + v_t[..., None] * k_t[:, :, None, :])
        return S, jnp.einsum('bhvk,bhk->bhv', S, r_t)

    xs = tuple(jnp.moveaxis(t, 1, 0) for t in (r, decay, k, v, a_vec, b_vec))
    S_fin, ys = lax.scan(step, wkv0.astype(F32), xs)
    y = jnp.moveaxis(ys, 0, 1)
    mean = jnp.mean(y, axis=-1, keepdims=True)
    var = jnp.mean(jnp.square(y - mean), axis=-1, keepdims=True)
    yn = ((y - mean) * lax.rsqrt(var + RWKV_GN_EPS)).reshape(B, L, G) * w['rwkv_gn_w'] + w['rwkv_gn_b']
    bonus = (jnp.sum(r * k * w['rwkv_r_k'], axis=-1, keepdims=True) * v).reshape(B, L, G)
    return (yn + bonus) * g, S_fin, z[:, -1]


def _sgu(z, w):
    B, L, _ = z.shape
    z = jax.nn.gelu(z, approximate=False)
    u, v = jnp.split(z, 2, axis=-1)
    mu = jnp.mean(v, axis=-1, keepdims=True)
    var = jnp.mean(jnp.square(v - mu), axis=-1, keepdims=True)
    v = (v - mu) * lax.rsqrt(var + LN_EPS) * w['sgu_ln_w'] + w['sgu_ln_b']
    n = -(-L // SGU_CHUNK)
    vp = jnp.pad(v, ((0, 0), (0, n * SGU_CHUNK - L), (0, 0))).reshape(B, n, SGU_CHUNK, SGU_HEADS, SGU_HEAD)
    mask = jnp.tril(jnp.ones((SGU_CHUNK, SGU_CHUNK), dtype=bool))
    wm = jnp.where(mask[None], w['sgu_w'], 0.0)
    s = jnp.einsum('hts,bnshd->bnthd', wm, vp) + w['sgu_b'].T[None, None, :, :, None]
    s = s.reshape(B, n * SGU_CHUNK, GROUP_W)[:, :L]
    return _rmsnorm(u * s, w['sgu_norm']), v


def _hgrn2_chunked(q, k, v, logf, S0):
    B, H, L, _ = q.shape
    C = min(HGRN_CHUNK, L)
    n = -(-L // C)
    pad = n * C - L

    def blocks(t):
        t = jnp.pad(t, ((0, 0), (0, 0), (0, pad), (0, 0)))
        return jnp.moveaxis(t.reshape(B, H, n, C, t.shape[-1]), 2, 0)

    mask = jnp.tril(jnp.ones((C, C), dtype=bool))[:, :, None]

    def step(S, inp):
        qc, kc, vc, gc = inp
        b = jnp.cumsum(gc, axis=2)
        diff = b[:, :, :, None, :] - b[:, :, None, :, :]
        dec = jnp.exp(jnp.where(mask, diff, -jnp.inf))
        att = jnp.einsum('bhtsd,bhsd->bhts', qc[:, :, :, None, :] * dec, kc)
        o = (jnp.einsum('bhts,bhsv->bhtv', att, vc)
             + jnp.einsum('bhtd,bhdv->bhtv', qc * jnp.exp(b), S))
        b_last = b[:, :, -1:, :]
        S = (jnp.exp(b_last[:, :, 0, :])[..., None] * S
             + jnp.einsum('bhsd,bhsv->bhdv', kc * jnp.exp(b_last - b), vc))
        return S, o

    S, o = lax.scan(step, S0, (blocks(q), blocks(k), blocks(v), blocks(logf)))
    o = jnp.moveaxis(o, 0, 2).reshape(B, H, n * C, -1)[:, :, :L]
    return o, S


def _hgrn2(z, S0, lb, w):
    B, L, _ = z.shape
    q, f, i_in, g = jnp.split(z, 4, axis=-1)
    q = jax.nn.silu(q)
    fg = lb + (1.0 - lb) * jax.nn.sigmoid(f)
    logf = jnp.log(fg)
    k = 1.0 - fg
    th = lambda t: t.reshape(B, L, HGRN_HEADS, HGRN_HEAD).transpose(0, 2, 1, 3)
    o, S = _hgrn2_chunked(th(q), th(k), th(i_in), th(logf), S0.astype(F32))
    o = o.transpose(0, 2, 1, 3)
    o = o * lax.rsqrt(jnp.mean(o * o, axis=-1, keepdims=True) + NORM_EPS)
    return o.reshape(B, L, GROUP_W) * w['hgrn_norm'] * jax.nn.silu(g), S


def _pool(z, hist, start_pos, w):
    B, L, _ = z.shape
    full = jnp.concatenate([hist.astype(F32), z], axis=1)
    cs = jnp.pad(jnp.cumsum(full, axis=1), ((0, 0), (1, 0), (0, 0)))
    pos = start_pos + jnp.arange(L)
    outs = []
    for gi, win in enumerate(POOL_WINDOWS):
        sl = slice(gi * POOL_CH, (gi + 1) * POOL_CH)
        hi = cs[:, POOL_HIST + 1:POOL_HIST + 1 + L, sl]
        lo = cs[:, POOL_HIST + 1 - win:POOL_HIST + 1 - win + L, sl]
        cnt = jnp.minimum(pos + 1, win).astype(F32)
        outs.append((hi - lo) / cnt[None, :, None])
    d = (jnp.concatenate(outs, axis=-1) - z).reshape(B, L, POOL_GROUPS, POOL_CH)
    y = jnp.einsum('blgc,gcd->blgd', d, w['pool_w']).reshape(B, L, GROUP_W) * w['pool_scale']
    return y, full[:, -POOL_HIST:]


def _trunk(x, p, wkv0, shift0, hgrn0, pool0, start_pos, W, lb_all):
    dtype = x.dtype
    wkv_l, shift_l, hgrn_l, pool_l, sgu_l = [], [], [], [], []
    c1, c2, c3 = RWKV_COLS, RWKV_COLS + SGU_COLS, RWKV_COLS + SGU_COLS + HGRN_COLS
    for i in range(DEPTH):
        w = {name: arr[i] for name, arr in W.items()}
        h = _rmsnorm(x, w['ln_mix_pre'])
        z = jnp.einsum('bld,dc->blc', h, w['w_in']).astype(F32)
        z_r, z_s, z_h, z_p = jnp.split(z, [c1, c2, c3], axis=-1)
        y_r, wkv_new, shift_new = _rwkv7(z_r, shift0[i], wkv0[i], w)
        y_s, v_rows = _sgu(z_s, w)
        y_h, hgrn_new = _hgrn2(z_h, hgrn0[i], lb_all[i], w)
        y_p, pool_new = _pool(z_p, pool0[i], start_pos, w)
        mix = jnp.einsum('blc,cd->bld', jnp.concatenate([y_r, y_s, y_h, y_p], axis=-1), w['w_out'])
        xf = x.astype(F32) + _rmsnorm(mix, w['ln_mix_post'])
        h2 = _rmsnorm(xf, w['ln_ffn_pre'])
        gate, up = jnp.split(jnp.einsum('bld,df->blf', h2, w['ffn_w_gu']), 2, axis=-1)
        ff = jnp.einsum('blf,fd->bld', jax.nn.silu(gate) * up, w['ffn_w_down'])
        xf = xf + _rmsnorm(ff, w['ln_ffn_post'])
        ple = jnp.einsum('ble,ed->bld', p[i].astype(F32), w['ple_proj'])
        xf = xf + jax.nn.sigmoid(jnp.einsum('bld,de->ble', xf, w['ple_gate'])) * ple
        x = xf.astype(dtype)
        wkv_l.append(wkv_new)
        shift_l.append(shift_new)
        hgrn_l.append(hgrn_new)
        pool_l.append(pool_new)
        sgu_l.append(v_rows)
    return (x, jnp.stack(wkv_l).astype(dtype), jnp.stack(shift_l).astype(dtype),
            jnp.stack(hgrn_l).astype(dtype), jnp.stack(pool_l).astype(dtype),
            jnp.stack(sgu_l).astype(dtype))


def setup_inputs(seed: int = 0) -> dict:
    key = jax.random.key(seed)
    ks = iter(jax.random.split(key, 64))
    nrm = lambda shape, scale: scale * jax.random.normal(next(ks), shape, F32)
    gain = lambda shape: 1.0 + 0.1 * jax.random.normal(next(ks), shape, F32)
    Dp = DEPTH
    return {
        'x_prompt': nrm((BATCH, SEQ, D_MODEL), 1.0),
        'x_sample': nrm((DEC_BATCH, DEC_SEQ, D_MODEL), 1.0),
        'state_rwkv_wkv': nrm((Dp, DEC_BATCH, RWKV_HEADS, RWKV_HEAD, RWKV_HEAD), 0.3),
        'state_rwkv_shift': nrm((Dp, DEC_BATCH, RWKV_COLS), 1.0),
        'state_hgrn': nrm((Dp, DEC_BATCH, HGRN_HEADS, HGRN_HEAD, HGRN_HEAD), 0.5),
        'state_pool': nrm((Dp, DEC_BATCH, POOL_HIST, GROUP_W), 1.0),
        'p_prompt': nrm((Dp, BATCH, SEQ, PLE_DIM), 1.0),
        'p_sample': nrm((Dp, DEC_BATCH, DEC_SEQ, PLE_DIM), 1.0),
        'ln_mix_pre': gain((Dp, D_MODEL)),
        'ln_mix_post': gain((Dp, D_MODEL)),
        'ln_ffn_pre': gain((Dp, D_MODEL)),
        'ln_ffn_post': gain((Dp, D_MODEL)),
        'w_in': nrm((Dp, D_MODEL, IN_COLS), D_MODEL ** -0.5),
        'rwkv_mu': jax.random.uniform(next(ks), (Dp, RWKV_COLS), F32),
        'rwkv_w_lora': nrm((Dp, W_LORA, GROUP_W), W_LORA ** -0.5),
        'rwkv_w0': nrm((Dp, GROUP_W), 0.5),
        'rwkv_a_lora': nrm((Dp, A_LORA, GROUP_W), A_LORA ** -0.5),
        'rwkv_a0': nrm((Dp, GROUP_W), 0.1),
        'rwkv_g_lora': nrm((Dp, G_LORA, GROUP_W), G_LORA ** -0.5),
        'rwkv_k_k': gain((Dp, GROUP_W)),
        'rwkv_k_a': gain((Dp, GROUP_W)),
        'rwkv_r_k': nrm((Dp, RWKV_HEADS, RWKV_HEAD), 0.1),
        'rwkv_gn_w': gain((Dp, GROUP_W)),
        'rwkv_gn_b': nrm((Dp, GROUP_W), 0.01),
        'sgu_ln_w': gain((Dp, GROUP_W)),
        'sgu_ln_b': nrm((Dp, GROUP_W), 0.02),
        'sgu_w': nrm((Dp, SGU_HEADS, SGU_CHUNK, SGU_CHUNK), SGU_CHUNK ** -0.5),
        'sgu_b': gain((Dp, SGU_HEADS, SGU_CHUNK)),
        'sgu_norm': gain((Dp, GROUP_W)),
        'hgrn_lb_logits': nrm((Dp, GROUP_W), 0.5),
        'hgrn_norm': gain((Dp, GROUP_W)),
        'pool_w': nrm((Dp, POOL_GROUPS, POOL_CH, POOL_CH), POOL_CH ** -0.5),
        'pool_scale': gain((Dp, GROUP_W)),
        'w_out': nrm((Dp, D_MODEL, D_MODEL), D_MODEL ** -0.5),
        'ffn_w_gu': nrm((Dp, D_MODEL, 2 * D_FF), D_MODEL ** -0.5),
        'ffn_w_down': nrm((Dp, D_FF, D_MODEL), D_FF ** -0.5),
        'ple_gate': nrm((Dp, D_MODEL, D_MODEL), D_MODEL ** -0.5),
        'ple_proj': nrm((Dp, PLE_DIM, D_MODEL), PLE_DIM ** -0.5),
    }


def reference(x_prompt, x_sample, state_rwkv_wkv, state_rwkv_shift, state_hgrn, state_pool,
              p_prompt, p_sample, ln_mix_pre, ln_mix_post, ln_ffn_pre, ln_ffn_post, w_in,
              rwkv_mu, rwkv_w_lora, rwkv_w0, rwkv_a_lora, rwkv_a0, rwkv_g_lora, rwkv_k_k,
              rwkv_k_a, rwkv_r_k, rwkv_gn_w, rwkv_gn_b, sgu_ln_w, sgu_ln_b, sgu_w, sgu_b,
              sgu_norm, hgrn_lb_logits, hgrn_norm, pool_w, pool_scale, w_out, ffn_w_gu,
              ffn_w_down, ple_gate, ple_proj):
    W = dict(ln_mix_pre=ln_mix_pre, ln_mix_post=ln_mix_post, ln_ffn_pre=ln_ffn_pre,
             ln_ffn_post=ln_ffn_post, w_in=w_in, rwkv_mu=rwkv_mu, rwkv_w_lora=rwkv_w_lora,
             rwkv_w0=rwkv_w0, rwkv_a_lora=rwkv_a_lora, rwkv_a0=rwkv_a0, rwkv_g_lora=rwkv_g_lora,
             rwkv_k_k=rwkv_k_k, rwkv_k_a=rwkv_k_a, rwkv_r_k=rwkv_r_k, rwkv_gn_w=rwkv_gn_w,
             rwkv_gn_b=rwkv_gn_b, sgu_ln_w=sgu_ln_w, sgu_ln_b=sgu_ln_b, sgu_w=sgu_w, sgu_b=sgu_b,
             sgu_norm=sgu_norm, hgrn_norm=hgrn_norm, pool_w=pool_w, pool_scale=pool_scale,
             w_out=w_out, ffn_w_gu=ffn_w_gu, ffn_w_down=ffn_w_down, ple_gate=ple_gate,
             ple_proj=ple_proj)
    lb_soft = jax.nn.softmax(hgrn_lb_logits.astype(F32), axis=0)
    lb_all = jnp.cumsum(lb_soft, axis=0) - lb_soft[0:1]
    zeros_wkv = jnp.zeros((DEPTH, BATCH, RWKV_HEADS, RWKV_HEAD, RWKV_HEAD), F32)
    zeros_shift = jnp.zeros((DEPTH, BATCH, RWKV_COLS), F32)
    zeros_hgrn = jnp.zeros((DEPTH, BATCH, HGRN_HEADS, HGRN_HEAD, HGRN_HEAD), F32)
    zeros_pool = jnp.zeros((DEPTH, BATCH, POOL_HIST, GROUP_W), F32)
    y_prompt, wkv_p, shift_p, hgrn_p, pool_p, _ = _trunk(
        x_prompt, p_prompt, zeros_wkv, zeros_shift, zeros_hgrn, zeros_pool, 0, W, lb_all)
    y_sample, wkv_s, shift_s, hgrn_s, pool_s, sgu_v_s = _trunk(
        x_sample, p_sample, state_rwkv_wkv, state_rwkv_shift, state_hgrn, state_pool,
        PAST_LEN, W, lb_all)
    return (y_prompt, y_sample, wkv_p, shift_p, hgrn_p, pool_p, wkv_s, shift_s, hgrn_s, pool_s, sgu_v_s)
```

```python
import functools

import jax
import jax.numpy as jnp
from jax import lax
from jax.experimental import pallas as pl
from jax.experimental.pallas import tpu as pltpu

F32 = jnp.float32
BF16 = jnp.bfloat16

D_MODEL = 2048
DEPTH = 2
PAST_LEN = 16384
GROUP_W = 512
RWKV_HEAD = 64
RWKV_HEADS = 8
RWKV_PAIRS = RWKV_HEADS // 2
W_LORA, A_LORA, G_LORA = 32, 32, 96
RWKV_GN_EPS = 64e-5
SGU_CHUNK = 128
SGU_HEADS = 4
HGRN_HEADS = 4
HGRN_HEAD = 128
HGRN_CHUNK = 64
HGRN_SUB = 16
POOL_WINDOWS = (2, 4, 8, 16)
POOL_CH = 128
POOL_HIST = 15
POOL_HALO = 32
D_FF = 5632
PLE_DIM = 256
NORM_EPS = 1e-6
LN_EPS = 1e-5
RWKV_COLS = 3 * GROUP_W + W_LORA + A_LORA + G_LORA
RWKV_PAD = 1792
LORA_OFF = 3 * GROUP_W
LORA_W = RWKV_PAD - LORA_OFF
SGU_COLS = 2 * GROUP_W
HGRN_COLS = 4 * GROUP_W
Z_HGRN, Z_SGU, Z_POOL, Z_RWKV = 0, 2048, 3072, 3584
Z_COLS = Z_RWKV + RWKV_PAD
VMEM_LIMIT = 56 * 1024 * 1024


def _cparams(sem):
    return pltpu.CompilerParams(dimension_semantics=sem, vmem_limit_bytes=VMEM_LIMIT)


def _sigmoid(x):
    return jax.nn.sigmoid(x)


def _silu(x):
    return x * jax.nn.sigmoid(x)


def _split3(x):
    hi = x.astype(BF16)
    r = x - hi.astype(F32)
    mid = r.astype(BF16)
    lo = (r - mid.astype(F32)).astype(BF16)
    return hi, mid, lo


def _dot_f32_lhs(x, m):
    hi, mid, lo = _split3(x)
    d = functools.partial(jnp.dot, preferred_element_type=F32)
    return d(hi, m) + d(mid, m) + d(lo, m)


def _dot_f32_rhs(m, x):
    hi, mid, lo = _split3(x)
    d = functools.partial(jnp.dot, preferred_element_type=F32)
    return d(m, hi) + d(m, mid) + d(m, lo)


def _rms(x, g):
    return x * lax.rsqrt(jnp.mean(x * x, axis=-1, keepdims=True) + NORM_EPS) * g


def _inproj_kernel(x_ref, g_ref, w_ref, z_ref, h_ref):
    @pl.when(pl.program_id(1) == 0)
    def _():
        h_ref[...] = _rms(x_ref[...], g_ref[...]).astype(BF16)

    z_ref[...] = jnp.dot(h_ref[...], w_ref[...], preferred_element_type=F32)


def _inproj(x, g, w, tm, tn):
    t = x.shape[0]
    return pl.pallas_call(
        _inproj_kernel,
        grid=(t // tm, Z_COLS // tn),
        in_specs=[pl.BlockSpec((tm, D_MODEL), lambda i, j: (i, 0)),
                  pl.BlockSpec((1, D_MODEL), lambda i, j: (0, 0)),
                  pl.BlockSpec((D_MODEL, tn), lambda i, j: (0, j))],
        out_specs=pl.BlockSpec((tm, tn), lambda i, j: (i, j)),
        out_shape=jax.ShapeDtypeStruct((t, Z_COLS), F32),
        scratch_shapes=[pltpu.VMEM((tm, D_MODEL), BF16)],
        compiler_params=_cparams(("parallel", "arbitrary")),
        name="inproj",
    )(x, g, w)


def _outproj_kernel(yr_ref, ys_ref, yh_ref, yp_ref, x_ref, w_ref, g_ref, o_ref):
    d = functools.partial(jnp.dot, preferred_element_type=F32)
    g4 = GROUP_W
    mix = (d(yr_ref[...], w_ref[0:g4, :]) + d(ys_ref[...], w_ref[g4:2 * g4, :])
           + d(yh_ref[...], w_ref[2 * g4:3 * g4, :]) + d(yp_ref[...], w_ref[3 * g4:4 * g4, :]))
    o_ref[...] = x_ref[...] + _rms(mix, g_ref[...])


def _outproj(ys, x, w, g, tm):
    t = x.shape[0]
    yspec = pl.BlockSpec((tm, GROUP_W), lambda i: (i, 0))
    return pl.pallas_call(
        _outproj_kernel,
        grid=(t // tm,),
        in_specs=[yspec, yspec, yspec, yspec,
                  pl.BlockSpec((tm, D_MODEL), lambda i: (i, 0)),
                  pl.BlockSpec((D_MODEL, D_MODEL), lambda i: (0, 0)),
                  pl.BlockSpec((1, D_MODEL), lambda i: (0, 0))],
        out_specs=pl.BlockSpec((tm, D_MODEL), lambda i: (i, 0)),
        out_shape=jax.ShapeDtypeStruct((t, D_MODEL), F32),
        compiler_params=_cparams(("parallel",)),
        name="outproj",
    )(*ys, x, w, g)


def _ffn_kernel(x_ref, gpre_ref, wg_ref, wu_ref, wd_ref, gpost_ref, o_ref, h_ref, acc_ref):
    j = pl.program_id(1)

    @pl.when(j == 0)
    def _():
        h_ref[...] = _rms(x_ref[...], gpre_ref[...]).astype(BF16)
        acc_ref[...] = jnp.zeros_like(acc_ref)

    h = h_ref[...]
    gate = jnp.dot(h, wg_ref[...], preferred_element_type=F32)
    up = jnp.dot(h, wu_ref[...], preferred_element_type=F32)
    act = (_silu(gate) * up).astype(BF16)
    acc_ref[...] += jnp.dot(act, wd_ref[...], preferred_element_type=F32)

    @pl.when(j == pl.num_programs(1) - 1)
    def _():
        o_ref[...] = x_ref[...] + _rms(acc_ref[...], gpost_ref[...])


def _ffn(x, gpre, w_gu, w_down, gpost, tm, tf):
    t = x.shape[0]
    nf = D_FF // tf
    return pl.pallas_call(
        _ffn_kernel,
        grid=(t // tm, nf),
        in_specs=[pl.BlockSpec((tm, D_MODEL), lambda i, j: (i, 0)),
                  pl.BlockSpec((1, D_MODEL), lambda i, j: (0, 0)),
                  pl.BlockSpec((D_MODEL, tf), lambda i, j: (0, j)),
                  pl.BlockSpec((D_MODEL, tf), lambda i, j: (0, j + nf)),
                  pl.BlockSpec((tf, D_MODEL), lambda i, j: (j, 0)),
                  pl.BlockSpec((1, D_MODEL), lambda i, j: (0, 0))],
        out_specs=pl.BlockSpec((tm, D_MODEL), lambda i, j: (i, 0)),
        out_shape=jax.ShapeDtypeStruct((t, D_MODEL), F32),
        scratch_shapes=[pltpu.VMEM((tm, D_MODEL), BF16), pltpu.VMEM((tm, D_MODEL), F32)],
        compiler_params=_cparams(("parallel", "arbitrary")),
        name="ffn",
    )(x, gpre, w_gu, w_gu, w_down, gpost)


def _ple_kernel(x_ref, xc_ref, p_ref, wgate_ref, wproj_ref, o_ref, h_ref):
    @pl.when(pl.program_id(1) == 0)
    def _():
        h_ref[...] = x_ref[...].astype(BF16)

    gate = _sigmoid(jnp.dot(h_ref[...], wgate_ref[...], preferred_element_type=F32))
    ple = jnp.dot(p_ref[...].astype(BF16), wproj_ref[...], preferred_element_type=F32)
    o_ref[...] = xc_ref[...] + gate * ple


def _ple(x, p, w_gate, w_proj, tm, tn):
    t = x.shape[0]
    return pl.pallas_call(
        _ple_kernel,
        grid=(t // tm, D_MODEL // tn),
        in_specs=[pl.BlockSpec((tm, D_MODEL), lambda i, j: (i, 0)),
                  pl.BlockSpec((tm, tn), lambda i, j: (i, j)),
                  pl.BlockSpec((tm, PLE_DIM), lambda i, j: (i, 0)),
                  pl.BlockSpec((D_MODEL, tn), lambda i, j: (0, j)),
                  pl.BlockSpec((PLE_DIM, tn), lambda i, j: (0, j))],
        out_specs=pl.BlockSpec((tm, tn), lambda i, j: (i, j)),
        out_shape=jax.ShapeDtypeStruct((t, D_MODEL), F32),
        scratch_shapes=[pltpu.VMEM((tm, D_MODEL), BF16)],
        compiler_params=_cparams(("parallel", "arbitrary")),
        name="ple",
    )(x, x, p, w_gate, w_proj)


def _head_ones():
    r = lax.broadcasted_iota(jnp.int32, (GROUP_W, GROUP_W), 0) // RWKV_HEAD
    c = lax.broadcasted_iota(jnp.int32, (GROUP_W, GROUP_W), 1) // RWKV_HEAD
    return (r == c).astype(BF16)


def _rwkv_kernel(z_ref, sh_ref, s0_ref, mu_ref, lw_ref, w0_ref, la_ref, a0_ref, lg_ref, kk_ref, ka_ref, rk_ref,
                 gnw_ref, gnb_ref, y_ref, sout_ref,
                 zbuf, s_scr, r_s, w_s, k_s, v_s, a_s, b_s, y_s, *, tc):
    tb = pl.program_id(1)

    @pl.when(tb == 0)
    def _():
        zbuf[pl.ds(0, 8), :] = jnp.broadcast_to(sh_ref[0], (8, RWKV_PAD))
        s_scr[...] = s0_ref[0]

    z = z_ref[0]
    zbuf[pl.ds(8, tc), :] = z
    z_prev = zbuf[pl.ds(7, tc), :]
    zbuf[pl.ds(7, 1), :] = zbuf[pl.ds(7 + tc, 1), :]
    zm = z + mu_ref[...] * (z_prev - z)
    g4 = GROUP_W
    r = zm[:, 0:g4]
    k = zm[:, g4:2 * g4]
    v = zm[:, 2 * g4:3 * g4]
    zl = zm[:, LORA_OFF:RWKV_PAD]
    d = functools.partial(jnp.dot, preferred_element_type=F32)
    wl = w0_ref[...] + d(jnp.tanh(zl).astype(BF16), lw_ref[...])
    w_log = -(jnp.maximum(-wl, 0.0) + jnp.log1p(jnp.exp(-jnp.abs(wl)))) - 0.5
    decay = jnp.exp(-jnp.exp(w_log))
    a = _sigmoid(a0_ref[...] + d(zl.astype(BF16), la_ref[...]))
    g = d(_sigmoid(zl).astype(BF16), lg_ref[...])
    ones = _head_ones()
    kk = k * kk_ref[...]
    kk = kk / jnp.maximum(jnp.sqrt(_dot_f32_lhs(kk * kk, ones)), 1e-12)
    k = k * (1.0 + (a - 1.0) * ka_ref[...])
    bonus = _dot_f32_lhs(r * k * rk_ref[...], ones) * v
    r_s[...] = r
    w_s[...] = decay
    k_s[...] = k
    v_s[...] = v
    a_s[...] = -kk
    b_s[...] = kk * a

    shp = (RWKV_HEAD, 2 * RWKV_HEAD)
    lane = lax.broadcasted_iota(jnp.int32, shp, 1)
    sub = lax.broadcasted_iota(jnp.int32, shp, 0)
    lo = lane < RWKV_HEAD
    eye = ((lane % RWKV_HEAD) == sub).astype(F32)

    def head_sums(x):
        s0 = jnp.sum(jnp.where(lo, x, 0.0), axis=1, keepdims=True)
        s1 = jnp.sum(jnp.where(lo, 0.0, x), axis=1, keepdims=True)
        return jnp.where(lo, s0, s1)

    gs = min(8, tc)

    def steps(gi, states):
        t0 = pl.multiple_of(gi * gs, gs)
        new = []
        for p in range(RWKV_PAIRS):
            sl = slice(p * 128, (p + 1) * 128)
            a8, v8, w8, b8, k8, r8 = (ref[pl.ds(t0, gs), sl] for ref in (a_s, v_s, w_s, b_s, k_s, r_s))
            s = states[p]
            ys = []
            for j in range(gs):
                row = lambda x8: x8[j:j + 1]
                sa = head_sums(s * row(a8))
                vcol = head_sums(eye * row(v8))
                s = s * row(w8) + sa * row(b8) + vcol * row(k8)
                ycol = head_sums(s * row(r8))
                ys.append(jnp.sum(eye * ycol, axis=0, keepdims=True))
            y_s[pl.ds(t0, gs), sl] = jnp.concatenate(ys, axis=0)
            new.append(s)
        return tuple(new)

    states = lax.fori_loop(0, tc // gs, steps, tuple(s_scr[p] for p in range(RWKV_PAIRS)))
    for p in range(RWKV_PAIRS):
        s_scr[p] = states[p]

    y = y_s[...]
    inv_n = 1.0 / RWKV_HEAD
    mean = _dot_f32_lhs(y, ones) * inv_n
    yc = y - mean
    var = _dot_f32_lhs(yc * yc, ones) * inv_n
    yn = yc * lax.rsqrt(var + RWKV_GN_EPS) * gnw_ref[...] + gnb_ref[...]
    y_ref[0] = ((yn + bonus) * g).astype(y_ref.dtype)

    @pl.when(tb == pl.num_programs(1) - 1)
    def _():
        sout_ref[0] = s_scr[...]


def _rwkv(z3, shift_prev, wkv0, pw, tc, out_dtype):
    b, l, _ = z3.shape
    vec = lambda n: pl.BlockSpec((1, n), lambda i, j: (0, 0))
    mat = lambda r, c: pl.BlockSpec((r, c), lambda i, j: (0, 0))
    st = pl.BlockSpec((1, RWKV_PAIRS, RWKV_HEAD, 128), lambda i, j: (i, 0, 0, 0))
    tile = lambda: pltpu.VMEM((tc, GROUP_W), F32)
    return pl.pallas_call(
        functools.partial(_rwkv_kernel, tc=tc),
        grid=(b, l // tc),
        in_specs=[pl.BlockSpec((1, tc, RWKV_PAD), lambda i, j: (i, j, Z_RWKV // RWKV_PAD)),
                  pl.BlockSpec((1, 1, RWKV_PAD), lambda i, j: (i, 0, 0)),
                  st, vec(RWKV_PAD), mat(LORA_W, GROUP_W), vec(GROUP_W), mat(LORA_W, GROUP_W), vec(GROUP_W),
                  mat(LORA_W, GROUP_W), vec(GROUP_W), vec(GROUP_W), vec(GROUP_W), vec(GROUP_W), vec(GROUP_W)],
        out_specs=[pl.BlockSpec((1, tc, GROUP_W), lambda i, j: (i, j, 0)), st],
        out_shape=[jax.ShapeDtypeStruct((b, l, GROUP_W), out_dtype),
                   jax.ShapeDtypeStruct((b, RWKV_PAIRS, RWKV_HEAD, 128), F32)],
        scratch_shapes=[pltpu.VMEM((tc + 8, RWKV_PAD), F32), pltpu.VMEM((RWKV_PAIRS, RWKV_HEAD, 128), F32),
                        tile(), tile(), tile(), tile(), tile(), tile(), tile()],
        compiler_params=_cparams(("parallel", "arbitrary")),
        name="rwkv7",
    )(z3, shift_prev, wkv0, pw["mu"], pw["lora_w"], pw["w0"], pw["lora_a"], pw["a0"], pw["lora_g"],
      pw["k_k"], pw["k_a"], pw["r_k"], pw["gn_w"], pw["gn_b"])


def _sgu_kernel(z_ref, lnw_ref, lnb_ref, wm_ref, bias_ref, nrm_ref, y_ref, v_ref, *, blk):
    z = z_ref[...]
    zg = 0.5 * z * (1.0 + lax.erf(z * 0.7071067811865476))
    u = zg[:, :GROUP_W]
    v = zg[:, GROUP_W:]
    mu = jnp.mean(v, axis=-1, keepdims=True)
    vc = v - mu
    var = jnp.mean(vc * vc, axis=-1, keepdims=True)
    vn = vc * lax.rsqrt(var + LN_EPS) * lnw_ref[...] + lnb_ref[...]
    v_ref[...] = vn
    row = lax.broadcasted_iota(jnp.int32, (SGU_CHUNK, SGU_CHUNK), 0)
    col = lax.broadcasted_iota(jnp.int32, (SGU_CHUNK, SGU_CHUNK), 1)
    mask = col <= row
    if blk < SGU_CHUNK:
        mask = mask & ((row // blk) == (col // blk))
    vb = vn.astype(BF16)
    parts = []
    for h in range(SGU_HEADS):
        sl = slice(h * 128, (h + 1) * 128)
        wm = jnp.where(mask, wm_ref[h], 0.0).astype(BF16)
        parts.append(jnp.dot(wm, vb[:, sl], preferred_element_type=F32) + bias_ref[:, h:h + 1])
    s = jnp.concatenate(parts, axis=1)
    y_ref[...] = _rms(u * s, nrm_ref[...]).astype(y_ref.dtype)


def _sgu(z2, pw, wm, bias, blk, out_dtype):
    t = z2.shape[0]
    vec = pl.BlockSpec((1, GROUP_W), lambda i: (0, 0))
    return pl.pallas_call(
        functools.partial(_sgu_kernel, blk=blk),
        grid=(t // SGU_CHUNK,),
        in_specs=[pl.BlockSpec((SGU_CHUNK, SGU_COLS), lambda i: (i, Z_SGU // SGU_COLS)),
                  vec, vec,
                  pl.BlockSpec((SGU_HEADS, SGU_CHUNK, SGU_CHUNK), lambda i: (0, 0, 0)),
                  pl.BlockSpec((SGU_CHUNK, SGU_HEADS), lambda i: (0, 0)),
                  vec],
        out_specs=[pl.BlockSpec((SGU_CHUNK, GROUP_W), lambda i: (i, 0)),
                   pl.BlockSpec((SGU_CHUNK, GROUP_W), lambda i: (i, 0))],
        out_shape=[jax.ShapeDtypeStruct((t, GROUP_W), out_dtype), jax.ShapeDtypeStruct((t, GROUP_W), F32)],
        compiler_params=_cparams(("parallel",)),
        name="sgu",
    )(z2, pw["ln_w"], pw["ln_b"], wm, bias, pw["norm"])


def _hgrn_kernel(zq_ref, zf_ref, zi_ref, zg_ref, lg_ref, nw_ref, s0_ref, y_ref, sout_ref, s_scr,
                 *, layer, chunk, sub, n_chunks):
    tb = pl.program_id(2)

    @pl.when(tb == 0)
    def _():
        s_scr[...] = s0_ref[0, 0]

    lg = lg_ref[...]
    e = jnp.exp(lg - jnp.max(lg, axis=0, keepdims=True))
    sm = e / jnp.sum(e, axis=0, keepdims=True)
    lb = jnp.sum(sm[0:layer + 1], axis=0, keepdims=True) - sm[0:1]

    tri = (lax.broadcasted_iota(jnp.int32, (chunk, chunk), 1)
           <= lax.broadcasted_iota(jnp.int32, (chunk, chunk), 0)).astype(BF16)
    eye = (lax.broadcasted_iota(jnp.int32, (HGRN_HEAD, HGRN_HEAD), 0)
           == lax.broadcasted_iota(jnp.int32, (HGRN_HEAD, HGRN_HEAD), 1)).astype(F32)
    rowid = lax.broadcasted_iota(jnp.int32, (sub, 1), 0)
    d = functools.partial(jnp.dot, preferred_element_type=F32)

    def chunk_body(c, carry):
        r0 = pl.multiple_of(c * chunk, chunk)
        zq = zq_ref[0, pl.ds(r0, chunk), :]
        q = _silu(zq)
        fg = lb + (1.0 - lb) * _sigmoid(zf_ref[0, pl.ds(r0, chunk), :])
        logf = jnp.log(fg)
        k = 1.0 - fg
        v = zi_ref[0, pl.ds(r0, chunk), :]
        vb = v.astype(BF16)
        b = _dot_f32_rhs(tri, logf)
        s = s_scr[...]
        o = d((q * jnp.exp(b)).astype(BF16), s.astype(BF16))
        outs = []
        for i in range(chunk // sub):
            lo_, hi_ = i * sub, (i + 1) * sub
            qi, bi, oi = q[lo_:hi_], b[lo_:hi_], o[lo_:hi_]
            if i > 0:
                bref = b[lo_ - 1:lo_]
                qt = (qi * jnp.exp(bi - bref)).astype(BF16)
                kt = (k[:lo_] * jnp.exp(bref - b[:lo_])).astype(BF16)
                att = lax.dot_general(qt, kt, (((1,), (1,)), ((), ())), preferred_element_type=F32)
                oi = oi + d(att.astype(BF16), vb[:lo_])
            for j in range(sub):
                sj = lo_ + j
                dec = jnp.exp(jnp.minimum(bi - b[sj:sj + 1], 0.0))
                att_j = jnp.sum(qi * dec * k[sj:sj + 1], axis=-1, keepdims=True)
                oi = oi + jnp.where(rowid >= j, att_j, 0.0) * v[sj:sj + 1]
            outs.append(oi)
        o = outs[0] if len(outs) == 1 else jnp.concatenate(outs, axis=0)
        blast = b[chunk - 1:chunk]
        kd = (k * jnp.exp(blast - b)).astype(BF16)
        upd = lax.dot_general(kd, vb, (((0,), (0,)), ((), ())), preferred_element_type=F32)
        e_col = jnp.sum(eye * jnp.exp(blast), axis=1, keepdims=True)
        s_scr[...] = e_col * s + upd
        on = o * lax.rsqrt(jnp.mean(o * o, axis=-1, keepdims=True) + NORM_EPS)
        y_ref[0, pl.ds(r0, chunk), :] = (on * nw_ref[...] * _silu(zg_ref[0, pl.ds(r0, chunk), :])).astype(y_ref.dtype)
        return carry

    lax.fori_loop(0, n_chunks, chunk_body, 0)

    @pl.when(tb == pl.num_programs(2) - 1)
    def _():
        sout_ref[0, 0] = s_scr[...]


def _hgrn(z3, s0, lb_logits, norm_w, layer, tc, chunk, sub, out_dtype):
    b, l, _ = z3.shape
    hb = Z_HGRN // HGRN_HEAD
    zspec = lambda part: pl.BlockSpec((1, tc, HGRN_HEAD), lambda i, h, j: (i, j, hb + part * HGRN_HEADS + h))
    st = pl.BlockSpec((1, 1, HGRN_HEAD, HGRN_HEAD), lambda i, h, j: (i, h, 0, 0))
    return pl.pallas_call(
        functools.partial(_hgrn_kernel, layer=layer, chunk=chunk, sub=sub, n_chunks=tc // chunk),
        grid=(b, HGRN_HEADS, l // tc),
        in_specs=[zspec(0), zspec(1), zspec(2), zspec(3),
                  pl.BlockSpec((DEPTH, HGRN_HEAD), lambda i, h, j: (0, h)),
                  pl.BlockSpec((1, HGRN_HEAD), lambda i, h, j: (0, h)),
                  st],
        out_specs=[pl.BlockSpec((1, tc, HGRN_HEAD), lambda i, h, j: (i, j, h)), st],
        out_shape=[jax.ShapeDtypeStruct((b, l, GROUP_W), out_dtype),
                   jax.ShapeDtypeStruct((b, HGRN_HEADS, HGRN_HEAD, HGRN_HEAD), F32)],
        scratch_shapes=[pltpu.VMEM((HGRN_HEAD, HGRN_HEAD), F32)],
        compiler_params=_cparams(("parallel", "parallel", "arbitrary")),
        name="hgrn2",
    )(z3, z3, z3, z3, lb_logits, norm_w, s0)


def _pool_kernel(z_ref, hist_ref, pw_ref, ps_ref, y_ref, xbuf, s2buf, s4buf, s8buf, *, tc, start_pos, carry):
    tb = pl.program_id(1)
    halo = POOL_HALO
    n = tc + halo

    @pl.when(tb == 0)
    def _():
        xbuf[pl.ds(0, 16), :] = jnp.zeros((16, GROUP_W), F32)
        xbuf[pl.ds(16, 16), :] = hist_ref[0]

    z = z_ref[0]
    xbuf[pl.ds(halo, tc), :] = z
    c1, c2, c3 = POOL_CH, 2 * POOL_CH, 3 * POOL_CH
    s2buf[pl.ds(8, n - 8), :] = xbuf[pl.ds(8, n - 8), :] + xbuf[pl.ds(7, n - 8), :]
    s4buf[pl.ds(16, n - 16), :] = s2buf[pl.ds(16, n - 16), c1:] + s2buf[pl.ds(14, n - 16), c1:]
    s8buf[pl.ds(24, n - 24), :] = s4buf[pl.ds(24, n - 24), c1:] + s4buf[pl.ds(20, n - 24), c1:]
    s16 = s8buf[pl.ds(halo, tc), c1:] + s8buf[pl.ds(halo - 8, tc), c1:]
    sums = (s2buf[pl.ds(halo, tc), 0:c1], s4buf[pl.ds(halo, tc), 0:c1], s8buf[pl.ds(halo, tc), 0:c1], s16)
    pos = start_pos + tb * tc + lax.broadcasted_iota(jnp.int32, (tc, 1), 0)
    for gi, win in enumerate(POOL_WINDOWS):
        sl = slice(gi * POOL_CH, (gi + 1) * POOL_CH)
        cnt = jnp.minimum(pos + 1, win).astype(F32)
        dlt = sums[gi] / cnt - z[:, sl]
        y = jnp.dot(dlt.astype(BF16), pw_ref[gi], preferred_element_type=F32) * ps_ref[:, sl]
        y_ref[0, :, sl] = y.astype(y_ref.dtype)
    if carry:
        xbuf[pl.ds(16, 16), :] = xbuf[pl.ds(n - 16, 16), :]


def _pool(z3, hist16, pool_w, pool_scale, start_pos, tc, out_dtype):
    b, l, _ = z3.shape
    n = tc + POOL_HALO
    return pl.pallas_call(
        functools.partial(_pool_kernel, tc=tc, start_pos=start_pos, carry=l > tc),
        grid=(b, l // tc),
        in_specs=[pl.BlockSpec((1, tc, GROUP_W), lambda i, j: (i, j, Z_POOL // GROUP_W)),
                  pl.BlockSpec((1, 16, GROUP_W), lambda i, j: (i, 0, 0)),
                  pl.BlockSpec((len(POOL_WINDOWS), POOL_CH, POOL_CH), lambda i, j: (0, 0, 0)),
                  pl.BlockSpec((1, GROUP_W), lambda i, j: (0, 0))],
        out_specs=pl.BlockSpec((1, tc, GROUP_W), lambda i, j: (i, j, 0)),
        out_shape=jax.ShapeDtypeStruct((b, l, GROUP_W), out_dtype),
        scratch_shapes=[pltpu.VMEM((n, GROUP_W), F32), pltpu.VMEM((n, GROUP_W), F32),
                        pltpu.VMEM((n, 3 * POOL_CH), F32), pltpu.VMEM((n, 2 * POOL_CH), F32)],
        compiler_params=_cparams(("parallel", "arbitrary")),
        name="pool",
    )(z3, hist16, pool_w, pool_scale)


def _pad_cols(a, n):
    return jnp.pad(a, ((0, 0),) * (a.ndim - 1) + ((0, n - a.shape[-1]),))


def _prep_layer(i, W):
    c1, c2, c3 = RWKV_COLS, RWKV_COLS + SGU_COLS, RWKV_COLS + SGU_COLS + HGRN_COLS
    w_in = W["w_in"][i]
    w_in_p = jnp.concatenate([w_in[:, c2:c3], w_in[:, c1:c2], w_in[:, c3:], _pad_cols(w_in[:, :c1], RWKV_PAD)],
                             axis=1).astype(BF16)
    row = lambda a: a.reshape(1, -1)
    lora = lambda a, off: jnp.pad(a, ((off, LORA_W - off - a.shape[0]), (0, 0))).astype(BF16)
    rw = dict(
        mu=_pad_cols(row(W["rwkv_mu"][i]), RWKV_PAD),
        lora_w=lora(W["rwkv_w_lora"][i], 0), w0=row(W["rwkv_w0"][i]),
        lora_a=lora(W["rwkv_a_lora"][i], W_LORA), a0=row(W["rwkv_a0"][i]),
        lora_g=lora(W["rwkv_g_lora"][i], W_LORA + A_LORA),
        k_k=row(W["rwkv_k_k"][i]), k_a=row(W["rwkv_k_a"][i]), r_k=row(W["rwkv_r_k"][i]),
        gn_w=row(W["rwkv_gn_w"][i]), gn_b=row(W["rwkv_gn_b"][i]))
    sg = dict(ln_w=row(W["sgu_ln_w"][i]), ln_b=row(W["sgu_ln_b"][i]), norm=row(W["sgu_norm"][i]))
    return dict(
        w_in=w_in_p, rw=rw, sg=sg, sgu_w=W["sgu_w"][i], sgu_b=W["sgu_b"][i],
        hgrn_norm=row(W["hgrn_norm"][i]), pool_w=W["pool_w"][i].astype(BF16), pool_scale=row(W["pool_scale"][i]),
        w_out=W["w_out"][i].astype(BF16), w_gu=W["ffn_w_gu"][i].astype(BF16), w_down=W["ffn_w_down"][i].astype(BF16),
        ple_gate=W["ple_gate"][i].astype(BF16), ple_proj=W["ple_proj"][i].astype(BF16),
        ln_mix_pre=row(W["ln_mix_pre"][i]), ln_mix_post=row(W["ln_mix_post"][i]),
        ln_ffn_pre=row(W["ln_ffn_pre"][i]), ln_ffn_post=row(W["ln_ffn_post"][i]))


def _to_pairs(s):
    b = s.shape[0]
    return s.reshape(b, RWKV_PAIRS, 2, RWKV_HEAD, RWKV_HEAD).transpose(0, 1, 3, 2, 4).reshape(b, RWKV_PAIRS, RWKV_HEAD, 128)


def _from_pairs(s):
    b = s.shape[0]
    return s.reshape(b, RWKV_PAIRS, RWKV_HEAD, 2, RWKV_HEAD).transpose(0, 1, 3, 2, 4).reshape(b, RWKV_HEADS, RWKV_HEAD, RWKV_HEAD)


def _trunk(x, p, wkv0, shift0, hgrn0, pool0, start_pos, layers, lb_logits, cfg):
    b, l, _ = x.shape
    t = b * l
    xf = x.reshape(t, D_MODEL)
    wkv_l, shift_l, hgrn_l, pool_l, sgu_l = [], [], [], [], []
    for i, lw in enumerate(layers):
        z2 = _inproj(xf, lw["ln_mix_pre"], lw["w_in"], cfg["tm"], cfg["tn_in"])
        z3 = z2.reshape(b, l, Z_COLS)
        shift_prev = _pad_cols(shift0[i], RWKV_PAD).reshape(b, 1, RWKV_PAD)
        y_r, wkv_new = _rwkv(z3, shift_prev, _to_pairs(wkv0[i]), lw["rw"], cfg["tc_rwkv"], cfg["ydt"])
        if cfg["sgu_blk"] == SGU_CHUNK:
            wm, bias = lw["sgu_w"], lw["sgu_b"].T
        else:
            rep = SGU_CHUNK // l
            wm = jnp.tile(lw["sgu_w"][:, :l, :l], (1, rep, rep))
            bias = jnp.tile(lw["sgu_b"][:, :l].T, (rep, 1))
        y_s, v_rows = _sgu(z2, lw["sg"], wm, bias, cfg["sgu_blk"], cfg["ydt"])
        y_h, hgrn_new = _hgrn(z3, hgrn0[i], lb_logits, lw["hgrn_norm"], i, cfg["tc_hgrn"], cfg["hgrn_chunk"],
                              cfg["hgrn_sub"], cfg["ydt"])
        hist16 = jnp.pad(pool0[i], ((0, 0), (1, 0), (0, 0)))
        y_p = _pool(z3, hist16, lw["pool_w"], lw["pool_scale"], start_pos, cfg["tc_pool"], cfg["ydt"])
        ys = [y.reshape(t, GROUP_W) for y in (y_r, y_s.reshape(b, l, GROUP_W), y_h, y_p)]
        xf = _outproj(ys, xf, lw["w_out"], lw["ln_mix_post"], cfg["tm"])
        xf = _ffn(xf, lw["ln_ffn_pre"], lw["w_gu"], lw["w_down"], lw["ln_ffn_post"], cfg["tm"], cfg["tf"])
        xf = _ple(xf, p[i].reshape(t, PLE_DIM), lw["ple_gate"], lw["ple_proj"], cfg["tm"], cfg["tn_ple"])
        z_r = z3[:, :, Z_RWKV:Z_RWKV + RWKV_COLS]
        z_p = z3[:, :, Z_POOL:Z_POOL + GROUP_W]
        wkv_l.append(_from_pairs(wkv_new))
        shift_l.append(z_r[:, -1])
        hgrn_l.append(hgrn_new)
        pool_l.append(jnp.concatenate([pool0[i], z_p], axis=1)[:, -POOL_HIST:])
        sgu_l.append(v_rows.reshape(b, l, GROUP_W))
    return (xf.reshape(b, l, D_MODEL), jnp.stack(wkv_l), jnp.stack(shift_l), jnp.stack(hgrn_l),
            jnp.stack(pool_l), jnp.stack(sgu_l))


def _cfg(l):
    if l >= 256:
        return dict(tm=512, tn_in=1792, tf=512, tn_ple=512, tc_rwkv=128, sgu_blk=SGU_CHUNK, tc_hgrn=256,
                    hgrn_chunk=HGRN_CHUNK, hgrn_sub=HGRN_SUB, tc_pool=256, ydt=BF16)
    return dict(tm=512, tn_in=1792, tf=512, tn_ple=512, tc_rwkv=l, sgu_blk=l, tc_hgrn=l,
                hgrn_chunk=l, hgrn_sub=l, tc_pool=l, ydt=BF16)


def kernel(x_prompt, x_sample, state_rwkv_wkv, state_rwkv_shift, state_hgrn, state_pool, p_prompt, p_sample,
           ln_mix_pre, ln_mix_post, ln_ffn_pre, ln_ffn_post, w_in, rwkv_mu, rwkv_w_lora, rwkv_w0, rwkv_a_lora,
           rwkv_a0, rwkv_g_lora, rwkv_k_k, rwkv_k_a, rwkv_r_k, rwkv_gn_w, rwkv_gn_b, sgu_ln_w, sgu_ln_b, sgu_w,
           sgu_b, sgu_norm, hgrn_lb_logits, hgrn_norm, pool_w, pool_scale, w_out, ffn_w_gu, ffn_w_down, ple_gate,
           ple_proj):
    W = dict(ln_mix_pre=ln_mix_pre, ln_mix_post=ln_mix_post, ln_ffn_pre=ln_ffn_pre, ln_ffn_post=ln_ffn_post,
             w_in=w_in, rwkv_mu=rwkv_mu, rwkv_w_lora=rwkv_w_lora, rwkv_w0=rwkv_w0, rwkv_a_lora=rwkv_a_lora,
             rwkv_a0=rwkv_a0, rwkv_g_lora=rwkv_g_lora, rwkv_k_k=rwkv_k_k, rwkv_k_a=rwkv_k_a,
             rwkv_r_k=rwkv_r_k.reshape(DEPTH, GROUP_W), rwkv_gn_w=rwkv_gn_w, rwkv_gn_b=rwkv_gn_b, sgu_ln_w=sgu_ln_w,
             sgu_ln_b=sgu_ln_b, sgu_w=sgu_w, sgu_b=sgu_b, sgu_norm=sgu_norm, hgrn_norm=hgrn_norm, pool_w=pool_w,
             pool_scale=pool_scale, w_out=w_out, ffn_w_gu=ffn_w_gu, ffn_w_down=ffn_w_down, ple_gate=ple_gate,
             ple_proj=ple_proj)
    layers = [_prep_layer(i, W) for i in range(DEPTH)]
    lb_logits = hgrn_lb_logits.astype(F32)
    bp, lp, _ = x_prompt.shape
    zeros = lambda *s: jnp.zeros((DEPTH, bp) + s, F32)
    y_prompt, wkv_p, shift_p, hgrn_p, pool_p, _ = _trunk(
        x_prompt, p_prompt, zeros(RWKV_HEADS, RWKV_HEAD, RWKV_HEAD), zeros(RWKV_COLS),
        zeros(HGRN_HEADS, HGRN_HEAD, HGRN_HEAD), zeros(POOL_HIST, GROUP_W), 0, layers, lb_logits, _cfg(lp))
    y_sample, wkv_s, shift_s, hgrn_s, pool_s, sgu_v_s = _trunk(
        x_sample, p_sample, state_rwkv_wkv, state_rwkv_shift, state_hgrn, state_pool, PAST_LEN, layers, lb_logits,
        _cfg(x_sample.shape[1]))
    return (y_prompt, y_sample, wkv_p, shift_p, hgrn_p, pool_p, wkv_s, shift_s, hgrn_s, pool_s, sgu_v_s)
```

```python
import functools

import jax
import jax.numpy as jnp
from jax import lax
from jax.experimental import pallas as pl
from jax.experimental.pallas import tpu as pltpu

F32 = jnp.float32
BF16 = jnp.bfloat16

D_MODEL = 2048
DEPTH = 2
PAST_LEN = 16384
GROUP_W = 512
RWKV_HEAD = 64
RWKV_HEADS = 8
RWKV_PAIRS = RWKV_HEADS // 2
RWKV_CHUNK = 64
W_LORA, A_LORA, G_LORA = 32, 32, 96
RWKV_GN_EPS = 64e-5
SGU_CHUNK = 128
SGU_HEADS = 4
HGRN_HEADS = 4
HGRN_HEAD = 128
HGRN_CHUNK = 64
HGRN_SUB = 16
POOL_WINDOWS = (2, 4, 8, 16)
POOL_CH = 128
POOL_HIST = 15
POOL_HALO = 32
D_FF = 5632
PLE_DIM = 256
NORM_EPS = 1e-6
LN_EPS = 1e-5
RWKV_COLS = 3 * GROUP_W + W_LORA + A_LORA + G_LORA
RWKV_PAD = 1792
LORA_OFF = 3 * GROUP_W
LORA_W = RWKV_PAD - LORA_OFF
SGU_COLS = 2 * GROUP_W
HGRN_COLS = 4 * GROUP_W
Z_HGRN, Z_SGU, Z_POOL, Z_RWKV = 0, 2048, 3072, 3584
Z_COLS = Z_RWKV + RWKV_PAD
VMEM_LIMIT = 56 * 1024 * 1024


def _cparams(sem):
    return pltpu.CompilerParams(dimension_semantics=sem, vmem_limit_bytes=VMEM_LIMIT)


def _sigmoid(x):
    return jax.nn.sigmoid(x)


def _silu(x):
    return x * jax.nn.sigmoid(x)


def _split3(x):
    hi = x.astype(BF16)
    r = x - hi.astype(F32)
    mid = r.astype(BF16)
    lo = (r - mid.astype(F32)).astype(BF16)
    return hi, mid, lo


def _dot_f32_lhs(x, m):
    hi, mid, lo = _split3(x)
    d = functools.partial(jnp.dot, preferred_element_type=F32)
    return d(hi, m) + d(mid, m) + d(lo, m)


def _dot_f32_rhs(m, x):
    hi, mid, lo = _split3(x)
    d = functools.partial(jnp.dot, preferred_element_type=F32)
    return d(m, hi) + d(m, mid) + d(m, lo)


def _rms(x, g):
    return x * lax.rsqrt(jnp.mean(x * x, axis=-1, keepdims=True) + NORM_EPS) * g


def _inproj_kernel(x_ref, g_ref, w_ref, z_ref, h_ref):
    @pl.when(pl.program_id(1) == 0)
    def _():
        h_ref[...] = _rms(x_ref[...], g_ref[...]).astype(BF16)

    z_ref[...] = jnp.dot(h_ref[...], w_ref[...], preferred_element_type=F32)


def _inproj(x, g, w, tm, tn):
    t = x.shape[0]
    return pl.pallas_call(
        _inproj_kernel,
        grid=(t // tm, Z_COLS // tn),
        in_specs=[pl.BlockSpec((tm, D_MODEL), lambda i, j: (i, 0)),
                  pl.BlockSpec((1, D_MODEL), lambda i, j: (0, 0)),
                  pl.BlockSpec((D_MODEL, tn), lambda i, j: (0, j))],
        out_specs=pl.BlockSpec((tm, tn), lambda i, j: (i, j)),
        out_shape=jax.ShapeDtypeStruct((t, Z_COLS), F32),
        scratch_shapes=[pltpu.VMEM((tm, D_MODEL), BF16)],
        compiler_params=_cparams(("parallel", "arbitrary")),
        name="inproj",
    )(x, g, w)


def _outproj_kernel(yr_ref, ys_ref, yh_ref, yp_ref, x_ref, w_ref, g_ref, o_ref):
    d = functools.partial(jnp.dot, preferred_element_type=F32)
    g4 = GROUP_W
    mix = (d(yr_ref[...], w_ref[0:g4, :]) + d(ys_ref[...], w_ref[g4:2 * g4, :])
           + d(yh_ref[...], w_ref[2 * g4:3 * g4, :]) + d(yp_ref[...], w_ref[3 * g4:4 * g4, :]))
    o_ref[...] = x_ref[...] + _rms(mix, g_ref[...])


def _outproj(ys, x, w, g, tm):
    t = x.shape[0]
    yspec = pl.BlockSpec((tm, GROUP_W), lambda i: (i, 0))
    return pl.pallas_call(
        _outproj_kernel,
        grid=(t // tm,),
        in_specs=[yspec, yspec, yspec, yspec,
                  pl.BlockSpec((tm, D_MODEL), lambda i: (i, 0)),
                  pl.BlockSpec((D_MODEL, D_MODEL), lambda i: (0, 0)),
                  pl.BlockSpec((1, D_MODEL), lambda i: (0, 0))],
        out_specs=pl.BlockSpec((tm, D_MODEL), lambda i: (i, 0)),
        out_shape=jax.ShapeDtypeStruct((t, D_MODEL), F32),
        compiler_params=_cparams(("parallel",)),
        name="outproj",
    )(*ys, x, w, g)


def _ffn_kernel(x_ref, gpre_ref, wg_ref, wu_ref, wd_ref, gpost_ref, o_ref, h_ref, acc_ref):
    j = pl.program_id(1)

    @pl.when(j == 0)
    def _():
        h_ref[...] = _rms(x_ref[...], gpre_ref[...]).astype(BF16)
        acc_ref[...] = jnp.zeros_like(acc_ref)

    h = h_ref[...]
    gate = jnp.dot(h, wg_ref[...], preferred_element_type=F32)
    up = jnp.dot(h, wu_ref[...], preferred_element_type=F32)
    act = (_silu(gate) * up).astype(BF16)
    acc_ref[...] += jnp.dot(act, wd_ref[...], preferred_element_type=F32)

    @pl.when(j == pl.num_programs(1) - 1)
    def _():
        o_ref[...] = x_ref[...] + _rms(acc_ref[...], gpost_ref[...])


def _ffn(x, gpre, w_gu, w_down, gpost, tm, tf):
    t = x.shape[0]
    nf = D_FF // tf
    return pl.pallas_call(
        _ffn_kernel,
        grid=(t // tm, nf),
        in_specs=[pl.BlockSpec((tm, D_MODEL), lambda i, j: (i, 0)),
                  pl.BlockSpec((1, D_MODEL), lambda i, j: (0, 0)),
                  pl.BlockSpec((D_MODEL, tf), lambda i, j: (0, j)),
                  pl.BlockSpec((D_MODEL, tf), lambda i, j: (0, j + nf)),
                  pl.BlockSpec((tf, D_MODEL), lambda i, j: (j, 0)),
                  pl.BlockSpec((1, D_MODEL), lambda i, j: (0, 0))],
        out_specs=pl.BlockSpec((tm, D_MODEL), lambda i, j: (i, 0)),
        out_shape=jax.ShapeDtypeStruct((t, D_MODEL), F32),
        scratch_shapes=[pltpu.VMEM((tm, D_MODEL), BF16), pltpu.VMEM((tm, D_MODEL), F32)],
        compiler_params=_cparams(("parallel", "arbitrary")),
        name="ffn",
    )(x, gpre, w_gu, w_gu, w_down, gpost)


def _ple_kernel(x_ref, xc_ref, p_ref, wgate_ref, wproj_ref, o_ref, h_ref):
    @pl.when(pl.program_id(1) == 0)
    def _():
        h_ref[...] = x_ref[...].astype(BF16)

    gate = _sigmoid(jnp.dot(h_ref[...], wgate_ref[...], preferred_element_type=F32))
    ple = jnp.dot(p_ref[...].astype(BF16), wproj_ref[...], preferred_element_type=F32)
    o_ref[...] = xc_ref[...] + gate * ple


def _ple(x, p, w_gate, w_proj, tm, tn):
    t = x.shape[0]
    return pl.pallas_call(
        _ple_kernel,
        grid=(t // tm, D_MODEL // tn),
        in_specs=[pl.BlockSpec((tm, D_MODEL), lambda i, j: (i, 0)),
                  pl.BlockSpec((tm, tn), lambda i, j: (i, j)),
                  pl.BlockSpec((tm, PLE_DIM), lambda i, j: (i, 0)),
                  pl.BlockSpec((D_MODEL, tn), lambda i, j: (0, j)),
                  pl.BlockSpec((PLE_DIM, tn), lambda i, j: (0, j))],
        out_specs=pl.BlockSpec((tm, tn), lambda i, j: (i, j)),
        out_shape=jax.ShapeDtypeStruct((t, D_MODEL), F32),
        scratch_shapes=[pltpu.VMEM((tm, D_MODEL), BF16)],
        compiler_params=_cparams(("parallel", "arbitrary")),
        name="ple",
    )(x, x, p, w_gate, w_proj)


def _head_ones():
    r = lax.broadcasted_iota(jnp.int32, (GROUP_W, GROUP_W), 0) // RWKV_HEAD
    c = lax.broadcasted_iota(jnp.int32, (GROUP_W, GROUP_W), 1) // RWKV_HEAD
    return (r == c).astype(BF16)


def _rwkv_prep(z, zbuf, mu_ref, lw_ref, w0_ref, la_ref, a0_ref, lg_ref, kk_ref, ka_ref, rk_ref, tc):
    zbuf[pl.ds(8, tc), :] = z
    z_prev = zbuf[pl.ds(7, tc), :]
    zbuf[pl.ds(7, 1), :] = zbuf[pl.ds(7 + tc, 1), :]
    zm = z + mu_ref[...] * (z_prev - z)
    g4 = GROUP_W
    r = zm[:, 0:g4]
    k = zm[:, g4:2 * g4]
    v = zm[:, 2 * g4:3 * g4]
    zl = zm[:, LORA_OFF:RWKV_PAD]
    d = functools.partial(jnp.dot, preferred_element_type=F32)
    wl = w0_ref[...] + d(jnp.tanh(zl).astype(BF16), lw_ref[...])
    w_log = -(jnp.maximum(-wl, 0.0) + jnp.log1p(jnp.exp(-jnp.abs(wl)))) - 0.5
    a = _sigmoid(a0_ref[...] + d(zl.astype(BF16), la_ref[...]))
    g = d(_sigmoid(zl).astype(BF16), lg_ref[...])
    ones = _head_ones()
    kk = k * kk_ref[...]
    kk = kk / jnp.maximum(jnp.sqrt(_dot_f32_lhs(kk * kk, ones)), 1e-12)
    k = k * (1.0 + (a - 1.0) * ka_ref[...])
    bonus = _dot_f32_lhs(r * k * rk_ref[...], ones) * v
    return r, k, v, -kk, kk * a, w_log, g, bonus


def _rwkv_post(y, bonus, g, gnw_ref, gnb_ref):
    ones = _head_ones()
    inv_n = 1.0 / RWKV_HEAD
    mean = _dot_f32_lhs(y, ones) * inv_n
    yc = y - mean
    var = _dot_f32_lhs(yc * yc, ones) * inv_n
    yn = yc * lax.rsqrt(var + RWKV_GN_EPS) * gnw_ref[...] + gnb_ref[...]
    return (yn + bonus) * g


def _rwkv_scan_kernel(z_ref, sh_ref, s0_ref, mu_ref, lw_ref, w0_ref, la_ref, a0_ref, lg_ref, kk_ref, ka_ref, rk_ref,
                      gnw_ref, gnb_ref, y_ref, sout_ref,
                      zbuf, r_s, w_s, k_s, v_s, a_s, b_s, y_s, *, tc):
    zbuf[pl.ds(0, 8), :] = jnp.broadcast_to(sh_ref[0], (8, RWKV_PAD))
    r, k, v, a_vec, b_vec, w_log, g, bonus = _rwkv_prep(
        z_ref[0], zbuf, mu_ref, lw_ref, w0_ref, la_ref, a0_ref, lg_ref, kk_ref, ka_ref, rk_ref, tc)
    r_s[...] = r
    w_s[...] = jnp.exp(-jnp.exp(w_log))
    k_s[...] = k
    v_s[...] = v
    a_s[...] = a_vec
    b_s[...] = b_vec

    shp = (RWKV_HEAD, 2 * RWKV_HEAD)
    lane = lax.broadcasted_iota(jnp.int32, shp, 1)
    sub = lax.broadcasted_iota(jnp.int32, shp, 0)
    lo = lane < RWKV_HEAD
    eye = ((lane % RWKV_HEAD) == sub).astype(F32)

    def head_sums(x):
        s0 = jnp.sum(jnp.where(lo, x, 0.0), axis=1, keepdims=True)
        s1 = jnp.sum(jnp.where(lo, 0.0, x), axis=1, keepdims=True)
        return jnp.where(lo, s0, s1)

    gs = min(8, tc)

    def steps(gi, states):
        t0 = pl.multiple_of(gi * gs, gs)
        new = []
        for p in range(RWKV_PAIRS):
            sl = slice(p * 128, (p + 1) * 128)
            a8, v8, w8, b8, k8, r8 = (ref[pl.ds(t0, gs), sl] for ref in (a_s, v_s, w_s, b_s, k_s, r_s))
            s = states[p]
            ys = []
            for j in range(gs):
                row = lambda x8: x8[j:j + 1]
                sa = head_sums(s * row(a8))
                vcol = head_sums(eye * row(v8))
                s = s * row(w8) + sa * row(b8) + vcol * row(k8)
                ycol = head_sums(s * row(r8))
                ys.append(jnp.sum(eye * ycol, axis=0, keepdims=True))
            y_s[pl.ds(t0, gs), sl] = jnp.concatenate(ys, axis=0)
            new.append(s)
        return tuple(new)

    states = lax.fori_loop(0, tc // gs, steps, tuple(s0_ref[0, p] for p in range(RWKV_PAIRS)))
    for p in range(RWKV_PAIRS):
        sout_ref[0, p] = states[p]
    y_ref[0] = _rwkv_post(y_s[...], bonus, g, gnw_ref, gnb_ref).astype(y_ref.dtype)


def _tri_inverse(l_mat, n):
    row = lax.broadcasted_iota(jnp.int32, (n, n), 0)
    col = lax.broadcasted_iota(jnp.int32, (n, n), 1)
    eye = (row == col).astype(F32)
    mm = lambda a, b: jnp.dot(a.astype(BF16), b.astype(BF16), preferred_element_type=F32)
    same = lambda m: (row // m) == (col // m)
    d1 = jnp.where(same(8), l_mat, 0.0)
    d2 = mm(d1, d1)
    d4 = mm(d2, d2)
    t = eye + d1 + d2 + mm(d1, d2)
    t = t + mm(t, d4)
    m = 8
    while m < RWKV_CHUNK:
        e = jnp.where(same(2 * m) & jnp.logical_not(same(m)), l_mat, 0.0)
        t = t + mm(mm(t, e), t)
        m *= 2
    return t


def _rwkv_chunk_kernel(z_ref, sh_ref, s0_ref, mu_ref, lw_ref, w0_ref, la_ref, a0_ref, lg_ref, kk_ref, ka_ref, rk_ref,
                       gnw_ref, gnb_ref, y_ref, sout_ref, zbuf, s_scr, *, n_chunks):
    c = RWKV_CHUNK
    n = 2 * c
    lo64 = lax.broadcasted_iota(jnp.int32, (c, 128), 1) < RWKV_HEAD
    for p in range(RWKV_PAIRS):
        sp = s0_ref[0, p]
        s_scr[p] = jnp.concatenate([jnp.where(lo64, sp, 0.0), jnp.where(lo64, 0.0, sp)], axis=0)
    zbuf[pl.ds(0, 8), :] = jnp.broadcast_to(sh_ref[0], (8, RWKV_PAD))

    def chunk_body(ci, carry):
        r0 = pl.multiple_of(ci * c, c)
        y = _rwkv_chunk(z_ref[0, pl.ds(r0, c), :], zbuf, s_scr, mu_ref, lw_ref, w0_ref, la_ref, a0_ref, lg_ref,
                        kk_ref, ka_ref, rk_ref, gnw_ref, gnb_ref)
        y_ref[0, pl.ds(r0, c), :] = y.astype(y_ref.dtype)
        return carry

    lax.fori_loop(0, n_chunks, chunk_body, 0)
    for p in range(RWKV_PAIRS):
        sout_ref[0, p] = s_scr[p, 0:c, :] + s_scr[p, c:n, :]


def _rwkv_chunk(z, zbuf, s_scr, mu_ref, lw_ref, w0_ref, la_ref, a0_ref, lg_ref, kk_ref, ka_ref, rk_ref,
                gnw_ref, gnb_ref):
    c = RWKV_CHUNK
    lo64 = lax.broadcasted_iota(jnp.int32, (c, 128), 1) < RWKV_HEAD
    r, k, v, a_vec, b_vec, w_log, g, bonus = _rwkv_prep(
        z, zbuf, mu_ref, lw_ref, w0_ref, la_ref, a0_ref, lg_ref, kk_ref, ka_ref, rk_ref, c)
    logw = -jnp.exp(w_log)
    tri = (lax.broadcasted_iota(jnp.int32, (c, c), 1) <= lax.broadcasted_iota(jnp.int32, (c, c), 0)).astype(BF16)
    cum = _dot_f32_rhs(tri, logw)
    c_end = cum[c - 1:c]
    e_neg = jnp.exp(-cum)
    e_end = jnp.exp(c_end - cum)
    at = a_vec * jnp.exp(cum - logw)
    rt = r * jnp.exp(cum)
    bt = b_vec * e_neg
    kt = k * e_neg
    bh = b_vec * e_end
    kh = k * e_end
    e_cend = jnp.exp(c_end)

    n = 2 * c
    row = lax.broadcasted_iota(jnp.int32, (n, n), 0)
    col = lax.broadcasted_iota(jnp.int32, (n, n), 1)
    same_head = (row // c) == (col // c)
    strict = same_head & (col < row)
    incl = same_head & (col <= row)
    zeros = jnp.zeros((n, n), BF16)
    mm = lambda x, y: jnp.dot(x, y, preferred_element_type=F32)
    nt = lambda x, y: lax.dot_general(x, y, (((1,), (1,)), ((), ())), preferred_element_type=F32)
    tn = lambda x, y: lax.dot_general(x, y, (((0,), (0,)), ((), ())), preferred_element_type=F32)
    stack = lambda x: jnp.concatenate([jnp.where(lo64, x, 0.0), jnp.where(lo64, 0.0, x)], axis=0)
    unstack = lambda x: x[0:c] + x[c:n]

    ys = []
    for p in range(RWKV_PAIRS):
        sl = slice(p * 128, (p + 1) * 128)
        xa, xr, yb, yk = stack(at[:, sl]), stack(rt[:, sl]), stack(bt[:, sl]), stack(kt[:, sl])
        vs, xbh, xkh = stack(v[:, sl]).astype(BF16), stack(bh[:, sl]), stack(kh[:, sl])
        gram = nt(jnp.concatenate([xa, xr], axis=0).astype(BF16), jnp.concatenate([yb, yk], axis=0).astype(BF16))
        l_ab = jnp.where(strict, gram[0:n, 0:n], 0.0)
        l_ak = jnp.where(strict, gram[0:n, n:2 * n], 0.0)
        m_rb = jnp.where(incl, gram[n:2 * n, 0:n], 0.0)
        m_rk = jnp.where(incl, gram[n:2 * n, n:2 * n], 0.0)
        t_inv = _tri_inverse(l_ab, n)
        w1 = mm(l_ak.astype(BF16), vs)
        tu = mm(t_inv.astype(BF16), jnp.concatenate([xa, w1], axis=1).astype(BF16))
        zmat = jnp.concatenate([tu.astype(BF16), jnp.concatenate([zeros, vs], axis=1)], axis=0)
        ry = mm(jnp.concatenate([m_rb, m_rk], axis=1).astype(BF16), zmat)
        ps = tn(zmat, jnp.concatenate([xbh, xkh], axis=0).astype(BF16))
        s = s_scr[p]
        sb = s.astype(BF16)
        rc = unstack(xr + ry[:, 0:n])
        ys.append(nt(rc.astype(BF16), sb) + unstack(ry[:, n:2 * n]))
        s_scr[p] = s * e_cend[:, sl] + mm(sb, ps[0:n].astype(BF16)) + ps[n:2 * n]

    y = jnp.concatenate(ys, axis=1)
    return _rwkv_post(y, bonus, g, gnw_ref, gnb_ref)


def _rwkv(z3, shift_prev, wkv0, pw, chunked, out_dtype):
    b, l, _ = z3.shape
    vec = lambda n: pl.BlockSpec((1, n), lambda i: (0, 0))
    mat = lambda r, c: pl.BlockSpec((r, c), lambda i: (0, 0))
    st = pl.BlockSpec((1, RWKV_PAIRS, RWKV_HEAD, 128), lambda i: (i, 0, 0, 0))
    tile = lambda: pltpu.VMEM((l, GROUP_W), F32)
    if chunked:
        body = functools.partial(_rwkv_chunk_kernel, n_chunks=l // RWKV_CHUNK)
        scratch = [pltpu.VMEM((RWKV_CHUNK + 8, RWKV_PAD), F32), pltpu.VMEM((RWKV_PAIRS, 128, 128), F32)]
    else:
        body = functools.partial(_rwkv_scan_kernel, tc=l)
        scratch = [pltpu.VMEM((l + 8, RWKV_PAD), F32), tile(), tile(), tile(), tile(), tile(), tile(), tile()]
    return pl.pallas_call(
        body,
        grid=(b,),
        in_specs=[pl.BlockSpec((1, l, RWKV_PAD), lambda i: (i, 0, Z_RWKV // RWKV_PAD)),
                  pl.BlockSpec((1, 1, RWKV_PAD), lambda i: (i, 0, 0)),
                  st, vec(RWKV_PAD), mat(LORA_W, GROUP_W), vec(GROUP_W), mat(LORA_W, GROUP_W), vec(GROUP_W),
                  mat(LORA_W, GROUP_W), vec(GROUP_W), vec(GROUP_W), vec(GROUP_W), vec(GROUP_W), vec(GROUP_W)],
        out_specs=[pl.BlockSpec((1, l, GROUP_W), lambda i: (i, 0, 0)), st],
        out_shape=[jax.ShapeDtypeStruct((b, l, GROUP_W), out_dtype),
                   jax.ShapeDtypeStruct((b, RWKV_PAIRS, RWKV_HEAD, 128), F32)],
        scratch_shapes=scratch,
        compiler_params=_cparams(("parallel",)),
        name="rwkv7",
    )(z3, shift_prev, wkv0, pw["mu"], pw["lora_w"], pw["w0"], pw["lora_a"], pw["a0"], pw["lora_g"],
      pw["k_k"], pw["k_a"], pw["r_k"], pw["gn_w"], pw["gn_b"])


def _sgu_kernel(z_ref, lnw_ref, lnb_ref, wm_ref, bias_ref, nrm_ref, y_ref, v_ref, *, blk):
    z = z_ref[...]
    zg = 0.5 * z * (1.0 + lax.erf(z * 0.7071067811865476))
    u = zg[:, :GROUP_W]
    v = zg[:, GROUP_W:]
    mu = jnp.mean(v, axis=-1, keepdims=True)
    vc = v - mu
    var = jnp.mean(vc * vc, axis=-1, keepdims=True)
    vn = vc * lax.rsqrt(var + LN_EPS) * lnw_ref[...] + lnb_ref[...]
    v_ref[...] = vn
    row = lax.broadcasted_iota(jnp.int32, (SGU_CHUNK, SGU_CHUNK), 0)
    col = lax.broadcasted_iota(jnp.int32, (SGU_CHUNK, SGU_CHUNK), 1)
    mask = col <= row
    if blk < SGU_CHUNK:
        mask = mask & ((row // blk) == (col // blk))
    vb = vn.astype(BF16)
    parts = []
    for h in range(SGU_HEADS):
        sl = slice(h * 128, (h + 1) * 128)
        wm = jnp.where(mask, wm_ref[h], 0.0).astype(BF16)
        parts.append(jnp.dot(wm, vb[:, sl], preferred_element_type=F32) + bias_ref[:, h:h + 1])
    s = jnp.concatenate(parts, axis=1)
    y_ref[...] = _rms(u * s, nrm_ref[...]).astype(y_ref.dtype)


def _sgu(z2, pw, wm, bias, blk, out_dtype):
    t = z2.shape[0]
    vec = pl.BlockSpec((1, GROUP_W), lambda i: (0, 0))
    return pl.pallas_call(
        functools.partial(_sgu_kernel, blk=blk),
        grid=(t // SGU_CHUNK,),
        in_specs=[pl.BlockSpec((SGU_CHUNK, SGU_COLS), lambda i: (i, Z_SGU // SGU_COLS)),
                  vec, vec,
                  pl.BlockSpec((SGU_HEADS, SGU_CHUNK, SGU_CHUNK), lambda i: (0, 0, 0)),
                  pl.BlockSpec((SGU_CHUNK, SGU_HEADS), lambda i: (0, 0)),
                  vec],
        out_specs=[pl.BlockSpec((SGU_CHUNK, GROUP_W), lambda i: (i, 0)),
                   pl.BlockSpec((SGU_CHUNK, GROUP_W), lambda i: (i, 0))],
        out_shape=[jax.ShapeDtypeStruct((t, GROUP_W), out_dtype), jax.ShapeDtypeStruct((t, GROUP_W), F32)],
        compiler_params=_cparams(("parallel",)),
        name="sgu",
    )(z2, pw["ln_w"], pw["ln_b"], wm, bias, pw["norm"])


def _hgrn_kernel(zq_ref, zf_ref, zi_ref, zg_ref, lg_ref, nw_ref, s0_ref, y_ref, sout_ref, s_scr,
                 *, layer, chunk, sub, n_chunks):
    tb = pl.program_id(2)

    @pl.when(tb == 0)
    def _():
        s_scr[...] = s0_ref[0, 0]

    lg = lg_ref[...]
    e = jnp.exp(lg - jnp.max(lg, axis=0, keepdims=True))
    sm = e / jnp.sum(e, axis=0, keepdims=True)
    lb = jnp.sum(sm[0:layer + 1], axis=0, keepdims=True) - sm[0:1]

    tri = (lax.broadcasted_iota(jnp.int32, (chunk, chunk), 1)
           <= lax.broadcasted_iota(jnp.int32, (chunk, chunk), 0)).astype(BF16)
    eye = (lax.broadcasted_iota(jnp.int32, (HGRN_HEAD, HGRN_HEAD), 0)
           == lax.broadcasted_iota(jnp.int32, (HGRN_HEAD, HGRN_HEAD), 1)).astype(F32)
    rowid = lax.broadcasted_iota(jnp.int32, (sub, 1), 0)
    d = functools.partial(jnp.dot, preferred_element_type=F32)

    def chunk_body(c, carry):
        r0 = pl.multiple_of(c * chunk, chunk)
        zq = zq_ref[0, pl.ds(r0, chunk), :]
        q = _silu(zq)
        fg = lb + (1.0 - lb) * _sigmoid(zf_ref[0, pl.ds(r0, chunk), :])
        logf = jnp.log(fg)
        k = 1.0 - fg
        v = zi_ref[0, pl.ds(r0, chunk), :]
        vb = v.astype(BF16)
        b = _dot_f32_rhs(tri, logf)
        s = s_scr[...]
        o = d((q * jnp.exp(b)).astype(BF16), s.astype(BF16))
        outs = []
        for i in range(chunk // sub):
            lo_, hi_ = i * sub, (i + 1) * sub
            qi, bi, oi = q[lo_:hi_], b[lo_:hi_], o[lo_:hi_]
            if i > 0:
                bref = b[lo_ - 1:lo_]
                qt = (qi * jnp.exp(bi - bref)).astype(BF16)
                kt = (k[:lo_] * jnp.exp(bref - b[:lo_])).astype(BF16)
                att = lax.dot_general(qt, kt, (((1,), (1,)), ((), ())), preferred_element_type=F32)
                oi = oi + d(att.astype(BF16), vb[:lo_])
            for j in range(sub):
                sj = lo_ + j
                dec = jnp.exp(jnp.minimum(bi - b[sj:sj + 1], 0.0))
                att_j = jnp.sum(qi * dec * k[sj:sj + 1], axis=-1, keepdims=True)
                oi = oi + jnp.where(rowid >= j, att_j, 0.0) * v[sj:sj + 1]
            outs.append(oi)
        o = outs[0] if len(outs) == 1 else jnp.concatenate(outs, axis=0)
        blast = b[chunk - 1:chunk]
        kd = (k * jnp.exp(blast - b)).astype(BF16)
        upd = lax.dot_general(kd, vb, (((0,), (0,)), ((), ())), preferred_element_type=F32)
        e_col = jnp.sum(eye * jnp.exp(blast), axis=1, keepdims=True)
        s_scr[...] = e_col * s + upd
        on = o * lax.rsqrt(jnp.mean(o * o, axis=-1, keepdims=True) + NORM_EPS)
        y_ref[0, pl.ds(r0, chunk), :] = (on * nw_ref[...] * _silu(zg_ref[0, pl.ds(r0, chunk), :])).astype(y_ref.dtype)
        return carry

    lax.fori_loop(0, n_chunks, chunk_body, 0)

    @pl.when(tb == pl.num_programs(2) - 1)
    def _():
        sout_ref[0, 0] = s_scr[...]


def _hgrn(z3, s0, lb_logits, norm_w, layer, tc, chunk, sub, out_dtype):
    b, l, _ = z3.shape
    hb = Z_HGRN // HGRN_HEAD
    zspec = lambda part: pl.BlockSpec((1, tc, HGRN_HEAD), lambda i, h, j: (i, j, hb + part * HGRN_HEADS + h))
    st = pl.BlockSpec((1, 1, HGRN_HEAD, HGRN_HEAD), lambda i, h, j: (i, h, 0, 0))
    return pl.pallas_call(
        functools.partial(_hgrn_kernel, layer=layer, chunk=chunk, sub=sub, n_chunks=tc // chunk),
        grid=(b, HGRN_HEADS, l // tc),
        in_specs=[zspec(0), zspec(1), zspec(2), zspec(3),
                  pl.BlockSpec((DEPTH, HGRN_HEAD), lambda i, h, j: (0, h)),
                  pl.BlockSpec((1, HGRN_HEAD), lambda i, h, j: (0, h)),
                  st],
        out_specs=[pl.BlockSpec((1, tc, HGRN_HEAD), lambda i, h, j: (i, j, h)), st],
        out_shape=[jax.ShapeDtypeStruct((b, l, GROUP_W), out_dtype),
                   jax.ShapeDtypeStruct((b, HGRN_HEADS, HGRN_HEAD, HGRN_HEAD), F32)],
        scratch_shapes=[pltpu.VMEM((HGRN_HEAD, HGRN_HEAD), F32)],
        compiler_params=_cparams(("parallel", "parallel", "arbitrary")),
        name="hgrn2",
    )(z3, z3, z3, z3, lb_logits, norm_w, s0)


def _pool_kernel(z_ref, hist_ref, pw_ref, ps_ref, y_ref, xbuf, s2buf, s4buf, s8buf, *, tc, start_pos, carry):
    tb = pl.program_id(1)
    halo = POOL_HALO
    n = tc + halo

    @pl.when(tb == 0)
    def _():
        xbuf[pl.ds(0, 16), :] = jnp.zeros((16, GROUP_W), F32)
        xbuf[pl.ds(16, 16), :] = hist_ref[0]

    z = z_ref[0]
    xbuf[pl.ds(halo, tc), :] = z
    c1, c2, c3 = POOL_CH, 2 * POOL_CH, 3 * POOL_CH
    s2buf[pl.ds(8, n - 8), :] = xbuf[pl.ds(8, n - 8), :] + xbuf[pl.ds(7, n - 8), :]
    s4buf[pl.ds(16, n - 16), :] = s2buf[pl.ds(16, n - 16), c1:] + s2buf[pl.ds(14, n - 16), c1:]
    s8buf[pl.ds(24, n - 24), :] = s4buf[pl.ds(24, n - 24), c1:] + s4buf[pl.ds(20, n - 24), c1:]
    s16 = s8buf[pl.ds(halo, tc), c1:] + s8buf[pl.ds(halo - 8, tc), c1:]
    sums = (s2buf[pl.ds(halo, tc), 0:c1], s4buf[pl.ds(halo, tc), 0:c1], s8buf[pl.ds(halo, tc), 0:c1], s16)
    pos = start_pos + tb * tc + lax.broadcasted_iota(jnp.int32, (tc, 1), 0)
    for gi, win in enumerate(POOL_WINDOWS):
        sl = slice(gi * POOL_CH, (gi + 1) * POOL_CH)
        cnt = jnp.minimum(pos + 1, win).astype(F32)
        dlt = sums[gi] / cnt - z[:, sl]
        y = jnp.dot(dlt.astype(BF16), pw_ref[gi], preferred_element_type=F32) * ps_ref[:, sl]
        y_ref[0, :, sl] = y.astype(y_ref.dtype)
    if carry:
        xbuf[pl.ds(16, 16), :] = xbuf[pl.ds(n - 16, 16), :]


def _pool(z3, hist16, pool_w, pool_scale, start_pos, tc, out_dtype):
    b, l, _ = z3.shape
    n = tc + POOL_HALO
    return pl.pallas_call(
        functools.partial(_pool_kernel, tc=tc, start_pos=start_pos, carry=l > tc),
        grid=(b, l // tc),
        in_specs=[pl.BlockSpec((1, tc, GROUP_W), lambda i, j: (i, j, Z_POOL // GROUP_W)),
                  pl.BlockSpec((1, 16, GROUP_W), lambda i, j: (i, 0, 0)),
                  pl.BlockSpec((len(POOL_WINDOWS), POOL_CH, POOL_CH), lambda i, j: (0, 0, 0)),
                  pl.BlockSpec((1, GROUP_W), lambda i, j: (0, 0))],
        out_specs=pl.BlockSpec((1, tc, GROUP_W), lambda i, j: (i, j, 0)),
        out_shape=jax.ShapeDtypeStruct((b, l, GROUP_W), out_dtype),
        scratch_shapes=[pltpu.VMEM((n, GROUP_W), F32), pltpu.VMEM((n, GROUP_W), F32),
                        pltpu.VMEM((n, 3 * POOL_CH), F32), pltpu.VMEM((n, 2 * POOL_CH), F32)],
        compiler_params=_cparams(("parallel", "arbitrary")),
        name="pool",
    )(z3, hist16, pool_w, pool_scale)


def _pad_cols(a, n):
    return jnp.pad(a, ((0, 0),) * (a.ndim - 1) + ((0, n - a.shape[-1]),))


def _prep_layer(i, W):
    c1, c2, c3 = RWKV_COLS, RWKV_COLS + SGU_COLS, RWKV_COLS + SGU_COLS + HGRN_COLS
    w_in = W["w_in"][i]
    w_in_p = jnp.concatenate([w_in[:, c2:c3], w_in[:, c1:c2], w_in[:, c3:], _pad_cols(w_in[:, :c1], RWKV_PAD)],
                             axis=1).astype(BF16)
    row = lambda a: a.reshape(1, -1)
    lora = lambda a, off: jnp.pad(a, ((off, LORA_W - off - a.shape[0]), (0, 0))).astype(BF16)
    rw = dict(
        mu=_pad_cols(row(W["rwkv_mu"][i]), RWKV_PAD),
        lora_w=lora(W["rwkv_w_lora"][i], 0), w0=row(W["rwkv_w0"][i]),
        lora_a=lora(W["rwkv_a_lora"][i], W_LORA), a0=row(W["rwkv_a0"][i]),
        lora_g=lora(W["rwkv_g_lora"][i], W_LORA + A_LORA),
        k_k=row(W["rwkv_k_k"][i]), k_a=row(W["rwkv_k_a"][i]), r_k=row(W["rwkv_r_k"][i]),
        gn_w=row(W["rwkv_gn_w"][i]), gn_b=row(W["rwkv_gn_b"][i]))
    sg = dict(ln_w=row(W["sgu_ln_w"][i]), ln_b=row(W["sgu_ln_b"][i]), norm=row(W["sgu_norm"][i]))
    return dict(
        w_in=w_in_p, rw=rw, sg=sg, sgu_w=W["sgu_w"][i], sgu_b=W["sgu_b"][i],
        hgrn_norm=row(W["hgrn_norm"][i]), pool_w=W["pool_w"][i].astype(BF16), pool_scale=row(W["pool_scale"][i]),
        w_out=W["w_out"][i].astype(BF16), w_gu=W["ffn_w_gu"][i].astype(BF16), w_down=W["ffn_w_down"][i].astype(BF16),
        ple_gate=W["ple_gate"][i].astype(BF16), ple_proj=W["ple_proj"][i].astype(BF16),
        ln_mix_pre=row(W["ln_mix_pre"][i]), ln_mix_post=row(W["ln_mix_post"][i]),
        ln_ffn_pre=row(W["ln_ffn_pre"][i]), ln_ffn_post=row(W["ln_ffn_post"][i]))


def _to_pairs(s):
    b = s.shape[0]
    return s.reshape(b, RWKV_PAIRS, 2, RWKV_HEAD, RWKV_HEAD).transpose(0, 1, 3, 2, 4).reshape(b, RWKV_PAIRS, RWKV_HEAD, 128)


def _from_pairs(s):
    b = s.shape[0]
    return s.reshape(b, RWKV_PAIRS, RWKV_HEAD, 2, RWKV_HEAD).transpose(0, 1, 3, 2, 4).reshape(b, RWKV_HEADS, RWKV_HEAD, RWKV_HEAD)


def _trunk(x, p, wkv0, shift0, hgrn0, pool0, start_pos, layers, lb_logits, cfg):
    b, l, _ = x.shape
    t = b * l
    xf = x.reshape(t, D_MODEL)
    wkv_l, shift_l, hgrn_l, pool_l, sgu_l = [], [], [], [], []
    for i, lw in enumerate(layers):
        z2 = _inproj(xf, lw["ln_mix_pre"], lw["w_in"], cfg["tm"], cfg["tn_in"])
        z3 = z2.reshape(b, l, Z_COLS)
        shift_prev = _pad_cols(shift0[i], RWKV_PAD).reshape(b, 1, RWKV_PAD)
        y_r, wkv_new = _rwkv(z3, shift_prev, _to_pairs(wkv0[i]), lw["rw"], cfg["rwkv_chunked"], cfg["ydt"])
        if cfg["sgu_blk"] == SGU_CHUNK:
            wm, bias = lw["sgu_w"], lw["sgu_b"].T
        else:
            rep = SGU_CHUNK // l
            wm = jnp.tile(lw["sgu_w"][:, :l, :l], (1, rep, rep))
            bias = jnp.tile(lw["sgu_b"][:, :l].T, (rep, 1))
        y_s, v_rows = _sgu(z2, lw["sg"], wm, bias, cfg["sgu_blk"], cfg["ydt"])
        y_h, hgrn_new = _hgrn(z3, hgrn0[i], lb_logits, lw["hgrn_norm"], i, cfg["tc_hgrn"], cfg["hgrn_chunk"],
                              cfg["hgrn_sub"], cfg["ydt"])
        hist16 = jnp.pad(pool0[i], ((0, 0), (1, 0), (0, 0)))
        y_p = _pool(z3, hist16, lw["pool_w"], lw["pool_scale"], start_pos, cfg["tc_pool"], cfg["ydt"])
        ys = [y.reshape(t, GROUP_W) for y in (y_r, y_s.reshape(b, l, GROUP_W), y_h, y_p)]
        xf = _outproj(ys, xf, lw["w_out"], lw["ln_mix_post"], cfg["tm"])
        xf = _ffn(xf, lw["ln_ffn_pre"], lw["w_gu"], lw["w_down"], lw["ln_ffn_post"], cfg["tm"], cfg["tf"])
        xf = _ple(xf, p[i].reshape(t, PLE_DIM), lw["ple_gate"], lw["ple_proj"], cfg["tm"], cfg["tn_ple"])
        z_r = z3[:, :, Z_RWKV:Z_RWKV + RWKV_COLS]
        z_p = z3[:, :, Z_POOL:Z_POOL + GROUP_W]
        wkv_l.append(_from_pairs(wkv_new))
        shift_l.append(z_r[:, -1])
        hgrn_l.append(hgrn_new)
        pool_l.append(jnp.concatenate([pool0[i], z_p], axis=1)[:, -POOL_HIST:])
        sgu_l.append(v_rows.reshape(b, l, GROUP_W))
    return (xf.reshape(b, l, D_MODEL), jnp.stack(wkv_l), jnp.stack(shift_l), jnp.stack(hgrn_l),
            jnp.stack(pool_l), jnp.stack(sgu_l))


def _cfg(l):
    if l >= 256:
        return dict(tm=512, tn_in=1792, tf=512, tn_ple=512, rwkv_chunked=True, sgu_blk=SGU_CHUNK, tc_hgrn=256,
                    hgrn_chunk=HGRN_CHUNK, hgrn_sub=HGRN_SUB, tc_pool=256, ydt=BF16)
    return dict(tm=512, tn_in=1792, tf=512, tn_ple=512, rwkv_chunked=False, sgu_blk=l, tc_hgrn=l,
                hgrn_chunk=l, hgrn_sub=l, tc_pool=l, ydt=BF16)


def kernel(x_prompt, x_sample, state_rwkv_wkv, state_rwkv_shift, state_hgrn, state_pool, p_prompt, p_sample,
           ln_mix_pre, ln_mix_post, ln_ffn_pre, ln_ffn_post, w_in, rwkv_mu, rwkv_w_lora, rwkv_w0, rwkv_a_lora,
           rwkv_a0, rwkv_g_lora, rwkv_k_k, rwkv_k_a, rwkv_r_k, rwkv_gn_w, rwkv_gn_b, sgu_ln_w, sgu_ln_b, sgu_w,
           sgu_b, sgu_norm, hgrn_lb_logits, hgrn_norm, pool_w, pool_scale, w_out, ffn_w_gu, ffn_w_down, ple_gate,
           ple_proj):
    W = dict(ln_mix_pre=ln_mix_pre, ln_mix_post=ln_mix_post, ln_ffn_pre=ln_ffn_pre, ln_ffn_post=ln_ffn_post,
             w_in=w_in, rwkv_mu=rwkv_mu, rwkv_w_lora=rwkv_w_lora, rwkv_w0=rwkv_w0, rwkv_a_lora=rwkv_a_lora,
             rwkv_a0=rwkv_a0, rwkv_g_lora=rwkv_g_lora, rwkv_k_k=rwkv_k_k, rwkv_k_a=rwkv_k_a,
             rwkv_r_k=rwkv_r_k.reshape(DEPTH, GROUP_W), rwkv_gn_w=rwkv_gn_w, rwkv_gn_b=rwkv_gn_b, sgu_ln_w=sgu_ln_w,
             sgu_ln_b=sgu_ln_b, sgu_w=sgu_w, sgu_b=sgu_b, sgu_norm=sgu_norm, hgrn_norm=hgrn_norm, pool_w=pool_w,
             pool_scale=pool_scale, w_out=w_out, ffn_w_gu=ffn_w_gu, ffn_w_down=ffn_w_down, ple_gate=ple_gate,
             ple_proj=ple_proj)
    layers = [_prep_layer(i, W) for i in range(DEPTH)]
    lb_logits = hgrn_lb_logits.astype(F32)
    bp, lp, _ = x_prompt.shape
    zeros = lambda *s: jnp.zeros((DEPTH, bp) + s, F32)
    y_prompt, wkv_p, shift_p, hgrn_p, pool_p, _ = _trunk(
        x_prompt, p_prompt, zeros(RWKV_HEADS, RWKV_HEAD, RWKV_HEAD), zeros(RWKV_COLS),
        zeros(HGRN_HEADS, HGRN_HEAD, HGRN_HEAD), zeros(POOL_HIST, GROUP_W), 0, layers, lb_logits, _cfg(lp))
    y_sample, wkv_s, shift_s, hgrn_s, pool_s, sgu_v_s = _trunk(
        x_sample, p_sample, state_rwkv_wkv, state_rwkv_shift, state_hgrn, state_pool, PAST_LEN, layers, lb_logits,
        _cfg(x_sample.shape[1]))
    return (y_prompt, y_sample, wkv_p, shift_p, hgrn_p, pool_p, wkv_s, shift_s, hgrn_s, pool_s, sgu_v_s)
```

```python
import functools

import jax
import jax.numpy as jnp
from jax import lax
from jax.experimental import pallas as pl
from jax.experimental.pallas import tpu as pltpu

F32 = jnp.float32
BF16 = jnp.bfloat16

D_MODEL = 2048
DEPTH = 2
PAST_LEN = 16384
GROUP_W = 512
RWKV_HEAD = 64
RWKV_HEADS = 8
RWKV_PAIRS = RWKV_HEADS // 2
RWKV_CHUNK = 64
W_LORA, A_LORA, G_LORA = 32, 32, 96
RWKV_GN_EPS = 64e-5
SGU_CHUNK = 128
SGU_HEADS = 4
HGRN_HEADS = 4
HGRN_HEAD = 128
HGRN_CHUNK = 64
HGRN_SUB = 16
POOL_WINDOWS = (2, 4, 8, 16)
POOL_CH = 128
POOL_HIST = 15
POOL_HALO = 32
D_FF = 5632
PLE_DIM = 256
NORM_EPS = 1e-6
LN_EPS = 1e-5
RWKV_COLS = 3 * GROUP_W + W_LORA + A_LORA + G_LORA
RWKV_PAD = 1792
LORA_OFF = 3 * GROUP_W
LORA_W = RWKV_PAD - LORA_OFF
SGU_COLS = 2 * GROUP_W
HGRN_COLS = 4 * GROUP_W
Z_HGRN, Z_SGU, Z_POOL, Z_RWKV = 0, 2048, 3072, 3584
Z_COLS = Z_RWKV + RWKV_PAD
SHORT_SEQS = 8
VMEM_LIMIT = 56 * 1024 * 1024


def _cparams(sem):
    return pltpu.CompilerParams(dimension_semantics=sem, vmem_limit_bytes=VMEM_LIMIT)


def _const_spec(shape, index_map):
    return pl.BlockSpec(shape, index_map, pipeline_mode=pl.Buffered(1))


def _sigmoid(x):
    return jax.nn.sigmoid(x)


def _silu(x):
    return x * jax.nn.sigmoid(x)


def _split3(x):
    hi = x.astype(BF16)
    r = x - hi.astype(F32)
    mid = r.astype(BF16)
    lo = (r - mid.astype(F32)).astype(BF16)
    return hi, mid, lo


def _dot_f32_lhs(x, m):
    hi, mid, lo = _split3(x)
    d = functools.partial(jnp.dot, preferred_element_type=F32)
    return d(hi, m) + d(mid, m) + d(lo, m)


def _dot_f32_rhs(m, x):
    hi, mid, lo = _split3(x)
    d = functools.partial(jnp.dot, preferred_element_type=F32)
    return d(m, hi) + d(m, mid) + d(m, lo)


def _rms(x, g):
    return x * lax.rsqrt(jnp.mean(x * x, axis=-1, keepdims=True) + NORM_EPS) * g


def _call_stacked(kernel, inputs, in_specs, stack_out, prev_stack, **kw):
    if prev_stack is None:
        return pl.pallas_call(kernel, in_specs=in_specs, **kw)(*inputs)
    n_in = len(inputs)

    def body(*refs):
        kernel(*refs[:n_in], *refs[n_in + 1:])

    return pl.pallas_call(body, in_specs=list(in_specs) + [pl.BlockSpec(memory_space=pl.ANY)],
                          input_output_aliases={n_in: stack_out}, **kw)(*inputs, prev_stack)


def _inproj_kernel(x_ref, g_ref, w_ref, z_ref, h_ref):
    @pl.when(pl.program_id(1) == 0)
    def _():
        h_ref[...] = _rms(x_ref[...], g_ref[...]).astype(BF16)

    z_ref[...] = jnp.dot(h_ref[...], w_ref[...], preferred_element_type=F32)


def _inproj(x, g, w, layer, tm, tn):
    t = x.shape[0]
    return pl.pallas_call(
        _inproj_kernel,
        grid=(t // tm, Z_COLS // tn),
        in_specs=[pl.BlockSpec((tm, D_MODEL), lambda i, j: (i, 0)),
                  _const_spec((1, D_MODEL), lambda i, j: (0, 0)),
                  pl.BlockSpec((None, D_MODEL, tn), lambda i, j: (layer, 0, j))],
        out_specs=pl.BlockSpec((tm, tn), lambda i, j: (i, j)),
        out_shape=jax.ShapeDtypeStruct((t, Z_COLS), F32),
        scratch_shapes=[pltpu.VMEM((tm, D_MODEL), BF16)],
        compiler_params=_cparams(("parallel", "arbitrary")),
        name="inproj",
    )(x, g, w)


def _outproj_kernel(yr_ref, ys_ref, yh_ref, yp_ref, x_ref, w_ref, g_ref, o_ref):
    d = functools.partial(jnp.dot, preferred_element_type=F32)
    g4 = GROUP_W
    mix = (d(yr_ref[...], w_ref[0:g4, :]) + d(ys_ref[...], w_ref[g4:2 * g4, :])
           + d(yh_ref[...], w_ref[2 * g4:3 * g4, :]) + d(yp_ref[...], w_ref[3 * g4:4 * g4, :]))
    o_ref[...] = x_ref[...] + _rms(mix, g_ref[...])


def _outproj(ys, x, w, g, layer, tm):
    t = x.shape[0]
    yspec = pl.BlockSpec((tm, GROUP_W), lambda i: (i, 0))
    return pl.pallas_call(
        _outproj_kernel,
        grid=(t // tm,),
        in_specs=[yspec, yspec, yspec, yspec,
                  pl.BlockSpec((tm, D_MODEL), lambda i: (i, 0)),
                  _const_spec((None, D_MODEL, D_MODEL), lambda i: (layer, 0, 0)),
                  _const_spec((1, D_MODEL), lambda i: (0, 0))],
        out_specs=pl.BlockSpec((tm, D_MODEL), lambda i: (i, 0)),
        out_shape=jax.ShapeDtypeStruct((t, D_MODEL), F32),
        compiler_params=_cparams(("parallel",)),
        name="outproj",
    )(*ys, x, w, g)


def _ffn_kernel(x_ref, gpre_ref, wg_ref, wu_ref, wd_ref, gpost_ref, p_ref, wgate_ref, wproj_ref, o_ref, h_ref, acc_ref):
    j = pl.program_id(1)

    @pl.when(j == 0)
    def _():
        h_ref[...] = _rms(x_ref[...], gpre_ref[...]).astype(BF16)
        acc_ref[...] = jnp.zeros_like(acc_ref)

    h = h_ref[...]
    gate = jnp.dot(h, wg_ref[...], preferred_element_type=F32)
    up = jnp.dot(h, wu_ref[...], preferred_element_type=F32)
    act = (_silu(gate) * up).astype(BF16)
    acc_ref[...] += jnp.dot(act, wd_ref[...], preferred_element_type=F32)

    @pl.when(j == pl.num_programs(1) - 1)
    def _():
        xf = x_ref[...] + _rms(acc_ref[...], gpost_ref[...])
        pgate = _sigmoid(jnp.dot(xf.astype(BF16), wgate_ref[...], preferred_element_type=F32))
        ple = jnp.dot(p_ref[...].astype(BF16), wproj_ref[...], preferred_element_type=F32)
        o_ref[...] = xf + pgate * ple


def _ffn(x, gpre, w_gu, w_down, gpost, p, w_gate, w_proj, layer, tm, tf):
    t = x.shape[0]
    nf = D_FF // tf
    return pl.pallas_call(
        _ffn_kernel,
        grid=(t // tm, nf),
        in_specs=[pl.BlockSpec((tm, D_MODEL), lambda i, j: (i, 0)),
                  _const_spec((1, D_MODEL), lambda i, j: (0, 0)),
                  pl.BlockSpec((None, D_MODEL, tf), lambda i, j: (layer, 0, j)),
                  pl.BlockSpec((None, D_MODEL, tf), lambda i, j: (layer, 0, j + nf)),
                  pl.BlockSpec((None, tf, D_MODEL), lambda i, j: (layer, j, 0)),
                  _const_spec((1, D_MODEL), lambda i, j: (0, 0)),
                  pl.BlockSpec((None, tm, PLE_DIM), lambda i, j: (layer, i, 0)),
                  _const_spec((None, D_MODEL, D_MODEL), lambda i, j: (layer, 0, 0)),
                  _const_spec((None, PLE_DIM, D_MODEL), lambda i, j: (layer, 0, 0))],
        out_specs=pl.BlockSpec((tm, D_MODEL), lambda i, j: (i, 0)),
        out_shape=jax.ShapeDtypeStruct((t, D_MODEL), F32),
        scratch_shapes=[pltpu.VMEM((tm, D_MODEL), BF16), pltpu.VMEM((tm, D_MODEL), F32)],
        compiler_params=_cparams(("parallel", "arbitrary")),
        name="ffn",
    )(x, gpre, w_gu, w_gu, w_down, gpost, p, w_gate, w_proj)


def _head_ones():
    r = lax.broadcasted_iota(jnp.int32, (GROUP_W, GROUP_W), 0) // RWKV_HEAD
    c = lax.broadcasted_iota(jnp.int32, (GROUP_W, GROUP_W), 1) // RWKV_HEAD
    return (r == c).astype(BF16)


def _rwkv_gates(z, z_prev, mu_ref, lw_ref, w0_ref, la_ref, a0_ref, lg_ref, kk_ref, ka_ref, rk_ref):
    zm = z + mu_ref[...] * (z_prev - z)
    g4 = GROUP_W
    r = zm[:, 0:g4]
    k = zm[:, g4:2 * g4]
    v = zm[:, 2 * g4:3 * g4]
    zl = zm[:, LORA_OFF:RWKV_PAD]
    d = functools.partial(jnp.dot, preferred_element_type=F32)
    wl = w0_ref[...] + d(jnp.tanh(zl).astype(BF16), lw_ref[...])
    w_log = -(jnp.maximum(-wl, 0.0) + jnp.log1p(jnp.exp(-jnp.abs(wl)))) - 0.5
    a = _sigmoid(a0_ref[...] + d(zl.astype(BF16), la_ref[...]))
    g = d(_sigmoid(zl).astype(BF16), lg_ref[...])
    ones = _head_ones()
    kk = k * kk_ref[...]
    kk = kk / jnp.maximum(jnp.sqrt(_dot_f32_lhs(kk * kk, ones)), 1e-12)
    k = k * (1.0 + (a - 1.0) * ka_ref[...])
    bonus = _dot_f32_lhs(r * k * rk_ref[...], ones) * v
    return r, k, v, -kk, kk * a, w_log, g, bonus


def _rwkv_post(y, bonus, g, gnw_ref, gnb_ref):
    ones = _head_ones()
    inv_n = 1.0 / RWKV_HEAD
    mean = _dot_f32_lhs(y, ones) * inv_n
    yc = y - mean
    var = _dot_f32_lhs(yc * yc, ones) * inv_n
    yn = yc * lax.rsqrt(var + RWKV_GN_EPS) * gnw_ref[...] + gnb_ref[...]
    return (yn + bonus) * g


def _rwkv_scan_kernel(z_ref, sh_ref, s0_ref, mu_ref, lw_ref, w0_ref, la_ref, a0_ref, lg_ref, kk_ref, ka_ref, rk_ref,
                      gnw_ref, gnb_ref, y_ref, sout_ref, zbuf, *, nseq, l):
    rows = nseq * l
    z = z_ref[...]
    zbuf[pl.ds(0, 8), :] = jnp.zeros((8, RWKV_PAD), F32)
    zbuf[pl.ds(8, rows), :] = z
    rid = lax.broadcasted_iota(jnp.int32, (rows, 1), 0)
    spread = ((lax.broadcasted_iota(jnp.int32, (rows, nseq), 0) // l)
              == lax.broadcasted_iota(jnp.int32, (rows, nseq), 1)).astype(BF16)
    z_prev = jnp.where(rid % l == 0, _dot_f32_rhs(spread, sh_ref[...]), zbuf[pl.ds(7, rows), :])
    r, k, v, a_vec, b_vec, w_log, g, bonus = _rwkv_gates(
        z, z_prev, mu_ref, lw_ref, w0_ref, la_ref, a0_ref, lg_ref, kk_ref, ka_ref, rk_ref)
    w = jnp.exp(-jnp.exp(w_log))

    shp = (RWKV_HEAD, 2 * RWKV_HEAD)
    lane = lax.broadcasted_iota(jnp.int32, shp, 1)
    sub = lax.broadcasted_iota(jnp.int32, shp, 0)
    lo = lane < RWKV_HEAD
    eye = ((lane % RWKV_HEAD) == sub).astype(F32)

    pair_ones = ((lax.broadcasted_iota(jnp.int32, (128, 128), 0) // RWKV_HEAD)
                 == (lax.broadcasted_iota(jnp.int32, (128, 128), 1) // RWKV_HEAD)).astype(BF16)

    def head_sums(tiles):
        x = jnp.concatenate(tiles, axis=0)
        hi = x.astype(BF16)
        mid = (x - hi.astype(F32)).astype(BF16)
        s = (jnp.dot(hi, pair_ones, preferred_element_type=F32)
             + jnp.dot(mid, pair_ones, preferred_element_type=F32))
        return [s[u * RWKV_HEAD:(u + 1) * RWKV_HEAD] for u in range(len(tiles))]

    units = [(b, p) for b in range(nseq) for p in range(RWKV_PAIRS)]
    row = lambda x, b, p, j: x[b * l + j:b * l + j + 1, p * 128:(p + 1) * 128]
    state = [jnp.concatenate([s0_ref[b, 2 * p], s0_ref[b, 2 * p + 1]], axis=1) for b, p in units]
    y_rows = {}
    for j in range(l):
        sa = head_sums([s * row(a_vec, b, p, j) for s, (b, p) in zip(state, units)])
        vcol = head_sums([eye * row(v, b, p, j) for b, p in units])
        state = [s * row(w, b, p, j) + sa_u * row(b_vec, b, p, j) + vc_u * row(k, b, p, j)
                 for s, sa_u, vc_u, (b, p) in zip(state, sa, vcol, units)]
        ycol = head_sums([s * row(r, b, p, j) for s, (b, p) in zip(state, units)])
        for yc_u, (b, p) in zip(ycol, units):
            y_rows[b * l + j, p] = jnp.sum(eye * yc_u, axis=0, keepdims=True)
    for s, (b, p) in zip(state, units):
        sout_ref[b, 2 * p] = s[:, 0:RWKV_HEAD]
        sout_ref[b, 2 * p + 1] = s[:, RWKV_HEAD:2 * RWKV_HEAD]
    y = jnp.concatenate([jnp.concatenate([y_rows[i, p] for i in range(rows)], axis=0) for p in range(RWKV_PAIRS)],
                        axis=1)
    y_ref[...] = _rwkv_post(y, bonus, g, gnw_ref, gnb_ref).astype(y_ref.dtype)


def _tri_inverse(l_mats, n):
    row = lax.broadcasted_iota(jnp.int32, (n, n), 0)
    col = lax.broadcasted_iota(jnp.int32, (n, n), 1)
    eye = (row == col).astype(F32)
    mm = lambda a, b: jnp.dot(a.astype(BF16), b.astype(BF16), preferred_element_type=F32)
    same = lambda m: (row // m) == (col // m)
    d1 = [jnp.where(same(8), x, 0.0) for x in l_mats]
    d2 = [mm(x, x) for x in d1]
    d4 = [mm(x, x) for x in d2]
    d3 = [mm(x, y) for x, y in zip(d1, d2)]
    t = [eye + x1 + x2 + x3 for x1, x2, x3 in zip(d1, d2, d3)]
    t = [x + mm(x, y) for x, y in zip(t, d4)]
    m = 8
    while m < RWKV_CHUNK:
        e = [jnp.where(same(2 * m) & jnp.logical_not(same(m)), x, 0.0) for x in l_mats]
        te = [mm(x, y) for x, y in zip(t, e)]
        t = [x + mm(y, x) for x, y in zip(t, te)]
        m *= 2
    return t


def _rwkv_chunk_local(chunks):
    c = RWKV_CHUNK
    n = 2 * c
    lo64 = lax.broadcasted_iota(jnp.int32, (c, 128), 1) < RWKV_HEAD
    tri = (lax.broadcasted_iota(jnp.int32, (c, c), 1) <= lax.broadcasted_iota(jnp.int32, (c, c), 0)).astype(BF16)
    stack = lambda x: jnp.concatenate([jnp.where(lo64, x, 0.0), jnp.where(lo64, 0.0, x)], axis=0)
    ops = []
    for r, k, v, a_vec, b_vec, logw in chunks:
        cum = _dot_f32_rhs(tri, logw)
        c_end = cum[c - 1:c]
        e_neg = jnp.exp(-cum)
        e_end = jnp.exp(c_end - cum)
        at = a_vec * jnp.exp(cum - logw)
        rt = r * jnp.exp(cum)
        bt = b_vec * e_neg
        kt = k * e_neg
        bh = b_vec * e_end
        kh = k * e_end
        e_cend = jnp.exp(c_end)
        for p in range(RWKV_PAIRS):
            sl = slice(p * 128, (p + 1) * 128)
            ops.append(dict(xa=stack(at[:, sl]), xr=stack(rt[:, sl]), yb=stack(bt[:, sl]), yk=stack(kt[:, sl]),
                            vs=stack(v[:, sl]).astype(BF16), xbh=stack(bh[:, sl]), xkh=stack(kh[:, sl]),
                            e_cend=e_cend[:, sl]))

    row = lax.broadcasted_iota(jnp.int32, (n, n), 0)
    col = lax.broadcasted_iota(jnp.int32, (n, n), 1)
    same_head = (row // c) == (col // c)
    strict = same_head & (col < row)
    incl = same_head & (col <= row)
    zeros = jnp.zeros((n, n), BF16)
    mm = lambda x, y: jnp.dot(x, y, preferred_element_type=F32)
    nt = lambda x, y: lax.dot_general(x, y, (((1,), (1,)), ((), ())), preferred_element_type=F32)
    tn = lambda x, y: lax.dot_general(x, y, (((0,), (0,)), ((), ())), preferred_element_type=F32)
    unstack = lambda x: x[0:c] + x[c:n]

    gram = [nt(jnp.concatenate([o["xa"], o["xr"]], axis=0).astype(BF16),
               jnp.concatenate([o["yb"], o["yk"]], axis=0).astype(BF16)) for o in ops]
    l_ab = [jnp.where(strict, x[0:n, 0:n], 0.0) for x in gram]
    l_ak = [jnp.where(strict, x[0:n, n:2 * n], 0.0) for x in gram]
    m_rbk = [jnp.concatenate([jnp.where(incl, x[n:2 * n, 0:n], 0.0), jnp.where(incl, x[n:2 * n, n:2 * n], 0.0)],
                             axis=1).astype(BF16) for x in gram]
    w1 = [mm(x.astype(BF16), o["vs"]) for x, o in zip(l_ak, ops)]
    t_inv = _tri_inverse(l_ab, n)
    tu = [mm(t.astype(BF16), jnp.concatenate([o["xa"], w], axis=1).astype(BF16))
          for t, o, w in zip(t_inv, ops, w1)]
    zmat = [jnp.concatenate([x.astype(BF16), jnp.concatenate([zeros, o["vs"]], axis=1)], axis=0)
            for x, o in zip(tu, ops)]
    ry = [mm(m, zm) for m, zm in zip(m_rbk, zmat)]
    ps = [tn(zm, jnp.concatenate([o["xbh"], o["xkh"]], axis=0).astype(BF16))
          for zm, o in zip(zmat, ops)]
    out = [(unstack(o["xr"] + y[:, 0:n]), unstack(y[:, n:2 * n]), o["e_cend"], q[0:n], q[n:2 * n])
           for o, y, q in zip(ops, ry, ps)]
    return [out[ci * RWKV_PAIRS:(ci + 1) * RWKV_PAIRS] for ci in range(len(chunks))]


def _rwkv_chunk_kernel(z_ref, sh_ref, s0_ref, mu_ref, lw_ref, w0_ref, la_ref, a0_ref, lg_ref, kk_ref, ka_ref, rk_ref,
                       gnw_ref, gnb_ref, y_ref, sout_ref, zbuf, s_scr, *, n_tiles):
    c = RWKV_CHUNK
    tile = 2 * c
    zero = jnp.zeros((RWKV_HEAD, RWKV_HEAD), F32)
    for p in range(RWKV_PAIRS):
        s_scr[p] = jnp.concatenate([jnp.concatenate([s0_ref[0, 2 * p], zero], axis=1),
                                    jnp.concatenate([zero, s0_ref[0, 2 * p + 1]], axis=1)], axis=0)
    zbuf[pl.ds(0, 8), :] = jnp.broadcast_to(sh_ref[0], (8, RWKV_PAD))
    nt = lambda x, y: lax.dot_general(x, y, (((1,), (1,)), ((), ())), preferred_element_type=F32)

    def tile_body(ti, carry):
        r0 = pl.multiple_of(ti * tile, tile)
        z = z_ref[0, pl.ds(r0, tile), :]
        zbuf[pl.ds(8, tile), :] = z
        z_prev = zbuf[pl.ds(7, tile), :]
        zbuf[pl.ds(7, 1), :] = zbuf[pl.ds(7 + tile, 1), :]
        r, k, v, a_vec, b_vec, w_log, g, bonus = _rwkv_gates(
            z, z_prev, mu_ref, lw_ref, w0_ref, la_ref, a0_ref, lg_ref, kk_ref, ka_ref, rk_ref)
        logw = -jnp.exp(w_log)
        local = _rwkv_chunk_local([tuple(x[ci * c:(ci + 1) * c] for x in (r, k, v, a_vec, b_vec, logw))
                                   for ci in range(2)])
        y_cols = []
        for p in range(RWKV_PAIRS):
            s = s_scr[p]
            ys = []
            for ci in range(2):
                rc, yloc, e_cend, trans, sloc = local[ci][p]
                sb = s.astype(BF16)
                ys.append(nt(rc.astype(BF16), sb) + yloc)
                s = s * e_cend + jnp.dot(sb, trans.astype(BF16), preferred_element_type=F32) + sloc
            s_scr[p] = s
            y_cols.append(jnp.concatenate(ys, axis=0))
        y = jnp.concatenate(y_cols, axis=1)
        y_ref[0, pl.ds(r0, tile), :] = _rwkv_post(y, bonus, g, gnw_ref, gnb_ref).astype(y_ref.dtype)
        return carry

    lax.fori_loop(0, n_tiles, tile_body, 0)
    for p in range(RWKV_PAIRS):
        sout_ref[0, 2 * p] = s_scr[p, 0:RWKV_HEAD, 0:RWKV_HEAD]
        sout_ref[0, 2 * p + 1] = s_scr[p, RWKV_HEAD:2 * RWKV_HEAD, RWKV_HEAD:2 * RWKV_HEAD]


def _rwkv(z2, shift_prev, wkv0, pw, layer, b, l, prev_stack, out_dtype):
    vec = lambda n: _const_spec((1, n), lambda i: (0, 0))
    mat = lambda r, c: _const_spec((r, c), lambda i: (0, 0))
    params = [pw["mu"], pw["lora_w"], pw["w0"], pw["lora_a"], pw["a0"], pw["lora_g"],
              pw["k_k"], pw["k_a"], pw["r_k"], pw["gn_w"], pw["gn_b"]]
    pspecs = [vec(RWKV_PAD), mat(LORA_W, GROUP_W), vec(GROUP_W), mat(LORA_W, GROUP_W), vec(GROUP_W),
              mat(LORA_W, GROUP_W), vec(GROUP_W), vec(GROUP_W), vec(GROUP_W), vec(GROUP_W), vec(GROUP_W)]
    if l % (2 * RWKV_CHUNK) == 0:
        nseq = 1
        z_in = z2.reshape(b, l, Z_COLS)
        shift_prev = shift_prev.reshape(b, 1, RWKV_PAD)
        shspec = pl.BlockSpec((1, 1, RWKV_PAD), lambda i: (i, 0, 0))
        body = functools.partial(_rwkv_chunk_kernel, n_tiles=l // (2 * RWKV_CHUNK))
        zspec = pl.BlockSpec((1, l, RWKV_PAD), lambda i: (i, 0, Z_RWKV // RWKV_PAD))
        yspec = pl.BlockSpec((1, l, GROUP_W), lambda i: (i, 0, 0))
        yshape = jax.ShapeDtypeStruct((b, l, GROUP_W), out_dtype)
        scratch = [pltpu.VMEM((2 * RWKV_CHUNK + 8, RWKV_PAD), F32), pltpu.VMEM((RWKV_PAIRS, 128, 128), F32)]
    else:
        nseq = SHORT_SEQS
        z_in = z2
        shspec = pl.BlockSpec((nseq, RWKV_PAD), lambda i: (i, 0))
        body = functools.partial(_rwkv_scan_kernel, nseq=nseq, l=l)
        zspec = pl.BlockSpec((nseq * l, RWKV_PAD), lambda i: (i, Z_RWKV // RWKV_PAD))
        yspec = pl.BlockSpec((nseq * l, GROUP_W), lambda i: (i, 0))
        yshape = jax.ShapeDtypeStruct((b * l, GROUP_W), out_dtype)
        scratch = [pltpu.VMEM((nseq * l + 8, RWKV_PAD), F32)]
    st = pl.BlockSpec((None, nseq, RWKV_HEADS, RWKV_HEAD, RWKV_HEAD), lambda i: (layer, i, 0, 0, 0))
    y, s_new = _call_stacked(
        body, [z_in, shift_prev, wkv0] + params,
        [zspec, shspec, st] + pspecs,
        1, prev_stack,
        grid=(b // nseq,),
        out_specs=[yspec, st],
        out_shape=[yshape, jax.ShapeDtypeStruct((DEPTH, b, RWKV_HEADS, RWKV_HEAD, RWKV_HEAD), F32)],
        scratch_shapes=scratch,
        compiler_params=_cparams(("parallel",)),
        name="rwkv7")
    return y.reshape(b * l, GROUP_W), s_new


def _sgu_kernel(z_ref, lnw_ref, lnb_ref, wm_ref, bias_ref, nrm_ref, y_ref, v_ref, *, blk):
    z = z_ref[...]
    zg = 0.5 * z * (1.0 + lax.erf(z * 0.7071067811865476))
    u = zg[:, :GROUP_W]
    v = zg[:, GROUP_W:]
    mu = jnp.mean(v, axis=-1, keepdims=True)
    vc = v - mu
    var = jnp.mean(vc * vc, axis=-1, keepdims=True)
    vn = vc * lax.rsqrt(var + LN_EPS) * lnw_ref[...] + lnb_ref[...]
    v_ref[...] = vn
    row = lax.broadcasted_iota(jnp.int32, (SGU_CHUNK, SGU_CHUNK), 0)
    col = lax.broadcasted_iota(jnp.int32, (SGU_CHUNK, SGU_CHUNK), 1)
    mask = col <= row
    if blk < SGU_CHUNK:
        mask = mask & ((row // blk) == (col // blk))
    vb = vn.astype(BF16)
    parts = []
    for h in range(SGU_HEADS):
        sl = slice(h * 128, (h + 1) * 128)
        wm = jnp.where(mask, wm_ref[h], 0.0).astype(BF16)
        parts.append(jnp.dot(wm, vb[:, sl], preferred_element_type=F32) + bias_ref[:, h:h + 1])
    s = jnp.concatenate(parts, axis=1)
    y_ref[...] = _rms(u * s, nrm_ref[...]).astype(y_ref.dtype)


def _sgu(z2, pw, wm, bias, blk, out_dtype):
    t = z2.shape[0]
    vec = _const_spec((1, GROUP_W), lambda i: (0, 0))
    return pl.pallas_call(
        functools.partial(_sgu_kernel, blk=blk),
        grid=(t // SGU_CHUNK,),
        in_specs=[pl.BlockSpec((SGU_CHUNK, SGU_COLS), lambda i: (i, Z_SGU // SGU_COLS)),
                  vec, vec,
                  _const_spec((SGU_HEADS, SGU_CHUNK, SGU_CHUNK), lambda i: (0, 0, 0)),
                  _const_spec((SGU_CHUNK, SGU_HEADS), lambda i: (0, 0)),
                  vec],
        out_specs=[pl.BlockSpec((SGU_CHUNK, GROUP_W), lambda i: (i, 0)),
                   pl.BlockSpec((SGU_CHUNK, GROUP_W), lambda i: (i, 0))],
        out_shape=[jax.ShapeDtypeStruct((t, GROUP_W), out_dtype), jax.ShapeDtypeStruct((t, GROUP_W), F32)],
        compiler_params=_cparams(("parallel",)),
        name="sgu",
    )(z2, pw["ln_w"], pw["ln_b"], wm, bias, pw["norm"])


def _hgrn_chunk(zq, zf, zi, zg, s, lb, nw, chunk, sub):
    tri = (lax.broadcasted_iota(jnp.int32, (chunk, chunk), 1)
           <= lax.broadcasted_iota(jnp.int32, (chunk, chunk), 0)).astype(BF16)
    eye = (lax.broadcasted_iota(jnp.int32, (HGRN_HEAD, HGRN_HEAD), 0)
           == lax.broadcasted_iota(jnp.int32, (HGRN_HEAD, HGRN_HEAD), 1)).astype(F32)
    rowid = lax.broadcasted_iota(jnp.int32, (sub, 1), 0)
    d = functools.partial(jnp.dot, preferred_element_type=F32)
    q = _silu(zq)
    fg = lb + (1.0 - lb) * _sigmoid(zf)
    logf = jnp.log(fg)
    k = 1.0 - fg
    v = zi
    vb = v.astype(BF16)
    b = _dot_f32_rhs(tri, logf)
    o = d((q * jnp.exp(b)).astype(BF16), s.astype(BF16))
    outs = []
    for i in range(chunk // sub):
        lo_, hi_ = i * sub, (i + 1) * sub
        qi, bi, oi = q[lo_:hi_], b[lo_:hi_], o[lo_:hi_]
        if i > 0:
            bref = b[lo_ - 1:lo_]
            qt = (qi * jnp.exp(bi - bref)).astype(BF16)
            kt = (k[:lo_] * jnp.exp(bref - b[:lo_])).astype(BF16)
            att = lax.dot_general(qt, kt, (((1,), (1,)), ((), ())), preferred_element_type=F32)
            oi = oi + d(att.astype(BF16), vb[:lo_])
        for j in range(sub):
            sj = lo_ + j
            dec = jnp.exp(jnp.minimum(bi - b[sj:sj + 1], 0.0))
            att_j = jnp.sum(qi * dec * k[sj:sj + 1], axis=-1, keepdims=True)
            oi = oi + jnp.where(rowid >= j, att_j, 0.0) * v[sj:sj + 1]
        outs.append(oi)
    o = outs[0] if len(outs) == 1 else jnp.concatenate(outs, axis=0)
    blast = b[chunk - 1:chunk]
    kd = (k * jnp.exp(blast - b)).astype(BF16)
    upd = lax.dot_general(kd, vb, (((0,), (0,)), ((), ())), preferred_element_type=F32)
    e_col = jnp.sum(eye * jnp.exp(blast), axis=1, keepdims=True)
    s_new = e_col * s + upd
    on = o * lax.rsqrt(jnp.mean(o * o, axis=-1, keepdims=True) + NORM_EPS)
    return on * nw * _silu(zg), s_new


def _hgrn_kernel(zq_ref, zf_ref, zi_ref, zg_ref, lg_ref, nw_ref, s0_ref, y_ref, sout_ref, s_scr,
                 *, layer, nseq, chunk, sub, n_chunks):
    lg = lg_ref[...]
    e = jnp.exp(lg - jnp.max(lg, axis=0, keepdims=True))
    sm = e / jnp.sum(e, axis=0, keepdims=True)
    lb = jnp.sum(sm[0:layer + 1], axis=0, keepdims=True) - sm[0:1]
    nw = nw_ref[...]
    for b in range(nseq):
        if n_chunks == 1:
            y, s_new = _hgrn_chunk(zq_ref[b], zf_ref[b], zi_ref[b], zg_ref[b], s0_ref[b, 0], lb, nw, chunk, sub)
            y_ref[b] = y.astype(y_ref.dtype)
            sout_ref[b, 0] = s_new
        else:
            s_scr[...] = s0_ref[b, 0]

            def chunk_body(c, carry):
                r0 = pl.multiple_of(c * chunk, chunk)
                rows = lambda ref: ref[b, pl.ds(r0, chunk), :]
                y, s_new = _hgrn_chunk(rows(zq_ref), rows(zf_ref), rows(zi_ref), rows(zg_ref), s_scr[...], lb, nw,
                                       chunk, sub)
                y_ref[b, pl.ds(r0, chunk), :] = y.astype(y_ref.dtype)
                s_scr[...] = s_new
                return carry

            lax.fori_loop(0, n_chunks, chunk_body, 0)
            sout_ref[b, 0] = s_scr[...]


def _hgrn(z3, s0, lb_logits, norm_w, layer, prev_stack, out_dtype):
    b, l, _ = z3.shape
    chunk = min(HGRN_CHUNK, l)
    sub = min(HGRN_SUB, chunk)
    nseq = 1 if l > chunk else SHORT_SEQS
    hb = Z_HGRN // HGRN_HEAD
    zspec = lambda part: pl.BlockSpec((nseq, l, HGRN_HEAD), lambda i, h: (i, 0, hb + part * HGRN_HEADS + h))
    st = pl.BlockSpec((None, nseq, 1, HGRN_HEAD, HGRN_HEAD), lambda i, h: (layer, i, h, 0, 0))
    return _call_stacked(
        functools.partial(_hgrn_kernel, layer=layer, nseq=nseq, chunk=chunk, sub=sub, n_chunks=l // chunk),
        [z3, z3, z3, z3, lb_logits, norm_w, s0],
        [zspec(0), zspec(1), zspec(2), zspec(3),
         pl.BlockSpec((DEPTH, HGRN_HEAD), lambda i, h: (0, h)),
         pl.BlockSpec((1, HGRN_HEAD), lambda i, h: (0, h)),
         st],
        1, prev_stack,
        grid=(b // nseq, HGRN_HEADS),
        out_specs=[pl.BlockSpec((nseq, l, HGRN_HEAD), lambda i, h: (i, 0, h)), st],
        out_shape=[jax.ShapeDtypeStruct((b, l, GROUP_W), out_dtype),
                   jax.ShapeDtypeStruct((DEPTH, b, HGRN_HEADS, HGRN_HEAD, HGRN_HEAD), F32)],
        scratch_shapes=[pltpu.VMEM((HGRN_HEAD, HGRN_HEAD), F32)],
        compiler_params=_cparams(("parallel", "parallel")),
        name="hgrn2")


def _pool_kernel(z_ref, hist_ref, pw_ref, ps_ref, y_ref, xbuf, s2buf, s4buf, s8buf, *, tc, start_pos, carry):
    tb = pl.program_id(1)
    halo = POOL_HALO
    n = tc + halo

    @pl.when(tb == 0)
    def _():
        xbuf[pl.ds(0, 16), :] = jnp.zeros((16, GROUP_W), F32)
        xbuf[pl.ds(16, 16), :] = hist_ref[0]

    z = z_ref[0]
    xbuf[pl.ds(halo, tc), :] = z
    c1 = POOL_CH
    s2buf[pl.ds(8, n - 8), :] = xbuf[pl.ds(8, n - 8), :] + xbuf[pl.ds(7, n - 8), :]
    s4buf[pl.ds(16, n - 16), :] = s2buf[pl.ds(16, n - 16), c1:] + s2buf[pl.ds(14, n - 16), c1:]
    s8buf[pl.ds(24, n - 24), :] = s4buf[pl.ds(24, n - 24), c1:] + s4buf[pl.ds(20, n - 24), c1:]
    s16 = s8buf[pl.ds(halo, tc), c1:] + s8buf[pl.ds(halo - 8, tc), c1:]
    sums = (s2buf[pl.ds(halo, tc), 0:c1], s4buf[pl.ds(halo, tc), 0:c1], s8buf[pl.ds(halo, tc), 0:c1], s16)
    pos = start_pos + tb * tc + lax.broadcasted_iota(jnp.int32, (tc, 1), 0)
    for gi, win in enumerate(POOL_WINDOWS):
        sl = slice(gi * POOL_CH, (gi + 1) * POOL_CH)
        cnt = jnp.minimum(pos + 1, win).astype(F32)
        dlt = sums[gi] / cnt - z[:, sl]
        y = jnp.dot(dlt.astype(BF16), pw_ref[gi], preferred_element_type=F32) * ps_ref[:, sl]
        y_ref[0, :, sl] = y.astype(y_ref.dtype)
    if carry:
        xbuf[pl.ds(16, 16), :] = xbuf[pl.ds(n - 16, 16), :]


def _pool(z3, hist16, pool_w, pool_scale, start_pos, tc, out_dtype):
    b, l, _ = z3.shape
    n = tc + POOL_HALO
    return pl.pallas_call(
        functools.partial(_pool_kernel, tc=tc, start_pos=start_pos, carry=l > tc),
        grid=(b, l // tc),
        in_specs=[pl.BlockSpec((1, tc, GROUP_W), lambda i, j: (i, j, Z_POOL // GROUP_W)),
                  pl.BlockSpec((1, 16, GROUP_W), lambda i, j: (i, 0, 0)),
                  _const_spec((len(POOL_WINDOWS), POOL_CH, POOL_CH), lambda i, j: (0, 0, 0)),
                  _const_spec((1, GROUP_W), lambda i, j: (0, 0))],
        out_specs=pl.BlockSpec((1, tc, GROUP_W), lambda i, j: (i, j, 0)),
        out_shape=jax.ShapeDtypeStruct((b, l, GROUP_W), out_dtype),
        scratch_shapes=[pltpu.VMEM((n, GROUP_W), F32), pltpu.VMEM((n, GROUP_W), F32),
                        pltpu.VMEM((n, 3 * POOL_CH), F32), pltpu.VMEM((n, 2 * POOL_CH), F32)],
        compiler_params=_cparams(("parallel", "arbitrary")),
        name="pool",
    )(z3, hist16, pool_w, pool_scale)


def _pad_cols(a, n):
    return jnp.pad(a, ((0, 0),) * (a.ndim - 1) + ((0, n - a.shape[-1]),))


def _prep_weights(W):
    c1, c2, c3 = RWKV_COLS, RWKV_COLS + SGU_COLS, RWKV_COLS + SGU_COLS + HGRN_COLS
    w_in = W["w_in"]
    w_in_p = jnp.concatenate(
        [w_in[:, :, c2:c3], w_in[:, :, c1:c2], w_in[:, :, c3:], _pad_cols(w_in[:, :, :c1], RWKV_PAD)],
        axis=2).astype(BF16)
    return dict(w_in=w_in_p, w_out=W["w_out"].astype(BF16), w_gu=W["ffn_w_gu"].astype(BF16),
                w_down=W["ffn_w_down"].astype(BF16), ple_gate=W["ple_gate"].astype(BF16),
                ple_proj=W["ple_proj"].astype(BF16))


def _prep_layer(i, W):
    row = lambda a: a.reshape(1, -1)
    lora = lambda a, off: jnp.pad(a, ((off, LORA_W - off - a.shape[0]), (0, 0))).astype(BF16)
    rw = dict(
        mu=_pad_cols(row(W["rwkv_mu"][i]), RWKV_PAD),
        lora_w=lora(W["rwkv_w_lora"][i], 0), w0=row(W["rwkv_w0"][i]),
        lora_a=lora(W["rwkv_a_lora"][i], W_LORA), a0=row(W["rwkv_a0"][i]),
        lora_g=lora(W["rwkv_g_lora"][i], W_LORA + A_LORA),
        k_k=row(W["rwkv_k_k"][i]), k_a=row(W["rwkv_k_a"][i]), r_k=row(W["rwkv_r_k"][i]),
        gn_w=row(W["rwkv_gn_w"][i]), gn_b=row(W["rwkv_gn_b"][i]))
    sg = dict(ln_w=row(W["sgu_ln_w"][i]), ln_b=row(W["sgu_ln_b"][i]), norm=row(W["sgu_norm"][i]))
    return dict(
        rw=rw, sg=sg, sgu_w=W["sgu_w"][i], sgu_b=W["sgu_b"][i],
        hgrn_norm=row(W["hgrn_norm"][i]), pool_w=W["pool_w"][i].astype(BF16), pool_scale=row(W["pool_scale"][i]),
        ln_mix_pre=row(W["ln_mix_pre"][i]), ln_mix_post=row(W["ln_mix_post"][i]),
        ln_ffn_pre=row(W["ln_ffn_pre"][i]), ln_ffn_post=row(W["ln_ffn_post"][i]))


def _trunk(x, p, wkv0, shift0, hgrn0, pool0, start_pos, big, layers, lb_logits, cfg):
    b, l, _ = x.shape
    t = b * l
    xf = x.reshape(t, D_MODEL)
    p2 = p.reshape(DEPTH, t, PLE_DIM)
    wkv_new = hgrn_new = None
    shift_l, pool_l, sgu_l = [], [], []
    for i, lw in enumerate(layers):
        z2 = _inproj(xf, lw["ln_mix_pre"], big["w_in"], i, cfg["tm"], cfg["tn_in"])
        z3 = z2.reshape(b, l, Z_COLS)
        y_r, wkv_new = _rwkv(z2, _pad_cols(shift0[i], RWKV_PAD), wkv0, lw["rw"], i, b, l, wkv_new, cfg["ydt"])
        if cfg["sgu_blk"] == SGU_CHUNK:
            wm, bias = lw["sgu_w"], lw["sgu_b"].T
        else:
            rep = SGU_CHUNK // l
            wm = jnp.tile(lw["sgu_w"][:, :l, :l], (1, rep, rep))
            bias = jnp.tile(lw["sgu_b"][:, :l].T, (rep, 1))
        y_s, v_rows = _sgu(z2, lw["sg"], wm, bias, cfg["sgu_blk"], cfg["ydt"])
        y_h, hgrn_new = _hgrn(z3, hgrn0, lb_logits, lw["hgrn_norm"], i, hgrn_new, cfg["ydt"])
        hist16 = jnp.pad(pool0[i], ((0, 0), (1, 0), (0, 0)))
        y_p = _pool(z3, hist16, lw["pool_w"], lw["pool_scale"], start_pos, cfg["tc_pool"], cfg["ydt"])
        ys = [y_r, y_s, y_h.reshape(t, GROUP_W), y_p.reshape(t, GROUP_W)]
        xf = _outproj(ys, xf, big["w_out"], lw["ln_mix_post"], i, cfg["tm"])
        xf = _ffn(xf, lw["ln_ffn_pre"], big["w_gu"], big["w_down"], lw["ln_ffn_post"], p2, big["ple_gate"],
                  big["ple_proj"], i, cfg["tm"], cfg["tf"])
        shift_l.append(z3[:, -1, Z_RWKV:Z_RWKV + RWKV_COLS])
        pool_l.append(jnp.concatenate([pool0[i], z3[:, :, Z_POOL:Z_POOL + GROUP_W]], axis=1)[:, -POOL_HIST:])
        sgu_l.append(v_rows.reshape(b, l, GROUP_W))
    return (xf.reshape(b, l, D_MODEL), wkv_new, jnp.stack(shift_l), hgrn_new, jnp.stack(pool_l), jnp.stack(sgu_l))


def _cfg(l):
    if l >= 256:
        return dict(tm=512, tn_in=1792, tf=512, sgu_blk=SGU_CHUNK, tc_pool=256, ydt=BF16)
    return dict(tm=512, tn_in=1792, tf=512, sgu_blk=l, tc_pool=l, ydt=BF16)


def kernel(x_prompt, x_sample, state_rwkv_wkv, state_rwkv_shift, state_hgrn, state_pool, p_prompt, p_sample,
           ln_mix_pre, ln_mix_post, ln_ffn_pre, ln_ffn_post, w_in, rwkv_mu, rwkv_w_lora, rwkv_w0, rwkv_a_lora,
           rwkv_a0, rwkv_g_lora, rwkv_k_k, rwkv_k_a, rwkv_r_k, rwkv_gn_w, rwkv_gn_b, sgu_ln_w, sgu_ln_b, sgu_w,
           sgu_b, sgu_norm, hgrn_lb_logits, hgrn_norm, pool_w, pool_scale, w_out, ffn_w_gu, ffn_w_down, ple_gate,
           ple_proj):
    W = dict(ln_mix_pre=ln_mix_pre, ln_mix_post=ln_mix_post, ln_ffn_pre=ln_ffn_pre, ln_ffn_post=ln_ffn_post,
             w_in=w_in, rwkv_mu=rwkv_mu, rwkv_w_lora=rwkv_w_lora, rwkv_w0=rwkv_w0, rwkv_a_lora=rwkv_a_lora,
             rwkv_a0=rwkv_a0, rwkv_g_lora=rwkv_g_lora, rwkv_k_k=rwkv_k_k, rwkv_k_a=rwkv_k_a,
             rwkv_r_k=rwkv_r_k.reshape(DEPTH, GROUP_W), rwkv_gn_w=rwkv_gn_w, rwkv_gn_b=rwkv_gn_b, sgu_ln_w=sgu_ln_w,
             sgu_ln_b=sgu_ln_b, sgu_w=sgu_w, sgu_b=sgu_b, sgu_norm=sgu_norm, hgrn_norm=hgrn_norm, pool_w=pool_w,
             pool_scale=pool_scale, w_out=w_out, ffn_w_gu=ffn_w_gu, ffn_w_down=ffn_w_down, ple_gate=ple_gate,
             ple_proj=ple_proj)
    big = _prep_weights(W)
    layers = [_prep_layer(i, W) for i in range(DEPTH)]
    lb_logits = hgrn_lb_logits.astype(F32)
    bp, lp, _ = x_prompt.shape
    zeros = lambda *s: jnp.zeros((DEPTH, bp) + s, F32)
    y_prompt, wkv_p, shift_p, hgrn_p, pool_p, _ = _trunk(
        x_prompt, p_prompt, zeros(RWKV_HEADS, RWKV_HEAD, RWKV_HEAD), zeros(RWKV_COLS),
        zeros(HGRN_HEADS, HGRN_HEAD, HGRN_HEAD), zeros(POOL_HIST, GROUP_W), 0, big, layers, lb_logits, _cfg(lp))
    y_sample, wkv_s, shift_s, hgrn_s, pool_s, sgu_v_s = _trunk(
        x_sample, p_sample, state_rwkv_wkv, state_rwkv_shift, state_hgrn, state_pool, PAST_LEN, big, layers,
        lb_logits, _cfg(x_sample.shape[1]))
    return (y_prompt, y_sample, wkv_p, shift_p, hgrn_p, pool_p, wkv_s, shift_s, hgrn_s, pool_s, sgu_v_s)
```

```python
import functools

import jax
import jax.numpy as jnp
from jax import lax
from jax.experimental import pallas as pl
from jax.experimental.pallas import tpu as pltpu

F32 = jnp.float32
BF16 = jnp.bfloat16

D_MODEL = 2048
DEPTH = 2
PAST_LEN = 16384
GROUP_W = 512
RWKV_HEAD = 64
RWKV_HEADS = 8
RWKV_PAIRS = RWKV_HEADS // 2
RWKV_CHUNK = 64
W_LORA, A_LORA, G_LORA = 32, 32, 96
RWKV_GN_EPS = 64e-5
SGU_CHUNK = 128
SGU_HEADS = 4
SGU_TILE_CHUNKS = 4
HGRN_HEADS = 4
HGRN_HEAD = 128
HGRN_CHUNK = 64
HGRN_SUB = 16
POOL_WINDOWS = (2, 4, 8, 16)
POOL_CH = 128
POOL_HIST = 15
POOL_HALO = 32
D_FF = 5632
PLE_DIM = 256
NORM_EPS = 1e-6
LN_EPS = 1e-5
RWKV_COLS = 3 * GROUP_W + W_LORA + A_LORA + G_LORA
RWKV_PAD = 1792
LORA_OFF = 3 * GROUP_W
LORA_W = RWKV_PAD - LORA_OFF
SGU_COLS = 2 * GROUP_W
HGRN_COLS = 4 * GROUP_W
Z_HGRN, Z_SGU, Z_POOL, Z_RWKV = 0, 2048, 3072, 3584
Z_COLS = Z_RWKV + RWKV_PAD
SHORT_SEQS = 8
VMEM_LIMIT = 56 * 1024 * 1024


def _cparams(sem):
    return pltpu.CompilerParams(dimension_semantics=sem, vmem_limit_bytes=VMEM_LIMIT)


def _const_spec(shape, index_map):
    return pl.BlockSpec(shape, index_map, pipeline_mode=pl.Buffered(1))


def _sigmoid(x):
    return jax.nn.sigmoid(x)


def _silu(x):
    return x * jax.nn.sigmoid(x)


def _split3(x):
    hi = x.astype(BF16)
    r = x - hi.astype(F32)
    mid = r.astype(BF16)
    lo = (r - mid.astype(F32)).astype(BF16)
    return hi, mid, lo


def _group_sums(x, width, terms=3):
    rows, cols = x.shape
    nblk = cols // 128
    xs = x if nblk == 1 else jnp.concatenate([x[:, i * 128:(i + 1) * 128] for i in range(nblk)], axis=0)
    ones = ((lax.broadcasted_iota(jnp.int32, (128, 128), 0) // width)
            == (lax.broadcasted_iota(jnp.int32, (128, 128), 1) // width)).astype(BF16)
    s = None
    for part in _split3(xs)[:terms]:
        y = jnp.dot(part, ones, preferred_element_type=F32)
        s = y if s is None else s + y
    return s if nblk == 1 else jnp.concatenate([s[i * rows:(i + 1) * rows] for i in range(nblk)], axis=1)


def _dot_f32_rhs(m, x):
    hi, mid, lo = _split3(x)
    d = functools.partial(jnp.dot, preferred_element_type=F32)
    return d(m, hi) + d(m, mid) + d(m, lo)


def _rms(x, g):
    return x * lax.rsqrt(jnp.mean(x * x, axis=-1, keepdims=True) + NORM_EPS) * g


def _call_stacked(kernel, inputs, in_specs, stack_out, prev_stack, **kw):
    if prev_stack is None:
        return pl.pallas_call(kernel, in_specs=in_specs, **kw)(*inputs)
    n_in = len(inputs)

    def body(*refs):
        kernel(*refs[:n_in], *refs[n_in + 1:])

    return pl.pallas_call(body, in_specs=list(in_specs) + [pl.BlockSpec(memory_space=pl.ANY)],
                          input_output_aliases={n_in: stack_out}, **kw)(*inputs, prev_stack)


def _inproj_kernel(x_ref, g_ref, w_ref, z_ref, h_ref):
    @pl.when(pl.program_id(1) == 0)
    def _():
        h_ref[...] = _rms(x_ref[...], g_ref[...]).astype(BF16)

    z_ref[...] = jnp.dot(h_ref[...], w_ref[...], preferred_element_type=F32)


def _inproj(x, g, w, layer, tm, tn):
    t = x.shape[0]
    return pl.pallas_call(
        _inproj_kernel,
        grid=(t // tm, Z_COLS // tn),
        in_specs=[pl.BlockSpec((tm, D_MODEL), lambda i, j: (i, 0)),
                  _const_spec((1, D_MODEL), lambda i, j: (0, 0)),
                  pl.BlockSpec((None, D_MODEL, tn), lambda i, j: (layer, 0, j))],
        out_specs=pl.BlockSpec((tm, tn), lambda i, j: (i, j)),
        out_shape=jax.ShapeDtypeStruct((t, Z_COLS), F32),
        scratch_shapes=[pltpu.VMEM((tm, D_MODEL), BF16)],
        compiler_params=_cparams(("parallel", "arbitrary")),
        name="inproj",
    )(x, g, w)


def _outproj_kernel(yr_ref, ys_ref, yh_ref, yp_ref, x_ref, w_ref, g_ref, o_ref):
    d = functools.partial(jnp.dot, preferred_element_type=F32)
    g4 = GROUP_W
    mix = (d(yr_ref[...], w_ref[0:g4, :]) + d(ys_ref[...], w_ref[g4:2 * g4, :])
           + d(yh_ref[...], w_ref[2 * g4:3 * g4, :]) + d(yp_ref[...], w_ref[3 * g4:4 * g4, :]))
    o_ref[...] = x_ref[...] + _rms(mix, g_ref[...])


def _outproj(ys, x, w, g, layer, tm):
    t = x.shape[0]
    yspec = pl.BlockSpec((tm, GROUP_W), lambda i: (i, 0))
    return pl.pallas_call(
        _outproj_kernel,
        grid=(t // tm,),
        in_specs=[yspec, yspec, yspec, yspec,
                  pl.BlockSpec((tm, D_MODEL), lambda i: (i, 0)),
                  _const_spec((None, D_MODEL, D_MODEL), lambda i: (layer, 0, 0)),
                  _const_spec((1, D_MODEL), lambda i: (0, 0))],
        out_specs=pl.BlockSpec((tm, D_MODEL), lambda i: (i, 0)),
        out_shape=jax.ShapeDtypeStruct((t, D_MODEL), F32),
        compiler_params=_cparams(("parallel",)),
        name="outproj",
    )(*ys, x, w, g)


def _ffn_kernel(x_ref, gpre_ref, wg_ref, wu_ref, wd_ref, gpost_ref, p_ref, wgate_ref, wproj_ref, o_ref, h_ref, acc_ref):
    j = pl.program_id(1)

    @pl.when(j == 0)
    def _():
        h_ref[...] = _rms(x_ref[...], gpre_ref[...]).astype(BF16)
        acc_ref[...] = jnp.zeros_like(acc_ref)

    h = h_ref[...]
    gate = jnp.dot(h, wg_ref[...], preferred_element_type=F32)
    up = jnp.dot(h, wu_ref[...], preferred_element_type=F32)
    act = (_silu(gate) * up).astype(BF16)
    acc_ref[...] += jnp.dot(act, wd_ref[...], preferred_element_type=F32)

    @pl.when(j == pl.num_programs(1) - 1)
    def _():
        xf = x_ref[...] + _rms(acc_ref[...], gpost_ref[...])
        pgate = _sigmoid(jnp.dot(xf.astype(BF16), wgate_ref[...], preferred_element_type=F32))
        ple = jnp.dot(p_ref[...].astype(BF16), wproj_ref[...], preferred_element_type=F32)
        o_ref[...] = xf + pgate * ple


def _ffn(x, gpre, w_gu, w_down, gpost, p, w_gate, w_proj, layer, tm, tf):
    t = x.shape[0]
    nf = D_FF // tf
    return pl.pallas_call(
        _ffn_kernel,
        grid=(t // tm, nf),
        in_specs=[pl.BlockSpec((tm, D_MODEL), lambda i, j: (i, 0)),
                  _const_spec((1, D_MODEL), lambda i, j: (0, 0)),
                  pl.BlockSpec((None, D_MODEL, tf), lambda i, j: (layer, 0, j)),
                  pl.BlockSpec((None, D_MODEL, tf), lambda i, j: (layer, 0, j + nf)),
                  pl.BlockSpec((None, tf, D_MODEL), lambda i, j: (layer, j, 0)),
                  _const_spec((1, D_MODEL), lambda i, j: (0, 0)),
                  pl.BlockSpec((None, tm, PLE_DIM), lambda i, j: (layer, i, 0)),
                  _const_spec((None, D_MODEL, D_MODEL), lambda i, j: (layer, 0, 0)),
                  _const_spec((None, PLE_DIM, D_MODEL), lambda i, j: (layer, 0, 0))],
        out_specs=pl.BlockSpec((tm, D_MODEL), lambda i, j: (i, 0)),
        out_shape=jax.ShapeDtypeStruct((t, D_MODEL), F32),
        scratch_shapes=[pltpu.VMEM((tm, D_MODEL), BF16), pltpu.VMEM((tm, D_MODEL), F32)],
        compiler_params=_cparams(("parallel", "arbitrary")),
        name="ffn",
    )(x, gpre, w_gu, w_gu, w_down, gpost, p, w_gate, w_proj)


def _rwkv_gates(z, z_prev, mu_ref, lw_ref, w0_ref, la_ref, a0_ref, lg_ref, kk_ref, ka_ref, rk_ref):
    zm = z + mu_ref[...] * (z_prev - z)
    g4 = GROUP_W
    r = zm[:, 0:g4]
    k = zm[:, g4:2 * g4]
    v = zm[:, 2 * g4:3 * g4]
    zl = zm[:, LORA_OFF:RWKV_PAD]
    d = functools.partial(jnp.dot, preferred_element_type=F32)
    wl = w0_ref[...] + d(jnp.tanh(zl).astype(BF16), lw_ref[...])
    w_log = -(jnp.maximum(-wl, 0.0) + jnp.log1p(jnp.exp(-jnp.abs(wl)))) - 0.5
    a = _sigmoid(a0_ref[...] + d(zl.astype(BF16), la_ref[...]))
    g = d(_sigmoid(zl).astype(BF16), lg_ref[...])
    kk = k * kk_ref[...]
    kk = kk / jnp.maximum(jnp.sqrt(_group_sums(kk * kk, RWKV_HEAD)), 1e-12)
    k = k * (1.0 + (a - 1.0) * ka_ref[...])
    bonus = _group_sums(r * k * rk_ref[...], RWKV_HEAD) * v
    return r, k, v, -kk, kk * a, w_log, g, bonus


def _rwkv_post(y, bonus, g, gnw_ref, gnb_ref):
    inv_n = 1.0 / RWKV_HEAD
    mean = _group_sums(y, RWKV_HEAD) * inv_n
    yc = y - mean
    var = _group_sums(yc * yc, RWKV_HEAD) * inv_n
    yn = yc * lax.rsqrt(var + RWKV_GN_EPS) * gnw_ref[...] + gnb_ref[...]
    return (yn + bonus) * g


def _rwkv_scan_kernel(z_ref, sh_ref, s0_ref, mu_ref, lw_ref, w0_ref, la_ref, a0_ref, lg_ref, kk_ref, ka_ref, rk_ref,
                      gnw_ref, gnb_ref, y_ref, sout_ref, zbuf, *, nseq, l):
    rows = nseq * l
    z = z_ref[...]
    zbuf[pl.ds(0, 8), :] = jnp.zeros((8, RWKV_PAD), F32)
    zbuf[pl.ds(8, rows), :] = z
    rid = lax.broadcasted_iota(jnp.int32, (rows, 1), 0)
    spread = ((lax.broadcasted_iota(jnp.int32, (rows, nseq), 0) // l)
              == lax.broadcasted_iota(jnp.int32, (rows, nseq), 1)).astype(BF16)
    z_prev = jnp.where(rid % l == 0, _dot_f32_rhs(spread, sh_ref[...]), zbuf[pl.ds(7, rows), :])
    r, k, v, a_vec, b_vec, w_log, g, bonus = _rwkv_gates(
        z, z_prev, mu_ref, lw_ref, w0_ref, la_ref, a0_ref, lg_ref, kk_ref, ka_ref, rk_ref)
    w = jnp.exp(-jnp.exp(w_log))

    shp = (RWKV_HEAD, 2 * RWKV_HEAD)
    lane = lax.broadcasted_iota(jnp.int32, shp, 1)
    sub = lax.broadcasted_iota(jnp.int32, shp, 0)
    lo = lane < RWKV_HEAD
    eye = ((lane % RWKV_HEAD) == sub).astype(F32)

    def head_sums(tiles):
        s = _group_sums(jnp.concatenate(tiles, axis=0), RWKV_HEAD, terms=2)
        return [s[u * RWKV_HEAD:(u + 1) * RWKV_HEAD] for u in range(len(tiles))]

    units = [(b, p) for b in range(nseq) for p in range(RWKV_PAIRS)]
    row = lambda x, b, p, j: x[b * l + j:b * l + j + 1, p * 128:(p + 1) * 128]
    state = [jnp.concatenate([s0_ref[b, 2 * p], s0_ref[b, 2 * p + 1]], axis=1) for b, p in units]
    y_rows = {}
    for j in range(l):
        sa = head_sums([s * row(a_vec, b, p, j) for s, (b, p) in zip(state, units)])
        vcol = head_sums([eye * row(v, b, p, j) for b, p in units])
        state = [s * row(w, b, p, j) + sa_u * row(b_vec, b, p, j) + vc_u * row(k, b, p, j)
                 for s, sa_u, vc_u, (b, p) in zip(state, sa, vcol, units)]
        ycol = head_sums([s * row(r, b, p, j) for s, (b, p) in zip(state, units)])
        for yc_u, (b, p) in zip(ycol, units):
            y_rows[b * l + j, p] = jnp.sum(eye * yc_u, axis=0, keepdims=True)
    for s, (b, p) in zip(state, units):
        sout_ref[b, 2 * p] = s[:, 0:RWKV_HEAD]
        sout_ref[b, 2 * p + 1] = s[:, RWKV_HEAD:2 * RWKV_HEAD]
    y = jnp.concatenate([jnp.concatenate([y_rows[i, p] for i in range(rows)], axis=0) for p in range(RWKV_PAIRS)],
                        axis=1)
    y_ref[...] = _rwkv_post(y, bonus, g, gnw_ref, gnb_ref).astype(y_ref.dtype)


def _tri_inverse(l_mats, n):
    row = lax.broadcasted_iota(jnp.int32, (n, n), 0)
    col = lax.broadcasted_iota(jnp.int32, (n, n), 1)
    eye = (row == col).astype(F32)
    mm = lambda a, b: jnp.dot(a.astype(BF16), b.astype(BF16), preferred_element_type=F32)
    same = lambda m: (row // m) == (col // m)
    d1 = [jnp.where(same(8), x, 0.0) for x in l_mats]
    d2 = [mm(x, x) for x in d1]
    d4 = [mm(x, x) for x in d2]
    d3 = [mm(x, y) for x, y in zip(d1, d2)]
    t = [eye + x1 + x2 + x3 for x1, x2, x3 in zip(d1, d2, d3)]
    t = [x + mm(x, y) for x, y in zip(t, d4)]
    m = 8
    while m < RWKV_CHUNK:
        e = [jnp.where(same(2 * m) & jnp.logical_not(same(m)), x, 0.0) for x in l_mats]
        te = [mm(x, y) for x, y in zip(t, e)]
        t = [x + mm(y, x) for x, y in zip(t, te)]
        m *= 2
    return t


def _rwkv_chunk_local(chunks):
    c = RWKV_CHUNK
    n = 2 * c
    lo64 = lax.broadcasted_iota(jnp.int32, (c, 128), 1) < RWKV_HEAD
    tri = (lax.broadcasted_iota(jnp.int32, (c, c), 1) <= lax.broadcasted_iota(jnp.int32, (c, c), 0)).astype(BF16)
    stack = lambda x: jnp.concatenate([jnp.where(lo64, x, 0.0), jnp.where(lo64, 0.0, x)], axis=0)
    ops = []
    for r, k, v, a_vec, b_vec, logw in chunks:
        cum = _dot_f32_rhs(tri, logw)
        c_end = cum[c - 1:c]
        e_neg = jnp.exp(-cum)
        e_end = jnp.exp(c_end - cum)
        at = a_vec * jnp.exp(cum - logw)
        rt = r * jnp.exp(cum)
        bt = b_vec * e_neg
        kt = k * e_neg
        bh = b_vec * e_end
        kh = k * e_end
        e_cend = jnp.exp(c_end)
        for p in range(RWKV_PAIRS):
            sl = slice(p * 128, (p + 1) * 128)
            ops.append(dict(xa=stack(at[:, sl]), xr=stack(rt[:, sl]), yb=stack(bt[:, sl]), yk=stack(kt[:, sl]),
                            vs=stack(v[:, sl]).astype(BF16), xbh=stack(bh[:, sl]), xkh=stack(kh[:, sl]),
                            e_cend=e_cend[:, sl]))

    row = lax.broadcasted_iota(jnp.int32, (n, n), 0)
    col = lax.broadcasted_iota(jnp.int32, (n, n), 1)
    same_head = (row // c) == (col // c)
    strict = same_head & (col < row)
    incl = same_head & (col <= row)
    zeros = jnp.zeros((n, n), BF16)
    mm = lambda x, y: jnp.dot(x, y, preferred_element_type=F32)
    nt = lambda x, y: lax.dot_general(x, y, (((1,), (1,)), ((), ())), preferred_element_type=F32)
    tn = lambda x, y: lax.dot_general(x, y, (((0,), (0,)), ((), ())), preferred_element_type=F32)
    unstack = lambda x: x[0:c] + x[c:n]

    gram = [nt(jnp.concatenate([o["xa"], o["xr"]], axis=0).astype(BF16),
               jnp.concatenate([o["yb"], o["yk"]], axis=0).astype(BF16)) for o in ops]
    l_ab = [jnp.where(strict, x[0:n, 0:n], 0.0) for x in gram]
    l_ak = [jnp.where(strict, x[0:n, n:2 * n], 0.0) for x in gram]
    m_rbk = [jnp.concatenate([jnp.where(incl, x[n:2 * n, 0:n], 0.0), jnp.where(incl, x[n:2 * n, n:2 * n], 0.0)],
                             axis=1).astype(BF16) for x in gram]
    w1 = [mm(x.astype(BF16), o["vs"]) for x, o in zip(l_ak, ops)]
    t_inv = _tri_inverse(l_ab, n)
    tu = [mm(t.astype(BF16), jnp.concatenate([o["xa"], w], axis=1).astype(BF16))
          for t, o, w in zip(t_inv, ops, w1)]
    zmat = [jnp.concatenate([x.astype(BF16), jnp.concatenate([zeros, o["vs"]], axis=1)], axis=0)
            for x, o in zip(tu, ops)]
    ry = [mm(m, zm) for m, zm in zip(m_rbk, zmat)]
    ps = [tn(zm, jnp.concatenate([o["xbh"], o["xkh"]], axis=0).astype(BF16))
          for zm, o in zip(zmat, ops)]
    out = [(unstack(o["xr"] + y[:, 0:n]), unstack(y[:, n:2 * n]), o["e_cend"], q[0:n], q[n:2 * n])
           for o, y, q in zip(ops, ry, ps)]
    return [out[ci * RWKV_PAIRS:(ci + 1) * RWKV_PAIRS] for ci in range(len(chunks))]


def _rwkv_chunk_kernel(z_ref, sh_ref, s0_ref, mu_ref, lw_ref, w0_ref, la_ref, a0_ref, lg_ref, kk_ref, ka_ref, rk_ref,
                       gnw_ref, gnb_ref, y_ref, sout_ref, zbuf, s_scr, *, n_tiles):
    c = RWKV_CHUNK
    tile = 2 * c
    zero = jnp.zeros((RWKV_HEAD, RWKV_HEAD), F32)
    for p in range(RWKV_PAIRS):
        s_scr[p] = jnp.concatenate([jnp.concatenate([s0_ref[0, 2 * p], zero], axis=1),
                                    jnp.concatenate([zero, s0_ref[0, 2 * p + 1]], axis=1)], axis=0)
    zbuf[pl.ds(0, 8), :] = jnp.broadcast_to(sh_ref[0], (8, RWKV_PAD))
    nt = lambda x, y: lax.dot_general(x, y, (((1,), (1,)), ((), ())), preferred_element_type=F32)

    def tile_body(ti, carry):
        r0 = pl.multiple_of(ti * tile, tile)
        z = z_ref[0, pl.ds(r0, tile), :]
        zbuf[pl.ds(8, tile), :] = z
        z_prev = zbuf[pl.ds(7, tile), :]
        zbuf[pl.ds(7, 1), :] = zbuf[pl.ds(7 + tile, 1), :]
        r, k, v, a_vec, b_vec, w_log, g, bonus = _rwkv_gates(
            z, z_prev, mu_ref, lw_ref, w0_ref, la_ref, a0_ref, lg_ref, kk_ref, ka_ref, rk_ref)
        logw = -jnp.exp(w_log)
        local = _rwkv_chunk_local([tuple(x[ci * c:(ci + 1) * c] for x in (r, k, v, a_vec, b_vec, logw))
                                   for ci in range(2)])
        y_cols = []
        for p in range(RWKV_PAIRS):
            s = s_scr[p]
            ys = []
            for ci in range(2):
                rc, yloc, e_cend, trans, sloc = local[ci][p]
                sb = s.astype(BF16)
                ys.append(nt(rc.astype(BF16), sb) + yloc)
                s = s * e_cend + jnp.dot(sb, trans.astype(BF16), preferred_element_type=F32) + sloc
            s_scr[p] = s
            y_cols.append(jnp.concatenate(ys, axis=0))
        y = jnp.concatenate(y_cols, axis=1)
        y_ref[0, pl.ds(r0, tile), :] = _rwkv_post(y, bonus, g, gnw_ref, gnb_ref).astype(y_ref.dtype)
        return carry

    lax.fori_loop(0, n_tiles, tile_body, 0)
    for p in range(RWKV_PAIRS):
        sout_ref[0, 2 * p] = s_scr[p, 0:RWKV_HEAD, 0:RWKV_HEAD]
        sout_ref[0, 2 * p + 1] = s_scr[p, RWKV_HEAD:2 * RWKV_HEAD, RWKV_HEAD:2 * RWKV_HEAD]


def _rwkv(z2, shift_prev, wkv0, pw, layer, b, l, prev_stack, out_dtype):
    vec = lambda n: _const_spec((1, n), lambda i: (0, 0))
    mat = lambda r, c: _const_spec((r, c), lambda i: (0, 0))
    params = [pw["mu"], pw["lora_w"], pw["w0"], pw["lora_a"], pw["a0"], pw["lora_g"],
              pw["k_k"], pw["k_a"], pw["r_k"], pw["gn_w"], pw["gn_b"]]
    pspecs = [vec(RWKV_PAD), mat(LORA_W, GROUP_W), vec(GROUP_W), mat(LORA_W, GROUP_W), vec(GROUP_W),
              mat(LORA_W, GROUP_W), vec(GROUP_W), vec(GROUP_W), vec(GROUP_W), vec(GROUP_W), vec(GROUP_W)]
    if l % (2 * RWKV_CHUNK) == 0:
        nseq = 1
        z_in = z2.reshape(b, l, Z_COLS)
        shift_prev = shift_prev.reshape(b, 1, RWKV_PAD)
        shspec = pl.BlockSpec((1, 1, RWKV_PAD), lambda i: (i, 0, 0))
        body = functools.partial(_rwkv_chunk_kernel, n_tiles=l // (2 * RWKV_CHUNK))
        zspec = pl.BlockSpec((1, l, RWKV_PAD), lambda i: (i, 0, Z_RWKV // RWKV_PAD))
        yspec = pl.BlockSpec((1, l, GROUP_W), lambda i: (i, 0, 0))
        yshape = jax.ShapeDtypeStruct((b, l, GROUP_W), out_dtype)
        scratch = [pltpu.VMEM((2 * RWKV_CHUNK + 8, RWKV_PAD), F32), pltpu.VMEM((RWKV_PAIRS, 128, 128), F32)]
    else:
        nseq = SHORT_SEQS
        z_in = z2
        shspec = pl.BlockSpec((nseq, RWKV_PAD), lambda i: (i, 0))
        body = functools.partial(_rwkv_scan_kernel, nseq=nseq, l=l)
        zspec = pl.BlockSpec((nseq * l, RWKV_PAD), lambda i: (i, Z_RWKV // RWKV_PAD))
        yspec = pl.BlockSpec((nseq * l, GROUP_W), lambda i: (i, 0))
        yshape = jax.ShapeDtypeStruct((b * l, GROUP_W), out_dtype)
        scratch = [pltpu.VMEM((nseq * l + 8, RWKV_PAD), F32)]
    st = pl.BlockSpec((None, nseq, RWKV_HEADS, RWKV_HEAD, RWKV_HEAD), lambda i: (layer, i, 0, 0, 0))
    y, s_new = _call_stacked(
        body, [z_in, shift_prev, wkv0] + params,
        [zspec, shspec, st] + pspecs,
        1, prev_stack,
        grid=(b // nseq,),
        out_specs=[yspec, st],
        out_shape=[yshape, jax.ShapeDtypeStruct((DEPTH, b, RWKV_HEADS, RWKV_HEAD, RWKV_HEAD), F32)],
        scratch_shapes=scratch,
        compiler_params=_cparams(("parallel",)),
        name="rwkv7")
    return y.reshape(b * l, GROUP_W), s_new


def _sgu_kernel(z_ref, lnw_ref, lnb_ref, wm_ref, bias_ref, nrm_ref, y_ref, v_ref, *, blk):
    z = z_ref[...]
    zg = 0.5 * z * (1.0 + lax.erf(z * 0.7071067811865476))
    u = zg[:, :GROUP_W]
    v = zg[:, GROUP_W:]
    mu = jnp.mean(v, axis=-1, keepdims=True)
    vc = v - mu
    var = jnp.mean(vc * vc, axis=-1, keepdims=True)
    vn = vc * lax.rsqrt(var + LN_EPS) * lnw_ref[...] + lnb_ref[...]
    v_ref[...] = vn
    row = lax.broadcasted_iota(jnp.int32, (SGU_CHUNK, SGU_CHUNK), 0)
    col = lax.broadcasted_iota(jnp.int32, (SGU_CHUNK, SGU_CHUNK), 1)
    mask = col <= row
    if blk < SGU_CHUNK:
        mask = mask & ((row // blk) == (col // blk))
    vb = vn.astype(BF16)
    wms = [jnp.where(mask, wm_ref[h], 0.0).astype(BF16) for h in range(SGU_HEADS)]
    rows = []
    for c in range(z.shape[0] // SGU_CHUNK):
        rs = slice(c * SGU_CHUNK, (c + 1) * SGU_CHUNK)
        rows.append(jnp.concatenate(
            [jnp.dot(wms[h], vb[rs, h * 128:(h + 1) * 128], preferred_element_type=F32) + bias_ref[:, h:h + 1]
             for h in range(SGU_HEADS)], axis=1))
    s = rows[0] if len(rows) == 1 else jnp.concatenate(rows, axis=0)
    y_ref[...] = _rms(u * s, nrm_ref[...]).astype(y_ref.dtype)


def _sgu(z2, pw, wm, bias, blk, out_dtype):
    t = z2.shape[0]
    tile = SGU_TILE_CHUNKS * SGU_CHUNK
    vec = _const_spec((1, GROUP_W), lambda i: (0, 0))
    return pl.pallas_call(
        functools.partial(_sgu_kernel, blk=blk),
        grid=(t // tile,),
        in_specs=[pl.BlockSpec((tile, SGU_COLS), lambda i: (i, Z_SGU // SGU_COLS)),
                  vec, vec,
                  _const_spec((SGU_HEADS, SGU_CHUNK, SGU_CHUNK), lambda i: (0, 0, 0)),
                  _const_spec((SGU_CHUNK, SGU_HEADS), lambda i: (0, 0)),
                  vec],
        out_specs=[pl.BlockSpec((tile, GROUP_W), lambda i: (i, 0)),
                   pl.BlockSpec((tile, GROUP_W), lambda i: (i, 0))],
        out_shape=[jax.ShapeDtypeStruct((t, GROUP_W), out_dtype), jax.ShapeDtypeStruct((t, GROUP_W), F32)],
        compiler_params=_cparams(("parallel",)),
        name="sgu",
    )(z2, pw["ln_w"], pw["ln_b"], wm, bias, pw["norm"])


def _hgrn_chunks(seqs, lb, nw, chunk, sub):
    nh = HGRN_HEADS
    hs = lambda x, h: x[:, h * HGRN_HEAD:(h + 1) * HGRN_HEAD]
    tri = (lax.broadcasted_iota(jnp.int32, (chunk, chunk), 1)
           <= lax.broadcasted_iota(jnp.int32, (chunk, chunk), 0)).astype(BF16)
    eye = (lax.broadcasted_iota(jnp.int32, (HGRN_HEAD, HGRN_HEAD), 0)
           == lax.broadcasted_iota(jnp.int32, (HGRN_HEAD, HGRN_HEAD), 1)).astype(F32)
    rowid = lax.broadcasted_iota(jnp.int32, (sub, 1), 0)
    d = functools.partial(jnp.dot, preferred_element_type=F32)
    nt = lambda x, y: lax.dot_general(x, y, (((1,), (1,)), ((), ())), preferred_element_type=F32)
    tn = lambda x, y: lax.dot_general(x, y, (((0,), (0,)), ((), ())), preferred_element_type=F32)

    pre = []
    for zq, zf, zi, zg, states in seqs:
        q = _silu(zq)
        fg = lb + (1.0 - lb) * _sigmoid(zf)
        k = 1.0 - fg
        b = _dot_f32_rhs(tri, jnp.log(fg))
        pre.append(dict(q=q, k=k, v=zi, vb=zi.astype(BF16), b=b, qe=(q * jnp.exp(b)).astype(BF16)))
    o = [[d(hs(u["qe"], h), st[h].astype(BF16)) for h in range(nh)] for u, (_, _, _, _, st) in zip(pre, seqs)]
    outs = [[[] for _ in range(nh)] for _ in seqs]
    for i in range(chunk // sub):
        lo_, hi_ = i * sub, (i + 1) * sub
        oi = [[o[n][h][lo_:hi_] for h in range(nh)] for n in range(len(seqs))]
        if i > 0:
            for n, u in enumerate(pre):
                bref = u["b"][lo_ - 1:lo_]
                qt = (u["q"][lo_:hi_] * jnp.exp(u["b"][lo_:hi_] - bref)).astype(BF16)
                kt = (u["k"][:lo_] * jnp.exp(bref - u["b"][:lo_])).astype(BF16)
                att = [nt(hs(qt, h), hs(kt, h)).astype(BF16) for h in range(nh)]
                for h in range(nh):
                    oi[n][h] = oi[n][h] + d(att[h], hs(u["vb"], h)[:lo_])
        tiles = []
        for u in pre:
            qi, bi = u["q"][lo_:hi_], u["b"][lo_:hi_]
            for j in range(sub):
                sj = lo_ + j
                pj = qi * jnp.exp(jnp.minimum(bi - u["b"][sj:sj + 1], 0.0)) * u["k"][sj:sj + 1]
                tiles.extend(hs(pj, h) for h in range(nh))
        att = _group_sums(jnp.concatenate(tiles, axis=0), HGRN_HEAD, terms=2)
        t = 0
        for n, u in enumerate(pre):
            for j in range(sub):
                sj = lo_ + j
                for h in range(nh):
                    a_j = jnp.where(rowid >= j, att[t * sub:(t + 1) * sub], 0.0)
                    oi[n][h] = oi[n][h] + a_j * hs(u["v"], h)[sj:sj + 1]
                    t += 1
        for n in range(len(seqs)):
            for h in range(nh):
                outs[n][h].append(oi[n][h])
    results = []
    for n, (u, (_, _, _, zg, st)) in enumerate(zip(pre, seqs)):
        o_n = jnp.concatenate([outs[n][h][0] if len(outs[n][h]) == 1 else jnp.concatenate(outs[n][h], axis=0)
                               for h in range(nh)], axis=1)
        blast = u["b"][chunk - 1:chunk]
        kd = (u["k"] * jnp.exp(blast - u["b"])).astype(BF16)
        e_last = jnp.exp(blast)
        new_states = []
        for h in range(nh):
            e_col = _group_sums(eye * hs(e_last, h), HGRN_HEAD)
            new_states.append(e_col * st[h] + tn(hs(kd, h), hs(u["vb"], h)))
        ms = _group_sums(o_n * o_n, HGRN_HEAD) * (1.0 / HGRN_HEAD)
        results.append((o_n * lax.rsqrt(ms + NORM_EPS) * nw * _silu(zg), new_states))
    return results


def _hgrn_kernel(zq_ref, zf_ref, zi_ref, zg_ref, lg_ref, nw_ref, s0_ref, y_ref, sout_ref, s_scr,
                 *, layer, nseq, chunk, sub, n_chunks):
    lg = lg_ref[...]
    e = jnp.exp(lg - jnp.max(lg, axis=0, keepdims=True))
    sm = e / jnp.sum(e, axis=0, keepdims=True)
    lb = jnp.sum(sm[0:layer + 1], axis=0, keepdims=True) - sm[0:1]
    nw = nw_ref[...]
    nh = HGRN_HEADS
    if n_chunks == 1:
        seqs = [(zq_ref[b], zf_ref[b], zi_ref[b], zg_ref[b], [s0_ref[b, h] for h in range(nh)])
                for b in range(nseq)]
        for b, (y, states) in enumerate(_hgrn_chunks(seqs, lb, nw, chunk, sub)):
            y_ref[b] = y.astype(y_ref.dtype)
            for h in range(nh):
                sout_ref[b, h] = states[h]
    else:
        s_scr[...] = s0_ref[...]

        def chunk_body(c, carry):
            r0 = pl.multiple_of(c * chunk, chunk)
            rows = lambda ref, b: ref[b, pl.ds(r0, chunk), :]
            seqs = [(rows(zq_ref, b), rows(zf_ref, b), rows(zi_ref, b), rows(zg_ref, b),
                     [s_scr[b, h] for h in range(nh)]) for b in range(nseq)]
            for b, (y, states) in enumerate(_hgrn_chunks(seqs, lb, nw, chunk, sub)):
                y_ref[b, pl.ds(r0, chunk), :] = y.astype(y_ref.dtype)
                for h in range(nh):
                    s_scr[b, h] = states[h]
            return carry

        lax.fori_loop(0, n_chunks, chunk_body, 0)
        sout_ref[...] = s_scr[...]


def _hgrn(z3, s0, lb_logits, norm_w, layer, prev_stack, out_dtype):
    b, l, _ = z3.shape
    chunk = min(HGRN_CHUNK, l)
    sub = min(HGRN_SUB, chunk)
    nseq = 1 if l > chunk else SHORT_SEQS
    cb = Z_HGRN // GROUP_W
    zspec = lambda part: pl.BlockSpec((nseq, l, GROUP_W), lambda i: (i, 0, cb + part))
    st = pl.BlockSpec((None, nseq, HGRN_HEADS, HGRN_HEAD, HGRN_HEAD), lambda i: (layer, i, 0, 0, 0))
    return _call_stacked(
        functools.partial(_hgrn_kernel, layer=layer, nseq=nseq, chunk=chunk, sub=sub, n_chunks=l // chunk),
        [z3, z3, z3, z3, lb_logits, norm_w, s0],
        [zspec(0), zspec(1), zspec(2), zspec(3),
         _const_spec((DEPTH, GROUP_W), lambda i: (0, 0)),
         _const_spec((1, GROUP_W), lambda i: (0, 0)),
         st],
        1, prev_stack,
        grid=(b // nseq,),
        out_specs=[pl.BlockSpec((nseq, l, GROUP_W), lambda i: (i, 0, 0)), st],
        out_shape=[jax.ShapeDtypeStruct((b, l, GROUP_W), out_dtype),
                   jax.ShapeDtypeStruct((DEPTH, b, HGRN_HEADS, HGRN_HEAD, HGRN_HEAD), F32)],
        scratch_shapes=[pltpu.VMEM((nseq, HGRN_HEADS, HGRN_HEAD, HGRN_HEAD), F32)],
        compiler_params=_cparams(("parallel",)),
        name="hgrn2")


def _pool_kernel(z_ref, hist_ref, pw_ref, ps_ref, y_ref, xbuf, s2buf, s4buf, s8buf,
                 *, nseq, tc, n_tiles, start_pos):
    halo = POOL_HALO
    n = tc + halo
    c1 = POOL_CH
    for b in range(nseq):
        xbuf[b, pl.ds(0, 16), :] = jnp.zeros((16, GROUP_W), F32)
        xbuf[b, pl.ds(16, 16), :] = hist_ref[b]

    def tile(r0, b):
        z = z_ref[b, pl.ds(r0, tc), :]
        xbuf[b, pl.ds(halo, tc), :] = z
        s2buf[b, pl.ds(8, n - 8), :] = xbuf[b, pl.ds(8, n - 8), :] + xbuf[b, pl.ds(7, n - 8), :]
        s4buf[b, pl.ds(16, n - 16), :] = s2buf[b, pl.ds(16, n - 16), c1:] + s2buf[b, pl.ds(14, n - 16), c1:]
        s8buf[b, pl.ds(24, n - 24), :] = s4buf[b, pl.ds(24, n - 24), c1:] + s4buf[b, pl.ds(20, n - 24), c1:]
        s16 = s8buf[b, pl.ds(halo, tc), c1:] + s8buf[b, pl.ds(halo - 8, tc), c1:]
        sums = (s2buf[b, pl.ds(halo, tc), 0:c1], s4buf[b, pl.ds(halo, tc), 0:c1], s8buf[b, pl.ds(halo, tc), 0:c1],
                s16)
        pos = start_pos + r0 + lax.broadcasted_iota(jnp.int32, (tc, 1), 0)
        ys = []
        for gi, win in enumerate(POOL_WINDOWS):
            sl = slice(gi * POOL_CH, (gi + 1) * POOL_CH)
            cnt = jnp.minimum(pos + 1, win).astype(F32)
            dlt = sums[gi] / cnt - z[:, sl]
            ys.append(jnp.dot(dlt.astype(BF16), pw_ref[gi], preferred_element_type=F32) * ps_ref[:, sl])
        y_ref[b, pl.ds(r0, tc), :] = jnp.concatenate(ys, axis=1).astype(y_ref.dtype)
        if n_tiles > 1:
            xbuf[b, pl.ds(16, 16), :] = xbuf[b, pl.ds(n - 16, 16), :]

    if n_tiles == 1:
        for b in range(nseq):
            tile(0, b)
    else:
        def tile_body(ti, carry):
            for b in range(nseq):
                tile(pl.multiple_of(ti * tc, tc), b)
            return carry

        lax.fori_loop(0, n_tiles, tile_body, 0)


def _pool(z3, hist16, pool_w, pool_scale, start_pos, tc, out_dtype):
    b, l, _ = z3.shape
    nseq = SHORT_SEQS if l < POOL_HALO else 1
    n = tc + POOL_HALO
    return pl.pallas_call(
        functools.partial(_pool_kernel, nseq=nseq, tc=tc, n_tiles=l // tc, start_pos=start_pos),
        grid=(b // nseq,),
        in_specs=[pl.BlockSpec((nseq, l, GROUP_W), lambda i: (i, 0, Z_POOL // GROUP_W)),
                  pl.BlockSpec((nseq, 16, GROUP_W), lambda i: (i, 0, 0)),
                  _const_spec((len(POOL_WINDOWS), POOL_CH, POOL_CH), lambda i: (0, 0, 0)),
                  _const_spec((1, GROUP_W), lambda i: (0, 0))],
        out_specs=pl.BlockSpec((nseq, l, GROUP_W), lambda i: (i, 0, 0)),
        out_shape=jax.ShapeDtypeStruct((b, l, GROUP_W), out_dtype),
        scratch_shapes=[pltpu.VMEM((nseq, n, GROUP_W), F32), pltpu.VMEM((nseq, n, GROUP_W), F32),
                        pltpu.VMEM((nseq, n, 3 * POOL_CH), F32), pltpu.VMEM((nseq, n, 2 * POOL_CH), F32)],
        compiler_params=_cparams(("parallel",)),
        name="pool",
    )(z3, hist16, pool_w, pool_scale)


def _pad_cols(a, n):
    return jnp.pad(a, ((0, 0),) * (a.ndim - 1) + ((0, n - a.shape[-1]),))


def _prep_weights(W):
    c1, c2, c3 = RWKV_COLS, RWKV_COLS + SGU_COLS, RWKV_COLS + SGU_COLS + HGRN_COLS
    w_in = W["w_in"]
    w_in_p = jnp.concatenate(
        [w_in[:, :, c2:c3], w_in[:, :, c1:c2], w_in[:, :, c3:], _pad_cols(w_in[:, :, :c1], RWKV_PAD)],
        axis=2).astype(BF16)
    return dict(w_in=w_in_p, w_out=W["w_out"].astype(BF16), w_gu=W["ffn_w_gu"].astype(BF16),
                w_down=W["ffn_w_down"].astype(BF16), ple_gate=W["ple_gate"].astype(BF16),
                ple_proj=W["ple_proj"].astype(BF16))


def _prep_layer(i, W):
    row = lambda a: a.reshape(1, -1)
    lora = lambda a, off: jnp.pad(a, ((off, LORA_W - off - a.shape[0]), (0, 0))).astype(BF16)
    rw = dict(
        mu=_pad_cols(row(W["rwkv_mu"][i]), RWKV_PAD),
        lora_w=lora(W["rwkv_w_lora"][i], 0), w0=row(W["rwkv_w0"][i]),
        lora_a=lora(W["rwkv_a_lora"][i], W_LORA), a0=row(W["rwkv_a0"][i]),
        lora_g=lora(W["rwkv_g_lora"][i], W_LORA + A_LORA),
        k_k=row(W["rwkv_k_k"][i]), k_a=row(W["rwkv_k_a"][i]), r_k=row(W["rwkv_r_k"][i]),
        gn_w=row(W["rwkv_gn_w"][i]), gn_b=row(W["rwkv_gn_b"][i]))
    sg = dict(ln_w=row(W["sgu_ln_w"][i]), ln_b=row(W["sgu_ln_b"][i]), norm=row(W["sgu_norm"][i]))
    return dict(
        rw=rw, sg=sg, sgu_w=W["sgu_w"][i], sgu_b=W["sgu_b"][i],
        hgrn_norm=row(W["hgrn_norm"][i]), pool_w=W["pool_w"][i].astype(BF16), pool_scale=row(W["pool_scale"][i]),
        ln_mix_pre=row(W["ln_mix_pre"][i]), ln_mix_post=row(W["ln_mix_post"][i]),
        ln_ffn_pre=row(W["ln_ffn_pre"][i]), ln_ffn_post=row(W["ln_ffn_post"][i]))


def _trunk(x, p, wkv0, shift0, hgrn0, pool0, start_pos, big, layers, lb_logits, cfg):
    b, l, _ = x.shape
    t = b * l
    xf = x.reshape(t, D_MODEL)
    p2 = p.reshape(DEPTH, t, PLE_DIM)
    wkv_new = hgrn_new = None
    shift_l, pool_l, sgu_l = [], [], []
    for i, lw in enumerate(layers):
        z2 = _inproj(xf, lw["ln_mix_pre"], big["w_in"], i, cfg["tm"], cfg["tn_in"])
        z3 = z2.reshape(b, l, Z_COLS)
        y_r, wkv_new = _rwkv(z2, _pad_cols(shift0[i], RWKV_PAD), wkv0, lw["rw"], i, b, l, wkv_new, cfg["ydt"])
        if cfg["sgu_blk"] == SGU_CHUNK:
            wm, bias = lw["sgu_w"], lw["sgu_b"].T
        else:
            rep = SGU_CHUNK // l
            wm = jnp.tile(lw["sgu_w"][:, :l, :l], (1, rep, rep))
            bias = jnp.tile(lw["sgu_b"][:, :l].T, (rep, 1))
        y_s, v_rows = _sgu(z2, lw["sg"], wm, bias, cfg["sgu_blk"], cfg["ydt"])
        y_h, hgrn_new = _hgrn(z3, hgrn0, lb_logits, lw["hgrn_norm"], i, hgrn_new, cfg["ydt"])
        hist16 = jnp.pad(pool0[i], ((0, 0), (1, 0), (0, 0)))
        y_p = _pool(z3, hist16, lw["pool_w"], lw["pool_scale"], start_pos, cfg["tc_pool"], cfg["ydt"])
        ys = [y_r, y_s, y_h.reshape(t, GROUP_W), y_p.reshape(t, GROUP_W)]
        xf = _outproj(ys, xf, big["w_out"], lw["ln_mix_post"], i, cfg["tm"])
        xf = _ffn(xf, lw["ln_ffn_pre"], big["w_gu"], big["w_down"], lw["ln_ffn_post"], p2, big["ple_gate"],
                  big["ple_proj"], i, cfg["tm"], cfg["tf"])
        shift_l.append(z3[:, -1, Z_RWKV:Z_RWKV + RWKV_COLS])
        pool_l.append(jnp.concatenate([pool0[i], z3[:, :, Z_POOL:Z_POOL + GROUP_W]], axis=1)[:, -POOL_HIST:])
        sgu_l.append(v_rows.reshape(b, l, GROUP_W))
    return (xf.reshape(b, l, D_MODEL), wkv_new, jnp.stack(shift_l), hgrn_new, jnp.stack(pool_l), jnp.stack(sgu_l))


def _cfg(l):
    if l >= 256:
        return dict(tm=512, tn_in=1792, tf=512, sgu_blk=SGU_CHUNK, tc_pool=256, ydt=BF16)
    return dict(tm=512, tn_in=1792, tf=512, sgu_blk=l, tc_pool=l, ydt=BF16)


def kernel(x_prompt, x_sample, state_rwkv_wkv, state_rwkv_shift, state_hgrn, state_pool, p_prompt, p_sample,
           ln_mix_pre, ln_mix_post, ln_ffn_pre, ln_ffn_post, w_in, rwkv_mu, rwkv_w_lora, rwkv_w0, rwkv_a_lora,
           rwkv_a0, rwkv_g_lora, rwkv_k_k, rwkv_k_a, rwkv_r_k, rwkv_gn_w, rwkv_gn_b, sgu_ln_w, sgu_ln_b, sgu_w,
           sgu_b, sgu_norm, hgrn_lb_logits, hgrn_norm, pool_w, pool_scale, w_out, ffn_w_gu, ffn_w_down, ple_gate,
           ple_proj):
    W = dict(ln_mix_pre=ln_mix_pre, ln_mix_post=ln_mix_post, ln_ffn_pre=ln_ffn_pre, ln_ffn_post=ln_ffn_post,
             w_in=w_in, rwkv_mu=rwkv_mu, rwkv_w_lora=rwkv_w_lora, rwkv_w0=rwkv_w0, rwkv_a_lora=rwkv_a_lora,
             rwkv_a0=rwkv_a0, rwkv_g_lora=rwkv_g_lora, rwkv_k_k=rwkv_k_k, rwkv_k_a=rwkv_k_a,
             rwkv_r_k=rwkv_r_k.reshape(DEPTH, GROUP_W), rwkv_gn_w=rwkv_gn_w, rwkv_gn_b=rwkv_gn_b, sgu_ln_w=sgu_ln_w,
             sgu_ln_b=sgu_ln_b, sgu_w=sgu_w, sgu_b=sgu_b, sgu_norm=sgu_norm, hgrn_norm=hgrn_norm, pool_w=pool_w,
             pool_scale=pool_scale, w_out=w_out, ffn_w_gu=ffn_w_gu, ffn_w_down=ffn_w_down, ple_gate=ple_gate,
             ple_proj=ple_proj)
    big = _prep_weights(W)
    layers = [_prep_layer(i, W) for i in range(DEPTH)]
    lb_logits = hgrn_lb_logits.astype(F32)
    bp, lp, _ = x_prompt.shape
    zeros = lambda *s: jnp.zeros((DEPTH, bp) + s, F32)
    y_prompt, wkv_p, shift_p, hgrn_p, pool_p, _ = _trunk(
        x_prompt, p_prompt, zeros(RWKV_HEADS, RWKV_HEAD, RWKV_HEAD), zeros(RWKV_COLS),
        zeros(HGRN_HEADS, HGRN_HEAD, HGRN_HEAD), zeros(POOL_HIST, GROUP_W), 0, big, layers, lb_logits, _cfg(lp))
    y_sample, wkv_s, shift_s, hgrn_s, pool_s, sgu_v_s = _trunk(
        x_sample, p_sample, state_rwkv_wkv, state_rwkv_shift, state_hgrn, state_pool, PAST_LEN, big, layers,
        lb_logits, _cfg(x_sample.shape[1]))
    return (y_prompt, y_sample, wkv_p, shift_p, hgrn_p, pool_p, wkv_s, shift_s, hgrn_s, pool_s, sgu_v_s)
```

```python
import functools

import jax
import jax.numpy as jnp
from jax import lax
from jax.experimental import pallas as pl
from jax.experimental.pallas import tpu as pltpu

F32 = jnp.float32
BF16 = jnp.bfloat16

D_MODEL = 2048
DEPTH = 2
PAST_LEN = 16384
GROUP_W = 512
RWKV_HEAD = 64
RWKV_HEADS = 8
RWKV_PAIRS = RWKV_HEADS // 2
RWKV_CHUNK = 64
RWKV_TILE_CHUNKS = 4
RWKV_TILE = RWKV_TILE_CHUNKS * RWKV_CHUNK
W_LORA, A_LORA, G_LORA = 32, 32, 96
RWKV_GN_EPS = 64e-5
SGU_CHUNK = 128
SGU_HEADS = 4
SGU_TILE_CHUNKS = 4
HGRN_HEADS = 4
HGRN_HEAD = 128
HGRN_CHUNK = 64
HGRN_SUB = 16
POOL_WINDOWS = (2, 4, 8, 16)
POOL_CH = 128
POOL_HIST = 15
POOL_HALO = 32
D_FF = 5632
PLE_DIM = 256
NORM_EPS = 1e-6
LN_EPS = 1e-5
RWKV_COLS = 3 * GROUP_W + W_LORA + A_LORA + G_LORA
RWKV_PAD = 1792
LORA_OFF = 3 * GROUP_W
LORA_W = RWKV_PAD - LORA_OFF
SGU_COLS = 2 * GROUP_W
HGRN_COLS = 4 * GROUP_W
Z_HGRN, Z_SGU, Z_POOL, Z_RWKV = 0, 2048, 3072, 3584
Z_COLS = Z_RWKV + RWKV_PAD
SHORT_SEQS = 8
VMEM_LIMIT = 56 * 1024 * 1024


def _cparams(sem):
    return pltpu.CompilerParams(dimension_semantics=sem, vmem_limit_bytes=VMEM_LIMIT)


def _const_spec(shape, index_map):
    return pl.BlockSpec(shape, index_map, pipeline_mode=pl.Buffered(1))


def _sigmoid(x):
    return jax.nn.sigmoid(x)


def _silu(x):
    return x * jax.nn.sigmoid(x)


def _split3(x):
    hi = x.astype(BF16)
    r = x - hi.astype(F32)
    mid = r.astype(BF16)
    lo = (r - mid.astype(F32)).astype(BF16)
    return hi, mid, lo


def _group_sums(x, width, terms=3):
    rows, cols = x.shape
    nblk = cols // 128
    xs = x if nblk == 1 else jnp.concatenate([x[:, i * 128:(i + 1) * 128] for i in range(nblk)], axis=0)
    ones = ((lax.broadcasted_iota(jnp.int32, (128, 128), 0) // width)
            == (lax.broadcasted_iota(jnp.int32, (128, 128), 1) // width)).astype(BF16)
    s = None
    for part in _split3(xs)[:terms]:
        y = jnp.dot(part, ones, preferred_element_type=F32)
        s = y if s is None else s + y
    return s if nblk == 1 else jnp.concatenate([s[i * rows:(i + 1) * rows] for i in range(nblk)], axis=1)


def _dot_f32_rhs(m, x):
    hi, mid, lo = _split3(x)
    d = functools.partial(jnp.dot, preferred_element_type=F32)
    return d(m, hi) + d(m, mid) + d(m, lo)


def _rms(x, g):
    return x * lax.rsqrt(jnp.mean(x * x, axis=-1, keepdims=True) + NORM_EPS) * g


def _call_stacked(kernel, inputs, in_specs, stack_out, prev_stack, **kw):
    if prev_stack is None:
        return pl.pallas_call(kernel, in_specs=in_specs, **kw)(*inputs)
    n_in = len(inputs)

    def body(*refs):
        kernel(*refs[:n_in], *refs[n_in + 1:])

    return pl.pallas_call(body, in_specs=list(in_specs) + [pl.BlockSpec(memory_space=pl.ANY)],
                          input_output_aliases={n_in: stack_out}, **kw)(*inputs, prev_stack)


def _stack_out_spec(tail, layer, first):
    zeros = (0,) * (len(tail) - 1)
    if first:
        return pl.BlockSpec((DEPTH,) + tail, lambda i: (0, i) + zeros)
    return pl.BlockSpec((None,) + tail, lambda i: (layer, i) + zeros)


def _own_layer(sout_ref, fill_layer):
    if fill_layer is None:
        return sout_ref
    for j in range(DEPTH):
        if j != fill_layer:
            sout_ref[j] = jnp.zeros(sout_ref.shape[1:], sout_ref.dtype)
    return sout_ref.at[fill_layer]


def _inproj_kernel(x_ref, g_ref, w_ref, z_ref, h_ref):
    @pl.when(pl.program_id(1) == 0)
    def _():
        h_ref[...] = _rms(x_ref[...], g_ref[...]).astype(BF16)

    z_ref[...] = jnp.dot(h_ref[...], w_ref[...], preferred_element_type=F32)


def _inproj(x, g, w, layer, tm, tn):
    t = x.shape[0]
    return pl.pallas_call(
        _inproj_kernel,
        grid=(t // tm, Z_COLS // tn),
        in_specs=[pl.BlockSpec((tm, D_MODEL), lambda i, j: (i, 0)),
                  _const_spec((1, D_MODEL), lambda i, j: (0, 0)),
                  pl.BlockSpec((None, D_MODEL, tn), lambda i, j: (layer, 0, j))],
        out_specs=pl.BlockSpec((tm, tn), lambda i, j: (i, j)),
        out_shape=jax.ShapeDtypeStruct((t, Z_COLS), F32),
        scratch_shapes=[pltpu.VMEM((tm, D_MODEL), BF16)],
        compiler_params=_cparams(("parallel", "arbitrary")),
        name="inproj",
    )(x, g, w)


def _outproj_kernel(yr_ref, ys_ref, yh_ref, yp_ref, x_ref, w_ref, g_ref, o_ref):
    d = functools.partial(jnp.dot, preferred_element_type=F32)
    g4 = GROUP_W
    mix = (d(yr_ref[...], w_ref[0:g4, :]) + d(ys_ref[...], w_ref[g4:2 * g4, :])
           + d(yh_ref[...], w_ref[2 * g4:3 * g4, :]) + d(yp_ref[...], w_ref[3 * g4:4 * g4, :]))
    o_ref[...] = x_ref[...] + _rms(mix, g_ref[...])


def _outproj(ys, x, w, g, layer, tm):
    t = x.shape[0]
    yspec = pl.BlockSpec((tm, GROUP_W), lambda i: (i, 0))
    return pl.pallas_call(
        _outproj_kernel,
        grid=(t // tm,),
        in_specs=[yspec, yspec, yspec, yspec,
                  pl.BlockSpec((tm, D_MODEL), lambda i: (i, 0)),
                  _const_spec((None, D_MODEL, D_MODEL), lambda i: (layer, 0, 0)),
                  _const_spec((1, D_MODEL), lambda i: (0, 0))],
        out_specs=pl.BlockSpec((tm, D_MODEL), lambda i: (i, 0)),
        out_shape=jax.ShapeDtypeStruct((t, D_MODEL), F32),
        compiler_params=_cparams(("parallel",)),
        name="outproj",
    )(*ys, x, w, g)


def _ffn_kernel(x_ref, gpre_ref, wg_ref, wu_ref, wd_ref, gpost_ref, p_ref, wgate_ref, wproj_ref, o_ref, h_ref, acc_ref):
    j = pl.program_id(1)

    @pl.when(j == 0)
    def _():
        h_ref[...] = _rms(x_ref[...], gpre_ref[...]).astype(BF16)
        acc_ref[...] = jnp.zeros_like(acc_ref)

    h = h_ref[...]
    gate = jnp.dot(h, wg_ref[...], preferred_element_type=F32)
    up = jnp.dot(h, wu_ref[...], preferred_element_type=F32)
    act = (_silu(gate) * up).astype(BF16)
    acc_ref[...] += jnp.dot(act, wd_ref[...], preferred_element_type=F32)

    @pl.when(j == pl.num_programs(1) - 1)
    def _():
        xf = x_ref[...] + _rms(acc_ref[...], gpost_ref[...])
        pgate = _sigmoid(jnp.dot(xf.astype(BF16), wgate_ref[...], preferred_element_type=F32))
        ple = jnp.dot(p_ref[...].astype(BF16), wproj_ref[...], preferred_element_type=F32)
        o_ref[...] = xf + pgate * ple


def _ffn(x, gpre, w_gu, w_down, gpost, p, w_gate, w_proj, layer, tm, tf):
    t = x.shape[0]
    nf = D_FF // tf
    return pl.pallas_call(
        _ffn_kernel,
        grid=(t // tm, nf),
        in_specs=[pl.BlockSpec((tm, D_MODEL), lambda i, j: (i, 0)),
                  _const_spec((1, D_MODEL), lambda i, j: (0, 0)),
                  pl.BlockSpec((None, D_MODEL, tf), lambda i, j: (layer, 0, j)),
                  pl.BlockSpec((None, D_MODEL, tf), lambda i, j: (layer, 0, j + nf)),
                  pl.BlockSpec((None, tf, D_MODEL), lambda i, j: (layer, j, 0)),
                  _const_spec((1, D_MODEL), lambda i, j: (0, 0)),
                  pl.BlockSpec((None, tm, PLE_DIM), lambda i, j: (layer, i, 0)),
                  _const_spec((None, D_MODEL, D_MODEL), lambda i, j: (layer, 0, 0)),
                  _const_spec((None, PLE_DIM, D_MODEL), lambda i, j: (layer, 0, 0))],
        out_specs=pl.BlockSpec((tm, D_MODEL), lambda i, j: (i, 0)),
        out_shape=jax.ShapeDtypeStruct((t, D_MODEL), F32),
        scratch_shapes=[pltpu.VMEM((tm, D_MODEL), BF16), pltpu.VMEM((tm, D_MODEL), F32)],
        compiler_params=_cparams(("parallel", "arbitrary")),
        name="ffn",
    )(x, gpre, w_gu, w_gu, w_down, gpost, p, w_gate, w_proj)


def _rwkv_gates(z, z_prev, mu_ref, lw_ref, w0_ref, la_ref, a0_ref, lg_ref, kk_ref, ka_ref, rk_ref):
    zm = z + mu_ref[...] * (z_prev - z)
    g4 = GROUP_W
    r = zm[:, 0:g4]
    k = zm[:, g4:2 * g4]
    v = zm[:, 2 * g4:3 * g4]
    zl = zm[:, LORA_OFF:RWKV_PAD]
    d = functools.partial(jnp.dot, preferred_element_type=F32)
    wl = w0_ref[...] + d(jnp.tanh(zl).astype(BF16), lw_ref[...])
    w_log = -(jnp.maximum(-wl, 0.0) + jnp.log1p(jnp.exp(-jnp.abs(wl)))) - 0.5
    a = _sigmoid(a0_ref[...] + d(zl.astype(BF16), la_ref[...]))
    g = d(_sigmoid(zl).astype(BF16), lg_ref[...])
    kk = k * kk_ref[...]
    kk = kk / jnp.maximum(jnp.sqrt(_group_sums(kk * kk, RWKV_HEAD)), 1e-12)
    k = k * (1.0 + (a - 1.0) * ka_ref[...])
    bonus = _group_sums(r * k * rk_ref[...], RWKV_HEAD) * v
    return r, k, v, -kk, kk * a, w_log, g, bonus


def _rwkv_post(y, bonus, g, gnw_ref, gnb_ref):
    inv_n = 1.0 / RWKV_HEAD
    mean = _group_sums(y, RWKV_HEAD) * inv_n
    yc = y - mean
    var = _group_sums(yc * yc, RWKV_HEAD) * inv_n
    yn = yc * lax.rsqrt(var + RWKV_GN_EPS) * gnw_ref[...] + gnb_ref[...]
    return (yn + bonus) * g


def _rwkv_scan_kernel(z_ref, sh_ref, s0_ref, mu_ref, lw_ref, w0_ref, la_ref, a0_ref, lg_ref, kk_ref, ka_ref, rk_ref,
                      gnw_ref, gnb_ref, y_ref, sout_ref, zbuf, *, nseq, l, fill_layer):
    rows = nseq * l
    z = z_ref[...]
    zbuf[pl.ds(0, 8), :] = jnp.zeros((8, RWKV_PAD), F32)
    zbuf[pl.ds(8, rows), :] = z
    rid = lax.broadcasted_iota(jnp.int32, (rows, 1), 0)
    spread = ((lax.broadcasted_iota(jnp.int32, (rows, nseq), 0) // l)
              == lax.broadcasted_iota(jnp.int32, (rows, nseq), 1)).astype(BF16)
    z_prev = jnp.where(rid % l == 0, _dot_f32_rhs(spread, sh_ref[...]), zbuf[pl.ds(7, rows), :])
    r, k, v, a_vec, b_vec, w_log, g, bonus = _rwkv_gates(
        z, z_prev, mu_ref, lw_ref, w0_ref, la_ref, a0_ref, lg_ref, kk_ref, ka_ref, rk_ref)
    w = jnp.exp(-jnp.exp(w_log))

    shp = (RWKV_HEAD, 2 * RWKV_HEAD)
    lane = lax.broadcasted_iota(jnp.int32, shp, 1)
    sub = lax.broadcasted_iota(jnp.int32, shp, 0)
    lo = lane < RWKV_HEAD
    eye = ((lane % RWKV_HEAD) == sub).astype(F32)

    def head_sums(tiles):
        s = _group_sums(jnp.concatenate(tiles, axis=0), RWKV_HEAD, terms=2)
        return [s[u * RWKV_HEAD:(u + 1) * RWKV_HEAD] for u in range(len(tiles))]

    units = [(b, p) for b in range(nseq) for p in range(RWKV_PAIRS)]
    row = lambda x, b, p, j: x[b * l + j:b * l + j + 1, p * 128:(p + 1) * 128]
    state = [jnp.concatenate([s0_ref[b, 2 * p], s0_ref[b, 2 * p + 1]], axis=1) for b, p in units]
    y_rows = {}
    for j in range(l):
        sa = head_sums([s * row(a_vec, b, p, j) for s, (b, p) in zip(state, units)])
        vcol = head_sums([eye * row(v, b, p, j) for b, p in units])
        state = [s * row(w, b, p, j) + sa_u * row(b_vec, b, p, j) + vc_u * row(k, b, p, j)
                 for s, sa_u, vc_u, (b, p) in zip(state, sa, vcol, units)]
        ycol = head_sums([s * row(r, b, p, j) for s, (b, p) in zip(state, units)])
        for yc_u, (b, p) in zip(ycol, units):
            y_rows[b * l + j, p] = jnp.sum(eye * yc_u, axis=0, keepdims=True)
    sout = _own_layer(sout_ref, fill_layer)
    for s, (b, p) in zip(state, units):
        sout[b, 2 * p] = s[:, 0:RWKV_HEAD]
        sout[b, 2 * p + 1] = s[:, RWKV_HEAD:2 * RWKV_HEAD]
    y = jnp.concatenate([jnp.concatenate([y_rows[i, p] for i in range(rows)], axis=0) for p in range(RWKV_PAIRS)],
                        axis=1)
    y_ref[...] = _rwkv_post(y, bonus, g, gnw_ref, gnb_ref).astype(y_ref.dtype)


def _tri_inverse(l_mats, n):
    row = lax.broadcasted_iota(jnp.int32, (n, n), 0)
    col = lax.broadcasted_iota(jnp.int32, (n, n), 1)
    eye = (row == col).astype(F32)
    mm = lambda a, b: jnp.dot(a.astype(BF16), b.astype(BF16), preferred_element_type=F32)
    same = lambda m: (row // m) == (col // m)
    d1 = [jnp.where(same(8), x, 0.0) for x in l_mats]
    d2 = [mm(x, x) for x in d1]
    d4 = [mm(x, x) for x in d2]
    d3 = [mm(x, y) for x, y in zip(d1, d2)]
    t = [eye + x1 + x2 + x3 for x1, x2, x3 in zip(d1, d2, d3)]
    t = [x + mm(x, y) for x, y in zip(t, d4)]
    m = 8
    while m < RWKV_CHUNK:
        e = [jnp.where(same(2 * m) & jnp.logical_not(same(m)), x, 0.0) for x in l_mats]
        te = [mm(x, y) for x, y in zip(t, e)]
        t = [x + mm(y, x) for x, y in zip(t, te)]
        m *= 2
    return t


def _rwkv_chunk_local(chunks):
    c = RWKV_CHUNK
    n = 2 * c
    lo64 = lax.broadcasted_iota(jnp.int32, (c, 128), 1) < RWKV_HEAD
    tri = (lax.broadcasted_iota(jnp.int32, (c, c), 1) <= lax.broadcasted_iota(jnp.int32, (c, c), 0)).astype(BF16)
    stack = lambda x: jnp.concatenate([jnp.where(lo64, x, 0.0), jnp.where(lo64, 0.0, x)], axis=0)
    ops = []
    for r, k, v, a_vec, b_vec, logw in chunks:
        cum = _dot_f32_rhs(tri, logw)
        c_end = cum[c - 1:c]
        e_neg = jnp.exp(-cum)
        e_end = jnp.exp(c_end - cum)
        at = a_vec * jnp.exp(cum - logw)
        rt = r * jnp.exp(cum)
        bt = b_vec * e_neg
        kt = k * e_neg
        bh = b_vec * e_end
        kh = k * e_end
        e_cend = jnp.exp(c_end)
        for p in range(RWKV_PAIRS):
            sl = slice(p * 128, (p + 1) * 128)
            ops.append(dict(xa=stack(at[:, sl]), xr=stack(rt[:, sl]), yb=stack(bt[:, sl]), yk=stack(kt[:, sl]),
                            vs=stack(v[:, sl]).astype(BF16), xbh=stack(bh[:, sl]), xkh=stack(kh[:, sl]),
                            e_cend=e_cend[:, sl]))

    row = lax.broadcasted_iota(jnp.int32, (n, n), 0)
    col = lax.broadcasted_iota(jnp.int32, (n, n), 1)
    same_head = (row // c) == (col // c)
    strict = same_head & (col < row)
    incl = same_head & (col <= row)
    zeros = jnp.zeros((n, n), BF16)
    mm = lambda x, y: jnp.dot(x, y, preferred_element_type=F32)
    nt = lambda x, y: lax.dot_general(x, y, (((1,), (1,)), ((), ())), preferred_element_type=F32)
    tn = lambda x, y: lax.dot_general(x, y, (((0,), (0,)), ((), ())), preferred_element_type=F32)
    unstack = lambda x: x[0:c] + x[c:n]

    gram = [nt(jnp.concatenate([o["xa"], o["xr"]], axis=0).astype(BF16),
               jnp.concatenate([o["yb"], o["yk"]], axis=0).astype(BF16)) for o in ops]
    l_ab = [jnp.where(strict, x[0:n, 0:n], 0.0) for x in gram]
    l_ak = [jnp.where(strict, x[0:n, n:2 * n], 0.0) for x in gram]
    m_rbk = [jnp.concatenate([jnp.where(incl, x[n:2 * n, 0:n], 0.0), jnp.where(incl, x[n:2 * n, n:2 * n], 0.0)],
                             axis=1).astype(BF16) for x in gram]
    w1 = [mm(x.astype(BF16), o["vs"]) for x, o in zip(l_ak, ops)]
    t_inv = _tri_inverse(l_ab, n)
    tu = [mm(t.astype(BF16), jnp.concatenate([o["xa"], w], axis=1).astype(BF16))
          for t, o, w in zip(t_inv, ops, w1)]
    zmat = [jnp.concatenate([x.astype(BF16), jnp.concatenate([zeros, o["vs"]], axis=1)], axis=0)
            for x, o in zip(tu, ops)]
    ry = [mm(m, zm) for m, zm in zip(m_rbk, zmat)]
    ps = [tn(zm, jnp.concatenate([o["xbh"], o["xkh"]], axis=0).astype(BF16))
          for zm, o in zip(zmat, ops)]
    out = [(unstack(o["xr"] + y[:, 0:n]), unstack(y[:, n:2 * n]), o["e_cend"], q[0:n], q[n:2 * n])
           for o, y, q in zip(ops, ry, ps)]
    return [out[ci * RWKV_PAIRS:(ci + 1) * RWKV_PAIRS] for ci in range(len(chunks))]


def _rwkv_chunk_kernel(z_ref, sh_ref, s0_ref, mu_ref, lw_ref, w0_ref, la_ref, a0_ref, lg_ref, kk_ref, ka_ref, rk_ref,
                       gnw_ref, gnb_ref, y_ref, sout_ref, zbuf, s_scr, *, n_tiles, fill_layer):
    c = RWKV_CHUNK
    tile = RWKV_TILE
    zero = jnp.zeros((RWKV_HEAD, RWKV_HEAD), F32)
    for p in range(RWKV_PAIRS):
        s_scr[p] = jnp.concatenate([jnp.concatenate([s0_ref[0, 2 * p], zero], axis=1),
                                    jnp.concatenate([zero, s0_ref[0, 2 * p + 1]], axis=1)], axis=0)
    zbuf[pl.ds(0, 8), :] = jnp.broadcast_to(sh_ref[0], (8, RWKV_PAD))
    nt = lambda x, y: lax.dot_general(x, y, (((1,), (1,)), ((), ())), preferred_element_type=F32)

    def tile_body(ti, carry):
        r0 = pl.multiple_of(ti * tile, tile)
        z = z_ref[0, pl.ds(r0, tile), :]
        zbuf[pl.ds(8, tile), :] = z
        z_prev = zbuf[pl.ds(7, tile), :]
        zbuf[pl.ds(7, 1), :] = zbuf[pl.ds(7 + tile, 1), :]
        r, k, v, a_vec, b_vec, w_log, g, bonus = _rwkv_gates(
            z, z_prev, mu_ref, lw_ref, w0_ref, la_ref, a0_ref, lg_ref, kk_ref, ka_ref, rk_ref)
        logw = -jnp.exp(w_log)
        local = _rwkv_chunk_local([tuple(x[ci * c:(ci + 1) * c] for x in (r, k, v, a_vec, b_vec, logw))
                                   for ci in range(RWKV_TILE_CHUNKS)])
        y_cols = []
        for p in range(RWKV_PAIRS):
            s = s_scr[p]
            ys = []
            for ci in range(RWKV_TILE_CHUNKS):
                rc, yloc, e_cend, trans, sloc = local[ci][p]
                sb = s.astype(BF16)
                ys.append(nt(rc.astype(BF16), sb) + yloc)
                s = s * e_cend + jnp.dot(sb, trans.astype(BF16), preferred_element_type=F32) + sloc
            s_scr[p] = s
            y_cols.append(jnp.concatenate(ys, axis=0))
        y = jnp.concatenate(y_cols, axis=1)
        y_ref[0, pl.ds(r0, tile), :] = _rwkv_post(y, bonus, g, gnw_ref, gnb_ref).astype(y_ref.dtype)
        return carry

    lax.fori_loop(0, n_tiles, tile_body, 0)
    sout = _own_layer(sout_ref, fill_layer)
    for p in range(RWKV_PAIRS):
        sout[0, 2 * p] = s_scr[p, 0:RWKV_HEAD, 0:RWKV_HEAD]
        sout[0, 2 * p + 1] = s_scr[p, RWKV_HEAD:2 * RWKV_HEAD, RWKV_HEAD:2 * RWKV_HEAD]


def _rwkv(z2, shift_prev, wkv0, pw, layer, b, l, prev_stack, out_dtype):
    vec = lambda n: _const_spec((1, n), lambda i: (0, 0))
    mat = lambda r, c: _const_spec((r, c), lambda i: (0, 0))
    params = [pw["mu"], pw["lora_w"], pw["w0"], pw["lora_a"], pw["a0"], pw["lora_g"],
              pw["k_k"], pw["k_a"], pw["r_k"], pw["gn_w"], pw["gn_b"]]
    pspecs = [vec(RWKV_PAD), mat(LORA_W, GROUP_W), vec(GROUP_W), mat(LORA_W, GROUP_W), vec(GROUP_W),
              mat(LORA_W, GROUP_W), vec(GROUP_W), vec(GROUP_W), vec(GROUP_W), vec(GROUP_W), vec(GROUP_W)]
    if l % RWKV_TILE == 0:
        nseq = 1
        z_in = z2.reshape(b, l, Z_COLS)
        shift_prev = shift_prev.reshape(b, 1, RWKV_PAD)
        shspec = pl.BlockSpec((1, 1, RWKV_PAD), lambda i: (i, 0, 0))
        body = functools.partial(_rwkv_chunk_kernel, n_tiles=l // RWKV_TILE)
        zspec = pl.BlockSpec((1, l, RWKV_PAD), lambda i: (i, 0, Z_RWKV // RWKV_PAD))
        yspec = pl.BlockSpec((1, l, GROUP_W), lambda i: (i, 0, 0))
        yshape = jax.ShapeDtypeStruct((b, l, GROUP_W), out_dtype)
        scratch = [pltpu.VMEM((RWKV_TILE + 8, RWKV_PAD), F32), pltpu.VMEM((RWKV_PAIRS, 128, 128), F32)]
    else:
        nseq = SHORT_SEQS
        z_in = z2
        shspec = pl.BlockSpec((nseq, RWKV_PAD), lambda i: (i, 0))
        body = functools.partial(_rwkv_scan_kernel, nseq=nseq, l=l)
        zspec = pl.BlockSpec((nseq * l, RWKV_PAD), lambda i: (i, Z_RWKV // RWKV_PAD))
        yspec = pl.BlockSpec((nseq * l, GROUP_W), lambda i: (i, 0))
        yshape = jax.ShapeDtypeStruct((b * l, GROUP_W), out_dtype)
        scratch = [pltpu.VMEM((nseq * l + 8, RWKV_PAD), F32)]
    first = prev_stack is None
    tail = (nseq, RWKV_HEADS, RWKV_HEAD, RWKV_HEAD)
    y, s_new = _call_stacked(
        functools.partial(body, fill_layer=layer if first else None), [z_in, shift_prev, wkv0] + params,
        [zspec, shspec, _stack_out_spec(tail, layer, False)] + pspecs,
        1, prev_stack,
        grid=(b // nseq,),
        out_specs=[yspec, _stack_out_spec(tail, layer, first)],
        out_shape=[yshape, jax.ShapeDtypeStruct((DEPTH, b, RWKV_HEADS, RWKV_HEAD, RWKV_HEAD), F32)],
        scratch_shapes=scratch,
        compiler_params=_cparams(("parallel",)),
        name="rwkv7")
    return y.reshape(b * l, GROUP_W), s_new


def _sgu_kernel(z_ref, lnw_ref, lnb_ref, wm_ref, bias_ref, nrm_ref, y_ref, v_ref, *, blk):
    z = z_ref[...]
    zg = 0.5 * z * (1.0 + lax.erf(z * 0.7071067811865476))
    u = zg[:, :GROUP_W]
    v = zg[:, GROUP_W:]
    mu = jnp.mean(v, axis=-1, keepdims=True)
    vc = v - mu
    var = jnp.mean(vc * vc, axis=-1, keepdims=True)
    vn = vc * lax.rsqrt(var + LN_EPS) * lnw_ref[...] + lnb_ref[...]
    v_ref[...] = vn
    row = lax.broadcasted_iota(jnp.int32, (SGU_CHUNK, SGU_CHUNK), 0)
    col = lax.broadcasted_iota(jnp.int32, (SGU_CHUNK, SGU_CHUNK), 1)
    mask = col <= row
    if blk < SGU_CHUNK:
        mask = mask & ((row // blk) == (col // blk))
    vb = vn.astype(BF16)
    wms = [jnp.where(mask, wm_ref[h], 0.0).astype(BF16) for h in range(SGU_HEADS)]
    rows = []
    for c in range(z.shape[0] // SGU_CHUNK):
        rs = slice(c * SGU_CHUNK, (c + 1) * SGU_CHUNK)
        rows.append(jnp.concatenate(
            [jnp.dot(wms[h], vb[rs, h * 128:(h + 1) * 128], preferred_element_type=F32) + bias_ref[:, h:h + 1]
             for h in range(SGU_HEADS)], axis=1))
    s = rows[0] if len(rows) == 1 else jnp.concatenate(rows, axis=0)
    y_ref[...] = _rms(u * s, nrm_ref[...]).astype(y_ref.dtype)


def _sgu(z2, pw, wm, bias, blk, out_dtype):
    t = z2.shape[0]
    tile = SGU_TILE_CHUNKS * SGU_CHUNK
    vec = _const_spec((1, GROUP_W), lambda i: (0, 0))
    return pl.pallas_call(
        functools.partial(_sgu_kernel, blk=blk),
        grid=(t // tile,),
        in_specs=[pl.BlockSpec((tile, SGU_COLS), lambda i: (i, Z_SGU // SGU_COLS)),
                  vec, vec,
                  _const_spec((SGU_HEADS, SGU_CHUNK, SGU_CHUNK), lambda i: (0, 0, 0)),
                  _const_spec((SGU_CHUNK, SGU_HEADS), lambda i: (0, 0)),
                  vec],
        out_specs=[pl.BlockSpec((tile, GROUP_W), lambda i: (i, 0)),
                   pl.BlockSpec((tile, GROUP_W), lambda i: (i, 0))],
        out_shape=[jax.ShapeDtypeStruct((t, GROUP_W), out_dtype), jax.ShapeDtypeStruct((t, GROUP_W), F32)],
        compiler_params=_cparams(("parallel",)),
        name="sgu",
    )(z2, pw["ln_w"], pw["ln_b"], wm, bias, pw["norm"])


def _hgrn_chunks(seqs, lb, nw, chunk, sub):
    nh = HGRN_HEADS
    hs = lambda x, h: x[:, h * HGRN_HEAD:(h + 1) * HGRN_HEAD]
    tri = (lax.broadcasted_iota(jnp.int32, (chunk, chunk), 1)
           <= lax.broadcasted_iota(jnp.int32, (chunk, chunk), 0)).astype(BF16)
    eye = (lax.broadcasted_iota(jnp.int32, (HGRN_HEAD, HGRN_HEAD), 0)
           == lax.broadcasted_iota(jnp.int32, (HGRN_HEAD, HGRN_HEAD), 1)).astype(F32)
    rowid = lax.broadcasted_iota(jnp.int32, (sub, 1), 0)
    d = functools.partial(jnp.dot, preferred_element_type=F32)
    nt = lambda x, y: lax.dot_general(x, y, (((1,), (1,)), ((), ())), preferred_element_type=F32)
    tn = lambda x, y: lax.dot_general(x, y, (((0,), (0,)), ((), ())), preferred_element_type=F32)

    pre = []
    for zq, zf, zi, zg, states in seqs:
        q = _silu(zq)
        fg = lb + (1.0 - lb) * _sigmoid(zf)
        k = 1.0 - fg
        b = _dot_f32_rhs(tri, jnp.log(fg))
        pre.append(dict(q=q, k=k, v=zi, vb=zi.astype(BF16), b=b, qe=(q * jnp.exp(b)).astype(BF16)))
    o = [[d(hs(u["qe"], h), st[h].astype(BF16)) for h in range(nh)] for u, (_, _, _, _, st) in zip(pre, seqs)]
    outs = [[[] for _ in range(nh)] for _ in seqs]
    for i in range(chunk // sub):
        lo_, hi_ = i * sub, (i + 1) * sub
        oi = [[o[n][h][lo_:hi_] for h in range(nh)] for n in range(len(seqs))]
        if i > 0:
            for n, u in enumerate(pre):
                bref = u["b"][lo_ - 1:lo_]
                qt = (u["q"][lo_:hi_] * jnp.exp(u["b"][lo_:hi_] - bref)).astype(BF16)
                kt = (u["k"][:lo_] * jnp.exp(bref - u["b"][:lo_])).astype(BF16)
                att = [nt(hs(qt, h), hs(kt, h)).astype(BF16) for h in range(nh)]
                for h in range(nh):
                    oi[n][h] = oi[n][h] + d(att[h], hs(u["vb"], h)[:lo_])
        tiles = []
        for u in pre:
            qi, bi = u["q"][lo_:hi_], u["b"][lo_:hi_]
            for j in range(sub):
                sj = lo_ + j
                pj = qi * jnp.exp(jnp.minimum(bi - u["b"][sj:sj + 1], 0.0)) * u["k"][sj:sj + 1]
                tiles.extend(hs(pj, h) for h in range(nh))
        att = _group_sums(jnp.concatenate(tiles, axis=0), HGRN_HEAD, terms=2)
        t = 0
        for n, u in enumerate(pre):
            for j in range(sub):
                sj = lo_ + j
                for h in range(nh):
                    a_j = jnp.where(rowid >= j, att[t * sub:(t + 1) * sub], 0.0)
                    oi[n][h] = oi[n][h] + a_j * hs(u["v"], h)[sj:sj + 1]
                    t += 1
        for n in range(len(seqs)):
            for h in range(nh):
                outs[n][h].append(oi[n][h])
    results = []
    for n, (u, (_, _, _, zg, st)) in enumerate(zip(pre, seqs)):
        o_n = jnp.concatenate([outs[n][h][0] if len(outs[n][h]) == 1 else jnp.concatenate(outs[n][h], axis=0)
                               for h in range(nh)], axis=1)
        blast = u["b"][chunk - 1:chunk]
        kd = (u["k"] * jnp.exp(blast - u["b"])).astype(BF16)
        e_last = jnp.exp(blast)
        new_states = []
        for h in range(nh):
            e_col = _group_sums(eye * hs(e_last, h), HGRN_HEAD)
            new_states.append(e_col * st[h] + tn(hs(kd, h), hs(u["vb"], h)))
        ms = _group_sums(o_n * o_n, HGRN_HEAD) * (1.0 / HGRN_HEAD)
        results.append((o_n * lax.rsqrt(ms + NORM_EPS) * nw * _silu(zg), new_states))
    return results


def _hgrn_kernel(zq_ref, zf_ref, zi_ref, zg_ref, lg_ref, nw_ref, s0_ref, y_ref, sout_ref, s_scr,
                 *, layer, nseq, chunk, sub, n_chunks, fill_layer):
    lg = lg_ref[...]
    e = jnp.exp(lg - jnp.max(lg, axis=0, keepdims=True))
    sm = e / jnp.sum(e, axis=0, keepdims=True)
    lb = jnp.sum(sm[0:layer + 1], axis=0, keepdims=True) - sm[0:1]
    nw = nw_ref[...]
    nh = HGRN_HEADS
    sout = _own_layer(sout_ref, fill_layer)
    if n_chunks == 1:
        rows = lambda ref, b: ref[b * chunk:(b + 1) * chunk, :]
        seqs = [(rows(zq_ref, b), rows(zf_ref, b), rows(zi_ref, b), rows(zg_ref, b),
                 [s0_ref[b, h] for h in range(nh)]) for b in range(nseq)]
        ys = []
        for b, (y, states) in enumerate(_hgrn_chunks(seqs, lb, nw, chunk, sub)):
            ys.append(y)
            for h in range(nh):
                sout[b, h] = states[h]
        y_ref[...] = jnp.concatenate(ys, axis=0).astype(y_ref.dtype)
    else:
        s_scr[...] = s0_ref[...]

        def chunk_body(c, carry):
            r0 = pl.multiple_of(c * chunk, chunk)
            rows = lambda ref, b: ref[b, pl.ds(r0, chunk), :]
            seqs = [(rows(zq_ref, b), rows(zf_ref, b), rows(zi_ref, b), rows(zg_ref, b),
                     [s_scr[b, h] for h in range(nh)]) for b in range(nseq)]
            for b, (y, states) in enumerate(_hgrn_chunks(seqs, lb, nw, chunk, sub)):
                y_ref[b, pl.ds(r0, chunk), :] = y.astype(y_ref.dtype)
                for h in range(nh):
                    s_scr[b, h] = states[h]
            return carry

        lax.fori_loop(0, n_chunks, chunk_body, 0)
        sout[...] = s_scr[...]


def _hgrn(z2, s0, lb_logits, norm_w, layer, b, l, prev_stack, out_dtype):
    chunk = min(HGRN_CHUNK, l)
    sub = min(HGRN_SUB, chunk)
    cb = Z_HGRN // GROUP_W
    if l > chunk:
        nseq = 1
        z_in = z2.reshape(b, l, Z_COLS)
        zspec = lambda part: pl.BlockSpec((1, l, GROUP_W), lambda i: (i, 0, cb + part))
        yspec = pl.BlockSpec((1, l, GROUP_W), lambda i: (i, 0, 0))
        yshape = jax.ShapeDtypeStruct((b, l, GROUP_W), out_dtype)
    else:
        nseq = SHORT_SEQS
        z_in = z2
        zspec = lambda part: pl.BlockSpec((nseq * l, GROUP_W), lambda i: (i, cb + part))
        yspec = pl.BlockSpec((nseq * l, GROUP_W), lambda i: (i, 0))
        yshape = jax.ShapeDtypeStruct((b * l, GROUP_W), out_dtype)
    first = prev_stack is None
    tail = (nseq, HGRN_HEADS, HGRN_HEAD, HGRN_HEAD)
    y, s_new = _call_stacked(
        functools.partial(_hgrn_kernel, layer=layer, nseq=nseq, chunk=chunk, sub=sub, n_chunks=l // chunk,
                          fill_layer=layer if first else None),
        [z_in, z_in, z_in, z_in, lb_logits, norm_w, s0],
        [zspec(0), zspec(1), zspec(2), zspec(3),
         _const_spec((DEPTH, GROUP_W), lambda i: (0, 0)),
         _const_spec((1, GROUP_W), lambda i: (0, 0)),
         _stack_out_spec(tail, layer, False)],
        1, prev_stack,
        grid=(b // nseq,),
        out_specs=[yspec, _stack_out_spec(tail, layer, first)],
        out_shape=[yshape, jax.ShapeDtypeStruct((DEPTH, b, HGRN_HEADS, HGRN_HEAD, HGRN_HEAD), F32)],
        scratch_shapes=[pltpu.VMEM((nseq, HGRN_HEADS, HGRN_HEAD, HGRN_HEAD), F32)],
        compiler_params=_cparams(("parallel",)),
        name="hgrn2")
    return y.reshape(b * l, GROUP_W), s_new


def _pool_kernel(z_ref, hist_ref, pw_ref, ps_ref, y_ref, xbuf, s2buf, s4buf, s8buf,
                 *, nseq, tc, n_tiles, start_pos, flat):
    halo = POOL_HALO
    n = tc + halo
    c1 = POOL_CH
    for b in range(nseq):
        xbuf[b, pl.ds(0, 16), :] = jnp.zeros((16, GROUP_W), F32)
        xbuf[b, pl.ds(16, 16), :] = hist_ref[b]

    def tile(r0, b, z):
        xbuf[b, pl.ds(halo, tc), :] = z
        s2buf[b, pl.ds(8, n - 8), :] = xbuf[b, pl.ds(8, n - 8), :] + xbuf[b, pl.ds(7, n - 8), :]
        s4buf[b, pl.ds(16, n - 16), :] = s2buf[b, pl.ds(16, n - 16), c1:] + s2buf[b, pl.ds(14, n - 16), c1:]
        s8buf[b, pl.ds(24, n - 24), :] = s4buf[b, pl.ds(24, n - 24), c1:] + s4buf[b, pl.ds(20, n - 24), c1:]
        s16 = s8buf[b, pl.ds(halo, tc), c1:] + s8buf[b, pl.ds(halo - 8, tc), c1:]
        sums = (s2buf[b, pl.ds(halo, tc), 0:c1], s4buf[b, pl.ds(halo, tc), 0:c1], s8buf[b, pl.ds(halo, tc), 0:c1],
                s16)
        pos = start_pos + r0 + lax.broadcasted_iota(jnp.int32, (tc, 1), 0)
        ys = []
        for gi, win in enumerate(POOL_WINDOWS):
            sl = slice(gi * POOL_CH, (gi + 1) * POOL_CH)
            cnt = jnp.minimum(pos + 1, win).astype(F32)
            dlt = sums[gi] / cnt - z[:, sl]
            ys.append(jnp.dot(dlt.astype(BF16), pw_ref[gi], preferred_element_type=F32) * ps_ref[:, sl])
        if n_tiles > 1:
            xbuf[b, pl.ds(16, 16), :] = xbuf[b, pl.ds(n - 16, 16), :]
        return jnp.concatenate(ys, axis=1)

    if flat:
        y_ref[...] = jnp.concatenate([tile(0, b, z_ref[b * tc:(b + 1) * tc, :]) for b in range(nseq)],
                                     axis=0).astype(y_ref.dtype)
    else:
        def tile_body(ti, carry):
            r0 = pl.multiple_of(ti * tc, tc)
            for b in range(nseq):
                y_ref[b, pl.ds(r0, tc), :] = tile(r0, b, z_ref[b, pl.ds(r0, tc), :]).astype(y_ref.dtype)
            return carry

        lax.fori_loop(0, n_tiles, tile_body, 0)


def _pool(z2, hist16, pool_w, pool_scale, start_pos, b, l, tc, out_dtype):
    cb = Z_POOL // GROUP_W
    if l >= POOL_HALO:
        nseq = 1
        z_in = z2.reshape(b, l, Z_COLS)
        zspec = pl.BlockSpec((1, l, GROUP_W), lambda i: (i, 0, cb))
        yspec = pl.BlockSpec((1, l, GROUP_W), lambda i: (i, 0, 0))
        yshape = jax.ShapeDtypeStruct((b, l, GROUP_W), out_dtype)
    else:
        nseq = SHORT_SEQS
        z_in = z2
        zspec = pl.BlockSpec((nseq * l, GROUP_W), lambda i: (i, cb))
        yspec = pl.BlockSpec((nseq * l, GROUP_W), lambda i: (i, 0))
        yshape = jax.ShapeDtypeStruct((b * l, GROUP_W), out_dtype)
    n = tc + POOL_HALO
    y = pl.pallas_call(
        functools.partial(_pool_kernel, nseq=nseq, tc=tc, n_tiles=l // tc, start_pos=start_pos, flat=nseq > 1),
        grid=(b // nseq,),
        in_specs=[zspec,
                  pl.BlockSpec((nseq, 16, GROUP_W), lambda i: (i, 0, 0)),
                  _const_spec((len(POOL_WINDOWS), POOL_CH, POOL_CH), lambda i: (0, 0, 0)),
                  _const_spec((1, GROUP_W), lambda i: (0, 0))],
        out_specs=yspec,
        out_shape=yshape,
        scratch_shapes=[pltpu.VMEM((nseq, n, GROUP_W), F32), pltpu.VMEM((nseq, n, GROUP_W), F32),
                        pltpu.VMEM((nseq, n, 3 * POOL_CH), F32), pltpu.VMEM((nseq, n, 2 * POOL_CH), F32)],
        compiler_params=_cparams(("parallel",)),
        name="pool",
    )(z_in, hist16, pool_w, pool_scale)
    return y.reshape(b * l, GROUP_W)


def _pad_cols(a, n):
    return jnp.pad(a, ((0, 0),) * (a.ndim - 1) + ((0, n - a.shape[-1]),))


def _prep_weights(W):
    c1, c2, c3 = RWKV_COLS, RWKV_COLS + SGU_COLS, RWKV_COLS + SGU_COLS + HGRN_COLS
    w_in = W["w_in"]
    w_in_p = jnp.concatenate(
        [w_in[:, :, c2:c3], w_in[:, :, c1:c2], w_in[:, :, c3:], _pad_cols(w_in[:, :, :c1], RWKV_PAD)],
        axis=2).astype(BF16)
    return dict(w_in=w_in_p, w_out=W["w_out"].astype(BF16), w_gu=W["ffn_w_gu"].astype(BF16),
                w_down=W["ffn_w_down"].astype(BF16), ple_gate=W["ple_gate"].astype(BF16),
                ple_proj=W["ple_proj"].astype(BF16))


def _prep_layer(i, W):
    row = lambda a: a.reshape(1, -1)
    lora = lambda a, off: jnp.pad(a, ((off, LORA_W - off - a.shape[0]), (0, 0))).astype(BF16)
    rw = dict(
        mu=_pad_cols(row(W["rwkv_mu"][i]), RWKV_PAD),
        lora_w=lora(W["rwkv_w_lora"][i], 0), w0=row(W["rwkv_w0"][i]),
        lora_a=lora(W["rwkv_a_lora"][i], W_LORA), a0=row(W["rwkv_a0"][i]),
        lora_g=lora(W["rwkv_g_lora"][i], W_LORA + A_LORA),
        k_k=row(W["rwkv_k_k"][i]), k_a=row(W["rwkv_k_a"][i]), r_k=row(W["rwkv_r_k"][i]),
        gn_w=row(W["rwkv_gn_w"][i]), gn_b=row(W["rwkv_gn_b"][i]))
    sg = dict(ln_w=row(W["sgu_ln_w"][i]), ln_b=row(W["sgu_ln_b"][i]), norm=row(W["sgu_norm"][i]))
    return dict(
        rw=rw, sg=sg, sgu_w=W["sgu_w"][i], sgu_b=W["sgu_b"][i],
        hgrn_norm=row(W["hgrn_norm"][i]), pool_w=W["pool_w"][i].astype(BF16), pool_scale=row(W["pool_scale"][i]),
        ln_mix_pre=row(W["ln_mix_pre"][i]), ln_mix_post=row(W["ln_mix_post"][i]),
        ln_ffn_pre=row(W["ln_ffn_pre"][i]), ln_ffn_post=row(W["ln_ffn_post"][i]))


def _trunk(x, p, wkv0, shift0, hgrn0, pool0, start_pos, big, layers, lb_logits, cfg):
    b, l, _ = x.shape
    t = b * l
    xf = x.reshape(t, D_MODEL)
    p2 = p.reshape(DEPTH, t, PLE_DIM)
    wkv_new = hgrn_new = None
    shift_l, pool_l, sgu_l = [], [], []
    for i, lw in enumerate(layers):
        z2 = _inproj(xf, lw["ln_mix_pre"], big["w_in"], i, min(cfg["tm_in"], t), cfg["tn_in"])
        y_r, wkv_new = _rwkv(z2, _pad_cols(shift0[i], RWKV_PAD), wkv0, lw["rw"], i, b, l, wkv_new, cfg["ydt"])
        if cfg["sgu_blk"] == SGU_CHUNK:
            wm, bias = lw["sgu_w"], lw["sgu_b"].T
        else:
            rep = SGU_CHUNK // l
            wm = jnp.tile(lw["sgu_w"][:, :l, :l], (1, rep, rep))
            bias = jnp.tile(lw["sgu_b"][:, :l].T, (rep, 1))
        y_s, v_rows = _sgu(z2, lw["sg"], wm, bias, cfg["sgu_blk"], cfg["ydt"])
        y_h, hgrn_new = _hgrn(z2, hgrn0, lb_logits, lw["hgrn_norm"], i, b, l, hgrn_new, cfg["ydt"])
        hist16 = jnp.pad(pool0[i], ((0, 0), (1, 0), (0, 0)))
        y_p = _pool(z2, hist16, lw["pool_w"], lw["pool_scale"], start_pos, b, l, cfg["tc_pool"], cfg["ydt"])
        xf = _outproj([y_r, y_s, y_h, y_p], xf, big["w_out"], lw["ln_mix_post"], i, cfg["tm"])
        xf = _ffn(xf, lw["ln_ffn_pre"], big["w_gu"], big["w_down"], lw["ln_ffn_post"], p2, big["ple_gate"],
                  big["ple_proj"], i, cfg["tm"], cfg["tf"])
        shift_l.append(z2[l - 1::l, Z_RWKV:Z_RWKV + RWKV_COLS])
        tail = min(l, POOL_HIST)
        z_tail = z2[:, Z_POOL:Z_POOL + GROUP_W].reshape(b, l, GROUP_W)[:, l - tail:]
        pool_l.append(jnp.concatenate([pool0[i][:, tail:], z_tail], axis=1))
        sgu_l.append(v_rows.reshape(b, l, GROUP_W))
    return (xf.reshape(b, l, D_MODEL), wkv_new, jnp.stack(shift_l), hgrn_new, jnp.stack(pool_l), jnp.stack(sgu_l))


def _cfg(l):
    if l >= 256:
        return dict(tm=512, tm_in=1024, tn_in=1792, tf=512, sgu_blk=SGU_CHUNK, tc_pool=256, ydt=BF16)
    return dict(tm=512, tm_in=512, tn_in=1792, tf=512, sgu_blk=l, tc_pool=l, ydt=BF16)


def kernel(x_prompt, x_sample, state_rwkv_wkv, state_rwkv_shift, state_hgrn, state_pool, p_prompt, p_sample,
           ln_mix_pre, ln_mix_post, ln_ffn_pre, ln_ffn_post, w_in, rwkv_mu, rwkv_w_lora, rwkv_w0, rwkv_a_lora,
           rwkv_a0, rwkv_g_lora, rwkv_k_k, rwkv_k_a, rwkv_r_k, rwkv_gn_w, rwkv_gn_b, sgu_ln_w, sgu_ln_b, sgu_w,
           sgu_b, sgu_norm, hgrn_lb_logits, hgrn_norm, pool_w, pool_scale, w_out, ffn_w_gu, ffn_w_down, ple_gate,
           ple_proj):
    W = dict(ln_mix_pre=ln_mix_pre, ln_mix_post=ln_mix_post, ln_ffn_pre=ln_ffn_pre, ln_ffn_post=ln_ffn_post,
             w_in=w_in, rwkv_mu=rwkv_mu, rwkv_w_lora=rwkv_w_lora, rwkv_w0=rwkv_w0, rwkv_a_lora=rwkv_a_lora,
             rwkv_a0=rwkv_a0, rwkv_g_lora=rwkv_g_lora, rwkv_k_k=rwkv_k_k, rwkv_k_a=rwkv_k_a,
             rwkv_r_k=rwkv_r_k.reshape(DEPTH, GROUP_W), rwkv_gn_w=rwkv_gn_w, rwkv_gn_b=rwkv_gn_b, sgu_ln_w=sgu_ln_w,
             sgu_ln_b=sgu_ln_b, sgu_w=sgu_w, sgu_b=sgu_b, sgu_norm=sgu_norm, hgrn_norm=hgrn_norm, pool_w=pool_w,
             pool_scale=pool_scale, w_out=w_out, ffn_w_gu=ffn_w_gu, ffn_w_down=ffn_w_down, ple_gate=ple_gate,
             ple_proj=ple_proj)
    big = _prep_weights(W)
    layers = [_prep_layer(i, W) for i in range(DEPTH)]
    lb_logits = hgrn_lb_logits.astype(F32)
    bp, lp, _ = x_prompt.shape
    zeros = lambda *s: jnp.zeros((DEPTH, bp) + s, F32)
    y_prompt, wkv_p, shift_p, hgrn_p, pool_p, _ = _trunk(
        x_prompt, p_prompt, zeros(RWKV_HEADS, RWKV_HEAD, RWKV_HEAD), zeros(RWKV_COLS),
        zeros(HGRN_HEADS, HGRN_HEAD, HGRN_HEAD), zeros(POOL_HIST, GROUP_W), 0, big, layers, lb_logits, _cfg(lp))
    y_sample, wkv_s, shift_s, hgrn_s, pool_s, sgu_v_s = _trunk(
        x_sample, p_sample, state_rwkv_wkv, state_rwkv_shift, state_hgrn, state_pool, PAST_LEN, big, layers,
        lb_logits, _cfg(x_sample.shape[1]))
    return (y_prompt, y_sample, wkv_p, shift_p, hgrn_p, pool_p, wkv_s, shift_s, hgrn_s, pool_s, sgu_v_s)
```

```python
import functools

import jax
import jax.numpy as jnp
from jax import lax
from jax.experimental import pallas as pl
from jax.experimental.pallas import tpu as pltpu

F32 = jnp.float32
BF16 = jnp.bfloat16

D_MODEL = 2048
DEPTH = 2
PAST_LEN = 16384
GROUP_W = 512
RWKV_HEAD = 64
RWKV_HEADS = 8
RWKV_PAIRS = RWKV_HEADS // 2
RWKV_CHUNK = 64
RWKV_TILE_CHUNKS = 4
RWKV_TILE = RWKV_TILE_CHUNKS * RWKV_CHUNK
W_LORA, A_LORA, G_LORA = 32, 32, 96
RWKV_GN_EPS = 64e-5
SGU_CHUNK = 128
SGU_HEADS = 4
SGU_TILE_CHUNKS = 4
HGRN_HEADS = 4
HGRN_HEAD = 128
HGRN_CHUNK = 64
HGRN_SUB = 16
POOL_WINDOWS = (2, 4, 8, 16)
POOL_CH = 128
POOL_HIST = 15
POOL_HALO = 32
D_FF = 5632
PLE_DIM = 256
PLE_COLS = 512
NORM_EPS = 1e-6
LN_EPS = 1e-5
RWKV_COLS = 3 * GROUP_W + W_LORA + A_LORA + G_LORA
RWKV_PAD = 1792
LORA_OFF = 3 * GROUP_W
LORA_W = RWKV_PAD - LORA_OFF
SGU_COLS = 2 * GROUP_W
HGRN_COLS = 4 * GROUP_W
Z_HGRN, Z_SGU, Z_POOL, Z_RWKV = 0, 2048, 3072, 3584
Z_COLS = Z_RWKV + RWKV_PAD
SHORT_SEQS = 8
VMEM_LIMIT = 56 * 1024 * 1024


def _cparams(sem):
    return pltpu.CompilerParams(dimension_semantics=sem, vmem_limit_bytes=VMEM_LIMIT)


def _const_spec(shape, index_map):
    return pl.BlockSpec(shape, index_map, pipeline_mode=pl.Buffered(1))


def _sigmoid(x):
    return jax.nn.sigmoid(x)


def _silu(x):
    return x * jax.nn.sigmoid(x)


def _split3(x):
    hi = x.astype(BF16)
    r = x - hi.astype(F32)
    mid = r.astype(BF16)
    lo = (r - mid.astype(F32)).astype(BF16)
    return hi, mid, lo


def _group_sums(x, width, terms=3):
    rows, cols = x.shape
    nblk = cols // 128
    xs = x if nblk == 1 else jnp.concatenate([x[:, i * 128:(i + 1) * 128] for i in range(nblk)], axis=0)
    ones = ((lax.broadcasted_iota(jnp.int32, (128, 128), 0) // width)
            == (lax.broadcasted_iota(jnp.int32, (128, 128), 1) // width)).astype(BF16)
    s = None
    for part in _split3(xs)[:terms]:
        y = jnp.dot(part, ones, preferred_element_type=F32)
        s = y if s is None else s + y
    return s if nblk == 1 else jnp.concatenate([s[i * rows:(i + 1) * rows] for i in range(nblk)], axis=1)


def _dot_f32_rhs(m, x):
    hi, mid, lo = _split3(x)
    d = functools.partial(jnp.dot, preferred_element_type=F32)
    return d(m, hi) + d(m, mid) + d(m, lo)


def _rms(x, g):
    return x * lax.rsqrt(jnp.mean(x * x, axis=-1, keepdims=True) + NORM_EPS) * g


def _call_stacked(kernel, inputs, in_specs, stack_out, prev_stack, **kw):
    if prev_stack is None:
        return pl.pallas_call(kernel, in_specs=in_specs, **kw)(*inputs)
    n_in = len(inputs)

    def body(*refs):
        kernel(*refs[:n_in], *refs[n_in + 1:])

    return pl.pallas_call(body, in_specs=list(in_specs) + [pl.BlockSpec(memory_space=pl.ANY)],
                          input_output_aliases={n_in: stack_out}, **kw)(*inputs, prev_stack)


def _stack_out_spec(tail, layer, first):
    zeros = (0,) * (len(tail) - 1)
    if first:
        return pl.BlockSpec((DEPTH,) + tail, lambda i: (0, i) + zeros)
    return pl.BlockSpec((None,) + tail, lambda i: (layer, i) + zeros)


def _own_layer(sout_ref, fill_layer):
    if fill_layer is None:
        return sout_ref
    for j in range(DEPTH):
        if j != fill_layer:
            sout_ref[j] = jnp.zeros(sout_ref.shape[1:], sout_ref.dtype)
    return sout_ref.at[fill_layer]


def _inproj_kernel(x_ref, g_ref, w_ref, z_ref, h_ref):
    @pl.when(pl.program_id(1) == 0)
    def _():
        h_ref[...] = _rms(x_ref[...], g_ref[...]).astype(BF16)

    z_ref[...] = jnp.dot(h_ref[...], w_ref[...], preferred_element_type=F32)


def _inproj(x, g, w, layer, tm, tn):
    t = x.shape[0]
    return pl.pallas_call(
        _inproj_kernel,
        grid=(t // tm, Z_COLS // tn),
        in_specs=[pl.BlockSpec((tm, D_MODEL), lambda i, j: (i, 0)),
                  _const_spec((1, D_MODEL), lambda i, j: (0, 0)),
                  pl.BlockSpec((None, D_MODEL, tn), lambda i, j: (layer, 0, j))],
        out_specs=pl.BlockSpec((tm, tn), lambda i, j: (i, j)),
        out_shape=jax.ShapeDtypeStruct((t, Z_COLS), F32),
        scratch_shapes=[pltpu.VMEM((tm, D_MODEL), BF16)],
        compiler_params=_cparams(("parallel", "arbitrary")),
        name="inproj",
    )(x, g, w)


def _outproj_kernel(yr_ref, ys_ref, yh_ref, yp_ref, x_ref, w_ref, g_ref, o_ref):
    d = functools.partial(jnp.dot, preferred_element_type=F32)
    g4 = GROUP_W
    mix = (d(yr_ref[...], w_ref[0:g4, :]) + d(ys_ref[...], w_ref[g4:2 * g4, :])
           + d(yh_ref[...], w_ref[2 * g4:3 * g4, :]) + d(yp_ref[...], w_ref[3 * g4:4 * g4, :]))
    o_ref[...] = x_ref[...] + _rms(mix, g_ref[...])


def _outproj(ys, x, w, g, layer, tm):
    t = x.shape[0]
    yspec = pl.BlockSpec((tm, GROUP_W), lambda i: (i, 0))
    return pl.pallas_call(
        _outproj_kernel,
        grid=(t // tm,),
        in_specs=[yspec, yspec, yspec, yspec,
                  pl.BlockSpec((tm, D_MODEL), lambda i: (i, 0)),
                  _const_spec((None, D_MODEL, D_MODEL), lambda i: (layer, 0, 0)),
                  _const_spec((1, D_MODEL), lambda i: (0, 0))],
        out_specs=pl.BlockSpec((tm, D_MODEL), lambda i: (i, 0)),
        out_shape=jax.ShapeDtypeStruct((t, D_MODEL), F32),
        compiler_params=_cparams(("parallel",)),
        name="outproj",
    )(*ys, x, w, g)


def _ffn_kernel(x_ref, gpre_ref, wg_ref, wu_ref, wd_ref, gpost_ref, p_ref, wgate_ref, wproj_ref, o_ref, h_ref, acc_ref):
    j = pl.program_id(1)

    @pl.when(j == 0)
    def _():
        h_ref[...] = _rms(x_ref[...], gpre_ref[...]).astype(BF16)
        acc_ref[...] = jnp.zeros_like(acc_ref)

    h = h_ref[...]
    gate = jnp.dot(h, wg_ref[...], preferred_element_type=F32)
    up = jnp.dot(h, wu_ref[...], preferred_element_type=F32)
    act = (_silu(gate) * up).astype(BF16)
    acc_ref[...] += jnp.dot(act, wd_ref[...], preferred_element_type=F32)

    @pl.when(j == pl.num_programs(1) - 1)
    def _():
        acc_ref[...] = x_ref[...] + _rms(acc_ref[...], gpost_ref[...])
        h_ref[...] = acc_ref[...].astype(BF16)
        pb = p_ref[...].astype(BF16)
        for c in range(D_MODEL // PLE_COLS):
            cs = slice(c * PLE_COLS, (c + 1) * PLE_COLS)
            pgate = _sigmoid(jnp.dot(h_ref[...], wgate_ref[:, cs], preferred_element_type=F32))
            ple = jnp.dot(pb, wproj_ref[:, cs], preferred_element_type=F32)
            o_ref[:, cs] = acc_ref[:, cs] + pgate * ple


def _ffn(x, gpre, w_gu, w_down, gpost, p, w_gate, w_proj, layer, tm, tf):
    t = x.shape[0]
    nf = D_FF // tf
    return pl.pallas_call(
        _ffn_kernel,
        grid=(t // tm, nf),
        in_specs=[pl.BlockSpec((tm, D_MODEL), lambda i, j: (i, 0), pipeline_mode=pl.Buffered(1)),
                  _const_spec((1, D_MODEL), lambda i, j: (0, 0)),
                  pl.BlockSpec((None, D_MODEL, tf), lambda i, j: (layer, 0, j)),
                  pl.BlockSpec((None, D_MODEL, tf), lambda i, j: (layer, 0, j + nf)),
                  pl.BlockSpec((None, tf, D_MODEL), lambda i, j: (layer, j, 0)),
                  _const_spec((1, D_MODEL), lambda i, j: (0, 0)),
                  pl.BlockSpec((None, tm, PLE_DIM), lambda i, j: (layer, i, 0)),
                  _const_spec((None, D_MODEL, D_MODEL), lambda i, j: (layer, 0, 0)),
                  _const_spec((None, PLE_DIM, D_MODEL), lambda i, j: (layer, 0, 0))],
        out_specs=pl.BlockSpec((tm, D_MODEL), lambda i, j: (i, 0)),
        out_shape=jax.ShapeDtypeStruct((t, D_MODEL), F32),
        scratch_shapes=[pltpu.VMEM((tm, D_MODEL), BF16), pltpu.VMEM((tm, D_MODEL), F32)],
        compiler_params=_cparams(("parallel", "arbitrary")),
        name="ffn",
    )(x, gpre, w_gu, w_gu, w_down, gpost, p, w_gate, w_proj)


def _rwkv_gates(z, z_prev, mu_ref, lw_ref, w0_ref, la_ref, a0_ref, lg_ref, kk_ref, ka_ref, rk_ref):
    zm = z + mu_ref[...] * (z_prev - z)
    g4 = GROUP_W
    r = zm[:, 0:g4]
    k = zm[:, g4:2 * g4]
    v = zm[:, 2 * g4:3 * g4]
    zl = zm[:, LORA_OFF:RWKV_PAD]
    d = functools.partial(jnp.dot, preferred_element_type=F32)
    wl = w0_ref[...] + d(jnp.tanh(zl).astype(BF16), lw_ref[...])
    w_log = -(jnp.maximum(-wl, 0.0) + jnp.log1p(jnp.exp(-jnp.abs(wl)))) - 0.5
    a = _sigmoid(a0_ref[...] + d(zl.astype(BF16), la_ref[...]))
    g = d(_sigmoid(zl).astype(BF16), lg_ref[...])
    kk = k * kk_ref[...]
    kk = kk / jnp.maximum(jnp.sqrt(_group_sums(kk * kk, RWKV_HEAD)), 1e-12)
    k = k * (1.0 + (a - 1.0) * ka_ref[...])
    bonus = _group_sums(r * k * rk_ref[...], RWKV_HEAD) * v
    return r, k, v, -kk, kk * a, w_log, g, bonus


def _rwkv_post(y, bonus, g, gnw_ref, gnb_ref):
    inv_n = 1.0 / RWKV_HEAD
    mean = _group_sums(y, RWKV_HEAD) * inv_n
    yc = y - mean
    var = _group_sums(yc * yc, RWKV_HEAD) * inv_n
    yn = yc * lax.rsqrt(var + RWKV_GN_EPS) * gnw_ref[...] + gnb_ref[...]
    return (yn + bonus) * g


def _rwkv_scan_kernel(z_ref, sh_ref, s0_ref, mu_ref, lw_ref, w0_ref, la_ref, a0_ref, lg_ref, kk_ref, ka_ref, rk_ref,
                      gnw_ref, gnb_ref, y_ref, sout_ref, shout_ref, zbuf, *, nseq, l, fill_layer):
    rows = nseq * l
    z = z_ref[...]
    shout_ref[...] = jnp.concatenate([z[(b + 1) * l - 1:(b + 1) * l] for b in range(nseq)], axis=0)
    zbuf[pl.ds(0, 8), :] = jnp.zeros((8, RWKV_PAD), F32)
    zbuf[pl.ds(8, rows), :] = z
    rid = lax.broadcasted_iota(jnp.int32, (rows, 1), 0)
    spread = ((lax.broadcasted_iota(jnp.int32, (rows, nseq), 0) // l)
              == lax.broadcasted_iota(jnp.int32, (rows, nseq), 1)).astype(BF16)
    z_prev = jnp.where(rid % l == 0, _dot_f32_rhs(spread, sh_ref[...]), zbuf[pl.ds(7, rows), :])
    r, k, v, a_vec, b_vec, w_log, g, bonus = _rwkv_gates(
        z, z_prev, mu_ref, lw_ref, w0_ref, la_ref, a0_ref, lg_ref, kk_ref, ka_ref, rk_ref)
    w = jnp.exp(-jnp.exp(w_log))

    shp = (RWKV_HEAD, 2 * RWKV_HEAD)
    lane = lax.broadcasted_iota(jnp.int32, shp, 1)
    sub = lax.broadcasted_iota(jnp.int32, shp, 0)
    lo = lane < RWKV_HEAD
    eye = ((lane % RWKV_HEAD) == sub).astype(F32)

    def head_sums(tiles):
        s = _group_sums(jnp.concatenate(tiles, axis=0), RWKV_HEAD, terms=2)
        return [s[u * RWKV_HEAD:(u + 1) * RWKV_HEAD] for u in range(len(tiles))]

    units = [(b, p) for b in range(nseq) for p in range(RWKV_PAIRS)]
    row = lambda x, b, p, j: x[b * l + j:b * l + j + 1, p * 128:(p + 1) * 128]
    state = [jnp.concatenate([s0_ref[b, 2 * p], s0_ref[b, 2 * p + 1]], axis=1) for b, p in units]
    y_rows = {}
    for j in range(l):
        sa = head_sums([s * row(a_vec, b, p, j) for s, (b, p) in zip(state, units)])
        vcol = head_sums([eye * row(v, b, p, j) for b, p in units])
        state = [s * row(w, b, p, j) + sa_u * row(b_vec, b, p, j) + vc_u * row(k, b, p, j)
                 for s, sa_u, vc_u, (b, p) in zip(state, sa, vcol, units)]
        ycol = head_sums([s * row(r, b, p, j) for s, (b, p) in zip(state, units)])
        for yc_u, (b, p) in zip(ycol, units):
            y_rows[b * l + j, p] = jnp.sum(eye * yc_u, axis=0, keepdims=True)
    sout = _own_layer(sout_ref, fill_layer)
    for s, (b, p) in zip(state, units):
        sout[b, 2 * p] = s[:, 0:RWKV_HEAD]
        sout[b, 2 * p + 1] = s[:, RWKV_HEAD:2 * RWKV_HEAD]
    y = jnp.concatenate([jnp.concatenate([y_rows[i, p] for i in range(rows)], axis=0) for p in range(RWKV_PAIRS)],
                        axis=1)
    y_ref[...] = _rwkv_post(y, bonus, g, gnw_ref, gnb_ref).astype(y_ref.dtype)


def _tri_inverse(l_mats, n):
    row = lax.broadcasted_iota(jnp.int32, (n, n), 0)
    col = lax.broadcasted_iota(jnp.int32, (n, n), 1)
    eye = (row == col).astype(F32)
    mm = lambda a, b: jnp.dot(a.astype(BF16), b.astype(BF16), preferred_element_type=F32)
    same = lambda m: (row // m) == (col // m)
    d1 = [jnp.where(same(8), x, 0.0) for x in l_mats]
    d2 = [mm(x, x) for x in d1]
    d4 = [mm(x, x) for x in d2]
    d3 = [mm(x, y) for x, y in zip(d1, d2)]
    t = [eye + x1 + x2 + x3 for x1, x2, x3 in zip(d1, d2, d3)]
    t = [x + mm(x, y) for x, y in zip(t, d4)]
    m = 8
    while m < RWKV_CHUNK:
        e = [jnp.where(same(2 * m) & jnp.logical_not(same(m)), x, 0.0) for x in l_mats]
        te = [mm(x, y) for x, y in zip(t, e)]
        t = [x + mm(y, x) for x, y in zip(t, te)]
        m *= 2
    return t


def _rwkv_chunk_local(chunks):
    c = RWKV_CHUNK
    n = 2 * c
    lo64 = lax.broadcasted_iota(jnp.int32, (c, 128), 1) < RWKV_HEAD
    tri = (lax.broadcasted_iota(jnp.int32, (c, c), 1) <= lax.broadcasted_iota(jnp.int32, (c, c), 0)).astype(BF16)
    stack = lambda x: jnp.concatenate([jnp.where(lo64, x, 0.0), jnp.where(lo64, 0.0, x)], axis=0)
    ops = []
    for r, k, v, a_vec, b_vec, logw in chunks:
        cum = _dot_f32_rhs(tri, logw)
        c_end = cum[c - 1:c]
        e_neg = jnp.exp(-cum)
        e_end = jnp.exp(c_end - cum)
        at = a_vec * jnp.exp(cum - logw)
        rt = r * jnp.exp(cum)
        bt = b_vec * e_neg
        kt = k * e_neg
        bh = b_vec * e_end
        kh = k * e_end
        e_cend = jnp.exp(c_end)
        for p in range(RWKV_PAIRS):
            sl = slice(p * 128, (p + 1) * 128)
            ops.append(dict(xa=stack(at[:, sl]), xr=stack(rt[:, sl]), yb=stack(bt[:, sl]), yk=stack(kt[:, sl]),
                            vs=stack(v[:, sl]).astype(BF16), xbh=stack(bh[:, sl]), xkh=stack(kh[:, sl]),
                            e_cend=e_cend[:, sl]))

    row = lax.broadcasted_iota(jnp.int32, (n, n), 0)
    col = lax.broadcasted_iota(jnp.int32, (n, n), 1)
    same_head = (row // c) == (col // c)
    strict = same_head & (col < row)
    incl = same_head & (col <= row)
    zeros = jnp.zeros((n, n), BF16)
    mm = lambda x, y: jnp.dot(x, y, preferred_element_type=F32)
    nt = lambda x, y: lax.dot_general(x, y, (((1,), (1,)), ((), ())), preferred_element_type=F32)
    tn = lambda x, y: lax.dot_general(x, y, (((0,), (0,)), ((), ())), preferred_element_type=F32)
    unstack = lambda x: x[0:c] + x[c:n]

    gram = [nt(jnp.concatenate([o["xa"], o["xr"]], axis=0).astype(BF16),
               jnp.concatenate([o["yb"], o["yk"]], axis=0).astype(BF16)) for o in ops]
    l_ab = [jnp.where(strict, x[0:n, 0:n], 0.0) for x in gram]
    l_ak = [jnp.where(strict, x[0:n, n:2 * n], 0.0) for x in gram]
    m_rbk = [jnp.concatenate([jnp.where(incl, x[n:2 * n, 0:n], 0.0), jnp.where(incl, x[n:2 * n, n:2 * n], 0.0)],
                             axis=1).astype(BF16) for x in gram]
    w1 = [mm(x.astype(BF16), o["vs"]) for x, o in zip(l_ak, ops)]
    t_inv = _tri_inverse(l_ab, n)
    tu = [mm(t.astype(BF16), jnp.concatenate([o["xa"], w], axis=1).astype(BF16))
          for t, o, w in zip(t_inv, ops, w1)]
    zmat = [jnp.concatenate([x.astype(BF16), jnp.concatenate([zeros, o["vs"]], axis=1)], axis=0)
            for x, o in zip(tu, ops)]
    ry = [mm(m, zm) for m, zm in zip(m_rbk, zmat)]
    ps = [tn(zm, jnp.concatenate([o["xbh"], o["xkh"]], axis=0).astype(BF16))
          for zm, o in zip(zmat, ops)]
    out = [(unstack(o["xr"] + y[:, 0:n]), unstack(y[:, n:2 * n]), o["e_cend"], q[0:n], q[n:2 * n])
           for o, y, q in zip(ops, ry, ps)]
    return [out[ci * RWKV_PAIRS:(ci + 1) * RWKV_PAIRS] for ci in range(len(chunks))]


def _rwkv_chunk_kernel(z_ref, sh_ref, s0_ref, mu_ref, lw_ref, w0_ref, la_ref, a0_ref, lg_ref, kk_ref, ka_ref, rk_ref,
                       gnw_ref, gnb_ref, y_ref, sout_ref, shout_ref, zbuf, s_scr, *, n_tiles, fill_layer):
    c = RWKV_CHUNK
    tile = RWKV_TILE
    zero = jnp.zeros((RWKV_HEAD, RWKV_HEAD), F32)
    for p in range(RWKV_PAIRS):
        s_scr[p] = jnp.concatenate([jnp.concatenate([s0_ref[0, 2 * p], zero], axis=1),
                                    jnp.concatenate([zero, s0_ref[0, 2 * p + 1]], axis=1)], axis=0)
    zbuf[pl.ds(0, 8), :] = jnp.broadcast_to(sh_ref[0], (8, RWKV_PAD))
    nt = lambda x, y: lax.dot_general(x, y, (((1,), (1,)), ((), ())), preferred_element_type=F32)

    def tile_body(ti, carry):
        r0 = pl.multiple_of(ti * tile, tile)
        z = z_ref[0, pl.ds(r0, tile), :]
        zbuf[pl.ds(8, tile), :] = z
        z_prev = zbuf[pl.ds(7, tile), :]
        zbuf[pl.ds(7, 1), :] = zbuf[pl.ds(7 + tile, 1), :]
        r, k, v, a_vec, b_vec, w_log, g, bonus = _rwkv_gates(
            z, z_prev, mu_ref, lw_ref, w0_ref, la_ref, a0_ref, lg_ref, kk_ref, ka_ref, rk_ref)
        logw = -jnp.exp(w_log)
        local = _rwkv_chunk_local([tuple(x[ci * c:(ci + 1) * c] for x in (r, k, v, a_vec, b_vec, logw))
                                   for ci in range(RWKV_TILE_CHUNKS)])
        y_cols = []
        for p in range(RWKV_PAIRS):
            s = s_scr[p]
            ys = []
            for ci in range(RWKV_TILE_CHUNKS):
                rc, yloc, e_cend, trans, sloc = local[ci][p]
                sb = s.astype(BF16)
                ys.append(nt(rc.astype(BF16), sb) + yloc)
                s = s * e_cend + jnp.dot(sb, trans.astype(BF16), preferred_element_type=F32) + sloc
            s_scr[p] = s
            y_cols.append(jnp.concatenate(ys, axis=0))
        y = jnp.concatenate(y_cols, axis=1)
        y_ref[0, pl.ds(r0, tile), :] = _rwkv_post(y, bonus, g, gnw_ref, gnb_ref).astype(y_ref.dtype)
        return carry

    lax.fori_loop(0, n_tiles, tile_body, 0)
    shout_ref[0] = zbuf[pl.ds(7, 1), :]
    sout = _own_layer(sout_ref, fill_layer)
    for p in range(RWKV_PAIRS):
        sout[0, 2 * p] = s_scr[p, 0:RWKV_HEAD, 0:RWKV_HEAD]
        sout[0, 2 * p + 1] = s_scr[p, RWKV_HEAD:2 * RWKV_HEAD, RWKV_HEAD:2 * RWKV_HEAD]


def _rwkv(z2, shift_prev, wkv0, pw, layer, b, l, prev_stack, out_dtype):
    vec = lambda n: _const_spec((1, n), lambda i: (0, 0))
    mat = lambda r, c: _const_spec((r, c), lambda i: (0, 0))
    params = [pw["mu"], pw["lora_w"], pw["w0"], pw["lora_a"], pw["a0"], pw["lora_g"],
              pw["k_k"], pw["k_a"], pw["r_k"], pw["gn_w"], pw["gn_b"]]
    pspecs = [vec(RWKV_PAD), mat(LORA_W, GROUP_W), vec(GROUP_W), mat(LORA_W, GROUP_W), vec(GROUP_W),
              mat(LORA_W, GROUP_W), vec(GROUP_W), vec(GROUP_W), vec(GROUP_W), vec(GROUP_W), vec(GROUP_W)]
    if l % RWKV_TILE == 0:
        nseq = 1
        z_in = z2.reshape(b, l, Z_COLS)
        shift_prev = shift_prev.reshape(b, 1, RWKV_PAD)
        shspec = pl.BlockSpec((1, 1, RWKV_PAD), lambda i: (i, 0, 0))
        body = functools.partial(_rwkv_chunk_kernel, n_tiles=l // RWKV_TILE)
        zspec = pl.BlockSpec((1, l, RWKV_PAD), lambda i: (i, 0, Z_RWKV // RWKV_PAD))
        yspec = pl.BlockSpec((1, l, GROUP_W), lambda i: (i, 0, 0))
        yshape = jax.ShapeDtypeStruct((b, l, GROUP_W), out_dtype)
        scratch = [pltpu.VMEM((RWKV_TILE + 8, RWKV_PAD), F32), pltpu.VMEM((RWKV_PAIRS, 128, 128), F32)]
    else:
        nseq = SHORT_SEQS
        z_in = z2
        shspec = pl.BlockSpec((nseq, RWKV_PAD), lambda i: (i, 0))
        body = functools.partial(_rwkv_scan_kernel, nseq=nseq, l=l)
        zspec = pl.BlockSpec((nseq * l, RWKV_PAD), lambda i: (i, Z_RWKV // RWKV_PAD))
        yspec = pl.BlockSpec((nseq * l, GROUP_W), lambda i: (i, 0))
        yshape = jax.ShapeDtypeStruct((b * l, GROUP_W), out_dtype)
        scratch = [pltpu.VMEM((nseq * l + 8, RWKV_PAD), F32)]
    first = prev_stack is None
    tail = (nseq, RWKV_HEADS, RWKV_HEAD, RWKV_HEAD)
    y, s_new, shift_new = _call_stacked(
        functools.partial(body, fill_layer=layer if first else None), [z_in, shift_prev, wkv0] + params,
        [zspec, shspec, _stack_out_spec(tail, layer, False)] + pspecs,
        1, prev_stack,
        grid=(b // nseq,),
        out_specs=[yspec, _stack_out_spec(tail, layer, first), shspec],
        out_shape=[yshape, jax.ShapeDtypeStruct((DEPTH, b, RWKV_HEADS, RWKV_HEAD, RWKV_HEAD), F32),
                   jax.ShapeDtypeStruct(shift_prev.shape, F32)],
        scratch_shapes=scratch,
        compiler_params=_cparams(("parallel",)),
        name="rwkv7")
    return y.reshape(b * l, GROUP_W), s_new, shift_new.reshape(b, RWKV_PAD)[:, :RWKV_COLS]


def _sgu_kernel(z_ref, lnw_ref, lnb_ref, wm_ref, bias_ref, nrm_ref, y_ref, v_ref, *, blk):
    z = z_ref[...]
    zg = 0.5 * z * (1.0 + lax.erf(z * 0.7071067811865476))
    u = zg[:, :GROUP_W]
    v = zg[:, GROUP_W:]
    mu = jnp.mean(v, axis=-1, keepdims=True)
    vc = v - mu
    var = jnp.mean(vc * vc, axis=-1, keepdims=True)
    vn = vc * lax.rsqrt(var + LN_EPS) * lnw_ref[...] + lnb_ref[...]
    v_ref[...] = vn
    row = lax.broadcasted_iota(jnp.int32, (SGU_CHUNK, SGU_CHUNK), 0)
    col = lax.broadcasted_iota(jnp.int32, (SGU_CHUNK, SGU_CHUNK), 1)
    mask = col <= row
    if blk < SGU_CHUNK:
        mask = mask & ((row // blk) == (col // blk))
    vb = vn.astype(BF16)
    wms = [jnp.where(mask, wm_ref[h], 0.0).astype(BF16) for h in range(SGU_HEADS)]
    rows = []
    for c in range(z.shape[0] // SGU_CHUNK):
        rs = slice(c * SGU_CHUNK, (c + 1) * SGU_CHUNK)
        rows.append(jnp.concatenate(
            [jnp.dot(wms[h], vb[rs, h * 128:(h + 1) * 128], preferred_element_type=F32) + bias_ref[:, h:h + 1]
             for h in range(SGU_HEADS)], axis=1))
    s = rows[0] if len(rows) == 1 else jnp.concatenate(rows, axis=0)
    y_ref[...] = _rms(u * s, nrm_ref[...]).astype(y_ref.dtype)


def _sgu(z2, pw, wm, bias, blk, out_dtype):
    t = z2.shape[0]
    tile = SGU_TILE_CHUNKS * SGU_CHUNK
    vec = _const_spec((1, GROUP_W), lambda i: (0, 0))
    return pl.pallas_call(
        functools.partial(_sgu_kernel, blk=blk),
        grid=(t // tile,),
        in_specs=[pl.BlockSpec((tile, SGU_COLS), lambda i: (i, Z_SGU // SGU_COLS)),
                  vec, vec,
                  _const_spec((SGU_HEADS, SGU_CHUNK, SGU_CHUNK), lambda i: (0, 0, 0)),
                  _const_spec((SGU_CHUNK, SGU_HEADS), lambda i: (0, 0)),
                  vec],
        out_specs=[pl.BlockSpec((tile, GROUP_W), lambda i: (i, 0)),
                   pl.BlockSpec((tile, GROUP_W), lambda i: (i, 0))],
        out_shape=[jax.ShapeDtypeStruct((t, GROUP_W), out_dtype), jax.ShapeDtypeStruct((t, GROUP_W), F32)],
        compiler_params=_cparams(("parallel",)),
        name="sgu",
    )(z2, pw["ln_w"], pw["ln_b"], wm, bias, pw["norm"])


def _hgrn_chunks(seqs, lb, nw, chunk, sub):
    nh = HGRN_HEADS
    hs = lambda x, h: x[:, h * HGRN_HEAD:(h + 1) * HGRN_HEAD]
    tri = (lax.broadcasted_iota(jnp.int32, (chunk, chunk), 1)
           <= lax.broadcasted_iota(jnp.int32, (chunk, chunk), 0)).astype(BF16)
    eye = (lax.broadcasted_iota(jnp.int32, (HGRN_HEAD, HGRN_HEAD), 0)
           == lax.broadcasted_iota(jnp.int32, (HGRN_HEAD, HGRN_HEAD), 1)).astype(F32)
    rowid = lax.broadcasted_iota(jnp.int32, (sub, 1), 0)
    d = functools.partial(jnp.dot, preferred_element_type=F32)
    nt = lambda x, y: lax.dot_general(x, y, (((1,), (1,)), ((), ())), preferred_element_type=F32)
    tn = lambda x, y: lax.dot_general(x, y, (((0,), (0,)), ((), ())), preferred_element_type=F32)

    pre = []
    for zq, zf, zi, zg, states in seqs:
        q = _silu(zq)
        fg = lb + (1.0 - lb) * _sigmoid(zf)
        k = 1.0 - fg
        b = _dot_f32_rhs(tri, jnp.log(fg))
        pre.append(dict(q=q, k=k, v=zi, vb=zi.astype(BF16), b=b, qe=(q * jnp.exp(b)).astype(BF16)))
    o = [[d(hs(u["qe"], h), st[h].astype(BF16)) for h in range(nh)] for u, (_, _, _, _, st) in zip(pre, seqs)]
    outs = [[[] for _ in range(nh)] for _ in seqs]
    for i in range(chunk // sub):
        lo_, hi_ = i * sub, (i + 1) * sub
        oi = [[o[n][h][lo_:hi_] for h in range(nh)] for n in range(len(seqs))]
        if i > 0:
            for n, u in enumerate(pre):
                bref = u["b"][lo_ - 1:lo_]
                qt = (u["q"][lo_:hi_] * jnp.exp(u["b"][lo_:hi_] - bref)).astype(BF16)
                kt = (u["k"][:lo_] * jnp.exp(bref - u["b"][:lo_])).astype(BF16)
                att = [nt(hs(qt, h), hs(kt, h)).astype(BF16) for h in range(nh)]
                for h in range(nh):
                    oi[n][h] = oi[n][h] + d(att[h], hs(u["vb"], h)[:lo_])
        tiles = []
        for u in pre:
            qi, bi = u["q"][lo_:hi_], u["b"][lo_:hi_]
            for j in range(sub):
                sj = lo_ + j
                pj = qi * jnp.exp(jnp.minimum(bi - u["b"][sj:sj + 1], 0.0)) * u["k"][sj:sj + 1]
                tiles.extend(hs(pj, h) for h in range(nh))
        att = _group_sums(jnp.concatenate(tiles, axis=0), HGRN_HEAD, terms=1)
        t = 0
        for n, u in enumerate(pre):
            for j in range(sub):
                sj = lo_ + j
                for h in range(nh):
                    a_j = jnp.where(rowid >= j, att[t * sub:(t + 1) * sub], 0.0)
                    oi[n][h] = oi[n][h] + a_j * hs(u["v"], h)[sj:sj + 1]
                    t += 1
        for n in range(len(seqs)):
            for h in range(nh):
                outs[n][h].append(oi[n][h])
    results = []
    for n, (u, (_, _, _, zg, st)) in enumerate(zip(pre, seqs)):
        o_n = jnp.concatenate([outs[n][h][0] if len(outs[n][h]) == 1 else jnp.concatenate(outs[n][h], axis=0)
                               for h in range(nh)], axis=1)
        blast = u["b"][chunk - 1:chunk]
        kd = (u["k"] * jnp.exp(blast - u["b"])).astype(BF16)
        e_last = jnp.exp(blast)
        new_states = []
        for h in range(nh):
            e_col = _group_sums(eye * hs(e_last, h), HGRN_HEAD)
            new_states.append(e_col * st[h] + tn(hs(kd, h), hs(u["vb"], h)))
        ms = _group_sums(o_n * o_n, HGRN_HEAD) * (1.0 / HGRN_HEAD)
        results.append((o_n * lax.rsqrt(ms + NORM_EPS) * nw * _silu(zg), new_states))
    return results


def _hgrn_kernel(zq_ref, zf_ref, zi_ref, zg_ref, lg_ref, nw_ref, s0_ref, y_ref, sout_ref, s_scr,
                 *, layer, nseq, chunk, sub, n_chunks, fill_layer):
    lg = lg_ref[...]
    e = jnp.exp(lg - jnp.max(lg, axis=0, keepdims=True))
    sm = e / jnp.sum(e, axis=0, keepdims=True)
    lb = jnp.sum(sm[0:layer + 1], axis=0, keepdims=True) - sm[0:1]
    nw = nw_ref[...]
    nh = HGRN_HEADS
    sout = _own_layer(sout_ref, fill_layer)
    if n_chunks == 1:
        rows = lambda ref, b: ref[b * chunk:(b + 1) * chunk, :]
        seqs = [(rows(zq_ref, b), rows(zf_ref, b), rows(zi_ref, b), rows(zg_ref, b),
                 [s0_ref[b, h] for h in range(nh)]) for b in range(nseq)]
        ys = []
        for b, (y, states) in enumerate(_hgrn_chunks(seqs, lb, nw, chunk, sub)):
            ys.append(y)
            for h in range(nh):
                sout[b, h] = states[h]
        y_ref[...] = jnp.concatenate(ys, axis=0).astype(y_ref.dtype)
    else:
        s_scr[...] = s0_ref[...]

        def chunk_body(c, carry):
            r0 = pl.multiple_of(c * chunk, chunk)
            rows = lambda ref, b: ref[b, pl.ds(r0, chunk), :]
            seqs = [(rows(zq_ref, b), rows(zf_ref, b), rows(zi_ref, b), rows(zg_ref, b),
                     [s_scr[b, h] for h in range(nh)]) for b in range(nseq)]
            for b, (y, states) in enumerate(_hgrn_chunks(seqs, lb, nw, chunk, sub)):
                y_ref[b, pl.ds(r0, chunk), :] = y.astype(y_ref.dtype)
                for h in range(nh):
                    s_scr[b, h] = states[h]
            return carry

        lax.fori_loop(0, n_chunks, chunk_body, 0)
        sout[...] = s_scr[...]


def _hgrn(z2, s0, lb_logits, norm_w, layer, b, l, prev_stack, out_dtype):
    chunk = min(HGRN_CHUNK, l)
    sub = min(HGRN_SUB, chunk)
    cb = Z_HGRN // GROUP_W
    if l > chunk:
        nseq = 1
        z_in = z2.reshape(b, l, Z_COLS)
        zspec = lambda part: pl.BlockSpec((1, l, GROUP_W), lambda i: (i, 0, cb + part))
        yspec = pl.BlockSpec((1, l, GROUP_W), lambda i: (i, 0, 0))
        yshape = jax.ShapeDtypeStruct((b, l, GROUP_W), out_dtype)
    else:
        nseq = SHORT_SEQS
        z_in = z2
        zspec = lambda part: pl.BlockSpec((nseq * l, GROUP_W), lambda i: (i, cb + part))
        yspec = pl.BlockSpec((nseq * l, GROUP_W), lambda i: (i, 0))
        yshape = jax.ShapeDtypeStruct((b * l, GROUP_W), out_dtype)
    first = prev_stack is None
    tail = (nseq, HGRN_HEADS, HGRN_HEAD, HGRN_HEAD)
    y, s_new = _call_stacked(
        functools.partial(_hgrn_kernel, layer=layer, nseq=nseq, chunk=chunk, sub=sub, n_chunks=l // chunk,
                          fill_layer=layer if first else None),
        [z_in, z_in, z_in, z_in, lb_logits, norm_w, s0],
        [zspec(0), zspec(1), zspec(2), zspec(3),
         _const_spec((DEPTH, GROUP_W), lambda i: (0, 0)),
         _const_spec((1, GROUP_W), lambda i: (0, 0)),
         _stack_out_spec(tail, layer, False)],
        1, prev_stack,
        grid=(b // nseq,),
        out_specs=[yspec, _stack_out_spec(tail, layer, first)],
        out_shape=[yshape, jax.ShapeDtypeStruct((DEPTH, b, HGRN_HEADS, HGRN_HEAD, HGRN_HEAD), F32)],
        scratch_shapes=[pltpu.VMEM((nseq, HGRN_HEADS, HGRN_HEAD, HGRN_HEAD), F32)],
        compiler_params=_cparams(("parallel",)),
        name="hgrn2")
    return y.reshape(b * l, GROUP_W), s_new


def _pool_kernel(z_ref, hist_ref, pw_ref, ps_ref, y_ref, hout_ref, xbuf, s2buf, s4buf, s8buf,
                 *, nseq, tc, n_tiles, start_pos, flat):
    halo = POOL_HALO
    n = tc + halo
    c1 = POOL_CH
    for b in range(nseq):
        xbuf[b, pl.ds(0, 16), :] = jnp.zeros((16, GROUP_W), F32)
        xbuf[b, pl.ds(16, 16), :] = hist_ref[b]

    def tile(r0, b, z):
        xbuf[b, pl.ds(halo, tc), :] = z
        s2buf[b, pl.ds(8, n - 8), :] = xbuf[b, pl.ds(8, n - 8), :] + xbuf[b, pl.ds(7, n - 8), :]
        s4buf[b, pl.ds(16, n - 16), :] = s2buf[b, pl.ds(16, n - 16), c1:] + s2buf[b, pl.ds(14, n - 16), c1:]
        s8buf[b, pl.ds(24, n - 24), :] = s4buf[b, pl.ds(24, n - 24), c1:] + s4buf[b, pl.ds(20, n - 24), c1:]
        s16 = s8buf[b, pl.ds(halo, tc), c1:] + s8buf[b, pl.ds(halo - 8, tc), c1:]
        sums = (s2buf[b, pl.ds(halo, tc), 0:c1], s4buf[b, pl.ds(halo, tc), 0:c1], s8buf[b, pl.ds(halo, tc), 0:c1],
                s16)
        pos = start_pos + r0 + lax.broadcasted_iota(jnp.int32, (tc, 1), 0)
        ys = []
        for gi, win in enumerate(POOL_WINDOWS):
            sl = slice(gi * POOL_CH, (gi + 1) * POOL_CH)
            cnt = jnp.minimum(pos + 1, win).astype(F32)
            dlt = sums[gi] / cnt - z[:, sl]
            ys.append(jnp.dot(dlt.astype(BF16), pw_ref[gi], preferred_element_type=F32) * ps_ref[:, sl])
        if n_tiles > 1:
            xbuf[b, pl.ds(16, 16), :] = xbuf[b, pl.ds(n - 16, 16), :]
        return jnp.concatenate(ys, axis=1)

    if flat:
        y_ref[...] = jnp.concatenate([tile(0, b, z_ref[b * tc:(b + 1) * tc, :]) for b in range(nseq)],
                                     axis=0).astype(y_ref.dtype)
    else:
        def tile_body(ti, carry):
            r0 = pl.multiple_of(ti * tc, tc)
            for b in range(nseq):
                y_ref[b, pl.ds(r0, tc), :] = tile(r0, b, z_ref[b, pl.ds(r0, tc), :]).astype(y_ref.dtype)
            return carry

        lax.fori_loop(0, n_tiles, tile_body, 0)
    for b in range(nseq):
        hout_ref[b] = xbuf[b, pl.ds(n - POOL_HIST, POOL_HIST), :]


def _pool(z2, hist16, pool_w, pool_scale, start_pos, b, l, tc, out_dtype):
    cb = Z_POOL // GROUP_W
    if l >= POOL_HALO:
        nseq = 1
        z_in = z2.reshape(b, l, Z_COLS)
        zspec = pl.BlockSpec((1, l, GROUP_W), lambda i: (i, 0, cb))
        yspec = pl.BlockSpec((1, l, GROUP_W), lambda i: (i, 0, 0))
        yshape = jax.ShapeDtypeStruct((b, l, GROUP_W), out_dtype)
    else:
        nseq = SHORT_SEQS
        z_in = z2
        zspec = pl.BlockSpec((nseq * l, GROUP_W), lambda i: (i, cb))
        yspec = pl.BlockSpec((nseq * l, GROUP_W), lambda i: (i, 0))
        yshape = jax.ShapeDtypeStruct((b * l, GROUP_W), out_dtype)
    n = tc + POOL_HALO
    y, hist_new = pl.pallas_call(
        functools.partial(_pool_kernel, nseq=nseq, tc=tc, n_tiles=l // tc, start_pos=start_pos, flat=nseq > 1),
        grid=(b // nseq,),
        in_specs=[zspec,
                  pl.BlockSpec((nseq, 16, GROUP_W), lambda i: (i, 0, 0)),
                  _const_spec((len(POOL_WINDOWS), POOL_CH, POOL_CH), lambda i: (0, 0, 0)),
                  _const_spec((1, GROUP_W), lambda i: (0, 0))],
        out_specs=[yspec, pl.BlockSpec((nseq, POOL_HIST, GROUP_W), lambda i: (i, 0, 0))],
        out_shape=[yshape, jax.ShapeDtypeStruct((b, POOL_HIST, GROUP_W), F32)],
        scratch_shapes=[pltpu.VMEM((nseq, n, GROUP_W), F32), pltpu.VMEM((nseq, n, GROUP_W), F32),
                        pltpu.VMEM((nseq, n, 3 * POOL_CH), F32), pltpu.VMEM((nseq, n, 2 * POOL_CH), F32)],
        compiler_params=_cparams(("parallel",)),
        name="pool",
    )(z_in, hist16, pool_w, pool_scale)
    return y.reshape(b * l, GROUP_W), hist_new


def _pad_cols(a, n):
    return jnp.pad(a, ((0, 0),) * (a.ndim - 1) + ((0, n - a.shape[-1]),))


def _prep_weights(W):
    c1, c2, c3 = RWKV_COLS, RWKV_COLS + SGU_COLS, RWKV_COLS + SGU_COLS + HGRN_COLS
    w_in = W["w_in"]
    w_in_p = jnp.concatenate(
        [w_in[:, :, c2:c3], w_in[:, :, c1:c2], w_in[:, :, c3:], _pad_cols(w_in[:, :, :c1], RWKV_PAD)],
        axis=2).astype(BF16)
    return dict(w_in=w_in_p, w_out=W["w_out"].astype(BF16), w_gu=W["ffn_w_gu"].astype(BF16),
                w_down=W["ffn_w_down"].astype(BF16), ple_gate=W["ple_gate"].astype(BF16),
                ple_proj=W["ple_proj"].astype(BF16))


def _prep_layer(i, W):
    row = lambda a: a.reshape(1, -1)
    lora = lambda a, off: jnp.pad(a, ((off, LORA_W - off - a.shape[0]), (0, 0))).astype(BF16)
    rw = dict(
        mu=_pad_cols(row(W["rwkv_mu"][i]), RWKV_PAD),
        lora_w=lora(W["rwkv_w_lora"][i], 0), w0=row(W["rwkv_w0"][i]),
        lora_a=lora(W["rwkv_a_lora"][i], W_LORA), a0=row(W["rwkv_a0"][i]),
        lora_g=lora(W["rwkv_g_lora"][i], W_LORA + A_LORA),
        k_k=row(W["rwkv_k_k"][i]), k_a=row(W["rwkv_k_a"][i]), r_k=row(W["rwkv_r_k"][i]),
        gn_w=row(W["rwkv_gn_w"][i]), gn_b=row(W["rwkv_gn_b"][i]))
    sg = dict(ln_w=row(W["sgu_ln_w"][i]), ln_b=row(W["sgu_ln_b"][i]), norm=row(W["sgu_norm"][i]))
    return dict(
        rw=rw, sg=sg, sgu_w=W["sgu_w"][i], sgu_b=W["sgu_b"][i],
        hgrn_norm=row(W["hgrn_norm"][i]), pool_w=W["pool_w"][i].astype(BF16), pool_scale=row(W["pool_scale"][i]),
        ln_mix_pre=row(W["ln_mix_pre"][i]), ln_mix_post=row(W["ln_mix_post"][i]),
        ln_ffn_pre=row(W["ln_ffn_pre"][i]), ln_ffn_post=row(W["ln_ffn_post"][i]))


def _trunk(x, p, wkv0, shift0, hgrn0, pool0, start_pos, big, layers, lb_logits, cfg):
    b, l, _ = x.shape
    t = b * l
    xf = x.reshape(t, D_MODEL)
    p2 = p.reshape(DEPTH, t, PLE_DIM)
    wkv_new = hgrn_new = None
    shift_l, pool_l, sgu_l = [], [], []
    for i, lw in enumerate(layers):
        z2 = _inproj(xf, lw["ln_mix_pre"], big["w_in"], i, min(cfg["tm_in"], t), cfg["tn_in"])
        y_r, wkv_new, shift_new = _rwkv(z2, _pad_cols(shift0[i], RWKV_PAD), wkv0, lw["rw"], i, b, l, wkv_new,
                                        cfg["ydt"])
        if cfg["sgu_blk"] == SGU_CHUNK:
            wm, bias = lw["sgu_w"], lw["sgu_b"].T
        else:
            rep = SGU_CHUNK // l
            wm = jnp.tile(lw["sgu_w"][:, :l, :l], (1, rep, rep))
            bias = jnp.tile(lw["sgu_b"][:, :l].T, (rep, 1))
        y_s, v_rows = _sgu(z2, lw["sg"], wm, bias, cfg["sgu_blk"], cfg["ydt"])
        y_h, hgrn_new = _hgrn(z2, hgrn0, lb_logits, lw["hgrn_norm"], i, b, l, hgrn_new, cfg["ydt"])
        hist16 = jnp.pad(pool0[i], ((0, 0), (1, 0), (0, 0)))
        y_p, pool_new = _pool(z2, hist16, lw["pool_w"], lw["pool_scale"], start_pos, b, l, cfg["tc_pool"],
                              cfg["ydt"])
        xf = _outproj([y_r, y_s, y_h, y_p], xf, big["w_out"], lw["ln_mix_post"], i, cfg["tm"])
        xf = _ffn(xf, lw["ln_ffn_pre"], big["w_gu"], big["w_down"], lw["ln_ffn_post"], p2, big["ple_gate"],
                  big["ple_proj"], i, min(cfg["tm_ffn"], t), cfg["tf"])
        shift_l.append(shift_new)
        pool_l.append(pool_new)
        sgu_l.append(v_rows.reshape(b, l, GROUP_W))
    return (xf.reshape(b, l, D_MODEL), wkv_new, jnp.stack(shift_l), hgrn_new, jnp.stack(pool_l), jnp.stack(sgu_l))


def _cfg(l):
    if l >= 256:
        return dict(tm=512, tm_in=1024, tn_in=1792, tm_ffn=1024, tf=256, sgu_blk=SGU_CHUNK, tc_pool=256, ydt=BF16)
    return dict(tm=512, tm_in=512, tn_in=1792, tm_ffn=512, tf=512, sgu_blk=l, tc_pool=l, ydt=BF16)


def kernel(x_prompt, x_sample, state_rwkv_wkv, state_rwkv_shift, state_hgrn, state_pool, p_prompt, p_sample,
           ln_mix_pre, ln_mix_post, ln_ffn_pre, ln_ffn_post, w_in, rwkv_mu, rwkv_w_lora, rwkv_w0, rwkv_a_lora,
           rwkv_a0, rwkv_g_lora, rwkv_k_k, rwkv_k_a, rwkv_r_k, rwkv_gn_w, rwkv_gn_b, sgu_ln_w, sgu_ln_b, sgu_w,
           sgu_b, sgu_norm, hgrn_lb_logits, hgrn_norm, pool_w, pool_scale, w_out, ffn_w_gu, ffn_w_down, ple_gate,
           ple_proj):
    W = dict(ln_mix_pre=ln_mix_pre, ln_mix_post=ln_mix_post, ln_ffn_pre=ln_ffn_pre, ln_ffn_post=ln_ffn_post,
             w_in=w_in, rwkv_mu=rwkv_mu, rwkv_w_lora=rwkv_w_lora, rwkv_w0=rwkv_w0, rwkv_a_lora=rwkv_a_lora,
             rwkv_a0=rwkv_a0, rwkv_g_lora=rwkv_g_lora, rwkv_k_k=rwkv_k_k, rwkv_k_a=rwkv_k_a,
             rwkv_r_k=rwkv_r_k.reshape(DEPTH, GROUP_W), rwkv_gn_w=rwkv_gn_w, rwkv_gn_b=rwkv_gn_b, sgu_ln_w=sgu_ln_w,
             sgu_ln_b=sgu_ln_b, sgu_w=sgu_w, sgu_b=sgu_b, sgu_norm=sgu_norm, hgrn_norm=hgrn_norm, pool_w=pool_w,
             pool_scale=pool_scale, w_out=w_out, ffn_w_gu=ffn_w_gu, ffn_w_down=ffn_w_down, ple_gate=ple_gate,
             ple_proj=ple_proj)
    big = _prep_weights(W)
    layers = [_prep_layer(i, W) for i in range(DEPTH)]
    lb_logits = hgrn_lb_logits.astype(F32)
    bp, lp, _ = x_prompt.shape
    zeros = lambda *s: jnp.zeros((DEPTH, bp) + s, F32)
    y_prompt, wkv_p, shift_p, hgrn_p, pool_p, _ = _trunk(
        x_prompt, p_prompt, zeros(RWKV_HEADS, RWKV_HEAD, RWKV_HEAD), zeros(RWKV_COLS),
        zeros(HGRN_HEADS, HGRN_HEAD, HGRN_HEAD), zeros(POOL_HIST, GROUP_W), 0, big, layers, lb_logits, _cfg(lp))
    y_sample, wkv_s, shift_s, hgrn_s, pool_s, sgu_v_s = _trunk(
        x_sample, p_sample, state_rwkv_wkv, state_rwkv_shift, state_hgrn, state_pool, PAST_LEN, big, layers,
        lb_logits, _cfg(x_sample.shape[1]))
    return (y_prompt, y_sample, wkv_p, shift_p, hgrn_p, pool_p, wkv_s, shift_s, hgrn_s, pool_s, sgu_v_s)
```

```python
import functools

import jax
import jax.numpy as jnp
from jax import lax
from jax.experimental import pallas as pl
from jax.experimental.pallas import tpu as pltpu

F32 = jnp.float32
BF16 = jnp.bfloat16

D_MODEL = 2048
DEPTH = 2
PAST_LEN = 16384
GROUP_W = 512
RWKV_HEAD = 64
RWKV_HEADS = 8
RWKV_PAIRS = RWKV_HEADS // 2
RWKV_CHUNK = 64
RWKV_TILE_CHUNKS = 4
RWKV_TILE = RWKV_TILE_CHUNKS * RWKV_CHUNK
W_LORA, A_LORA, G_LORA = 32, 32, 96
RWKV_GN_EPS = 64e-5
SGU_CHUNK = 128
SGU_HEADS = 4
SGU_TILE_CHUNKS = 4
HGRN_HEADS = 4
HGRN_HEAD = 128
HGRN_CHUNK = 64
HGRN_SUB = 16
POOL_WINDOWS = (2, 4, 8, 16)
POOL_CH = 128
POOL_HIST = 15
POOL_HALO = 32
D_FF = 5632
PLE_DIM = 256
PLE_COLS = 512
FFN_TF = 512
NORM_EPS = 1e-6
LN_EPS = 1e-5
RWKV_COLS = 3 * GROUP_W + W_LORA + A_LORA + G_LORA
RWKV_PAD = 1792
LORA_OFF = 3 * GROUP_W
LORA_W = RWKV_PAD - LORA_OFF
SGU_COLS = 2 * GROUP_W
HGRN_COLS = 4 * GROUP_W
Z_HGRN, Z_SGU, Z_POOL, Z_RWKV = 0, 2048, 3072, 3584
Z_COLS = Z_RWKV + RWKV_PAD
SHORT_SEQS = 8
VMEM_LIMIT = 56 * 1024 * 1024


def _cparams(sem):
    return pltpu.CompilerParams(dimension_semantics=sem, vmem_limit_bytes=VMEM_LIMIT)


def _const_spec(shape, index_map):
    return pl.BlockSpec(shape, index_map, pipeline_mode=pl.Buffered(1))


def _sigmoid(x):
    return jax.nn.sigmoid(x)


def _silu(x):
    return x * jax.nn.sigmoid(x)


def _split3(x):
    hi = x.astype(BF16)
    r = x - hi.astype(F32)
    mid = r.astype(BF16)
    lo = (r - mid.astype(F32)).astype(BF16)
    return hi, mid, lo


def _group_sums(x, width, terms=3):
    rows, cols = x.shape
    nblk = cols // 128
    xs = x if nblk == 1 else jnp.concatenate([x[:, i * 128:(i + 1) * 128] for i in range(nblk)], axis=0)
    ones = ((lax.broadcasted_iota(jnp.int32, (128, 128), 0) // width)
            == (lax.broadcasted_iota(jnp.int32, (128, 128), 1) // width)).astype(BF16)
    s = None
    for part in _split3(xs)[:terms]:
        y = jnp.dot(part, ones, preferred_element_type=F32)
        s = y if s is None else s + y
    return s if nblk == 1 else jnp.concatenate([s[i * rows:(i + 1) * rows] for i in range(nblk)], axis=1)


def _dot_f32_rhs(m, x):
    hi, mid, lo = _split3(x)
    d = functools.partial(jnp.dot, preferred_element_type=F32)
    return d(m, hi) + d(m, mid) + d(m, lo)


def _rms(x, g):
    return x * lax.rsqrt(jnp.mean(x * x, axis=-1, keepdims=True) + NORM_EPS) * g


def _call_stacked(kernel, inputs, in_specs, stack_out, prev_stack, **kw):
    if prev_stack is None:
        return pl.pallas_call(kernel, in_specs=in_specs, **kw)(*inputs)
    n_in = len(inputs)

    def body(*refs):
        kernel(*refs[:n_in], *refs[n_in + 1:])

    return pl.pallas_call(body, in_specs=list(in_specs) + [pl.BlockSpec(memory_space=pl.ANY)],
                          input_output_aliases={n_in: stack_out}, **kw)(*inputs, prev_stack)


def _stack_out_spec(tail, layer, first):
    zeros = (0,) * (len(tail) - 1)
    if first:
        return pl.BlockSpec((DEPTH,) + tail, lambda i: (0, i) + zeros)
    return pl.BlockSpec((None,) + tail, lambda i: (layer, i) + zeros)


def _own_layer(sout_ref, fill_layer):
    if fill_layer is None:
        return sout_ref
    for j in range(DEPTH):
        if j != fill_layer:
            sout_ref[j] = jnp.zeros(sout_ref.shape[1:], sout_ref.dtype)
    return sout_ref.at[fill_layer]


def _inproj_kernel(x_ref, g_ref, w_ref, z_ref, h_ref):
    @pl.when(pl.program_id(1) == 0)
    def _():
        h_ref[...] = _rms(x_ref[...], g_ref[...]).astype(BF16)

    z_ref[...] = jnp.dot(h_ref[...], w_ref[...], preferred_element_type=F32)


def _inproj(x, g, w, layer, tm, tn):
    t = x.shape[0]
    return pl.pallas_call(
        _inproj_kernel,
        grid=(t // tm, Z_COLS // tn),
        in_specs=[pl.BlockSpec((tm, D_MODEL), lambda i, j: (i, 0)),
                  _const_spec((1, D_MODEL), lambda i, j: (0, 0)),
                  pl.BlockSpec((None, D_MODEL, tn), lambda i, j: (layer, 0, j))],
        out_specs=pl.BlockSpec((tm, tn), lambda i, j: (i, j)),
        out_shape=jax.ShapeDtypeStruct((t, Z_COLS), F32),
        scratch_shapes=[pltpu.VMEM((tm, D_MODEL), BF16)],
        compiler_params=_cparams(("parallel", "arbitrary")),
        name="inproj",
    )(x, g, w)


def _outproj_kernel(yr_ref, ys_ref, yh_ref, yp_ref, x_ref, w_ref, g_ref, o_ref):
    d = functools.partial(jnp.dot, preferred_element_type=F32)
    g4 = GROUP_W
    mix = (d(yr_ref[...], w_ref[0:g4, :]) + d(ys_ref[...], w_ref[g4:2 * g4, :])
           + d(yh_ref[...], w_ref[2 * g4:3 * g4, :]) + d(yp_ref[...], w_ref[3 * g4:4 * g4, :]))
    o_ref[...] = x_ref[...] + _rms(mix, g_ref[...])


def _outproj(ys, x, w, g, layer, tm):
    t = x.shape[0]
    yspec = pl.BlockSpec((tm, GROUP_W), lambda i: (i, 0))
    return pl.pallas_call(
        _outproj_kernel,
        grid=(t // tm,),
        in_specs=[yspec, yspec, yspec, yspec,
                  pl.BlockSpec((tm, D_MODEL), lambda i: (i, 0)),
                  _const_spec((None, D_MODEL, D_MODEL), lambda i: (layer, 0, 0)),
                  _const_spec((1, D_MODEL), lambda i: (0, 0))],
        out_specs=pl.BlockSpec((tm, D_MODEL), lambda i: (i, 0)),
        out_shape=jax.ShapeDtypeStruct((t, D_MODEL), F32),
        compiler_params=_cparams(("parallel",)),
        name="outproj",
    )(*ys, x, w, g)


def _ffn_kernel(x_ref, gpre_ref, wg_ref, wu_ref, wd_ref, gpost_ref, p_ref, wgate_ref, wproj_ref, o_ref, h_ref, acc_ref):
    j = pl.program_id(1)

    @pl.when(j == 0)
    def _():
        h_ref[...] = _rms(x_ref[...], gpre_ref[...]).astype(BF16)
        acc_ref[...] = jnp.zeros_like(acc_ref)

    h = h_ref[...]
    gate = jnp.dot(h, wg_ref[...], preferred_element_type=F32)
    up = jnp.dot(h, wu_ref[...], preferred_element_type=F32)
    act = (_silu(gate) * up).astype(BF16)
    acc_ref[...] += jnp.dot(act, wd_ref[...], preferred_element_type=F32)

    @pl.when(j == pl.num_programs(1) - 1)
    def _():
        acc_ref[...] = x_ref[...] + _rms(acc_ref[...], gpost_ref[...])
        h_ref[...] = acc_ref[...].astype(BF16)
        pb = p_ref[...].astype(BF16)
        for c in range(D_MODEL // PLE_COLS):
            cs = slice(c * PLE_COLS, (c + 1) * PLE_COLS)
            pgate = _sigmoid(jnp.dot(h_ref[...], wgate_ref[:, cs], preferred_element_type=F32))
            ple = jnp.dot(pb, wproj_ref[:, cs], preferred_element_type=F32)
            o_ref[:, cs] = acc_ref[:, cs] + pgate * ple


def _ffn(x, gpre, w_gu, w_down, gpost, p, w_gate, w_proj, layer, tm):
    t = x.shape[0]
    tf = FFN_TF
    nf = D_FF // tf
    return pl.pallas_call(
        _ffn_kernel,
        grid=(t // tm, nf),
        in_specs=[pl.BlockSpec((tm, D_MODEL), lambda i, j: (i, 0)),
                  _const_spec((1, D_MODEL), lambda i, j: (0, 0)),
                  pl.BlockSpec((None, None, D_MODEL, tf), lambda i, j: (layer, j, 0, 0)),
                  pl.BlockSpec((None, None, D_MODEL, tf), lambda i, j: (layer, j + nf, 0, 0)),
                  pl.BlockSpec((None, tf, D_MODEL), lambda i, j: (layer, j, 0)),
                  _const_spec((1, D_MODEL), lambda i, j: (0, 0)),
                  pl.BlockSpec((None, tm, PLE_DIM), lambda i, j: (layer, i, 0)),
                  _const_spec((None, D_MODEL, D_MODEL), lambda i, j: (layer, 0, 0)),
                  _const_spec((None, PLE_DIM, D_MODEL), lambda i, j: (layer, 0, 0))],
        out_specs=pl.BlockSpec((tm, D_MODEL), lambda i, j: (i, 0)),
        out_shape=jax.ShapeDtypeStruct((t, D_MODEL), F32),
        scratch_shapes=[pltpu.VMEM((tm, D_MODEL), BF16), pltpu.VMEM((tm, D_MODEL), F32)],
        compiler_params=_cparams(("parallel", "arbitrary")),
        name="ffn",
    )(x, gpre, w_gu, w_gu, w_down, gpost, p, w_gate, w_proj)


def _rwkv_gates(z, z_prev, mu_ref, lw_ref, w0_ref, la_ref, a0_ref, lg_ref, kk_ref, ka_ref, rk_ref):
    zm = z + mu_ref[...] * (z_prev - z)
    g4 = GROUP_W
    r = zm[:, 0:g4]
    k = zm[:, g4:2 * g4]
    v = zm[:, 2 * g4:3 * g4]
    zl = zm[:, LORA_OFF:RWKV_PAD]
    d = functools.partial(jnp.dot, preferred_element_type=F32)
    wl = w0_ref[...] + d(jnp.tanh(zl).astype(BF16), lw_ref[...])
    w_log = -(jnp.maximum(-wl, 0.0) + jnp.log1p(jnp.exp(-jnp.abs(wl)))) - 0.5
    a = _sigmoid(a0_ref[...] + d(zl.astype(BF16), la_ref[...]))
    g = d(_sigmoid(zl).astype(BF16), lg_ref[...])
    kk = k * kk_ref[...]
    kk = kk / jnp.maximum(jnp.sqrt(_group_sums(kk * kk, RWKV_HEAD)), 1e-12)
    k = k * (1.0 + (a - 1.0) * ka_ref[...])
    bonus = _group_sums(r * k * rk_ref[...], RWKV_HEAD) * v
    return r, k, v, -kk, kk * a, w_log, g, bonus


def _rwkv_post(y, bonus, g, gnw_ref, gnb_ref):
    inv_n = 1.0 / RWKV_HEAD
    mean = _group_sums(y, RWKV_HEAD) * inv_n
    yc = y - mean
    var = _group_sums(yc * yc, RWKV_HEAD) * inv_n
    yn = yc * lax.rsqrt(var + RWKV_GN_EPS) * gnw_ref[...] + gnb_ref[...]
    return (yn + bonus) * g


def _rwkv_scan_kernel(z_ref, sh_ref, s0_ref, mu_ref, lw_ref, w0_ref, la_ref, a0_ref, lg_ref, kk_ref, ka_ref, rk_ref,
                      gnw_ref, gnb_ref, y_ref, sout_ref, shout_ref, zbuf, *, nseq, l, fill_layer):
    rows = nseq * l
    z = z_ref[...]
    shout_ref[...] = jnp.concatenate([z[(b + 1) * l - 1:(b + 1) * l] for b in range(nseq)], axis=0)
    zbuf[pl.ds(0, 8), :] = jnp.zeros((8, RWKV_PAD), F32)
    zbuf[pl.ds(8, rows), :] = z
    rid = lax.broadcasted_iota(jnp.int32, (rows, 1), 0)
    spread = ((lax.broadcasted_iota(jnp.int32, (rows, nseq), 0) // l)
              == lax.broadcasted_iota(jnp.int32, (rows, nseq), 1)).astype(BF16)
    z_prev = jnp.where(rid % l == 0, _dot_f32_rhs(spread, sh_ref[...]), zbuf[pl.ds(7, rows), :])
    r, k, v, a_vec, b_vec, w_log, g, bonus = _rwkv_gates(
        z, z_prev, mu_ref, lw_ref, w0_ref, la_ref, a0_ref, lg_ref, kk_ref, ka_ref, rk_ref)
    w = jnp.exp(-jnp.exp(w_log))

    shp = (RWKV_HEAD, 2 * RWKV_HEAD)
    lane = lax.broadcasted_iota(jnp.int32, shp, 1)
    sub = lax.broadcasted_iota(jnp.int32, shp, 0)
    lo = lane < RWKV_HEAD
    eye = ((lane % RWKV_HEAD) == sub).astype(F32)

    def head_sums(tiles):
        s = _group_sums(jnp.concatenate(tiles, axis=0), RWKV_HEAD, terms=2)
        return [s[u * RWKV_HEAD:(u + 1) * RWKV_HEAD] for u in range(len(tiles))]

    units = [(b, p) for b in range(nseq) for p in range(RWKV_PAIRS)]
    row = lambda x, b, p, j: x[b * l + j:b * l + j + 1, p * 128:(p + 1) * 128]
    state = [jnp.concatenate([s0_ref[b, 2 * p], s0_ref[b, 2 * p + 1]], axis=1) for b, p in units]
    y_rows = {}
    for j in range(l):
        sa = head_sums([s * row(a_vec, b, p, j) for s, (b, p) in zip(state, units)])
        vcol = head_sums([eye * row(v, b, p, j) for b, p in units])
        state = [s * row(w, b, p, j) + sa_u * row(b_vec, b, p, j) + vc_u * row(k, b, p, j)
                 for s, sa_u, vc_u, (b, p) in zip(state, sa, vcol, units)]
        ycol = head_sums([s * row(r, b, p, j) for s, (b, p) in zip(state, units)])
        for yc_u, (b, p) in zip(ycol, units):
            y_rows[b * l + j, p] = jnp.sum(eye * yc_u, axis=0, keepdims=True)
    sout = _own_layer(sout_ref, fill_layer)
    for s, (b, p) in zip(state, units):
        sout[b, 2 * p] = s[:, 0:RWKV_HEAD]
        sout[b, 2 * p + 1] = s[:, RWKV_HEAD:2 * RWKV_HEAD]
    y = jnp.concatenate([jnp.concatenate([y_rows[i, p] for i in range(rows)], axis=0) for p in range(RWKV_PAIRS)],
                        axis=1)
    y_ref[...] = _rwkv_post(y, bonus, g, gnw_ref, gnb_ref).astype(y_ref.dtype)


def _tri_inverse(l_mats, n):
    row = lax.broadcasted_iota(jnp.int32, (n, n), 0)
    col = lax.broadcasted_iota(jnp.int32, (n, n), 1)
    eye = (row == col).astype(F32)
    mm = lambda a, b: jnp.dot(a.astype(BF16), b.astype(BF16), preferred_element_type=F32)
    same = lambda m: (row // m) == (col // m)
    d1 = [jnp.where(same(8), x, 0.0) for x in l_mats]
    d2 = [mm(x, x) for x in d1]
    d4 = [mm(x, x) for x in d2]
    d3 = [mm(x, y) for x, y in zip(d1, d2)]
    t = [eye + x1 + x2 + x3 for x1, x2, x3 in zip(d1, d2, d3)]
    t = [x + mm(x, y) for x, y in zip(t, d4)]
    m = 8
    while m < RWKV_CHUNK:
        e = [jnp.where(same(2 * m) & jnp.logical_not(same(m)), x, 0.0) for x in l_mats]
        te = [mm(x, y) for x, y in zip(t, e)]
        t = [x + mm(y, x) for x, y in zip(t, te)]
        m *= 2
    return t


def _rwkv_chunk_local(chunks):
    c = RWKV_CHUNK
    n = 2 * c
    lo64 = lax.broadcasted_iota(jnp.int32, (c, 128), 1) < RWKV_HEAD
    tri = (lax.broadcasted_iota(jnp.int32, (c, c), 1) <= lax.broadcasted_iota(jnp.int32, (c, c), 0)).astype(BF16)
    stack = lambda x: jnp.concatenate([jnp.where(lo64, x, 0.0), jnp.where(lo64, 0.0, x)], axis=0)
    ops = []
    for r, k, v, a_vec, b_vec, logw in chunks:
        cum = _dot_f32_rhs(tri, logw)
        c_end = cum[c - 1:c]
        e_neg = jnp.exp(-cum)
        e_end = jnp.exp(c_end - cum)
        at = a_vec * jnp.exp(cum - logw)
        rt = r * jnp.exp(cum)
        bt = b_vec * e_neg
        kt = k * e_neg
        bh = b_vec * e_end
        kh = k * e_end
        e_cend = jnp.exp(c_end)
        for p in range(RWKV_PAIRS):
            sl = slice(p * 128, (p + 1) * 128)
            ops.append(dict(xa=stack(at[:, sl]), xr=stack(rt[:, sl]), yb=stack(bt[:, sl]), yk=stack(kt[:, sl]),
                            vs=stack(v[:, sl]).astype(BF16), xbh=stack(bh[:, sl]), xkh=stack(kh[:, sl]),
                            e_cend=e_cend[:, sl]))

    row = lax.broadcasted_iota(jnp.int32, (n, n), 0)
    col = lax.broadcasted_iota(jnp.int32, (n, n), 1)
    same_head = (row // c) == (col // c)
    strict = same_head & (col < row)
    incl = same_head & (col <= row)
    zeros = jnp.zeros((n, n), BF16)
    mm = lambda x, y: jnp.dot(x, y, preferred_element_type=F32)
    nt = lambda x, y: lax.dot_general(x, y, (((1,), (1,)), ((), ())), preferred_element_type=F32)
    tn = lambda x, y: lax.dot_general(x, y, (((0,), (0,)), ((), ())), preferred_element_type=F32)
    unstack = lambda x: x[0:c] + x[c:n]

    gram = [nt(jnp.concatenate([o["xa"], o["xr"]], axis=0).astype(BF16),
               jnp.concatenate([o["yb"], o["yk"]], axis=0).astype(BF16)) for o in ops]
    l_ab = [jnp.where(strict, x[0:n, 0:n], 0.0) for x in gram]
    l_ak = [jnp.where(strict, x[0:n, n:2 * n], 0.0) for x in gram]
    m_rbk = [jnp.concatenate([jnp.where(incl, x[n:2 * n, 0:n], 0.0), jnp.where(incl, x[n:2 * n, n:2 * n], 0.0)],
                             axis=1).astype(BF16) for x in gram]
    w1 = [mm(x.astype(BF16), o["vs"]) for x, o in zip(l_ak, ops)]
    t_inv = _tri_inverse(l_ab, n)
    tu = [mm(t.astype(BF16), jnp.concatenate([o["xa"], w], axis=1).astype(BF16))
          for t, o, w in zip(t_inv, ops, w1)]
    zmat = [jnp.concatenate([x.astype(BF16), jnp.concatenate([zeros, o["vs"]], axis=1)], axis=0)
            for x, o in zip(tu, ops)]
    ry = [mm(m, zm) for m, zm in zip(m_rbk, zmat)]
    ps = [tn(zm, jnp.concatenate([o["xbh"], o["xkh"]], axis=0).astype(BF16))
          for zm, o in zip(zmat, ops)]
    out = [(unstack(o["xr"] + y[:, 0:n]), unstack(y[:, n:2 * n]), o["e_cend"], q[0:n], q[n:2 * n])
           for o, y, q in zip(ops, ry, ps)]
    return [out[ci * RWKV_PAIRS:(ci + 1) * RWKV_PAIRS] for ci in range(len(chunks))]


def _rwkv_chunk_kernel(z_ref, sh_ref, s0_ref, mu_ref, lw_ref, w0_ref, la_ref, a0_ref, lg_ref, kk_ref, ka_ref, rk_ref,
                       gnw_ref, gnb_ref, y_ref, sout_ref, shout_ref, zbuf, s_scr, *, n_tiles, fill_layer):
    c = RWKV_CHUNK
    tile = RWKV_TILE
    zero = jnp.zeros((RWKV_HEAD, RWKV_HEAD), F32)
    for p in range(RWKV_PAIRS):
        s_scr[p] = jnp.concatenate([jnp.concatenate([s0_ref[0, 2 * p], zero], axis=1),
                                    jnp.concatenate([zero, s0_ref[0, 2 * p + 1]], axis=1)], axis=0)
    zbuf[pl.ds(0, 8), :] = jnp.broadcast_to(sh_ref[0], (8, RWKV_PAD))
    nt = lambda x, y: lax.dot_general(x, y, (((1,), (1,)), ((), ())), preferred_element_type=F32)

    def tile_body(ti, carry):
        r0 = pl.multiple_of(ti * tile, tile)
        z = z_ref[0, pl.ds(r0, tile), :]
        zbuf[pl.ds(8, tile), :] = z
        z_prev = zbuf[pl.ds(7, tile), :]
        zbuf[pl.ds(7, 1), :] = zbuf[pl.ds(7 + tile, 1), :]
        r, k, v, a_vec, b_vec, w_log, g, bonus = _rwkv_gates(
            z, z_prev, mu_ref, lw_ref, w0_ref, la_ref, a0_ref, lg_ref, kk_ref, ka_ref, rk_ref)
        logw = -jnp.exp(w_log)
        local = _rwkv_chunk_local([tuple(x[ci * c:(ci + 1) * c] for x in (r, k, v, a_vec, b_vec, logw))
                                   for ci in range(RWKV_TILE_CHUNKS)])
        y_cols = []
        for p in range(RWKV_PAIRS):
            s = s_scr[p]
            ys = []
            for ci in range(RWKV_TILE_CHUNKS):
                rc, yloc, e_cend, trans, sloc = local[ci][p]
                sb = s.astype(BF16)
                ys.append(nt(rc.astype(BF16), sb) + yloc)
                s = s * e_cend + jnp.dot(sb, trans.astype(BF16), preferred_element_type=F32) + sloc
            s_scr[p] = s
            y_cols.append(jnp.concatenate(ys, axis=0))
        y = jnp.concatenate(y_cols, axis=1)
        y_ref[0, pl.ds(r0, tile), :] = _rwkv_post(y, bonus, g, gnw_ref, gnb_ref).astype(y_ref.dtype)
        return carry

    lax.fori_loop(0, n_tiles, tile_body, 0)
    shout_ref[0] = zbuf[pl.ds(7, 1), :]
    sout = _own_layer(sout_ref, fill_layer)
    for p in range(RWKV_PAIRS):
        sout[0, 2 * p] = s_scr[p, 0:RWKV_HEAD, 0:RWKV_HEAD]
        sout[0, 2 * p + 1] = s_scr[p, RWKV_HEAD:2 * RWKV_HEAD, RWKV_HEAD:2 * RWKV_HEAD]


def _rwkv(z2, shift_prev, wkv0, pw, layer, b, l, prev_stack, out_dtype):
    vec = lambda n: _const_spec((1, n), lambda i: (0, 0))
    mat = lambda r, c: _const_spec((r, c), lambda i: (0, 0))
    params = [pw["mu"], pw["lora_w"], pw["w0"], pw["lora_a"], pw["a0"], pw["lora_g"],
              pw["k_k"], pw["k_a"], pw["r_k"], pw["gn_w"], pw["gn_b"]]
    pspecs = [vec(RWKV_PAD), mat(LORA_W, GROUP_W), vec(GROUP_W), mat(LORA_W, GROUP_W), vec(GROUP_W),
              mat(LORA_W, GROUP_W), vec(GROUP_W), vec(GROUP_W), vec(GROUP_W), vec(GROUP_W), vec(GROUP_W)]
    if l % RWKV_TILE == 0:
        nseq = 1
        z_in = z2.reshape(b, l, Z_COLS)
        shift_prev = shift_prev.reshape(b, 1, RWKV_PAD)
        shspec = pl.BlockSpec((1, 1, RWKV_PAD), lambda i: (i, 0, 0))
        body = functools.partial(_rwkv_chunk_kernel, n_tiles=l // RWKV_TILE)
        zspec = pl.BlockSpec((1, l, RWKV_PAD), lambda i: (i, 0, Z_RWKV // RWKV_PAD))
        yspec = pl.BlockSpec((1, l, GROUP_W), lambda i: (i, 0, 0))
        yshape = jax.ShapeDtypeStruct((b, l, GROUP_W), out_dtype)
        scratch = [pltpu.VMEM((RWKV_TILE + 8, RWKV_PAD), F32), pltpu.VMEM((RWKV_PAIRS, 128, 128), F32)]
    else:
        nseq = SHORT_SEQS
        z_in = z2
        shspec = pl.BlockSpec((nseq, RWKV_PAD), lambda i: (i, 0))
        body = functools.partial(_rwkv_scan_kernel, nseq=nseq, l=l)
        zspec = pl.BlockSpec((nseq * l, RWKV_PAD), lambda i: (i, Z_RWKV // RWKV_PAD))
        yspec = pl.BlockSpec((nseq * l, GROUP_W), lambda i: (i, 0))
        yshape = jax.ShapeDtypeStruct((b * l, GROUP_W), out_dtype)
        scratch = [pltpu.VMEM((nseq * l + 8, RWKV_PAD), F32)]
    first = prev_stack is None
    tail = (nseq, RWKV_HEADS, RWKV_HEAD, RWKV_HEAD)
    y, s_new, shift_new = _call_stacked(
        functools.partial(body, fill_layer=layer if first else None), [z_in, shift_prev, wkv0] + params,
        [zspec, shspec, _stack_out_spec(tail, layer, False)] + pspecs,
        1, prev_stack,
        grid=(b // nseq,),
        out_specs=[yspec, _stack_out_spec(tail, layer, first), shspec],
        out_shape=[yshape, jax.ShapeDtypeStruct((DEPTH, b, RWKV_HEADS, RWKV_HEAD, RWKV_HEAD), F32),
                   jax.ShapeDtypeStruct(shift_prev.shape, F32)],
        scratch_shapes=scratch,
        compiler_params=_cparams(("parallel",)),
        name="rwkv7")
    return y.reshape(b * l, GROUP_W), s_new, shift_new.reshape(b, RWKV_PAD)[:, :RWKV_COLS]


def _sgu_kernel(z_ref, lnw_ref, lnb_ref, wm_ref, bias_ref, nrm_ref, y_ref, v_ref, *, blk):
    z = z_ref[...]
    zg = 0.5 * z * (1.0 + lax.erf(z * 0.7071067811865476))
    u = zg[:, :GROUP_W]
    v = zg[:, GROUP_W:]
    mu = jnp.mean(v, axis=-1, keepdims=True)
    vc = v - mu
    var = jnp.mean(vc * vc, axis=-1, keepdims=True)
    vn = vc * lax.rsqrt(var + LN_EPS) * lnw_ref[...] + lnb_ref[...]
    v_ref[...] = vn
    row = lax.broadcasted_iota(jnp.int32, (SGU_CHUNK, SGU_CHUNK), 0)
    col = lax.broadcasted_iota(jnp.int32, (SGU_CHUNK, SGU_CHUNK), 1)
    mask = col <= row
    if blk < SGU_CHUNK:
        mask = mask & ((row // blk) == (col // blk))
    vb = vn.astype(BF16)
    wms = [jnp.where(mask, wm_ref[h], 0.0).astype(BF16) for h in range(SGU_HEADS)]
    rows = []
    for c in range(z.shape[0] // SGU_CHUNK):
        rs = slice(c * SGU_CHUNK, (c + 1) * SGU_CHUNK)
        rows.append(jnp.concatenate(
            [jnp.dot(wms[h], vb[rs, h * 128:(h + 1) * 128], preferred_element_type=F32) + bias_ref[:, h:h + 1]
             for h in range(SGU_HEADS)], axis=1))
    s = rows[0] if len(rows) == 1 else jnp.concatenate(rows, axis=0)
    y_ref[...] = _rms(u * s, nrm_ref[...]).astype(y_ref.dtype)


def _sgu(z2, pw, wm, bias, blk, out_dtype):
    t = z2.shape[0]
    tile = SGU_TILE_CHUNKS * SGU_CHUNK
    vec = _const_spec((1, GROUP_W), lambda i: (0, 0))
    return pl.pallas_call(
        functools.partial(_sgu_kernel, blk=blk),
        grid=(t // tile,),
        in_specs=[pl.BlockSpec((tile, SGU_COLS), lambda i: (i, Z_SGU // SGU_COLS)),
                  vec, vec,
                  _const_spec((SGU_HEADS, SGU_CHUNK, SGU_CHUNK), lambda i: (0, 0, 0)),
                  _const_spec((SGU_CHUNK, SGU_HEADS), lambda i: (0, 0)),
                  vec],
        out_specs=[pl.BlockSpec((tile, GROUP_W), lambda i: (i, 0)),
                   pl.BlockSpec((tile, GROUP_W), lambda i: (i, 0))],
        out_shape=[jax.ShapeDtypeStruct((t, GROUP_W), out_dtype), jax.ShapeDtypeStruct((t, GROUP_W), F32)],
        compiler_params=_cparams(("parallel",)),
        name="sgu",
    )(z2, pw["ln_w"], pw["ln_b"], wm, bias, pw["norm"])


def _hgrn_chunks(seqs, lb, nw, chunk, sub):
    nh = HGRN_HEADS
    hs = lambda x, h: x[:, h * HGRN_HEAD:(h + 1) * HGRN_HEAD]
    tri = (lax.broadcasted_iota(jnp.int32, (chunk, chunk), 1)
           <= lax.broadcasted_iota(jnp.int32, (chunk, chunk), 0)).astype(BF16)
    eye = (lax.broadcasted_iota(jnp.int32, (HGRN_HEAD, HGRN_HEAD), 0)
           == lax.broadcasted_iota(jnp.int32, (HGRN_HEAD, HGRN_HEAD), 1)).astype(F32)
    rowid = lax.broadcasted_iota(jnp.int32, (sub, 1), 0)
    d = functools.partial(jnp.dot, preferred_element_type=F32)
    nt = lambda x, y: lax.dot_general(x, y, (((1,), (1,)), ((), ())), preferred_element_type=F32)
    tn = lambda x, y: lax.dot_general(x, y, (((0,), (0,)), ((), ())), preferred_element_type=F32)

    pre = []
    for zq, zf, zi, zg, states in seqs:
        q = _silu(zq)
        fg = lb + (1.0 - lb) * _sigmoid(zf)
        k = 1.0 - fg
        b = _dot_f32_rhs(tri, jnp.log(fg))
        pre.append(dict(q=q, k=k, v=zi, vb=zi.astype(BF16), b=b, qe=(q * jnp.exp(b)).astype(BF16)))
    o = [[d(hs(u["qe"], h), st[h].astype(BF16)) for h in range(nh)] for u, (_, _, _, _, st) in zip(pre, seqs)]
    outs = [[[] for _ in range(nh)] for _ in seqs]
    for i in range(chunk // sub):
        lo_, hi_ = i * sub, (i + 1) * sub
        oi = [[o[n][h][lo_:hi_] for h in range(nh)] for n in range(len(seqs))]
        if i > 0:
            for n, u in enumerate(pre):
                bref = u["b"][lo_ - 1:lo_]
                qt = (u["q"][lo_:hi_] * jnp.exp(u["b"][lo_:hi_] - bref)).astype(BF16)
                kt = (u["k"][:lo_] * jnp.exp(bref - u["b"][:lo_])).astype(BF16)
                att = [nt(hs(qt, h), hs(kt, h)).astype(BF16) for h in range(nh)]
                for h in range(nh):
                    oi[n][h] = oi[n][h] + d(att[h], hs(u["vb"], h)[:lo_])
        tiles = []
        for u in pre:
            qi, bi = u["q"][lo_:hi_], u["b"][lo_:hi_]
            for j in range(sub):
                sj = lo_ + j
                pj = qi * jnp.exp(jnp.minimum(bi - u["b"][sj:sj + 1], 0.0)) * u["k"][sj:sj + 1]
                tiles.extend(hs(pj, h) for h in range(nh))
        att = _group_sums(jnp.concatenate(tiles, axis=0), HGRN_HEAD, terms=1)
        t = 0
        for n, u in enumerate(pre):
            for j in range(sub):
                sj = lo_ + j
                for h in range(nh):
                    a_j = jnp.where(rowid >= j, att[t * sub:(t + 1) * sub], 0.0)
                    oi[n][h] = oi[n][h] + a_j * hs(u["v"], h)[sj:sj + 1]
                    t += 1
        for n in range(len(seqs)):
            for h in range(nh):
                outs[n][h].append(oi[n][h])
    results = []
    for n, (u, (_, _, _, zg, st)) in enumerate(zip(pre, seqs)):
        o_n = jnp.concatenate([outs[n][h][0] if len(outs[n][h]) == 1 else jnp.concatenate(outs[n][h], axis=0)
                               for h in range(nh)], axis=1)
        blast = u["b"][chunk - 1:chunk]
        kd = (u["k"] * jnp.exp(blast - u["b"])).astype(BF16)
        e_last = jnp.exp(blast)
        new_states = []
        for h in range(nh):
            e_col = _group_sums(eye * hs(e_last, h), HGRN_HEAD)
            new_states.append(e_col * st[h] + tn(hs(kd, h), hs(u["vb"], h)))
        ms = _group_sums(o_n * o_n, HGRN_HEAD) * (1.0 / HGRN_HEAD)
        results.append((o_n * lax.rsqrt(ms + NORM_EPS) * nw * _silu(zg), new_states))
    return results


def _hgrn_kernel(zq_ref, zf_ref, zi_ref, zg_ref, lg_ref, nw_ref, s0_ref, y_ref, sout_ref, s_scr,
                 *, layer, nseq, chunk, sub, n_chunks, fill_layer):
    lg = lg_ref[...]
    e = jnp.exp(lg - jnp.max(lg, axis=0, keepdims=True))
    sm = e / jnp.sum(e, axis=0, keepdims=True)
    lb = jnp.sum(sm[0:layer + 1], axis=0, keepdims=True) - sm[0:1]
    nw = nw_ref[...]
    nh = HGRN_HEADS
    sout = _own_layer(sout_ref, fill_layer)
    if n_chunks == 1:
        rows = lambda ref, b: ref[b * chunk:(b + 1) * chunk, :]
        seqs = [(rows(zq_ref, b), rows(zf_ref, b), rows(zi_ref, b), rows(zg_ref, b),
                 [s0_ref[b, h] for h in range(nh)]) for b in range(nseq)]
        ys = []
        for b, (y, states) in enumerate(_hgrn_chunks(seqs, lb, nw, chunk, sub)):
            ys.append(y)
            for h in range(nh):
                sout[b, h] = states[h]
        y_ref[...] = jnp.concatenate(ys, axis=0).astype(y_ref.dtype)
    else:
        s_scr[...] = s0_ref[...]

        def chunk_body(c, carry):
            r0 = pl.multiple_of(c * chunk, chunk)
            rows = lambda ref, b: ref[b, pl.ds(r0, chunk), :]
            seqs = [(rows(zq_ref, b), rows(zf_ref, b), rows(zi_ref, b), rows(zg_ref, b),
                     [s_scr[b, h] for h in range(nh)]) for b in range(nseq)]
            for b, (y, states) in enumerate(_hgrn_chunks(seqs, lb, nw, chunk, sub)):
                y_ref[b, pl.ds(r0, chunk), :] = y.astype(y_ref.dtype)
                for h in range(nh):
                    s_scr[b, h] = states[h]
            return carry

        lax.fori_loop(0, n_chunks, chunk_body, 0)
        sout[...] = s_scr[...]


def _hgrn(z2, s0, lb_logits, norm_w, layer, b, l, prev_stack, out_dtype):
    chunk = min(HGRN_CHUNK, l)
    sub = min(HGRN_SUB, chunk)
    cb = Z_HGRN // GROUP_W
    if l > chunk:
        nseq = 1
        z_in = z2.reshape(b, l, Z_COLS)
        zspec = lambda part: pl.BlockSpec((1, l, GROUP_W), lambda i: (i, 0, cb + part))
        yspec = pl.BlockSpec((1, l, GROUP_W), lambda i: (i, 0, 0))
        yshape = jax.ShapeDtypeStruct((b, l, GROUP_W), out_dtype)
    else:
        nseq = SHORT_SEQS
        z_in = z2
        zspec = lambda part: pl.BlockSpec((nseq * l, GROUP_W), lambda i: (i, cb + part))
        yspec = pl.BlockSpec((nseq * l, GROUP_W), lambda i: (i, 0))
        yshape = jax.ShapeDtypeStruct((b * l, GROUP_W), out_dtype)
    first = prev_stack is None
    tail = (nseq, HGRN_HEADS, HGRN_HEAD, HGRN_HEAD)
    y, s_new = _call_stacked(
        functools.partial(_hgrn_kernel, layer=layer, nseq=nseq, chunk=chunk, sub=sub, n_chunks=l // chunk,
                          fill_layer=layer if first else None),
        [z_in, z_in, z_in, z_in, lb_logits, norm_w, s0],
        [zspec(0), zspec(1), zspec(2), zspec(3),
         _const_spec((DEPTH, GROUP_W), lambda i: (0, 0)),
         _const_spec((1, GROUP_W), lambda i: (0, 0)),
         _stack_out_spec(tail, layer, False)],
        1, prev_stack,
        grid=(b // nseq,),
        out_specs=[yspec, _stack_out_spec(tail, layer, first)],
        out_shape=[yshape, jax.ShapeDtypeStruct((DEPTH, b, HGRN_HEADS, HGRN_HEAD, HGRN_HEAD), F32)],
        scratch_shapes=[pltpu.VMEM((nseq, HGRN_HEADS, HGRN_HEAD, HGRN_HEAD), F32)],
        compiler_params=_cparams(("parallel",)),
        name="hgrn2")
    return y.reshape(b * l, GROUP_W), s_new


def _pool_kernel(z_ref, hist_ref, pw_ref, ps_ref, y_ref, hout_ref, xbuf, s2buf, s4buf, s8buf,
                 *, nseq, tc, n_tiles, start_pos, flat):
    halo = POOL_HALO
    n = tc + halo
    c1 = POOL_CH
    for b in range(nseq):
        xbuf[b, pl.ds(0, 16), :] = jnp.zeros((16, GROUP_W), F32)
        xbuf[b, pl.ds(16, 16), :] = hist_ref[b]

    def tile(r0, b, z):
        xbuf[b, pl.ds(halo, tc), :] = z
        s2buf[b, pl.ds(8, n - 8), :] = xbuf[b, pl.ds(8, n - 8), :] + xbuf[b, pl.ds(7, n - 8), :]
        s4buf[b, pl.ds(16, n - 16), :] = s2buf[b, pl.ds(16, n - 16), c1:] + s2buf[b, pl.ds(14, n - 16), c1:]
        s8buf[b, pl.ds(24, n - 24), :] = s4buf[b, pl.ds(24, n - 24), c1:] + s4buf[b, pl.ds(20, n - 24), c1:]
        s16 = s8buf[b, pl.ds(halo, tc), c1:] + s8buf[b, pl.ds(halo - 8, tc), c1:]
        sums = (s2buf[b, pl.ds(halo, tc), 0:c1], s4buf[b, pl.ds(halo, tc), 0:c1], s8buf[b, pl.ds(halo, tc), 0:c1],
                s16)
        pos = start_pos + r0 + lax.broadcasted_iota(jnp.int32, (tc, 1), 0)
        ys = []
        for gi, win in enumerate(POOL_WINDOWS):
            sl = slice(gi * POOL_CH, (gi + 1) * POOL_CH)
            cnt = jnp.minimum(pos + 1, win).astype(F32)
            dlt = sums[gi] / cnt - z[:, sl]
            ys.append(jnp.dot(dlt.astype(BF16), pw_ref[gi], preferred_element_type=F32) * ps_ref[:, sl])
        if n_tiles > 1:
            xbuf[b, pl.ds(16, 16), :] = xbuf[b, pl.ds(n - 16, 16), :]
        return jnp.concatenate(ys, axis=1)

    if flat:
        y_ref[...] = jnp.concatenate([tile(0, b, z_ref[b * tc:(b + 1) * tc, :]) for b in range(nseq)],
                                     axis=0).astype(y_ref.dtype)
    else:
        def tile_body(ti, carry):
            r0 = pl.multiple_of(ti * tc, tc)
            for b in range(nseq):
                y_ref[b, pl.ds(r0, tc), :] = tile(r0, b, z_ref[b, pl.ds(r0, tc), :]).astype(y_ref.dtype)
            return carry

        lax.fori_loop(0, n_tiles, tile_body, 0)
    for b in range(nseq):
        hout_ref[b] = xbuf[b, pl.ds(n - POOL_HIST, POOL_HIST), :]


def _pool(z2, hist16, pool_w, pool_scale, start_pos, b, l, tc, out_dtype):
    cb = Z_POOL // GROUP_W
    if l >= POOL_HALO:
        nseq = 1
        z_in = z2.reshape(b, l, Z_COLS)
        zspec = pl.BlockSpec((1, l, GROUP_W), lambda i: (i, 0, cb))
        yspec = pl.BlockSpec((1, l, GROUP_W), lambda i: (i, 0, 0))
        yshape = jax.ShapeDtypeStruct((b, l, GROUP_W), out_dtype)
    else:
        nseq = SHORT_SEQS
        z_in = z2
        zspec = pl.BlockSpec((nseq * l, GROUP_W), lambda i: (i, cb))
        yspec = pl.BlockSpec((nseq * l, GROUP_W), lambda i: (i, 0))
        yshape = jax.ShapeDtypeStruct((b * l, GROUP_W), out_dtype)
    n = tc + POOL_HALO
    y, hist_new = pl.pallas_call(
        functools.partial(_pool_kernel, nseq=nseq, tc=tc, n_tiles=l // tc, start_pos=start_pos, flat=nseq > 1),
        grid=(b // nseq,),
        in_specs=[zspec,
                  pl.BlockSpec((nseq, 16, GROUP_W), lambda i: (i, 0, 0)),
                  _const_spec((len(POOL_WINDOWS), POOL_CH, POOL_CH), lambda i: (0, 0, 0)),
                  _const_spec((1, GROUP_W), lambda i: (0, 0))],
        out_specs=[yspec, pl.BlockSpec((nseq, POOL_HIST, GROUP_W), lambda i: (i, 0, 0))],
        out_shape=[yshape, jax.ShapeDtypeStruct((b, POOL_HIST, GROUP_W), F32)],
        scratch_shapes=[pltpu.VMEM((nseq, n, GROUP_W), F32), pltpu.VMEM((nseq, n, GROUP_W), F32),
                        pltpu.VMEM((nseq, n, 3 * POOL_CH), F32), pltpu.VMEM((nseq, n, 2 * POOL_CH), F32)],
        compiler_params=_cparams(("parallel",)),
        name="pool",
    )(z_in, hist16, pool_w, pool_scale)
    return y.reshape(b * l, GROUP_W), hist_new


def _pad_cols(a, n):
    return jnp.pad(a, ((0, 0),) * (a.ndim - 1) + ((0, n - a.shape[-1]),))


def _w_in_kernel(w_ref, o_ref):
    c1, c2, c3 = RWKV_COLS, RWKV_COLS + SGU_COLS, RWKV_COLS + SGU_COLS + HGRN_COLS
    o_ref[:, Z_HGRN:Z_HGRN + HGRN_COLS] = w_ref[:, c2:c3].astype(BF16)
    o_ref[:, Z_SGU:Z_SGU + SGU_COLS] = w_ref[:, c1:c2].astype(BF16)
    o_ref[:, Z_POOL:Z_POOL + GROUP_W] = w_ref[:, c3:].astype(BF16)
    o_ref[:, Z_RWKV:Z_RWKV + RWKV_COLS] = w_ref[:, :c1].astype(BF16)
    o_ref[:, Z_RWKV + RWKV_COLS:] = jnp.zeros((o_ref.shape[0], RWKV_PAD - RWKV_COLS), BF16)


def _w_in_layout(w_in, rows=256):
    in_cols = w_in.shape[2]
    return pl.pallas_call(
        _w_in_kernel,
        grid=(DEPTH, D_MODEL // rows),
        in_specs=[pl.BlockSpec((None, rows, in_cols), lambda l, i: (l, i, 0))],
        out_specs=pl.BlockSpec((None, rows, Z_COLS), lambda l, i: (l, i, 0)),
        out_shape=jax.ShapeDtypeStruct((DEPTH, D_MODEL, Z_COLS), BF16),
        compiler_params=_cparams(("parallel", "parallel")),
        name="w_in_layout",
    )(w_in)


def _prep_weights(W):
    w_gu = W["ffn_w_gu"].astype(BF16).reshape(DEPTH, D_MODEL, 2 * D_FF // FFN_TF, FFN_TF).transpose(0, 2, 1, 3)
    return dict(w_in=_w_in_layout(W["w_in"]), w_out=W["w_out"].astype(BF16), w_gu=w_gu,
                w_down=W["ffn_w_down"].astype(BF16), ple_gate=W["ple_gate"].astype(BF16),
                ple_proj=W["ple_proj"].astype(BF16))


def _prep_layer(i, W):
    row = lambda a: a.reshape(1, -1)
    lora = lambda a, off: jnp.pad(a, ((off, LORA_W - off - a.shape[0]), (0, 0))).astype(BF16)
    rw = dict(
        mu=_pad_cols(row(W["rwkv_mu"][i]), RWKV_PAD),
        lora_w=lora(W["rwkv_w_lora"][i], 0), w0=row(W["rwkv_w0"][i]),
        lora_a=lora(W["rwkv_a_lora"][i], W_LORA), a0=row(W["rwkv_a0"][i]),
        lora_g=lora(W["rwkv_g_lora"][i], W_LORA + A_LORA),
        k_k=row(W["rwkv_k_k"][i]), k_a=row(W["rwkv_k_a"][i]), r_k=row(W["rwkv_r_k"][i]),
        gn_w=row(W["rwkv_gn_w"][i]), gn_b=row(W["rwkv_gn_b"][i]))
    sg = dict(ln_w=row(W["sgu_ln_w"][i]), ln_b=row(W["sgu_ln_b"][i]), norm=row(W["sgu_norm"][i]))
    return dict(
        rw=rw, sg=sg, sgu_w=W["sgu_w"][i], sgu_b=W["sgu_b"][i],
        hgrn_norm=row(W["hgrn_norm"][i]), pool_w=W["pool_w"][i].astype(BF16), pool_scale=row(W["pool_scale"][i]),
        ln_mix_pre=row(W["ln_mix_pre"][i]), ln_mix_post=row(W["ln_mix_post"][i]),
        ln_ffn_pre=row(W["ln_ffn_pre"][i]), ln_ffn_post=row(W["ln_ffn_post"][i]))


def _trunk(x, p, wkv0, shift0, hgrn0, pool0, start_pos, big, layers, lb_logits, cfg):
    b, l, _ = x.shape
    t = b * l
    xf = x.reshape(t, D_MODEL)
    p2 = p.reshape(DEPTH, t, PLE_DIM)
    wkv_new = hgrn_new = None
    shift_l, pool_l, sgu_l = [], [], []
    for i, lw in enumerate(layers):
        z2 = _inproj(xf, lw["ln_mix_pre"], big["w_in"], i, min(cfg["tm_in"], t), cfg["tn_in"])
        y_r, wkv_new, shift_new = _rwkv(z2, _pad_cols(shift0[i], RWKV_PAD), wkv0, lw["rw"], i, b, l, wkv_new,
                                        cfg["ydt"])
        if cfg["sgu_blk"] == SGU_CHUNK:
            wm, bias = lw["sgu_w"], lw["sgu_b"].T
        else:
            rep = SGU_CHUNK // l
            wm = jnp.tile(lw["sgu_w"][:, :l, :l], (1, rep, rep))
            bias = jnp.tile(lw["sgu_b"][:, :l].T, (rep, 1))
        y_s, v_rows = _sgu(z2, lw["sg"], wm, bias, cfg["sgu_blk"], cfg["ydt"])
        y_h, hgrn_new = _hgrn(z2, hgrn0, lb_logits, lw["hgrn_norm"], i, b, l, hgrn_new, cfg["ydt"])
        hist16 = jnp.pad(pool0[i], ((0, 0), (1, 0), (0, 0)))
        y_p, pool_new = _pool(z2, hist16, lw["pool_w"], lw["pool_scale"], start_pos, b, l, cfg["tc_pool"],
                              cfg["ydt"])
        xf = _outproj([y_r, y_s, y_h, y_p], xf, big["w_out"], lw["ln_mix_post"], i, cfg["tm"])
        xf = _ffn(xf, lw["ln_ffn_pre"], big["w_gu"], big["w_down"], lw["ln_ffn_post"], p2, big["ple_gate"],
                  big["ple_proj"], i, cfg["tm"])
        shift_l.append(shift_new)
        pool_l.append(pool_new)
        sgu_l.append(v_rows.reshape(b, l, GROUP_W))
    return (xf.reshape(b, l, D_MODEL), wkv_new, jnp.stack(shift_l), hgrn_new, jnp.stack(pool_l), jnp.stack(sgu_l))


def _cfg(l):
    if l >= 256:
        return dict(tm=512, tm_in=1024, tn_in=1792, sgu_blk=SGU_CHUNK, tc_pool=256, ydt=BF16)
    return dict(tm=512, tm_in=512, tn_in=1792, sgu_blk=l, tc_pool=l, ydt=BF16)


def kernel(x_prompt, x_sample, state_rwkv_wkv, state_rwkv_shift, state_hgrn, state_pool, p_prompt, p_sample,
           ln_mix_pre, ln_mix_post, ln_ffn_pre, ln_ffn_post, w_in, rwkv_mu, rwkv_w_lora, rwkv_w0, rwkv_a_lora,
           rwkv_a0, rwkv_g_lora, rwkv_k_k, rwkv_k_a, rwkv_r_k, rwkv_gn_w, rwkv_gn_b, sgu_ln_w, sgu_ln_b, sgu_w,
           sgu_b, sgu_norm, hgrn_lb_logits, hgrn_norm, pool_w, pool_scale, w_out, ffn_w_gu, ffn_w_down, ple_gate,
           ple_proj):
    W = dict(ln_mix_pre=ln_mix_pre, ln_mix_post=ln_mix_post, ln_ffn_pre=ln_ffn_pre, ln_ffn_post=ln_ffn_post,
             w_in=w_in, rwkv_mu=rwkv_mu, rwkv_w_lora=rwkv_w_lora, rwkv_w0=rwkv_w0, rwkv_a_lora=rwkv_a_lora,
             rwkv_a0=rwkv_a0, rwkv_g_lora=rwkv_g_lora, rwkv_k_k=rwkv_k_k, rwkv_k_a=rwkv_k_a,
             rwkv_r_k=rwkv_r_k.reshape(DEPTH, GROUP_W), rwkv_gn_w=rwkv_gn_w, rwkv_gn_b=rwkv_gn_b, sgu_ln_w=sgu_ln_w,
             sgu_ln_b=sgu_ln_b, sgu_w=sgu_w, sgu_b=sgu_b, sgu_norm=sgu_norm, hgrn_norm=hgrn_norm, pool_w=pool_w,
             pool_scale=pool_scale, w_out=w_out, ffn_w_gu=ffn_w_gu, ffn_w_down=ffn_w_down, ple_gate=ple_gate,
             ple_proj=ple_proj)
    big = _prep_weights(W)
    layers = [_prep_layer(i, W) for i in range(DEPTH)]
    lb_logits = hgrn_lb_logits.astype(F32)
    bp, lp, _ = x_prompt.shape
    zeros = lambda *s: jnp.zeros((DEPTH, bp) + s, F32)
    y_prompt, wkv_p, shift_p, hgrn_p, pool_p, _ = _trunk(
        x_prompt, p_prompt, zeros(RWKV_HEADS, RWKV_HEAD, RWKV_HEAD), zeros(RWKV_COLS),
        zeros(HGRN_HEADS, HGRN_HEAD, HGRN_HEAD), zeros(POOL_HIST, GROUP_W), 0, big, layers, lb_logits, _cfg(lp))
    y_sample, wkv_s, shift_s, hgrn_s, pool_s, sgu_v_s = _trunk(
        x_sample, p_sample, state_rwkv_wkv, state_rwkv_shift, state_hgrn, state_pool, PAST_LEN, big, layers,
        lb_logits, _cfg(x_sample.shape[1]))
    return (y_prompt, y_sample, wkv_p, shift_p, hgrn_p, pool_p, wkv_s, shift_s, hgrn_s, pool_s, sgu_v_s)
```

```python
import functools

import jax
import jax.numpy as jnp
from jax import lax
from jax.experimental import pallas as pl
from jax.experimental.pallas import tpu as pltpu

F32 = jnp.float32
BF16 = jnp.bfloat16

D_MODEL = 2048
DEPTH = 2
PAST_LEN = 16384
GROUP_W = 512
RWKV_HEAD = 64
RWKV_HEADS = 8
RWKV_PAIRS = RWKV_HEADS // 2
RWKV_CHUNK = 64
RWKV_TILE_CHUNKS = 4
RWKV_TILE = RWKV_TILE_CHUNKS * RWKV_CHUNK
W_LORA, A_LORA, G_LORA = 32, 32, 96
RWKV_GN_EPS = 64e-5
SGU_CHUNK = 128
SGU_HEADS = 4
SGU_TILE_CHUNKS = 4
HGRN_HEADS = 4
HGRN_HEAD = 128
HGRN_CHUNK = 64
HGRN_SUB = 16
POOL_WINDOWS = (2, 4, 8, 16)
POOL_CH = 128
POOL_HIST = 15
POOL_HALO = 32
D_FF = 5632
PLE_DIM = 256
PLE_COLS = 512
FFN_TF = 512
NORM_EPS = 1e-6
LN_EPS = 1e-5
RWKV_COLS = 3 * GROUP_W + W_LORA + A_LORA + G_LORA
RWKV_PAD = 1792
LORA_OFF = 3 * GROUP_W
LORA_W = RWKV_PAD - LORA_OFF
SGU_COLS = 2 * GROUP_W
HGRN_COLS = 4 * GROUP_W
Z_HGRN, Z_SGU, Z_POOL, Z_RWKV = 0, 2048, 3072, 3584
Z_COLS = Z_RWKV + RWKV_PAD
SHORT_SEQS = 8
VMEM_LIMIT = 56 * 1024 * 1024


def _cparams(sem):
    return pltpu.CompilerParams(dimension_semantics=sem, vmem_limit_bytes=VMEM_LIMIT)


def _const_spec(shape, index_map):
    return pl.BlockSpec(shape, index_map, pipeline_mode=pl.Buffered(1))


def _sigmoid(x):
    return jax.nn.sigmoid(x)


def _silu(x):
    return x * jax.nn.sigmoid(x)


def _split3(x):
    hi = x.astype(BF16)
    r = x - hi.astype(F32)
    mid = r.astype(BF16)
    lo = (r - mid.astype(F32)).astype(BF16)
    return hi, mid, lo


def _group_sums(x, width, terms=3):
    rows, cols = x.shape
    nblk = cols // 128
    xs = x if nblk == 1 else jnp.concatenate([x[:, i * 128:(i + 1) * 128] for i in range(nblk)], axis=0)
    ones = ((lax.broadcasted_iota(jnp.int32, (128, 128), 0) // width)
            == (lax.broadcasted_iota(jnp.int32, (128, 128), 1) // width)).astype(BF16)
    s = None
    for part in _split3(xs)[:terms]:
        y = jnp.dot(part, ones, preferred_element_type=F32)
        s = y if s is None else s + y
    return s if nblk == 1 else jnp.concatenate([s[i * rows:(i + 1) * rows] for i in range(nblk)], axis=1)


def _dot_f32_rhs(m, x):
    hi, mid, lo = _split3(x)
    d = functools.partial(jnp.dot, preferred_element_type=F32)
    return d(m, hi) + d(m, mid) + d(m, lo)


def _rms(x, g):
    return x * lax.rsqrt(jnp.mean(x * x, axis=-1, keepdims=True) + NORM_EPS) * g


def _call_stacked(kernel, inputs, in_specs, stack_out, prev_stack, **kw):
    if prev_stack is None:
        return pl.pallas_call(kernel, in_specs=in_specs, **kw)(*inputs)
    n_in = len(inputs)

    def body(*refs):
        kernel(*refs[:n_in], *refs[n_in + 1:])

    return pl.pallas_call(body, in_specs=list(in_specs) + [pl.BlockSpec(memory_space=pl.ANY)],
                          input_output_aliases={n_in: stack_out}, **kw)(*inputs, prev_stack)


def _stack_out_spec(tail, layer, first):
    zeros = (0,) * (len(tail) - 1)
    if first:
        return pl.BlockSpec((DEPTH,) + tail, lambda i: (0, i) + zeros)
    return pl.BlockSpec((None,) + tail, lambda i: (layer, i) + zeros)


def _own_layer(sout_ref, fill_layer):
    if fill_layer is None:
        return sout_ref
    for j in range(DEPTH):
        if j != fill_layer:
            sout_ref[j] = jnp.zeros(sout_ref.shape[1:], sout_ref.dtype)
    return sout_ref.at[fill_layer]


def _inproj_kernel(x_ref, g_ref, w_ref, z_ref, h_ref):
    @pl.when(pl.program_id(1) == 0)
    def _():
        h_ref[...] = _rms(x_ref[...], g_ref[...]).astype(BF16)

    z_ref[...] = jnp.dot(h_ref[...], w_ref[...], preferred_element_type=F32)


def _inproj(x, g, w, layer, tm, tn):
    t = x.shape[0]
    return pl.pallas_call(
        _inproj_kernel,
        grid=(t // tm, Z_COLS // tn),
        in_specs=[pl.BlockSpec((tm, D_MODEL), lambda i, j: (i, 0)),
                  _const_spec((1, D_MODEL), lambda i, j: (0, 0)),
                  pl.BlockSpec((None, D_MODEL, tn), lambda i, j: (layer, 0, j))],
        out_specs=pl.BlockSpec((tm, tn), lambda i, j: (i, j)),
        out_shape=jax.ShapeDtypeStruct((t, Z_COLS), F32),
        scratch_shapes=[pltpu.VMEM((tm, D_MODEL), BF16)],
        compiler_params=_cparams(("parallel", "arbitrary")),
        name="inproj",
    )(x, g, w)


def _outproj_kernel(yr_ref, ys_ref, yh_ref, yp_ref, x_ref, w_ref, g_ref, o_ref):
    d = functools.partial(jnp.dot, preferred_element_type=F32)
    g4 = GROUP_W
    mix = (d(yr_ref[...], w_ref[0:g4, :]) + d(ys_ref[...], w_ref[g4:2 * g4, :])
           + d(yh_ref[...], w_ref[2 * g4:3 * g4, :]) + d(yp_ref[...], w_ref[3 * g4:4 * g4, :]))
    o_ref[...] = x_ref[...] + _rms(mix, g_ref[...])


def _outproj(ys, x, w, g, layer, tm):
    t = x.shape[0]
    yspec = pl.BlockSpec((tm, GROUP_W), lambda i: (i, 0))
    return pl.pallas_call(
        _outproj_kernel,
        grid=(t // tm,),
        in_specs=[yspec, yspec, yspec, yspec,
                  pl.BlockSpec((tm, D_MODEL), lambda i: (i, 0)),
                  _const_spec((None, D_MODEL, D_MODEL), lambda i: (layer, 0, 0)),
                  _const_spec((1, D_MODEL), lambda i: (0, 0))],
        out_specs=pl.BlockSpec((tm, D_MODEL), lambda i: (i, 0)),
        out_shape=jax.ShapeDtypeStruct((t, D_MODEL), F32),
        compiler_params=_cparams(("parallel",)),
        name="outproj",
    )(*ys, x, w, g)


def _ffn_kernel(x_ref, gpre_ref, wg_ref, wu_ref, wd_ref, gpost_ref, p_ref, wgate_ref, wproj_ref, o_ref, h_ref, acc_ref):
    j = pl.program_id(1)

    @pl.when(j == 0)
    def _():
        h_ref[...] = _rms(x_ref[...], gpre_ref[...]).astype(BF16)
        acc_ref[...] = jnp.zeros_like(acc_ref)

    h = h_ref[...]
    gate = jnp.dot(h, wg_ref[...], preferred_element_type=F32)
    up = jnp.dot(h, wu_ref[...], preferred_element_type=F32)
    act = (_silu(gate) * up).astype(BF16)
    acc_ref[...] += jnp.dot(act, wd_ref[...], preferred_element_type=F32)

    @pl.when(j == pl.num_programs(1) - 1)
    def _():
        acc_ref[...] = x_ref[...] + _rms(acc_ref[...], gpost_ref[...])
        h_ref[...] = acc_ref[...].astype(BF16)
        pb = p_ref[...].astype(BF16)
        for c in range(D_MODEL // PLE_COLS):
            cs = slice(c * PLE_COLS, (c + 1) * PLE_COLS)
            pgate = _sigmoid(jnp.dot(h_ref[...], wgate_ref[:, cs], preferred_element_type=F32))
            ple = jnp.dot(pb, wproj_ref[:, cs], preferred_element_type=F32)
            o_ref[:, cs] = acc_ref[:, cs] + pgate * ple


def _ffn(x, gpre, w_gu, w_down, gpost, p, w_gate, w_proj, layer, tm):
    t = x.shape[0]
    tf = FFN_TF
    nf = D_FF // tf
    return pl.pallas_call(
        _ffn_kernel,
        grid=(t // tm, nf),
        in_specs=[pl.BlockSpec((tm, D_MODEL), lambda i, j: (i, 0)),
                  _const_spec((1, D_MODEL), lambda i, j: (0, 0)),
                  pl.BlockSpec((None, D_MODEL, tf), lambda i, j: (layer, 0, j)),
                  pl.BlockSpec((None, D_MODEL, tf), lambda i, j: (layer, 0, j + nf)),
                  pl.BlockSpec((None, tf, D_MODEL), lambda i, j: (layer, j, 0)),
                  _const_spec((1, D_MODEL), lambda i, j: (0, 0)),
                  pl.BlockSpec((None, tm, PLE_DIM), lambda i, j: (layer, i, 0)),
                  _const_spec((None, D_MODEL, D_MODEL), lambda i, j: (layer, 0, 0)),
                  _const_spec((None, PLE_DIM, D_MODEL), lambda i, j: (layer, 0, 0))],
        out_specs=pl.BlockSpec((tm, D_MODEL), lambda i, j: (i, 0)),
        out_shape=jax.ShapeDtypeStruct((t, D_MODEL), F32),
        scratch_shapes=[pltpu.VMEM((tm, D_MODEL), BF16), pltpu.VMEM((tm, D_MODEL), F32)],
        compiler_params=_cparams(("parallel", "arbitrary")),
        name="ffn",
    )(x, gpre, w_gu, w_gu, w_down, gpost, p, w_gate, w_proj)


def _rwkv_gates(z, z_prev, mu_ref, lw_ref, w0_ref, la_ref, a0_ref, lg_ref, kk_ref, ka_ref, rk_ref):
    zm = z + mu_ref[...] * (z_prev - z)
    g4 = GROUP_W
    r = zm[:, 0:g4]
    k = zm[:, g4:2 * g4]
    v = zm[:, 2 * g4:3 * g4]
    zl = zm[:, LORA_OFF:RWKV_PAD]
    d = functools.partial(jnp.dot, preferred_element_type=F32)
    wl = w0_ref[...] + d(jnp.tanh(zl).astype(BF16), lw_ref[...])
    w_log = -(jnp.maximum(-wl, 0.0) + jnp.log1p(jnp.exp(-jnp.abs(wl)))) - 0.5
    a = _sigmoid(a0_ref[...] + d(zl.astype(BF16), la_ref[...]))
    g = d(_sigmoid(zl).astype(BF16), lg_ref[...])
    kk = k * kk_ref[...]
    kk = kk / jnp.maximum(jnp.sqrt(_group_sums(kk * kk, RWKV_HEAD)), 1e-12)
    k = k * (1.0 + (a - 1.0) * ka_ref[...])
    bonus = _group_sums(r * k * rk_ref[...], RWKV_HEAD) * v
    return r, k, v, -kk, kk * a, w_log, g, bonus


def _rwkv_post(y, bonus, g, gnw_ref, gnb_ref):
    inv_n = 1.0 / RWKV_HEAD
    mean = _group_sums(y, RWKV_HEAD) * inv_n
    yc = y - mean
    var = _group_sums(yc * yc, RWKV_HEAD) * inv_n
    yn = yc * lax.rsqrt(var + RWKV_GN_EPS) * gnw_ref[...] + gnb_ref[...]
    return (yn + bonus) * g


def _rwkv_scan_kernel(z_ref, sh_ref, s0_ref, mu_ref, lw_ref, w0_ref, la_ref, a0_ref, lg_ref, kk_ref, ka_ref, rk_ref,
                      gnw_ref, gnb_ref, y_ref, sout_ref, shout_ref, zbuf, *, nseq, l, fill_layer):
    rows = nseq * l
    z = z_ref[...]
    shout_ref[...] = jnp.concatenate([z[(b + 1) * l - 1:(b + 1) * l] for b in range(nseq)], axis=0)
    zbuf[pl.ds(0, 8), :] = jnp.zeros((8, RWKV_PAD), F32)
    zbuf[pl.ds(8, rows), :] = z
    rid = lax.broadcasted_iota(jnp.int32, (rows, 1), 0)
    spread = ((lax.broadcasted_iota(jnp.int32, (rows, nseq), 0) // l)
              == lax.broadcasted_iota(jnp.int32, (rows, nseq), 1)).astype(BF16)
    z_prev = jnp.where(rid % l == 0, _dot_f32_rhs(spread, sh_ref[...]), zbuf[pl.ds(7, rows), :])
    r, k, v, a_vec, b_vec, w_log, g, bonus = _rwkv_gates(
        z, z_prev, mu_ref, lw_ref, w0_ref, la_ref, a0_ref, lg_ref, kk_ref, ka_ref, rk_ref)
    w = jnp.exp(-jnp.exp(w_log))

    shp = (RWKV_HEAD, 2 * RWKV_HEAD)
    lane = lax.broadcasted_iota(jnp.int32, shp, 1)
    sub = lax.broadcasted_iota(jnp.int32, shp, 0)
    lo = lane < RWKV_HEAD
    eye = ((lane % RWKV_HEAD) == sub).astype(F32)

    def head_sums(tiles):
        s = _group_sums(jnp.concatenate(tiles, axis=0), RWKV_HEAD, terms=2)
        return [s[u * RWKV_HEAD:(u + 1) * RWKV_HEAD] for u in range(len(tiles))]

    units = [(b, p) for b in range(nseq) for p in range(RWKV_PAIRS)]
    row = lambda x, b, p, j: x[b * l + j:b * l + j + 1, p * 128:(p + 1) * 128]
    state = [jnp.concatenate([s0_ref[b, 2 * p], s0_ref[b, 2 * p + 1]], axis=1) for b, p in units]
    y_rows = {}
    for j in range(l):
        sa = head_sums([s * row(a_vec, b, p, j) for s, (b, p) in zip(state, units)])
        vcol = head_sums([eye * row(v, b, p, j) for b, p in units])
        state = [s * row(w, b, p, j) + sa_u * row(b_vec, b, p, j) + vc_u * row(k, b, p, j)
                 for s, sa_u, vc_u, (b, p) in zip(state, sa, vcol, units)]
        ycol = head_sums([s * row(r, b, p, j) for s, (b, p) in zip(state, units)])
        for yc_u, (b, p) in zip(ycol, units):
            y_rows[b * l + j, p] = jnp.sum(eye * yc_u, axis=0, keepdims=True)
    sout = _own_layer(sout_ref, fill_layer)
    for s, (b, p) in zip(state, units):
        sout[b, 2 * p] = s[:, 0:RWKV_HEAD]
        sout[b, 2 * p + 1] = s[:, RWKV_HEAD:2 * RWKV_HEAD]
    y = jnp.concatenate([jnp.concatenate([y_rows[i, p] for i in range(rows)], axis=0) for p in range(RWKV_PAIRS)],
                        axis=1)
    y_ref[...] = _rwkv_post(y, bonus, g, gnw_ref, gnb_ref).astype(y_ref.dtype)


def _tri_inverse(l_mats, n):
    row = lax.broadcasted_iota(jnp.int32, (n, n), 0)
    col = lax.broadcasted_iota(jnp.int32, (n, n), 1)
    eye = (row == col).astype(F32)
    mm = lambda a, b: jnp.dot(a.astype(BF16), b.astype(BF16), preferred_element_type=F32)
    same = lambda m: (row // m) == (col // m)
    d1 = [jnp.where(same(8), x, 0.0) for x in l_mats]
    d2 = [mm(x, x) for x in d1]
    d4 = [mm(x, x) for x in d2]
    d3 = [mm(x, y) for x, y in zip(d1, d2)]
    t = [eye + x1 + x2 + x3 for x1, x2, x3 in zip(d1, d2, d3)]
    t = [x + mm(x, y) for x, y in zip(t, d4)]
    m = 8
    while m < RWKV_CHUNK:
        e = [jnp.where(same(2 * m) & jnp.logical_not(same(m)), x, 0.0) for x in l_mats]
        te = [mm(x, y) for x, y in zip(t, e)]
        t = [x + mm(y, x) for x, y in zip(t, te)]
        m *= 2
    return t


def _rwkv_chunk_local(chunks):
    c = RWKV_CHUNK
    n = 2 * c
    lo64 = lax.broadcasted_iota(jnp.int32, (c, 128), 1) < RWKV_HEAD
    tri = (lax.broadcasted_iota(jnp.int32, (c, c), 1) <= lax.broadcasted_iota(jnp.int32, (c, c), 0)).astype(BF16)
    stack = lambda x: jnp.concatenate([jnp.where(lo64, x, 0.0), jnp.where(lo64, 0.0, x)], axis=0)
    ops = []
    for r, k, v, a_vec, b_vec, logw in chunks:
        cum = _dot_f32_rhs(tri, logw)
        c_end = cum[c - 1:c]
        e_neg = jnp.exp(-cum)
        e_end = jnp.exp(c_end - cum)
        at = a_vec * jnp.exp(cum - logw)
        rt = r * jnp.exp(cum)
        bt = b_vec * e_neg
        kt = k * e_neg
        bh = b_vec * e_end
        kh = k * e_end
        e_cend = jnp.exp(c_end)
        for p in range(RWKV_PAIRS):
            sl = slice(p * 128, (p + 1) * 128)
            ops.append(dict(xa=stack(at[:, sl]), xr=stack(rt[:, sl]), yb=stack(bt[:, sl]), yk=stack(kt[:, sl]),
                            vs=stack(v[:, sl]).astype(BF16), xbh=stack(bh[:, sl]), xkh=stack(kh[:, sl]),
                            e_cend=e_cend[:, sl]))

    row = lax.broadcasted_iota(jnp.int32, (n, n), 0)
    col = lax.broadcasted_iota(jnp.int32, (n, n), 1)
    same_head = (row // c) == (col // c)
    strict = same_head & (col < row)
    incl = same_head & (col <= row)
    zeros = jnp.zeros((n, n), BF16)
    mm = lambda x, y: jnp.dot(x, y, preferred_element_type=F32)
    nt = lambda x, y: lax.dot_general(x, y, (((1,), (1,)), ((), ())), preferred_element_type=F32)
    tn = lambda x, y: lax.dot_general(x, y, (((0,), (0,)), ((), ())), preferred_element_type=F32)
    unstack = lambda x: x[0:c] + x[c:n]

    gram = [nt(jnp.concatenate([o["xa"], o["xr"]], axis=0).astype(BF16),
               jnp.concatenate([o["yb"], o["yk"]], axis=0).astype(BF16)) for o in ops]
    l_ab = [jnp.where(strict, x[0:n, 0:n], 0.0) for x in gram]
    l_ak = [jnp.where(strict, x[0:n, n:2 * n], 0.0) for x in gram]
    m_rbk = [jnp.concatenate([jnp.where(incl, x[n:2 * n, 0:n], 0.0), jnp.where(incl, x[n:2 * n, n:2 * n], 0.0)],
                             axis=1).astype(BF16) for x in gram]
    w1 = [mm(x.astype(BF16), o["vs"]) for x, o in zip(l_ak, ops)]
    t_inv = _tri_inverse(l_ab, n)
    tu = [mm(t.astype(BF16), jnp.concatenate([o["xa"], w], axis=1).astype(BF16))
          for t, o, w in zip(t_inv, ops, w1)]
    zmat = [jnp.concatenate([x.astype(BF16), jnp.concatenate([zeros, o["vs"]], axis=1)], axis=0)
            for x, o in zip(tu, ops)]
    ry = [mm(m, zm) for m, zm in zip(m_rbk, zmat)]
    ps = [tn(zm, jnp.concatenate([o["xbh"], o["xkh"]], axis=0).astype(BF16))
          for zm, o in zip(zmat, ops)]
    out = [(unstack(o["xr"] + y[:, 0:n]), unstack(y[:, n:2 * n]), o["e_cend"], q[0:n], q[n:2 * n])
           for o, y, q in zip(ops, ry, ps)]
    return [out[ci * RWKV_PAIRS:(ci + 1) * RWKV_PAIRS] for ci in range(len(chunks))]


def _rwkv_chunk_kernel(z_ref, sh_ref, s0_ref, mu_ref, lw_ref, w0_ref, la_ref, a0_ref, lg_ref, kk_ref, ka_ref, rk_ref,
                       gnw_ref, gnb_ref, y_ref, sout_ref, shout_ref, zbuf, s_scr, *, n_tiles, fill_layer):
    c = RWKV_CHUNK
    tile = RWKV_TILE
    zero = jnp.zeros((RWKV_HEAD, RWKV_HEAD), F32)
    for p in range(RWKV_PAIRS):
        s_scr[p] = jnp.concatenate([jnp.concatenate([s0_ref[0, 2 * p], zero], axis=1),
                                    jnp.concatenate([zero, s0_ref[0, 2 * p + 1]], axis=1)], axis=0)
    zbuf[pl.ds(0, 8), :] = jnp.broadcast_to(sh_ref[0], (8, RWKV_PAD))
    nt = lambda x, y: lax.dot_general(x, y, (((1,), (1,)), ((), ())), preferred_element_type=F32)

    def tile_body(ti, carry):
        r0 = pl.multiple_of(ti * tile, tile)
        z = z_ref[0, pl.ds(r0, tile), :]
        zbuf[pl.ds(8, tile), :] = z
        z_prev = zbuf[pl.ds(7, tile), :]
        zbuf[pl.ds(7, 1), :] = zbuf[pl.ds(7 + tile, 1), :]
        r, k, v, a_vec, b_vec, w_log, g, bonus = _rwkv_gates(
            z, z_prev, mu_ref, lw_ref, w0_ref, la_ref, a0_ref, lg_ref, kk_ref, ka_ref, rk_ref)
        logw = -jnp.exp(w_log)
        local = _rwkv_chunk_local([tuple(x[ci * c:(ci + 1) * c] for x in (r, k, v, a_vec, b_vec, logw))
                                   for ci in range(RWKV_TILE_CHUNKS)])
        y_cols = []
        for p in range(RWKV_PAIRS):
            s = s_scr[p]
            ys = []
            for ci in range(RWKV_TILE_CHUNKS):
                rc, yloc, e_cend, trans, sloc = local[ci][p]
                sb = s.astype(BF16)
                ys.append(nt(rc.astype(BF16), sb) + yloc)
                s = s * e_cend + jnp.dot(sb, trans.astype(BF16), preferred_element_type=F32) + sloc
            s_scr[p] = s
            y_cols.append(jnp.concatenate(ys, axis=0))
        y = jnp.concatenate(y_cols, axis=1)
        y_ref[0, pl.ds(r0, tile), :] = _rwkv_post(y, bonus, g, gnw_ref, gnb_ref).astype(y_ref.dtype)
        return carry

    lax.fori_loop(0, n_tiles, tile_body, 0)
    shout_ref[0] = zbuf[pl.ds(7, 1), :]
    sout = _own_layer(sout_ref, fill_layer)
    for p in range(RWKV_PAIRS):
        sout[0, 2 * p] = s_scr[p, 0:RWKV_HEAD, 0:RWKV_HEAD]
        sout[0, 2 * p + 1] = s_scr[p, RWKV_HEAD:2 * RWKV_HEAD, RWKV_HEAD:2 * RWKV_HEAD]


def _rwkv(z2, shift_prev, wkv0, pw, layer, b, l, prev_stack, out_dtype):
    vec = lambda n: _const_spec((1, n), lambda i: (0, 0))
    mat = lambda r, c: _const_spec((r, c), lambda i: (0, 0))
    params = [pw["mu"], pw["lora_w"], pw["w0"], pw["lora_a"], pw["a0"], pw["lora_g"],
              pw["k_k"], pw["k_a"], pw["r_k"], pw["gn_w"], pw["gn_b"]]
    pspecs = [vec(RWKV_PAD), mat(LORA_W, GROUP_W), vec(GROUP_W), mat(LORA_W, GROUP_W), vec(GROUP_W),
              mat(LORA_W, GROUP_W), vec(GROUP_W), vec(GROUP_W), vec(GROUP_W), vec(GROUP_W), vec(GROUP_W)]
    if l % RWKV_TILE == 0:
        nseq = 1
        z_in = z2.reshape(b, l, Z_COLS)
        shift_prev = shift_prev.reshape(b, 1, RWKV_PAD)
        shspec = pl.BlockSpec((1, 1, RWKV_PAD), lambda i: (i, 0, 0))
        body = functools.partial(_rwkv_chunk_kernel, n_tiles=l // RWKV_TILE)
        zspec = pl.BlockSpec((1, l, RWKV_PAD), lambda i: (i, 0, Z_RWKV // RWKV_PAD))
        yspec = pl.BlockSpec((1, l, GROUP_W), lambda i: (i, 0, 0))
        yshape = jax.ShapeDtypeStruct((b, l, GROUP_W), out_dtype)
        scratch = [pltpu.VMEM((RWKV_TILE + 8, RWKV_PAD), F32), pltpu.VMEM((RWKV_PAIRS, 128, 128), F32)]
    else:
        nseq = SHORT_SEQS
        z_in = z2
        shspec = pl.BlockSpec((nseq, RWKV_PAD), lambda i: (i, 0))
        body = functools.partial(_rwkv_scan_kernel, nseq=nseq, l=l)
        zspec = pl.BlockSpec((nseq * l, RWKV_PAD), lambda i: (i, Z_RWKV // RWKV_PAD))
        yspec = pl.BlockSpec((nseq * l, GROUP_W), lambda i: (i, 0))
        yshape = jax.ShapeDtypeStruct((b * l, GROUP_W), out_dtype)
        scratch = [pltpu.VMEM((nseq * l + 8, RWKV_PAD), F32)]
    first = prev_stack is None
    tail = (nseq, RWKV_HEADS, RWKV_HEAD, RWKV_HEAD)
    y, s_new, shift_new = _call_stacked(
        functools.partial(body, fill_layer=layer if first else None), [z_in, shift_prev, wkv0] + params,
        [zspec, shspec, _stack_out_spec(tail, layer, False)] + pspecs,
        1, prev_stack,
        grid=(b // nseq,),
        out_specs=[yspec, _stack_out_spec(tail, layer, first), shspec],
        out_shape=[yshape, jax.ShapeDtypeStruct((DEPTH, b, RWKV_HEADS, RWKV_HEAD, RWKV_HEAD), F32),
                   jax.ShapeDtypeStruct(shift_prev.shape, F32)],
        scratch_shapes=scratch,
        compiler_params=_cparams(("parallel",)),
        name="rwkv7")
    return y.reshape(b * l, GROUP_W), s_new, shift_new.reshape(b, RWKV_PAD)[:, :RWKV_COLS]


def _sgu_kernel(z_ref, lnw_ref, lnb_ref, wm_ref, bias_ref, nrm_ref, y_ref, v_ref, *, blk):
    z = z_ref[...]
    zg = 0.5 * z * (1.0 + lax.erf(z * 0.7071067811865476))
    u = zg[:, :GROUP_W]
    v = zg[:, GROUP_W:]
    mu = jnp.mean(v, axis=-1, keepdims=True)
    vc = v - mu
    var = jnp.mean(vc * vc, axis=-1, keepdims=True)
    vn = vc * lax.rsqrt(var + LN_EPS) * lnw_ref[...] + lnb_ref[...]
    v_ref[...] = vn
    row = lax.broadcasted_iota(jnp.int32, (SGU_CHUNK, SGU_CHUNK), 0)
    col = lax.broadcasted_iota(jnp.int32, (SGU_CHUNK, SGU_CHUNK), 1)
    mask = col <= row
    if blk < SGU_CHUNK:
        mask = mask & ((row // blk) == (col // blk))
    vb = vn.astype(BF16)
    wms = [jnp.where(mask, wm_ref[h], 0.0).astype(BF16) for h in range(SGU_HEADS)]
    rows = []
    for c in range(z.shape[0] // SGU_CHUNK):
        rs = slice(c * SGU_CHUNK, (c + 1) * SGU_CHUNK)
        rows.append(jnp.concatenate(
            [jnp.dot(wms[h], vb[rs, h * 128:(h + 1) * 128], preferred_element_type=F32) + bias_ref[:, h:h + 1]
             for h in range(SGU_HEADS)], axis=1))
    s = rows[0] if len(rows) == 1 else jnp.concatenate(rows, axis=0)
    y_ref[...] = _rms(u * s, nrm_ref[...]).astype(y_ref.dtype)


def _sgu(z2, pw, wm, bias, blk, out_dtype):
    t = z2.shape[0]
    tile = SGU_TILE_CHUNKS * SGU_CHUNK
    vec = _const_spec((1, GROUP_W), lambda i: (0, 0))
    return pl.pallas_call(
        functools.partial(_sgu_kernel, blk=blk),
        grid=(t // tile,),
        in_specs=[pl.BlockSpec((tile, SGU_COLS), lambda i: (i, Z_SGU // SGU_COLS)),
                  vec, vec,
                  _const_spec((SGU_HEADS, SGU_CHUNK, SGU_CHUNK), lambda i: (0, 0, 0)),
                  _const_spec((SGU_CHUNK, SGU_HEADS), lambda i: (0, 0)),
                  vec],
        out_specs=[pl.BlockSpec((tile, GROUP_W), lambda i: (i, 0)),
                   pl.BlockSpec((tile, GROUP_W), lambda i: (i, 0))],
        out_shape=[jax.ShapeDtypeStruct((t, GROUP_W), out_dtype), jax.ShapeDtypeStruct((t, GROUP_W), F32)],
        compiler_params=_cparams(("parallel",)),
        name="sgu",
    )(z2, pw["ln_w"], pw["ln_b"], wm, bias, pw["norm"])


def _hgrn_chunks(seqs, lb, nw, chunk, sub):
    nh = HGRN_HEADS
    hs = lambda x, h: x[:, h * HGRN_HEAD:(h + 1) * HGRN_HEAD]
    tri = (lax.broadcasted_iota(jnp.int32, (chunk, chunk), 1)
           <= lax.broadcasted_iota(jnp.int32, (chunk, chunk), 0)).astype(BF16)
    eye = (lax.broadcasted_iota(jnp.int32, (HGRN_HEAD, HGRN_HEAD), 0)
           == lax.broadcasted_iota(jnp.int32, (HGRN_HEAD, HGRN_HEAD), 1)).astype(F32)
    rowid = lax.broadcasted_iota(jnp.int32, (sub, 1), 0)
    d = functools.partial(jnp.dot, preferred_element_type=F32)
    nt = lambda x, y: lax.dot_general(x, y, (((1,), (1,)), ((), ())), preferred_element_type=F32)
    tn = lambda x, y: lax.dot_general(x, y, (((0,), (0,)), ((), ())), preferred_element_type=F32)

    pre = []
    for zq, zf, zi, zg, states in seqs:
        q = _silu(zq)
        fg = lb + (1.0 - lb) * _sigmoid(zf)
        k = 1.0 - fg
        b = _dot_f32_rhs(tri, jnp.log(fg))
        pre.append(dict(q=q, k=k, v=zi, vb=zi.astype(BF16), b=b, qe=(q * jnp.exp(b)).astype(BF16)))
    o = [[d(hs(u["qe"], h), st[h].astype(BF16)) for h in range(nh)] for u, (_, _, _, _, st) in zip(pre, seqs)]
    outs = [[[] for _ in range(nh)] for _ in seqs]
    for i in range(chunk // sub):
        lo_, hi_ = i * sub, (i + 1) * sub
        oi = [[o[n][h][lo_:hi_] for h in range(nh)] for n in range(len(seqs))]
        if i > 0:
            for n, u in enumerate(pre):
                bref = u["b"][lo_ - 1:lo_]
                qt = (u["q"][lo_:hi_] * jnp.exp(u["b"][lo_:hi_] - bref)).astype(BF16)
                kt = (u["k"][:lo_] * jnp.exp(bref - u["b"][:lo_])).astype(BF16)
                att = [nt(hs(qt, h), hs(kt, h)).astype(BF16) for h in range(nh)]
                for h in range(nh):
                    oi[n][h] = oi[n][h] + d(att[h], hs(u["vb"], h)[:lo_])
        tiles = []
        for u in pre:
            qi, bi = u["q"][lo_:hi_], u["b"][lo_:hi_]
            for j in range(sub):
                sj = lo_ + j
                pj = qi * jnp.exp(jnp.minimum(bi - u["b"][sj:sj + 1], 0.0)) * u["k"][sj:sj + 1]
                tiles.extend(hs(pj, h) for h in range(nh))
        att = _group_sums(jnp.concatenate(tiles, axis=0), HGRN_HEAD, terms=1)
        t = 0
        for n, u in enumerate(pre):
            for j in range(sub):
                sj = lo_ + j
                for h in range(nh):
                    a_j = jnp.where(rowid >= j, att[t * sub:(t + 1) * sub], 0.0)
                    oi[n][h] = oi[n][h] + a_j * hs(u["v"], h)[sj:sj + 1]
                    t += 1
        for n in range(len(seqs)):
            for h in range(nh):
                outs[n][h].append(oi[n][h])
    results = []
    for n, (u, (_, _, _, zg, st)) in enumerate(zip(pre, seqs)):
        o_n = jnp.concatenate([outs[n][h][0] if len(outs[n][h]) == 1 else jnp.concatenate(outs[n][h], axis=0)
                               for h in range(nh)], axis=1)
        blast = u["b"][chunk - 1:chunk]
        kd = (u["k"] * jnp.exp(blast - u["b"])).astype(BF16)
        e_last = jnp.exp(blast)
        new_states = []
        for h in range(nh):
            e_col = _group_sums(eye * hs(e_last, h), HGRN_HEAD)
            new_states.append(e_col * st[h] + tn(hs(kd, h), hs(u["vb"], h)))
        ms = _group_sums(o_n * o_n, HGRN_HEAD) * (1.0 / HGRN_HEAD)
        results.append((o_n * lax.rsqrt(ms + NORM_EPS) * nw * _silu(zg), new_states))
    return results


def _hgrn_kernel(zq_ref, zf_ref, zi_ref, zg_ref, lg_ref, nw_ref, s0_ref, y_ref, sout_ref, s_scr,
                 *, layer, nseq, chunk, sub, n_chunks, fill_layer):
    lg = lg_ref[...]
    e = jnp.exp(lg - jnp.max(lg, axis=0, keepdims=True))
    sm = e / jnp.sum(e, axis=0, keepdims=True)
    lb = jnp.sum(sm[0:layer + 1], axis=0, keepdims=True) - sm[0:1]
    nw = nw_ref[...]
    nh = HGRN_HEADS
    sout = _own_layer(sout_ref, fill_layer)
    if n_chunks == 1:
        rows = lambda ref, b: ref[b * chunk:(b + 1) * chunk, :]
        seqs = [(rows(zq_ref, b), rows(zf_ref, b), rows(zi_ref, b), rows(zg_ref, b),
                 [s0_ref[b, h] for h in range(nh)]) for b in range(nseq)]
        ys = []
        for b, (y, states) in enumerate(_hgrn_chunks(seqs, lb, nw, chunk, sub)):
            ys.append(y)
            for h in range(nh):
                sout[b, h] = states[h]
        y_ref[...] = jnp.concatenate(ys, axis=0).astype(y_ref.dtype)
    else:
        s_scr[...] = s0_ref[...]

        def chunk_body(c, carry):
            r0 = pl.multiple_of(c * chunk, chunk)
            rows = lambda ref, b: ref[b, pl.ds(r0, chunk), :]
            seqs = [(rows(zq_ref, b), rows(zf_ref, b), rows(zi_ref, b), rows(zg_ref, b),
                     [s_scr[b, h] for h in range(nh)]) for b in range(nseq)]
            for b, (y, states) in enumerate(_hgrn_chunks(seqs, lb, nw, chunk, sub)):
                y_ref[b, pl.ds(r0, chunk), :] = y.astype(y_ref.dtype)
                for h in range(nh):
                    s_scr[b, h] = states[h]
            return carry

        lax.fori_loop(0, n_chunks, chunk_body, 0)
        sout[...] = s_scr[...]


def _hgrn(z2, s0, lb_logits, norm_w, layer, b, l, prev_stack, out_dtype):
    chunk = min(HGRN_CHUNK, l)
    sub = min(HGRN_SUB, chunk)
    cb = Z_HGRN // GROUP_W
    if l > chunk:
        nseq = 1
        z_in = z2.reshape(b, l, Z_COLS)
        zspec = lambda part: pl.BlockSpec((1, l, GROUP_W), lambda i: (i, 0, cb + part))
        yspec = pl.BlockSpec((1, l, GROUP_W), lambda i: (i, 0, 0))
        yshape = jax.ShapeDtypeStruct((b, l, GROUP_W), out_dtype)
    else:
        nseq = SHORT_SEQS
        z_in = z2
        zspec = lambda part: pl.BlockSpec((nseq * l, GROUP_W), lambda i: (i, cb + part))
        yspec = pl.BlockSpec((nseq * l, GROUP_W), lambda i: (i, 0))
        yshape = jax.ShapeDtypeStruct((b * l, GROUP_W), out_dtype)
    first = prev_stack is None
    tail = (nseq, HGRN_HEADS, HGRN_HEAD, HGRN_HEAD)
    y, s_new = _call_stacked(
        functools.partial(_hgrn_kernel, layer=layer, nseq=nseq, chunk=chunk, sub=sub, n_chunks=l // chunk,
                          fill_layer=layer if first else None),
        [z_in, z_in, z_in, z_in, lb_logits, norm_w, s0],
        [zspec(0), zspec(1), zspec(2), zspec(3),
         _const_spec((DEPTH, GROUP_W), lambda i: (0, 0)),
         _const_spec((1, GROUP_W), lambda i: (0, 0)),
         _stack_out_spec(tail, layer, False)],
        1, prev_stack,
        grid=(b // nseq,),
        out_specs=[yspec, _stack_out_spec(tail, layer, first)],
        out_shape=[yshape, jax.ShapeDtypeStruct((DEPTH, b, HGRN_HEADS, HGRN_HEAD, HGRN_HEAD), F32)],
        scratch_shapes=[pltpu.VMEM((nseq, HGRN_HEADS, HGRN_HEAD, HGRN_HEAD), F32)],
        compiler_params=_cparams(("parallel",)),
        name="hgrn2")
    return y.reshape(b * l, GROUP_W), s_new


def _pool_kernel(z_ref, hist_ref, pw_ref, ps_ref, y_ref, hout_ref, xbuf, s2buf, s4buf, s8buf,
                 *, nseq, tc, n_tiles, start_pos, flat):
    halo = POOL_HALO
    n = tc + halo
    c1 = POOL_CH
    for b in range(nseq):
        xbuf[b, pl.ds(0, 16), :] = jnp.zeros((16, GROUP_W), F32)
        xbuf[b, pl.ds(16, 16), :] = hist_ref[b]

    def tile(r0, b, z):
        xbuf[b, pl.ds(halo, tc), :] = z
        s2buf[b, pl.ds(8, n - 8), :] = xbuf[b, pl.ds(8, n - 8), :] + xbuf[b, pl.ds(7, n - 8), :]
        s4buf[b, pl.ds(16, n - 16), :] = s2buf[b, pl.ds(16, n - 16), c1:] + s2buf[b, pl.ds(14, n - 16), c1:]
        s8buf[b, pl.ds(24, n - 24), :] = s4buf[b, pl.ds(24, n - 24), c1:] + s4buf[b, pl.ds(20, n - 24), c1:]
        s16 = s8buf[b, pl.ds(halo, tc), c1:] + s8buf[b, pl.ds(halo - 8, tc), c1:]
        sums = (s2buf[b, pl.ds(halo, tc), 0:c1], s4buf[b, pl.ds(halo, tc), 0:c1], s8buf[b, pl.ds(halo, tc), 0:c1],
                s16)
        pos = start_pos + r0 + lax.broadcasted_iota(jnp.int32, (tc, 1), 0)
        ys = []
        for gi, win in enumerate(POOL_WINDOWS):
            sl = slice(gi * POOL_CH, (gi + 1) * POOL_CH)
            cnt = jnp.minimum(pos + 1, win).astype(F32)
            dlt = sums[gi] / cnt - z[:, sl]
            ys.append(jnp.dot(dlt.astype(BF16), pw_ref[gi], preferred_element_type=F32) * ps_ref[:, sl])
        if n_tiles > 1:
            xbuf[b, pl.ds(16, 16), :] = xbuf[b, pl.ds(n - 16, 16), :]
        return jnp.concatenate(ys, axis=1)

    if flat:
        y_ref[...] = jnp.concatenate([tile(0, b, z_ref[b * tc:(b + 1) * tc, :]) for b in range(nseq)],
                                     axis=0).astype(y_ref.dtype)
    else:
        def tile_body(ti, carry):
            r0 = pl.multiple_of(ti * tc, tc)
            for b in range(nseq):
                y_ref[b, pl.ds(r0, tc), :] = tile(r0, b, z_ref[b, pl.ds(r0, tc), :]).astype(y_ref.dtype)
            return carry

        lax.fori_loop(0, n_tiles, tile_body, 0)
    for b in range(nseq):
        hout_ref[b] = xbuf[b, pl.ds(n - POOL_HIST, POOL_HIST), :]


def _pool(z2, hist16, pool_w, pool_scale, start_pos, b, l, tc, out_dtype):
    cb = Z_POOL // GROUP_W
    if l >= POOL_HALO:
        nseq = 1
        z_in = z2.reshape(b, l, Z_COLS)
        zspec = pl.BlockSpec((1, l, GROUP_W), lambda i: (i, 0, cb))
        yspec = pl.BlockSpec((1, l, GROUP_W), lambda i: (i, 0, 0))
        yshape = jax.ShapeDtypeStruct((b, l, GROUP_W), out_dtype)
    else:
        nseq = SHORT_SEQS
        z_in = z2
        zspec = pl.BlockSpec((nseq * l, GROUP_W), lambda i: (i, cb))
        yspec = pl.BlockSpec((nseq * l, GROUP_W), lambda i: (i, 0))
        yshape = jax.ShapeDtypeStruct((b * l, GROUP_W), out_dtype)
    n = tc + POOL_HALO
    y, hist_new = pl.pallas_call(
        functools.partial(_pool_kernel, nseq=nseq, tc=tc, n_tiles=l // tc, start_pos=start_pos, flat=nseq > 1),
        grid=(b // nseq,),
        in_specs=[zspec,
                  pl.BlockSpec((nseq, 16, GROUP_W), lambda i: (i, 0, 0)),
                  _const_spec((len(POOL_WINDOWS), POOL_CH, POOL_CH), lambda i: (0, 0, 0)),
                  _const_spec((1, GROUP_W), lambda i: (0, 0))],
        out_specs=[yspec, pl.BlockSpec((nseq, POOL_HIST, GROUP_W), lambda i: (i, 0, 0))],
        out_shape=[yshape, jax.ShapeDtypeStruct((b, POOL_HIST, GROUP_W), F32)],
        scratch_shapes=[pltpu.VMEM((nseq, n, GROUP_W), F32), pltpu.VMEM((nseq, n, GROUP_W), F32),
                        pltpu.VMEM((nseq, n, 3 * POOL_CH), F32), pltpu.VMEM((nseq, n, 2 * POOL_CH), F32)],
        compiler_params=_cparams(("parallel",)),
        name="pool",
    )(z_in, hist16, pool_w, pool_scale)
    return y.reshape(b * l, GROUP_W), hist_new


def _pad_cols(a, n):
    return jnp.pad(a, ((0, 0),) * (a.ndim - 1) + ((0, n - a.shape[-1]),))


def _w_in_kernel(w_ref, o_ref):
    c1, c2, c3 = RWKV_COLS, RWKV_COLS + SGU_COLS, RWKV_COLS + SGU_COLS + HGRN_COLS
    o_ref[:, Z_HGRN:Z_HGRN + HGRN_COLS] = w_ref[:, c2:c3].astype(BF16)
    o_ref[:, Z_SGU:Z_SGU + SGU_COLS] = w_ref[:, c1:c2].astype(BF16)
    o_ref[:, Z_POOL:Z_POOL + GROUP_W] = w_ref[:, c3:].astype(BF16)
    o_ref[:, Z_RWKV:Z_RWKV + RWKV_COLS] = w_ref[:, :c1].astype(BF16)
    o_ref[:, Z_RWKV + RWKV_COLS:] = jnp.zeros((o_ref.shape[0], RWKV_PAD - RWKV_COLS), BF16)


def _w_in_layout(w_in, rows=256):
    in_cols = w_in.shape[2]
    return pl.pallas_call(
        _w_in_kernel,
        grid=(DEPTH, D_MODEL // rows),
        in_specs=[pl.BlockSpec((None, rows, in_cols), lambda l, i: (l, i, 0))],
        out_specs=pl.BlockSpec((None, rows, Z_COLS), lambda l, i: (l, i, 0)),
        out_shape=jax.ShapeDtypeStruct((DEPTH, D_MODEL, Z_COLS), BF16),
        compiler_params=_cparams(("parallel", "parallel")),
        name="w_in_layout",
    )(w_in)


def _prep_weights(W):
    return dict(w_in=_w_in_layout(W["w_in"]), w_out=W["w_out"].astype(BF16), w_gu=W["ffn_w_gu"].astype(BF16),
                w_down=W["ffn_w_down"].astype(BF16), ple_gate=W["ple_gate"].astype(BF16),
                ple_proj=W["ple_proj"].astype(BF16))


def _prep_layer(i, W):
    row = lambda a: a.reshape(1, -1)
    lora = lambda a, off: jnp.pad(a, ((off, LORA_W - off - a.shape[0]), (0, 0))).astype(BF16)
    rw = dict(
        mu=_pad_cols(row(W["rwkv_mu"][i]), RWKV_PAD),
        lora_w=lora(W["rwkv_w_lora"][i], 0), w0=row(W["rwkv_w0"][i]),
        lora_a=lora(W["rwkv_a_lora"][i], W_LORA), a0=row(W["rwkv_a0"][i]),
        lora_g=lora(W["rwkv_g_lora"][i], W_LORA + A_LORA),
        k_k=row(W["rwkv_k_k"][i]), k_a=row(W["rwkv_k_a"][i]), r_k=row(W["rwkv_r_k"][i]),
        gn_w=row(W["rwkv_gn_w"][i]), gn_b=row(W["rwkv_gn_b"][i]))
    sg = dict(ln_w=row(W["sgu_ln_w"][i]), ln_b=row(W["sgu_ln_b"][i]), norm=row(W["sgu_norm"][i]))
    return dict(
        rw=rw, sg=sg, sgu_w=W["sgu_w"][i], sgu_b=W["sgu_b"][i],
        hgrn_norm=row(W["hgrn_norm"][i]), pool_w=W["pool_w"][i].astype(BF16), pool_scale=row(W["pool_scale"][i]),
        ln_mix_pre=row(W["ln_mix_pre"][i]), ln_mix_post=row(W["ln_mix_post"][i]),
        ln_ffn_pre=row(W["ln_ffn_pre"][i]), ln_ffn_post=row(W["ln_ffn_post"][i]))


def _trunk(x, p, wkv0, shift0, hgrn0, pool0, start_pos, big, layers, lb_logits, cfg):
    b, l, _ = x.shape
    t = b * l
    xf = x.reshape(t, D_MODEL)
    p2 = p.reshape(DEPTH, t, PLE_DIM)
    wkv_new = hgrn_new = None
    shift_l, pool_l, sgu_l = [], [], []
    for i, lw in enumerate(layers):
        z2 = _inproj(xf, lw["ln_mix_pre"], big["w_in"], i, min(cfg["tm_in"], t), cfg["tn_in"])
        y_r, wkv_new, shift_new = _rwkv(z2, _pad_cols(shift0[i], RWKV_PAD), wkv0, lw["rw"], i, b, l, wkv_new,
                                        cfg["ydt"])
        if cfg["sgu_blk"] == SGU_CHUNK:
            wm, bias = lw["sgu_w"], lw["sgu_b"].T
        else:
            rep = SGU_CHUNK // l
            wm = jnp.tile(lw["sgu_w"][:, :l, :l], (1, rep, rep))
            bias = jnp.tile(lw["sgu_b"][:, :l].T, (rep, 1))
        y_s, v_rows = _sgu(z2, lw["sg"], wm, bias, cfg["sgu_blk"], cfg["ydt"])
        y_h, hgrn_new = _hgrn(z2, hgrn0, lb_logits, lw["hgrn_norm"], i, b, l, hgrn_new, cfg["ydt"])
        hist16 = jnp.pad(pool0[i], ((0, 0), (1, 0), (0, 0)))
        y_p, pool_new = _pool(z2, hist16, lw["pool_w"], lw["pool_scale"], start_pos, b, l, cfg["tc_pool"],
                              cfg["ydt"])
        xf = _outproj([y_r, y_s, y_h, y_p], xf, big["w_out"], lw["ln_mix_post"], i, cfg["tm"])
        xf = _ffn(xf, lw["ln_ffn_pre"], big["w_gu"], big["w_down"], lw["ln_ffn_post"], p2, big["ple_gate"],
                  big["ple_proj"], i, cfg["tm"])
        shift_l.append(shift_new)
        pool_l.append(pool_new)
        sgu_l.append(v_rows.reshape(b, l, GROUP_W))
    return (xf.reshape(b, l, D_MODEL), wkv_new, jnp.stack(shift_l), hgrn_new, jnp.stack(pool_l), jnp.stack(sgu_l))


def _cfg(l):
    if l >= 256:
        return dict(tm=512, tm_in=1024, tn_in=1792, sgu_blk=SGU_CHUNK, tc_pool=256, ydt=BF16)
    return dict(tm=512, tm_in=512, tn_in=1792, sgu_blk=l, tc_pool=l, ydt=BF16)


def kernel(x_prompt, x_sample, state_rwkv_wkv, state_rwkv_shift, state_hgrn, state_pool, p_prompt, p_sample,
           ln_mix_pre, ln_mix_post, ln_ffn_pre, ln_ffn_post, w_in, rwkv_mu, rwkv_w_lora, rwkv_w0, rwkv_a_lora,
           rwkv_a0, rwkv_g_lora, rwkv_k_k, rwkv_k_a, rwkv_r_k, rwkv_gn_w, rwkv_gn_b, sgu_ln_w, sgu_ln_b, sgu_w,
           sgu_b, sgu_norm, hgrn_lb_logits, hgrn_norm, pool_w, pool_scale, w_out, ffn_w_gu, ffn_w_down, ple_gate,
           ple_proj):
    W = dict(ln_mix_pre=ln_mix_pre, ln_mix_post=ln_mix_post, ln_ffn_pre=ln_ffn_pre, ln_ffn_post=ln_ffn_post,
             w_in=w_in, rwkv_mu=rwkv_mu, rwkv_w_lora=rwkv_w_lora, rwkv_w0=rwkv_w0, rwkv_a_lora=rwkv_a_lora,
             rwkv_a0=rwkv_a0, rwkv_g_lora=rwkv_g_lora, rwkv_k_k=rwkv_k_k, rwkv_k_a=rwkv_k_a,
             rwkv_r_k=rwkv_r_k.reshape(DEPTH, GROUP_W), rwkv_gn_w=rwkv_gn_w, rwkv_gn_b=rwkv_gn_b, sgu_ln_w=sgu_ln_w,
             sgu_ln_b=sgu_ln_b, sgu_w=sgu_w, sgu_b=sgu_b, sgu_norm=sgu_norm, hgrn_norm=hgrn_norm, pool_w=pool_w,
             pool_scale=pool_scale, w_out=w_out, ffn_w_gu=ffn_w_gu, ffn_w_down=ffn_w_down, ple_gate=ple_gate,
             ple_proj=ple_proj)
    big = _prep_weights(W)
    layers = [_prep_layer(i, W) for i in range(DEPTH)]
    lb_logits = hgrn_lb_logits.astype(F32)
    bp, lp, _ = x_prompt.shape
    zeros = lambda *s: jnp.zeros((DEPTH, bp) + s, F32)
    y_prompt, wkv_p, shift_p, hgrn_p, pool_p, _ = _trunk(
        x_prompt, p_prompt, zeros(RWKV_HEADS, RWKV_HEAD, RWKV_HEAD), zeros(RWKV_COLS),
        zeros(HGRN_HEADS, HGRN_HEAD, HGRN_HEAD), zeros(POOL_HIST, GROUP_W), 0, big, layers, lb_logits, _cfg(lp))
    y_sample, wkv_s, shift_s, hgrn_s, pool_s, sgu_v_s = _trunk(
        x_sample, p_sample, state_rwkv_wkv, state_rwkv_shift, state_hgrn, state_pool, PAST_LEN, big, layers,
        lb_logits, _cfg(x_sample.shape[1]))
    return (y_prompt, y_sample, wkv_p, shift_p, hgrn_p, pool_p, wkv_s, shift_s, hgrn_s, pool_s, sgu_v_s)
```

```python
import functools

import jax
import jax.numpy as jnp
from jax import lax
from jax.experimental import pallas as pl
from jax.experimental.pallas import tpu as pltpu

F32 = jnp.float32
BF16 = jnp.bfloat16

D_MODEL = 2048
DEPTH = 2
PAST_LEN = 16384
GROUP_W = 512
RWKV_HEAD = 64
RWKV_HEADS = 8
RWKV_PAIRS = RWKV_HEADS // 2
RWKV_CHUNK = 64
RWKV_TILE_CHUNKS = 4
RWKV_TILE = RWKV_TILE_CHUNKS * RWKV_CHUNK
W_LORA, A_LORA, G_LORA = 32, 32, 96
RWKV_GN_EPS = 64e-5
SGU_CHUNK = 128
SGU_HEADS = 4
SGU_TILE_CHUNKS = 4
HGRN_HEADS = 4
HGRN_HEAD = 128
HGRN_CHUNK = 64
HGRN_SUB = 16
POOL_WINDOWS = (2, 4, 8, 16)
POOL_CH = 128
POOL_HIST = 15
POOL_HALO = 32
D_FF = 5632
PLE_DIM = 256
PLE_COLS = 512
FFN_TF = 512
NORM_EPS = 1e-6
LN_EPS = 1e-5
RWKV_COLS = 3 * GROUP_W + W_LORA + A_LORA + G_LORA
RWKV_PAD = 1792
LORA_OFF = 3 * GROUP_W
LORA_W = RWKV_PAD - LORA_OFF
SGU_COLS = 2 * GROUP_W
HGRN_COLS = 4 * GROUP_W
Z_HGRN, Z_SGU, Z_POOL, Z_RWKV = 0, 2048, 3072, 3584
Z_COLS = Z_RWKV + RWKV_PAD
SHORT_SEQS = 8
VMEM_LIMIT = 56 * 1024 * 1024


def _cparams(sem):
    return pltpu.CompilerParams(dimension_semantics=sem, vmem_limit_bytes=VMEM_LIMIT)


def _const_spec(shape, index_map):
    return pl.BlockSpec(shape, index_map, pipeline_mode=pl.Buffered(1))


def _sigmoid(x):
    return jax.nn.sigmoid(x)


def _silu(x):
    return x * jax.nn.sigmoid(x)


def _split3(x):
    hi = x.astype(BF16)
    r = x - hi.astype(F32)
    mid = r.astype(BF16)
    lo = (r - mid.astype(F32)).astype(BF16)
    return hi, mid, lo


def _group_sums(x, width, terms=3):
    rows, cols = x.shape
    nblk = cols // 128
    xs = x if nblk == 1 else jnp.concatenate([x[:, i * 128:(i + 1) * 128] for i in range(nblk)], axis=0)
    ones = ((lax.broadcasted_iota(jnp.int32, (128, 128), 0) // width)
            == (lax.broadcasted_iota(jnp.int32, (128, 128), 1) // width)).astype(BF16)
    s = None
    for part in _split3(xs)[:terms]:
        y = jnp.dot(part, ones, preferred_element_type=F32)
        s = y if s is None else s + y
    return s if nblk == 1 else jnp.concatenate([s[i * rows:(i + 1) * rows] for i in range(nblk)], axis=1)


def _dot_f32_rhs(m, x):
    hi, mid, lo = _split3(x)
    d = functools.partial(jnp.dot, preferred_element_type=F32)
    return d(m, hi) + d(m, mid) + d(m, lo)


def _rms(x, g):
    return x * lax.rsqrt(jnp.mean(x * x, axis=-1, keepdims=True) + NORM_EPS) * g


def _call_stacked(kernel, inputs, in_specs, stack_out, prev_stack, **kw):
    if prev_stack is None:
        return pl.pallas_call(kernel, in_specs=in_specs, **kw)(*inputs)
    n_in = len(inputs)

    def body(*refs):
        kernel(*refs[:n_in], *refs[n_in + 1:])

    return pl.pallas_call(body, in_specs=list(in_specs) + [pl.BlockSpec(memory_space=pl.ANY)],
                          input_output_aliases={n_in: stack_out}, **kw)(*inputs, prev_stack)


def _stack_out_spec(tail, layer, first):
    zeros = (0,) * (len(tail) - 1)
    if first:
        return pl.BlockSpec((DEPTH,) + tail, lambda i: (0, i) + zeros)
    return pl.BlockSpec((None,) + tail, lambda i: (layer, i) + zeros)


def _own_layer(sout_ref, fill_layer):
    if fill_layer is None:
        return sout_ref
    for j in range(DEPTH):
        if j != fill_layer:
            sout_ref[j] = jnp.zeros(sout_ref.shape[1:], sout_ref.dtype)
    return sout_ref.at[fill_layer]


def _inproj_kernel(x_ref, g_ref, w_ref, z_ref, h_ref):
    @pl.when(pl.program_id(1) == 0)
    def _():
        h_ref[...] = _rms(x_ref[...], g_ref[...]).astype(BF16)

    z_ref[...] = jnp.dot(h_ref[...], w_ref[...], preferred_element_type=F32)


def _inproj(x, g, w, layer, tm, tn):
    t = x.shape[0]
    return pl.pallas_call(
        _inproj_kernel,
        grid=(t // tm, Z_COLS // tn),
        in_specs=[pl.BlockSpec((tm, D_MODEL), lambda i, j: (i, 0)),
                  _const_spec((1, D_MODEL), lambda i, j: (0, 0)),
                  pl.BlockSpec((None, D_MODEL, tn), lambda i, j: (layer, 0, j))],
        out_specs=pl.BlockSpec((tm, tn), lambda i, j: (i, j)),
        out_shape=jax.ShapeDtypeStruct((t, Z_COLS), F32),
        scratch_shapes=[pltpu.VMEM((tm, D_MODEL), BF16)],
        compiler_params=_cparams(("parallel", "arbitrary")),
        name="inproj",
    )(x, g, w)


def _outproj_kernel(yr_ref, ys_ref, yh_ref, yp_ref, x_ref, w_ref, g_ref, o_ref):
    d = functools.partial(jnp.dot, preferred_element_type=F32)
    g4 = GROUP_W
    mix = (d(yr_ref[...], w_ref[0:g4, :]) + d(ys_ref[...], w_ref[g4:2 * g4, :])
           + d(yh_ref[...], w_ref[2 * g4:3 * g4, :]) + d(yp_ref[...], w_ref[3 * g4:4 * g4, :]))
    o_ref[...] = x_ref[...] + _rms(mix, g_ref[...])


def _outproj(ys, x, w, g, layer, tm):
    t = x.shape[0]
    yspec = pl.BlockSpec((tm, GROUP_W), lambda i: (i, 0))
    return pl.pallas_call(
        _outproj_kernel,
        grid=(t // tm,),
        in_specs=[yspec, yspec, yspec, yspec,
                  pl.BlockSpec((tm, D_MODEL), lambda i: (i, 0)),
                  _const_spec((None, D_MODEL, D_MODEL), lambda i: (layer, 0, 0)),
                  _const_spec((1, D_MODEL), lambda i: (0, 0))],
        out_specs=pl.BlockSpec((tm, D_MODEL), lambda i: (i, 0)),
        out_shape=jax.ShapeDtypeStruct((t, D_MODEL), F32),
        compiler_params=_cparams(("parallel",)),
        name="outproj",
    )(*ys, x, w, g)


def _ffn_kernel(x_ref, gpre_ref, wg_ref, wu_ref, wd_ref, gpost_ref, p_ref, wgate_ref, wproj_ref, o_ref, h_ref, acc_ref):
    j = pl.program_id(1)

    @pl.when(j == 0)
    def _():
        h_ref[...] = _rms(x_ref[...], gpre_ref[...]).astype(BF16)
        acc_ref[...] = jnp.zeros_like(acc_ref)

    h = h_ref[...]
    gate = jnp.dot(h, wg_ref[...], preferred_element_type=F32)
    up = jnp.dot(h, wu_ref[...], preferred_element_type=F32)
    act = (_silu(gate) * up).astype(BF16)
    acc_ref[...] += jnp.dot(act, wd_ref[...], preferred_element_type=F32)

    @pl.when(j == pl.num_programs(1) - 1)
    def _():
        acc_ref[...] = x_ref[...] + _rms(acc_ref[...], gpost_ref[...])
        h_ref[...] = acc_ref[...].astype(BF16)
        pb = p_ref[...].astype(BF16)
        for c in range(D_MODEL // PLE_COLS):
            cs = slice(c * PLE_COLS, (c + 1) * PLE_COLS)
            pgate = _sigmoid(jnp.dot(h_ref[...], wgate_ref[:, cs], preferred_element_type=F32))
            ple = jnp.dot(pb, wproj_ref[:, cs], preferred_element_type=F32)
            o_ref[:, cs] = acc_ref[:, cs] + pgate * ple


def _ffn(x, gpre, w_gu, w_down, gpost, p, w_gate, w_proj, layer, tm):
    t = x.shape[0]
    tf = FFN_TF
    nf = D_FF // tf
    return pl.pallas_call(
        _ffn_kernel,
        grid=(t // tm, nf),
        in_specs=[pl.BlockSpec((tm, D_MODEL), lambda i, j: (i, 0)),
                  _const_spec((1, D_MODEL), lambda i, j: (0, 0)),
                  pl.BlockSpec((None, D_MODEL, tf), lambda i, j: (layer, 0, j)),
                  pl.BlockSpec((None, D_MODEL, tf), lambda i, j: (layer, 0, j + nf)),
                  pl.BlockSpec((None, tf, D_MODEL), lambda i, j: (layer, j, 0)),
                  _const_spec((1, D_MODEL), lambda i, j: (0, 0)),
                  pl.BlockSpec((None, tm, PLE_DIM), lambda i, j: (layer, i, 0)),
                  _const_spec((None, D_MODEL, D_MODEL), lambda i, j: (layer, 0, 0)),
                  _const_spec((None, PLE_DIM, D_MODEL), lambda i, j: (layer, 0, 0))],
        out_specs=pl.BlockSpec((tm, D_MODEL), lambda i, j: (i, 0)),
        out_shape=jax.ShapeDtypeStruct((t, D_MODEL), F32),
        scratch_shapes=[pltpu.VMEM((tm, D_MODEL), BF16), pltpu.VMEM((tm, D_MODEL), F32)],
        compiler_params=_cparams(("parallel", "arbitrary")),
        name="ffn",
    )(x, gpre, w_gu, w_gu, w_down, gpost, p, w_gate, w_proj)


def _rwkv_gates(z, z_prev, mu_ref, lw_ref, w0_ref, la_ref, a0_ref, lg_ref, kk_ref, ka_ref, rk_ref):
    zm = z + mu_ref[...] * (z_prev - z)
    g4 = GROUP_W
    r = zm[:, 0:g4]
    k = zm[:, g4:2 * g4]
    v = zm[:, 2 * g4:3 * g4]
    zl = zm[:, LORA_OFF:RWKV_PAD]
    d = functools.partial(jnp.dot, preferred_element_type=F32)
    wl = w0_ref[...] + d(jnp.tanh(zl).astype(BF16), lw_ref[...])
    w_log = -(jnp.maximum(-wl, 0.0) + jnp.log1p(jnp.exp(-jnp.abs(wl)))) - 0.5
    a = _sigmoid(a0_ref[...] + d(zl.astype(BF16), la_ref[...]))
    g = d(_sigmoid(zl).astype(BF16), lg_ref[...])
    kk = k * kk_ref[...]
    kk = kk / jnp.maximum(jnp.sqrt(_group_sums(kk * kk, RWKV_HEAD)), 1e-12)
    k = k * (1.0 + (a - 1.0) * ka_ref[...])
    bonus = _group_sums(r * k * rk_ref[...], RWKV_HEAD) * v
    return r, k, v, -kk, kk * a, w_log, g, bonus


def _rwkv_post(y, bonus, g, gnw_ref, gnb_ref):
    inv_n = 1.0 / RWKV_HEAD
    mean = _group_sums(y, RWKV_HEAD) * inv_n
    yc = y - mean
    var = _group_sums(yc * yc, RWKV_HEAD) * inv_n
    yn = yc * lax.rsqrt(var + RWKV_GN_EPS) * gnw_ref[...] + gnb_ref[...]
    return (yn + bonus) * g


def _rwkv_lanes_gates_kernel(z_ref, sh_ref, mu_ref, lw_ref, w0_ref, la_ref, a0_ref, lg_ref, kk_ref, ka_ref, rk_ref,
                             r_ref, w_ref, k_ref, v_ref, a_ref, b_ref, bonus_ref, g_ref, shout_ref, *, l, nb):
    z = jnp.concatenate([z_ref[t] for t in range(l)], axis=0)
    z_prev = jnp.concatenate([sh_ref[...]] + [z_ref[t] for t in range(l - 1)], axis=0)
    shout_ref[...] = z_ref[l - 1]
    r, k, v, a_vec, b_vec, w_log, g, bonus = _rwkv_gates(
        z, z_prev, mu_ref, lw_ref, w0_ref, la_ref, a0_ref, lg_ref, kk_ref, ka_ref, rk_ref)
    w = jnp.exp(-jnp.exp(w_log))
    for x, ref in ((r, r_ref), (w, w_ref), (k, k_ref), (v, v_ref), (a_vec, a_ref), (b_vec, b_ref),
                   (bonus, bonus_ref), (g, g_ref)):
        for t in range(l):
            ref[t] = x[t * nb:(t + 1) * nb].T


def _rwkv_lanes_step_kernel(r_ref, w_ref, k_ref, v_ref, a_ref, b_ref, bonus_ref, g_ref, gnw_ref, gnb_ref, s0_ref,
                            y_ref, sout_ref, *, l, fill_layer):
    sout = _own_layer(sout_ref, fill_layer)
    vb = 8
    for t in range(l):
        a, w, b, k, r = a_ref[t], w_ref[t], b_ref[t], k_ref[t], r_ref[t]
        src = s0_ref if t == 0 else sout

        def rows(i, carry):
            v0 = pl.multiple_of(i * vb, vb)
            s = src[0, pl.ds(v0, vb)]
            sa = jnp.sum(s * a[None], axis=1)
            vv = v_ref[t, pl.ds(v0, vb), :]
            s = s * w[None] + sa[:, None, :] * b[None] + vv[:, None, :] * k[None]
            sout[0, pl.ds(v0, vb)] = s
            y_ref[t, pl.ds(v0, vb), :] = jnp.sum(s * r[None], axis=1)
            return carry

        lax.fori_loop(0, RWKV_HEAD // vb, rows, 0)
    inv_n = 1.0 / RWKV_HEAD
    for t in range(l):
        y = y_ref[t]
        mean = jnp.sum(y, axis=0, keepdims=True) * inv_n
        yc = y - mean
        var = jnp.sum(yc * yc, axis=0, keepdims=True) * inv_n
        yn = yc * lax.rsqrt(var + RWKV_GN_EPS) * gnw_ref[...] + gnb_ref[...]
        y_ref[t] = (yn + bonus_ref[t]) * g_ref[t]


def _rwkv_lanes(z2, shift_prev, wkv0_t, pw, layer, b, l, prev_stack, out_dtype):
    vec = lambda n: _const_spec((1, n), lambda i: (0, 0))
    mat = lambda r, c: _const_spec((r, c), lambda i: (0, 0))
    zt = z2[:, Z_RWKV:Z_RWKV + RWKV_PAD].reshape(b, l, RWKV_PAD).transpose(1, 0, 2)
    op_shape = jax.ShapeDtypeStruct((l, GROUP_W, b), F32)
    full = lambda shape: _const_spec(shape, lambda i: (0,) * len(shape))
    outs = pl.pallas_call(
        functools.partial(_rwkv_lanes_gates_kernel, l=l, nb=b),
        grid=(1,),
        in_specs=[full((l, b, RWKV_PAD)), full((b, RWKV_PAD)),
                  vec(RWKV_PAD), mat(LORA_W, GROUP_W), vec(GROUP_W), mat(LORA_W, GROUP_W), vec(GROUP_W),
                  mat(LORA_W, GROUP_W), vec(GROUP_W), vec(GROUP_W), vec(GROUP_W)],
        out_specs=[full((l, GROUP_W, b))] * 8 + [full((b, RWKV_PAD))],
        out_shape=[op_shape] * 8 + [jax.ShapeDtypeStruct((b, RWKV_PAD), F32)],
        compiler_params=_cparams(("arbitrary",)),
        name="rwkv7_gates",
    )(zt, shift_prev, pw["mu"], pw["lora_w"], pw["w0"], pw["lora_a"], pw["a0"], pw["lora_g"],
      pw["k_k"], pw["k_a"], pw["r_k"])
    ops, shift_new = outs[:8], outs[8]
    first = prev_stack is None
    tail = (1, RWKV_HEAD, RWKV_HEAD, b)
    op_spec = pl.BlockSpec((l, RWKV_HEAD, b), lambda i: (0, i, 0))
    col_spec = pl.BlockSpec((RWKV_HEAD, 1), lambda i: (i, 0))
    y, s_new = _call_stacked(
        functools.partial(_rwkv_lanes_step_kernel, l=l, fill_layer=layer if first else None),
        list(ops) + [pw["gn_w"].reshape(GROUP_W, 1), pw["gn_b"].reshape(GROUP_W, 1), wkv0_t],
        [op_spec] * 8 + [col_spec, col_spec, _stack_out_spec(tail, layer, False)],
        1, prev_stack,
        grid=(RWKV_HEADS,),
        out_specs=[op_spec, _stack_out_spec(tail, layer, first)],
        out_shape=[op_shape, jax.ShapeDtypeStruct((DEPTH, RWKV_HEADS, RWKV_HEAD, RWKV_HEAD, b), F32)],
        compiler_params=_cparams(("parallel",)),
        name="rwkv7_steps")
    y = y.transpose(2, 0, 1).reshape(b * l, GROUP_W).astype(out_dtype)
    return y, s_new, shift_new[:, :RWKV_COLS]


def _tri_inverse(l_mats, n):
    row = lax.broadcasted_iota(jnp.int32, (n, n), 0)
    col = lax.broadcasted_iota(jnp.int32, (n, n), 1)
    eye = (row == col).astype(F32)
    mm = lambda a, b: jnp.dot(a.astype(BF16), b.astype(BF16), preferred_element_type=F32)
    same = lambda m: (row // m) == (col // m)
    d1 = [jnp.where(same(8), x, 0.0) for x in l_mats]
    d2 = [mm(x, x) for x in d1]
    d4 = [mm(x, x) for x in d2]
    d3 = [mm(x, y) for x, y in zip(d1, d2)]
    t = [eye + x1 + x2 + x3 for x1, x2, x3 in zip(d1, d2, d3)]
    t = [x + mm(x, y) for x, y in zip(t, d4)]
    m = 8
    while m < RWKV_CHUNK:
        e = [jnp.where(same(2 * m) & jnp.logical_not(same(m)), x, 0.0) for x in l_mats]
        te = [mm(x, y) for x, y in zip(t, e)]
        t = [x + mm(y, x) for x, y in zip(t, te)]
        m *= 2
    return t


def _rwkv_chunk_local(chunks):
    c = RWKV_CHUNK
    n = 2 * c
    lo64 = lax.broadcasted_iota(jnp.int32, (c, 128), 1) < RWKV_HEAD
    tri = (lax.broadcasted_iota(jnp.int32, (c, c), 1) <= lax.broadcasted_iota(jnp.int32, (c, c), 0)).astype(BF16)
    stack = lambda x: jnp.concatenate([jnp.where(lo64, x, 0.0), jnp.where(lo64, 0.0, x)], axis=0)
    ops = []
    for r, k, v, a_vec, b_vec, logw in chunks:
        cum = _dot_f32_rhs(tri, logw)
        c_end = cum[c - 1:c]
        e_neg = jnp.exp(-cum)
        e_end = jnp.exp(c_end - cum)
        at = a_vec * jnp.exp(cum - logw)
        rt = r * jnp.exp(cum)
        bt = b_vec * e_neg
        kt = k * e_neg
        bh = b_vec * e_end
        kh = k * e_end
        e_cend = jnp.exp(c_end)
        for p in range(RWKV_PAIRS):
            sl = slice(p * 128, (p + 1) * 128)
            ops.append(dict(xa=stack(at[:, sl]), xr=stack(rt[:, sl]), yb=stack(bt[:, sl]), yk=stack(kt[:, sl]),
                            vs=stack(v[:, sl]).astype(BF16), xbh=stack(bh[:, sl]), xkh=stack(kh[:, sl]),
                            e_cend=e_cend[:, sl]))

    row = lax.broadcasted_iota(jnp.int32, (n, n), 0)
    col = lax.broadcasted_iota(jnp.int32, (n, n), 1)
    same_head = (row // c) == (col // c)
    strict = same_head & (col < row)
    incl = same_head & (col <= row)
    zeros = jnp.zeros((n, n), BF16)
    mm = lambda x, y: jnp.dot(x, y, preferred_element_type=F32)
    nt = lambda x, y: lax.dot_general(x, y, (((1,), (1,)), ((), ())), preferred_element_type=F32)
    tn = lambda x, y: lax.dot_general(x, y, (((0,), (0,)), ((), ())), preferred_element_type=F32)
    unstack = lambda x: x[0:c] + x[c:n]

    gram = [nt(jnp.concatenate([o["xa"], o["xr"]], axis=0).astype(BF16),
               jnp.concatenate([o["yb"], o["yk"]], axis=0).astype(BF16)) for o in ops]
    l_ab = [jnp.where(strict, x[0:n, 0:n], 0.0) for x in gram]
    l_ak = [jnp.where(strict, x[0:n, n:2 * n], 0.0) for x in gram]
    m_rbk = [jnp.concatenate([jnp.where(incl, x[n:2 * n, 0:n], 0.0), jnp.where(incl, x[n:2 * n, n:2 * n], 0.0)],
                             axis=1).astype(BF16) for x in gram]
    w1 = [mm(x.astype(BF16), o["vs"]) for x, o in zip(l_ak, ops)]
    t_inv = _tri_inverse(l_ab, n)
    tu = [mm(t.astype(BF16), jnp.concatenate([o["xa"], w], axis=1).astype(BF16))
          for t, o, w in zip(t_inv, ops, w1)]
    zmat = [jnp.concatenate([x.astype(BF16), jnp.concatenate([zeros, o["vs"]], axis=1)], axis=0)
            for x, o in zip(tu, ops)]
    ry = [mm(m, zm) for m, zm in zip(m_rbk, zmat)]
    ps = [tn(zm, jnp.concatenate([o["xbh"], o["xkh"]], axis=0).astype(BF16))
          for zm, o in zip(zmat, ops)]
    out = [(unstack(o["xr"] + y[:, 0:n]), unstack(y[:, n:2 * n]), o["e_cend"], q[0:n], q[n:2 * n])
           for o, y, q in zip(ops, ry, ps)]
    return [out[ci * RWKV_PAIRS:(ci + 1) * RWKV_PAIRS] for ci in range(len(chunks))]


def _rwkv_chunk_kernel(z_ref, sh_ref, s0_ref, mu_ref, lw_ref, w0_ref, la_ref, a0_ref, lg_ref, kk_ref, ka_ref, rk_ref,
                       gnw_ref, gnb_ref, y_ref, sout_ref, shout_ref, zbuf, s_scr, *, n_tiles, fill_layer):
    c = RWKV_CHUNK
    tile = RWKV_TILE
    zero = jnp.zeros((RWKV_HEAD, RWKV_HEAD), F32)
    for p in range(RWKV_PAIRS):
        s_scr[p] = jnp.concatenate([jnp.concatenate([s0_ref[0, 2 * p], zero], axis=1),
                                    jnp.concatenate([zero, s0_ref[0, 2 * p + 1]], axis=1)], axis=0)
    zbuf[pl.ds(0, 8), :] = jnp.broadcast_to(sh_ref[0], (8, RWKV_PAD))
    nt = lambda x, y: lax.dot_general(x, y, (((1,), (1,)), ((), ())), preferred_element_type=F32)

    def tile_body(ti, carry):
        r0 = pl.multiple_of(ti * tile, tile)
        z = z_ref[0, pl.ds(r0, tile), :]
        zbuf[pl.ds(8, tile), :] = z
        z_prev = zbuf[pl.ds(7, tile), :]
        zbuf[pl.ds(7, 1), :] = zbuf[pl.ds(7 + tile, 1), :]
        r, k, v, a_vec, b_vec, w_log, g, bonus = _rwkv_gates(
            z, z_prev, mu_ref, lw_ref, w0_ref, la_ref, a0_ref, lg_ref, kk_ref, ka_ref, rk_ref)
        logw = -jnp.exp(w_log)
        local = _rwkv_chunk_local([tuple(x[ci * c:(ci + 1) * c] for x in (r, k, v, a_vec, b_vec, logw))
                                   for ci in range(RWKV_TILE_CHUNKS)])
        y_cols = []
        for p in range(RWKV_PAIRS):
            s = s_scr[p]
            ys = []
            for ci in range(RWKV_TILE_CHUNKS):
                rc, yloc, e_cend, trans, sloc = local[ci][p]
                sb = s.astype(BF16)
                ys.append(nt(rc.astype(BF16), sb) + yloc)
                s = s * e_cend + jnp.dot(sb, trans.astype(BF16), preferred_element_type=F32) + sloc
            s_scr[p] = s
            y_cols.append(jnp.concatenate(ys, axis=0))
        y = jnp.concatenate(y_cols, axis=1)
        y_ref[0, pl.ds(r0, tile), :] = _rwkv_post(y, bonus, g, gnw_ref, gnb_ref).astype(y_ref.dtype)
        return carry

    lax.fori_loop(0, n_tiles, tile_body, 0)
    shout_ref[0] = zbuf[pl.ds(7, 1), :]
    sout = _own_layer(sout_ref, fill_layer)
    for p in range(RWKV_PAIRS):
        sout[0, 2 * p] = s_scr[p, 0:RWKV_HEAD, 0:RWKV_HEAD]
        sout[0, 2 * p + 1] = s_scr[p, RWKV_HEAD:2 * RWKV_HEAD, RWKV_HEAD:2 * RWKV_HEAD]


def _rwkv(z2, shift_prev, wkv0, pw, layer, b, l, prev_stack, out_dtype):
    vec = lambda n: _const_spec((1, n), lambda i: (0, 0))
    mat = lambda r, c: _const_spec((r, c), lambda i: (0, 0))
    params = [pw["mu"], pw["lora_w"], pw["w0"], pw["lora_a"], pw["a0"], pw["lora_g"],
              pw["k_k"], pw["k_a"], pw["r_k"], pw["gn_w"], pw["gn_b"]]
    pspecs = [vec(RWKV_PAD), mat(LORA_W, GROUP_W), vec(GROUP_W), mat(LORA_W, GROUP_W), vec(GROUP_W),
              mat(LORA_W, GROUP_W), vec(GROUP_W), vec(GROUP_W), vec(GROUP_W), vec(GROUP_W), vec(GROUP_W)]
    nseq = 1
    z_in = z2.reshape(b, l, Z_COLS)
    shift_prev = shift_prev.reshape(b, 1, RWKV_PAD)
    shspec = pl.BlockSpec((1, 1, RWKV_PAD), lambda i: (i, 0, 0))
    body = functools.partial(_rwkv_chunk_kernel, n_tiles=l // RWKV_TILE)
    zspec = pl.BlockSpec((1, l, RWKV_PAD), lambda i: (i, 0, Z_RWKV // RWKV_PAD))
    yspec = pl.BlockSpec((1, l, GROUP_W), lambda i: (i, 0, 0))
    yshape = jax.ShapeDtypeStruct((b, l, GROUP_W), out_dtype)
    scratch = [pltpu.VMEM((RWKV_TILE + 8, RWKV_PAD), F32), pltpu.VMEM((RWKV_PAIRS, 128, 128), F32)]
    first = prev_stack is None
    tail = (nseq, RWKV_HEADS, RWKV_HEAD, RWKV_HEAD)
    y, s_new, shift_new = _call_stacked(
        functools.partial(body, fill_layer=layer if first else None), [z_in, shift_prev, wkv0] + params,
        [zspec, shspec, _stack_out_spec(tail, layer, False)] + pspecs,
        1, prev_stack,
        grid=(b // nseq,),
        out_specs=[yspec, _stack_out_spec(tail, layer, first), shspec],
        out_shape=[yshape, jax.ShapeDtypeStruct((DEPTH, b, RWKV_HEADS, RWKV_HEAD, RWKV_HEAD), F32),
                   jax.ShapeDtypeStruct(shift_prev.shape, F32)],
        scratch_shapes=scratch,
        compiler_params=_cparams(("parallel",)),
        name="rwkv7")
    return y.reshape(b * l, GROUP_W), s_new, shift_new.reshape(b, RWKV_PAD)[:, :RWKV_COLS]


def _sgu_kernel(z_ref, lnw_ref, lnb_ref, wm_ref, bias_ref, nrm_ref, y_ref, v_ref, *, blk):
    z = z_ref[...]
    zg = 0.5 * z * (1.0 + lax.erf(z * 0.7071067811865476))
    u = zg[:, :GROUP_W]
    v = zg[:, GROUP_W:]
    mu = jnp.mean(v, axis=-1, keepdims=True)
    vc = v - mu
    var = jnp.mean(vc * vc, axis=-1, keepdims=True)
    vn = vc * lax.rsqrt(var + LN_EPS) * lnw_ref[...] + lnb_ref[...]
    v_ref[...] = vn
    row = lax.broadcasted_iota(jnp.int32, (SGU_CHUNK, SGU_CHUNK), 0)
    col = lax.broadcasted_iota(jnp.int32, (SGU_CHUNK, SGU_CHUNK), 1)
    mask = col <= row
    if blk < SGU_CHUNK:
        mask = mask & ((row // blk) == (col // blk))
    vb = vn.astype(BF16)
    wms = [jnp.where(mask, wm_ref[h], 0.0).astype(BF16) for h in range(SGU_HEADS)]
    rows = []
    for c in range(z.shape[0] // SGU_CHUNK):
        rs = slice(c * SGU_CHUNK, (c + 1) * SGU_CHUNK)
        rows.append(jnp.concatenate(
            [jnp.dot(wms[h], vb[rs, h * 128:(h + 1) * 128], preferred_element_type=F32) + bias_ref[:, h:h + 1]
             for h in range(SGU_HEADS)], axis=1))
    s = rows[0] if len(rows) == 1 else jnp.concatenate(rows, axis=0)
    y_ref[...] = _rms(u * s, nrm_ref[...]).astype(y_ref.dtype)


def _sgu(z2, pw, wm, bias, blk, out_dtype):
    t = z2.shape[0]
    tile = SGU_TILE_CHUNKS * SGU_CHUNK
    vec = _const_spec((1, GROUP_W), lambda i: (0, 0))
    return pl.pallas_call(
        functools.partial(_sgu_kernel, blk=blk),
        grid=(t // tile,),
        in_specs=[pl.BlockSpec((tile, SGU_COLS), lambda i: (i, Z_SGU // SGU_COLS)),
                  vec, vec,
                  _const_spec((SGU_HEADS, SGU_CHUNK, SGU_CHUNK), lambda i: (0, 0, 0)),
                  _const_spec((SGU_CHUNK, SGU_HEADS), lambda i: (0, 0)),
                  vec],
        out_specs=[pl.BlockSpec((tile, GROUP_W), lambda i: (i, 0)),
                   pl.BlockSpec((tile, GROUP_W), lambda i: (i, 0))],
        out_shape=[jax.ShapeDtypeStruct((t, GROUP_W), out_dtype), jax.ShapeDtypeStruct((t, GROUP_W), F32)],
        compiler_params=_cparams(("parallel",)),
        name="sgu",
    )(z2, pw["ln_w"], pw["ln_b"], wm, bias, pw["norm"])


def _hgrn_chunks(seqs, lb, nw, chunk, sub):
    nh = HGRN_HEADS
    hs = lambda x, h: x[:, h * HGRN_HEAD:(h + 1) * HGRN_HEAD]
    tri = (lax.broadcasted_iota(jnp.int32, (chunk, chunk), 1)
           <= lax.broadcasted_iota(jnp.int32, (chunk, chunk), 0)).astype(BF16)
    eye = (lax.broadcasted_iota(jnp.int32, (HGRN_HEAD, HGRN_HEAD), 0)
           == lax.broadcasted_iota(jnp.int32, (HGRN_HEAD, HGRN_HEAD), 1)).astype(F32)
    rowid = lax.broadcasted_iota(jnp.int32, (sub, 1), 0)
    d = functools.partial(jnp.dot, preferred_element_type=F32)
    nt = lambda x, y: lax.dot_general(x, y, (((1,), (1,)), ((), ())), preferred_element_type=F32)
    tn = lambda x, y: lax.dot_general(x, y, (((0,), (0,)), ((), ())), preferred_element_type=F32)

    pre = []
    for zq, zf, zi, zg, states in seqs:
        q = _silu(zq)
        fg = lb + (1.0 - lb) * _sigmoid(zf)
        k = 1.0 - fg
        b = _dot_f32_rhs(tri, jnp.log(fg))
        pre.append(dict(q=q, k=k, v=zi, vb=zi.astype(BF16), b=b, qe=(q * jnp.exp(b)).astype(BF16)))
    o = [[d(hs(u["qe"], h), st[h].astype(BF16)) for h in range(nh)] for u, (_, _, _, _, st) in zip(pre, seqs)]
    outs = [[[] for _ in range(nh)] for _ in seqs]
    for i in range(chunk // sub):
        lo_, hi_ = i * sub, (i + 1) * sub
        oi = [[o[n][h][lo_:hi_] for h in range(nh)] for n in range(len(seqs))]
        if i > 0:
            for n, u in enumerate(pre):
                bref = u["b"][lo_ - 1:lo_]
                qt = (u["q"][lo_:hi_] * jnp.exp(u["b"][lo_:hi_] - bref)).astype(BF16)
                kt = (u["k"][:lo_] * jnp.exp(bref - u["b"][:lo_])).astype(BF16)
                att = [nt(hs(qt, h), hs(kt, h)).astype(BF16) for h in range(nh)]
                for h in range(nh):
                    oi[n][h] = oi[n][h] + d(att[h], hs(u["vb"], h)[:lo_])
        tiles = []
        for u in pre:
            qi, bi = u["q"][lo_:hi_], u["b"][lo_:hi_]
            for j in range(sub):
                sj = lo_ + j
                pj = qi * jnp.exp(jnp.minimum(bi - u["b"][sj:sj + 1], 0.0)) * u["k"][sj:sj + 1]
                tiles.extend(hs(pj, h) for h in range(nh))
        att = _group_sums(jnp.concatenate(tiles, axis=0), HGRN_HEAD, terms=1)
        t = 0
        for n, u in enumerate(pre):
            for j in range(sub):
                sj = lo_ + j
                for h in range(nh):
                    a_j = jnp.where(rowid >= j, att[t * sub:(t + 1) * sub], 0.0)
                    oi[n][h] = oi[n][h] + a_j * hs(u["v"], h)[sj:sj + 1]
                    t += 1
        for n in range(len(seqs)):
            for h in range(nh):
                outs[n][h].append(oi[n][h])
    results = []
    for n, (u, (_, _, _, zg, st)) in enumerate(zip(pre, seqs)):
        o_n = jnp.concatenate([outs[n][h][0] if len(outs[n][h]) == 1 else jnp.concatenate(outs[n][h], axis=0)
                               for h in range(nh)], axis=1)
        blast = u["b"][chunk - 1:chunk]
        kd = (u["k"] * jnp.exp(blast - u["b"])).astype(BF16)
        e_last = jnp.exp(blast)
        new_states = []
        for h in range(nh):
            e_col = _group_sums(eye * hs(e_last, h), HGRN_HEAD)
            new_states.append(e_col * st[h] + tn(hs(kd, h), hs(u["vb"], h)))
        ms = _group_sums(o_n * o_n, HGRN_HEAD) * (1.0 / HGRN_HEAD)
        results.append((o_n * lax.rsqrt(ms + NORM_EPS) * nw * _silu(zg), new_states))
    return results


def _hgrn_kernel(zq_ref, zf_ref, zi_ref, zg_ref, lg_ref, nw_ref, s0_ref, y_ref, sout_ref, s_scr,
                 *, layer, nseq, chunk, sub, n_chunks, fill_layer):
    lg = lg_ref[...]
    e = jnp.exp(lg - jnp.max(lg, axis=0, keepdims=True))
    sm = e / jnp.sum(e, axis=0, keepdims=True)
    lb = jnp.sum(sm[0:layer + 1], axis=0, keepdims=True) - sm[0:1]
    nw = nw_ref[...]
    nh = HGRN_HEADS
    sout = _own_layer(sout_ref, fill_layer)
    if n_chunks == 1:
        rows = lambda ref, b: ref[b * chunk:(b + 1) * chunk, :]
        seqs = [(rows(zq_ref, b), rows(zf_ref, b), rows(zi_ref, b), rows(zg_ref, b),
                 [s0_ref[b, h] for h in range(nh)]) for b in range(nseq)]
        ys = []
        for b, (y, states) in enumerate(_hgrn_chunks(seqs, lb, nw, chunk, sub)):
            ys.append(y)
            for h in range(nh):
                sout[b, h] = states[h]
        y_ref[...] = jnp.concatenate(ys, axis=0).astype(y_ref.dtype)
    else:
        s_scr[...] = s0_ref[...]

        def chunk_body(c, carry):
            r0 = pl.multiple_of(c * chunk, chunk)
            rows = lambda ref, b: ref[b, pl.ds(r0, chunk), :]
            seqs = [(rows(zq_ref, b), rows(zf_ref, b), rows(zi_ref, b), rows(zg_ref, b),
                     [s_scr[b, h] for h in range(nh)]) for b in range(nseq)]
            for b, (y, states) in enumerate(_hgrn_chunks(seqs, lb, nw, chunk, sub)):
                y_ref[b, pl.ds(r0, chunk), :] = y.astype(y_ref.dtype)
                for h in range(nh):
                    s_scr[b, h] = states[h]
            return carry

        lax.fori_loop(0, n_chunks, chunk_body, 0)
        sout[...] = s_scr[...]


def _hgrn(z2, s0, lb_logits, norm_w, layer, b, l, prev_stack, out_dtype):
    chunk = min(HGRN_CHUNK, l)
    sub = min(HGRN_SUB, chunk)
    cb = Z_HGRN // GROUP_W
    if l > chunk:
        nseq = 1
        z_in = z2.reshape(b, l, Z_COLS)
        zspec = lambda part: pl.BlockSpec((1, l, GROUP_W), lambda i: (i, 0, cb + part))
        yspec = pl.BlockSpec((1, l, GROUP_W), lambda i: (i, 0, 0))
        yshape = jax.ShapeDtypeStruct((b, l, GROUP_W), out_dtype)
    else:
        nseq = SHORT_SEQS
        z_in = z2
        zspec = lambda part: pl.BlockSpec((nseq * l, GROUP_W), lambda i: (i, cb + part))
        yspec = pl.BlockSpec((nseq * l, GROUP_W), lambda i: (i, 0))
        yshape = jax.ShapeDtypeStruct((b * l, GROUP_W), out_dtype)
    first = prev_stack is None
    tail = (nseq, HGRN_HEADS, HGRN_HEAD, HGRN_HEAD)
    y, s_new = _call_stacked(
        functools.partial(_hgrn_kernel, layer=layer, nseq=nseq, chunk=chunk, sub=sub, n_chunks=l // chunk,
                          fill_layer=layer if first else None),
        [z_in, z_in, z_in, z_in, lb_logits, norm_w, s0],
        [zspec(0), zspec(1), zspec(2), zspec(3),
         _const_spec((DEPTH, GROUP_W), lambda i: (0, 0)),
         _const_spec((1, GROUP_W), lambda i: (0, 0)),
         _stack_out_spec(tail, layer, False)],
        1, prev_stack,
        grid=(b // nseq,),
        out_specs=[yspec, _stack_out_spec(tail, layer, first)],
        out_shape=[yshape, jax.ShapeDtypeStruct((DEPTH, b, HGRN_HEADS, HGRN_HEAD, HGRN_HEAD), F32)],
        scratch_shapes=[pltpu.VMEM((nseq, HGRN_HEADS, HGRN_HEAD, HGRN_HEAD), F32)],
        compiler_params=_cparams(("parallel",)),
        name="hgrn2")
    return y.reshape(b * l, GROUP_W), s_new


def _pool_kernel(z_ref, hist_ref, pw_ref, ps_ref, y_ref, hout_ref, xbuf, s2buf, s4buf, s8buf,
                 *, nseq, tc, n_tiles, start_pos, flat):
    halo = POOL_HALO
    n = tc + halo
    c1 = POOL_CH
    for b in range(nseq):
        xbuf[b, pl.ds(0, 16), :] = jnp.zeros((16, GROUP_W), F32)
        xbuf[b, pl.ds(16, 16), :] = hist_ref[b]

    def tile(r0, b, z):
        xbuf[b, pl.ds(halo, tc), :] = z
        s2buf[b, pl.ds(8, n - 8), :] = xbuf[b, pl.ds(8, n - 8), :] + xbuf[b, pl.ds(7, n - 8), :]
        s4buf[b, pl.ds(16, n - 16), :] = s2buf[b, pl.ds(16, n - 16), c1:] + s2buf[b, pl.ds(14, n - 16), c1:]
        s8buf[b, pl.ds(24, n - 24), :] = s4buf[b, pl.ds(24, n - 24), c1:] + s4buf[b, pl.ds(20, n - 24), c1:]
        s16 = s8buf[b, pl.ds(halo, tc), c1:] + s8buf[b, pl.ds(halo - 8, tc), c1:]
        sums = (s2buf[b, pl.ds(halo, tc), 0:c1], s4buf[b, pl.ds(halo, tc), 0:c1], s8buf[b, pl.ds(halo, tc), 0:c1],
                s16)
        pos = start_pos + r0 + lax.broadcasted_iota(jnp.int32, (tc, 1), 0)
        ys = []
        for gi, win in enumerate(POOL_WINDOWS):
            sl = slice(gi * POOL_CH, (gi + 1) * POOL_CH)
            cnt = jnp.minimum(pos + 1, win).astype(F32)
            dlt = sums[gi] / cnt - z[:, sl]
            ys.append(jnp.dot(dlt.astype(BF16), pw_ref[gi], preferred_element_type=F32) * ps_ref[:, sl])
        if n_tiles > 1:
            xbuf[b, pl.ds(16, 16), :] = xbuf[b, pl.ds(n - 16, 16), :]
        return jnp.concatenate(ys, axis=1)

    if flat:
        y_ref[...] = jnp.concatenate([tile(0, b, z_ref[b * tc:(b + 1) * tc, :]) for b in range(nseq)],
                                     axis=0).astype(y_ref.dtype)
    else:
        def tile_body(ti, carry):
            r0 = pl.multiple_of(ti * tc, tc)
            for b in range(nseq):
                y_ref[b, pl.ds(r0, tc), :] = tile(r0, b, z_ref[b, pl.ds(r0, tc), :]).astype(y_ref.dtype)
            return carry

        lax.fori_loop(0, n_tiles, tile_body, 0)
    for b in range(nseq):
        hout_ref[b] = xbuf[b, pl.ds(n - POOL_HIST, POOL_HIST), :]


def _pool(z2, hist16, pool_w, pool_scale, start_pos, b, l, tc, out_dtype):
    cb = Z_POOL // GROUP_W
    if l >= POOL_HALO:
        nseq = 1
        z_in = z2.reshape(b, l, Z_COLS)
        zspec = pl.BlockSpec((1, l, GROUP_W), lambda i: (i, 0, cb))
        yspec = pl.BlockSpec((1, l, GROUP_W), lambda i: (i, 0, 0))
        yshape = jax.ShapeDtypeStruct((b, l, GROUP_W), out_dtype)
    else:
        nseq = SHORT_SEQS
        z_in = z2
        zspec = pl.BlockSpec((nseq * l, GROUP_W), lambda i: (i, cb))
        yspec = pl.BlockSpec((nseq * l, GROUP_W), lambda i: (i, 0))
        yshape = jax.ShapeDtypeStruct((b * l, GROUP_W), out_dtype)
    n = tc + POOL_HALO
    y, hist_new = pl.pallas_call(
        functools.partial(_pool_kernel, nseq=nseq, tc=tc, n_tiles=l // tc, start_pos=start_pos, flat=nseq > 1),
        grid=(b // nseq,),
        in_specs=[zspec,
                  pl.BlockSpec((nseq, 16, GROUP_W), lambda i: (i, 0, 0)),
                  _const_spec((len(POOL_WINDOWS), POOL_CH, POOL_CH), lambda i: (0, 0, 0)),
                  _const_spec((1, GROUP_W), lambda i: (0, 0))],
        out_specs=[yspec, pl.BlockSpec((nseq, POOL_HIST, GROUP_W), lambda i: (i, 0, 0))],
        out_shape=[yshape, jax.ShapeDtypeStruct((b, POOL_HIST, GROUP_W), F32)],
        scratch_shapes=[pltpu.VMEM((nseq, n, GROUP_W), F32), pltpu.VMEM((nseq, n, GROUP_W), F32),
                        pltpu.VMEM((nseq, n, 3 * POOL_CH), F32), pltpu.VMEM((nseq, n, 2 * POOL_CH), F32)],
        compiler_params=_cparams(("parallel",)),
        name="pool",
    )(z_in, hist16, pool_w, pool_scale)
    return y.reshape(b * l, GROUP_W), hist_new


def _pad_cols(a, n):
    return jnp.pad(a, ((0, 0),) * (a.ndim - 1) + ((0, n - a.shape[-1]),))


def _w_in_kernel(w_ref, o_ref):
    c1, c2, c3 = RWKV_COLS, RWKV_COLS + SGU_COLS, RWKV_COLS + SGU_COLS + HGRN_COLS
    o_ref[:, Z_HGRN:Z_HGRN + HGRN_COLS] = w_ref[:, c2:c3].astype(BF16)
    o_ref[:, Z_SGU:Z_SGU + SGU_COLS] = w_ref[:, c1:c2].astype(BF16)
    o_ref[:, Z_POOL:Z_POOL + GROUP_W] = w_ref[:, c3:].astype(BF16)
    o_ref[:, Z_RWKV:Z_RWKV + RWKV_COLS] = w_ref[:, :c1].astype(BF16)
    o_ref[:, Z_RWKV + RWKV_COLS:] = jnp.zeros((o_ref.shape[0], RWKV_PAD - RWKV_COLS), BF16)


def _w_in_layout(w_in, rows=256):
    in_cols = w_in.shape[2]
    return pl.pallas_call(
        _w_in_kernel,
        grid=(DEPTH, D_MODEL // rows),
        in_specs=[pl.BlockSpec((None, rows, in_cols), lambda l, i: (l, i, 0))],
        out_specs=pl.BlockSpec((None, rows, Z_COLS), lambda l, i: (l, i, 0)),
        out_shape=jax.ShapeDtypeStruct((DEPTH, D_MODEL, Z_COLS), BF16),
        compiler_params=_cparams(("parallel", "parallel")),
        name="w_in_layout",
    )(w_in)


def _prep_weights(W):
    return dict(w_in=_w_in_layout(W["w_in"]), w_out=W["w_out"].astype(BF16), w_gu=W["ffn_w_gu"].astype(BF16),
                w_down=W["ffn_w_down"].astype(BF16), ple_gate=W["ple_gate"].astype(BF16),
                ple_proj=W["ple_proj"].astype(BF16))


def _prep_layer(i, W):
    row = lambda a: a.reshape(1, -1)
    lora = lambda a, off: jnp.pad(a, ((off, LORA_W - off - a.shape[0]), (0, 0))).astype(BF16)
    rw = dict(
        mu=_pad_cols(row(W["rwkv_mu"][i]), RWKV_PAD),
        lora_w=lora(W["rwkv_w_lora"][i], 0), w0=row(W["rwkv_w0"][i]),
        lora_a=lora(W["rwkv_a_lora"][i], W_LORA), a0=row(W["rwkv_a0"][i]),
        lora_g=lora(W["rwkv_g_lora"][i], W_LORA + A_LORA),
        k_k=row(W["rwkv_k_k"][i]), k_a=row(W["rwkv_k_a"][i]), r_k=row(W["rwkv_r_k"][i]),
        gn_w=row(W["rwkv_gn_w"][i]), gn_b=row(W["rwkv_gn_b"][i]))
    sg = dict(ln_w=row(W["sgu_ln_w"][i]), ln_b=row(W["sgu_ln_b"][i]), norm=row(W["sgu_norm"][i]))
    return dict(
        rw=rw, sg=sg, sgu_w=W["sgu_w"][i], sgu_b=W["sgu_b"][i],
        hgrn_norm=row(W["hgrn_norm"][i]), pool_w=W["pool_w"][i].astype(BF16), pool_scale=row(W["pool_scale"][i]),
        ln_mix_pre=row(W["ln_mix_pre"][i]), ln_mix_post=row(W["ln_mix_post"][i]),
        ln_ffn_pre=row(W["ln_ffn_pre"][i]), ln_ffn_post=row(W["ln_ffn_post"][i]))


def _trunk(x, p, wkv0, shift0, hgrn0, pool0, start_pos, big, layers, lb_logits, cfg):
    b, l, _ = x.shape
    t = b * l
    xf = x.reshape(t, D_MODEL)
    p2 = p.reshape(DEPTH, t, PLE_DIM)
    wkv_new = hgrn_new = None
    shift_l, pool_l, sgu_l = [], [], []
    long_seq = l % RWKV_TILE == 0
    rwkv = _rwkv if long_seq else _rwkv_lanes
    if not long_seq:
        wkv0 = wkv0.transpose(0, 2, 3, 4, 1)
    for i, lw in enumerate(layers):
        z2 = _inproj(xf, lw["ln_mix_pre"], big["w_in"], i, min(cfg["tm_in"], t), cfg["tn_in"])
        y_r, wkv_new, shift_new = rwkv(z2, _pad_cols(shift0[i], RWKV_PAD), wkv0, lw["rw"], i, b, l, wkv_new,
                                       cfg["ydt"])
        if cfg["sgu_blk"] == SGU_CHUNK:
            wm, bias = lw["sgu_w"], lw["sgu_b"].T
        else:
            rep = SGU_CHUNK // l
            wm = jnp.tile(lw["sgu_w"][:, :l, :l], (1, rep, rep))
            bias = jnp.tile(lw["sgu_b"][:, :l].T, (rep, 1))
        y_s, v_rows = _sgu(z2, lw["sg"], wm, bias, cfg["sgu_blk"], cfg["ydt"])
        y_h, hgrn_new = _hgrn(z2, hgrn0, lb_logits, lw["hgrn_norm"], i, b, l, hgrn_new, cfg["ydt"])
        hist16 = jnp.pad(pool0[i], ((0, 0), (1, 0), (0, 0)))
        y_p, pool_new = _pool(z2, hist16, lw["pool_w"], lw["pool_scale"], start_pos, b, l, cfg["tc_pool"],
                              cfg["ydt"])
        xf = _outproj([y_r, y_s, y_h, y_p], xf, big["w_out"], lw["ln_mix_post"], i, cfg["tm"])
        xf = _ffn(xf, lw["ln_ffn_pre"], big["w_gu"], big["w_down"], lw["ln_ffn_post"], p2, big["ple_gate"],
                  big["ple_proj"], i, cfg["tm"])
        shift_l.append(shift_new)
        pool_l.append(pool_new)
        sgu_l.append(v_rows.reshape(b, l, GROUP_W))
    if not long_seq:
        wkv_new = wkv_new.transpose(0, 4, 1, 2, 3)
    return (xf.reshape(b, l, D_MODEL), wkv_new, jnp.stack(shift_l), hgrn_new, jnp.stack(pool_l), jnp.stack(sgu_l))


def _cfg(l):
    if l >= 256:
        return dict(tm=512, tm_in=1024, tn_in=1792, sgu_blk=SGU_CHUNK, tc_pool=256, ydt=BF16)
    return dict(tm=512, tm_in=512, tn_in=1792, sgu_blk=l, tc_pool=l, ydt=BF16)


def kernel(x_prompt, x_sample, state_rwkv_wkv, state_rwkv_shift, state_hgrn, state_pool, p_prompt, p_sample,
           ln_mix_pre, ln_mix_post, ln_ffn_pre, ln_ffn_post, w_in, rwkv_mu, rwkv_w_lora, rwkv_w0, rwkv_a_lora,
           rwkv_a0, rwkv_g_lora, rwkv_k_k, rwkv_k_a, rwkv_r_k, rwkv_gn_w, rwkv_gn_b, sgu_ln_w, sgu_ln_b, sgu_w,
           sgu_b, sgu_norm, hgrn_lb_logits, hgrn_norm, pool_w, pool_scale, w_out, ffn_w_gu, ffn_w_down, ple_gate,
           ple_proj):
    W = dict(ln_mix_pre=ln_mix_pre, ln_mix_post=ln_mix_post, ln_ffn_pre=ln_ffn_pre, ln_ffn_post=ln_ffn_post,
             w_in=w_in, rwkv_mu=rwkv_mu, rwkv_w_lora=rwkv_w_lora, rwkv_w0=rwkv_w0, rwkv_a_lora=rwkv_a_lora,
             rwkv_a0=rwkv_a0, rwkv_g_lora=rwkv_g_lora, rwkv_k_k=rwkv_k_k, rwkv_k_a=rwkv_k_a,
             rwkv_r_k=rwkv_r_k.reshape(DEPTH, GROUP_W), rwkv_gn_w=rwkv_gn_w, rwkv_gn_b=rwkv_gn_b, sgu_ln_w=sgu_ln_w,
             sgu_ln_b=sgu_ln_b, sgu_w=sgu_w, sgu_b=sgu_b, sgu_norm=sgu_norm, hgrn_norm=hgrn_norm, pool_w=pool_w,
             pool_scale=pool_scale, w_out=w_out, ffn_w_gu=ffn_w_gu, ffn_w_down=ffn_w_down, ple_gate=ple_gate,
             ple_proj=ple_proj)
    big = _prep_weights(W)
    layers = [_prep_layer(i, W) for i in range(DEPTH)]
    lb_logits = hgrn_lb_logits.astype(F32)
    bp, lp, _ = x_prompt.shape
    zeros = lambda *s: jnp.zeros((DEPTH, bp) + s, F32)
    y_prompt, wkv_p, shift_p, hgrn_p, pool_p, _ = _trunk(
        x_prompt, p_prompt, zeros(RWKV_HEADS, RWKV_HEAD, RWKV_HEAD), zeros(RWKV_COLS),
        zeros(HGRN_HEADS, HGRN_HEAD, HGRN_HEAD), zeros(POOL_HIST, GROUP_W), 0, big, layers, lb_logits, _cfg(lp))
    y_sample, wkv_s, shift_s, hgrn_s, pool_s, sgu_v_s = _trunk(
        x_sample, p_sample, state_rwkv_wkv, state_rwkv_shift, state_hgrn, state_pool, PAST_LEN, big, layers,
        lb_logits, _cfg(x_sample.shape[1]))
    return (y_prompt, y_sample, wkv_p, shift_p, hgrn_p, pool_p, wkv_s, shift_s, hgrn_s, pool_s, sgu_v_s)
```

```python
import functools

import jax
import jax.numpy as jnp
from jax import lax
from jax.experimental import pallas as pl
from jax.experimental.pallas import tpu as pltpu

F32 = jnp.float32
BF16 = jnp.bfloat16

D_MODEL = 2048
DEPTH = 2
PAST_LEN = 16384
GROUP_W = 512
RWKV_HEAD = 64
RWKV_HEADS = 8
RWKV_PAIRS = RWKV_HEADS // 2
RWKV_CHUNK = 64
RWKV_TILE_CHUNKS = 4
RWKV_TILE = RWKV_TILE_CHUNKS * RWKV_CHUNK
W_LORA, A_LORA, G_LORA = 32, 32, 96
RWKV_GN_EPS = 64e-5
SGU_CHUNK = 128
SGU_HEADS = 4
SGU_TILE_CHUNKS = 4
HGRN_HEADS = 4
HGRN_HEAD = 128
HGRN_CHUNK = 64
HGRN_SUB = 16
POOL_WINDOWS = (2, 4, 8, 16)
POOL_CH = 128
POOL_HIST = 15
POOL_HALO = 32
D_FF = 5632
PLE_DIM = 256
PLE_COLS = 512
FFN_TF = 512
NORM_EPS = 1e-6
LN_EPS = 1e-5
RWKV_COLS = 3 * GROUP_W + W_LORA + A_LORA + G_LORA
RWKV_PAD = 1792
LORA_OFF = 3 * GROUP_W
LORA_W = RWKV_PAD - LORA_OFF
SGU_COLS = 2 * GROUP_W
HGRN_COLS = 4 * GROUP_W
Z_HGRN, Z_SGU, Z_POOL, Z_RWKV = 0, 2048, 3072, 3584
Z_COLS = Z_RWKV + RWKV_PAD
SHORT_SEQS = 8
VMEM_LIMIT = 56 * 1024 * 1024


def _cparams(sem):
    return pltpu.CompilerParams(dimension_semantics=sem, vmem_limit_bytes=VMEM_LIMIT)


def _const_spec(shape, index_map):
    return pl.BlockSpec(shape, index_map, pipeline_mode=pl.Buffered(1))


def _sigmoid(x):
    return jax.nn.sigmoid(x)


def _silu(x):
    return x * jax.nn.sigmoid(x)


def _split3(x):
    hi = x.astype(BF16)
    r = x - hi.astype(F32)
    mid = r.astype(BF16)
    lo = (r - mid.astype(F32)).astype(BF16)
    return hi, mid, lo


def _group_sums(x, width, terms=3):
    rows, cols = x.shape
    nblk = cols // 128
    xs = x if nblk == 1 else jnp.concatenate([x[:, i * 128:(i + 1) * 128] for i in range(nblk)], axis=0)
    ones = ((lax.broadcasted_iota(jnp.int32, (128, 128), 0) // width)
            == (lax.broadcasted_iota(jnp.int32, (128, 128), 1) // width)).astype(BF16)
    s = None
    for part in _split3(xs)[:terms]:
        y = jnp.dot(part, ones, preferred_element_type=F32)
        s = y if s is None else s + y
    return s if nblk == 1 else jnp.concatenate([s[i * rows:(i + 1) * rows] for i in range(nblk)], axis=1)


def _dot_f32_rhs(m, x):
    hi, mid, lo = _split3(x)
    d = functools.partial(jnp.dot, preferred_element_type=F32)
    return d(m, hi) + d(m, mid) + d(m, lo)


def _rms(x, g):
    return x * lax.rsqrt(jnp.mean(x * x, axis=-1, keepdims=True) + NORM_EPS) * g


def _call_stacked(kernel, inputs, in_specs, stack_out, prev_stack, **kw):
    if prev_stack is None:
        return pl.pallas_call(kernel, in_specs=in_specs, **kw)(*inputs)
    n_in = len(inputs)

    def body(*refs):
        kernel(*refs[:n_in], *refs[n_in + 1:])

    return pl.pallas_call(body, in_specs=list(in_specs) + [pl.BlockSpec(memory_space=pl.ANY)],
                          input_output_aliases={n_in: stack_out}, **kw)(*inputs, prev_stack)


def _stack_out_spec(tail, layer, first):
    zeros = (0,) * (len(tail) - 1)
    if first:
        return pl.BlockSpec((DEPTH,) + tail, lambda i: (0, i) + zeros)
    return pl.BlockSpec((None,) + tail, lambda i: (layer, i) + zeros)


def _own_layer(sout_ref, fill_layer):
    if fill_layer is None:
        return sout_ref
    for j in range(DEPTH):
        if j != fill_layer:
            sout_ref[j] = jnp.zeros(sout_ref.shape[1:], sout_ref.dtype)
    return sout_ref.at[fill_layer]


def _inproj_kernel(x_ref, g_ref, w_ref, z_ref, h_ref):
    @pl.when(pl.program_id(1) == 0)
    def _():
        h_ref[...] = _rms(x_ref[...], g_ref[...]).astype(BF16)

    z_ref[...] = jnp.dot(h_ref[...], w_ref[...], preferred_element_type=F32)


def _inproj(x, g, w, layer, tm, tn):
    t = x.shape[0]
    return pl.pallas_call(
        _inproj_kernel,
        grid=(t // tm, Z_COLS // tn),
        in_specs=[pl.BlockSpec((tm, D_MODEL), lambda i, j: (i, 0)),
                  _const_spec((1, D_MODEL), lambda i, j: (0, 0)),
                  pl.BlockSpec((None, D_MODEL, tn), lambda i, j: (layer, 0, j))],
        out_specs=pl.BlockSpec((tm, tn), lambda i, j: (i, j)),
        out_shape=jax.ShapeDtypeStruct((t, Z_COLS), F32),
        scratch_shapes=[pltpu.VMEM((tm, D_MODEL), BF16)],
        compiler_params=_cparams(("parallel", "arbitrary")),
        name="inproj",
    )(x, g, w)


def _outproj_kernel(yr_ref, ys_ref, yh_ref, yp_ref, x_ref, w_ref, g_ref, o_ref):
    d = functools.partial(jnp.dot, preferred_element_type=F32)
    g4 = GROUP_W
    mix = (d(yr_ref[...], w_ref[0:g4, :]) + d(ys_ref[...], w_ref[g4:2 * g4, :])
           + d(yh_ref[...], w_ref[2 * g4:3 * g4, :]) + d(yp_ref[...], w_ref[3 * g4:4 * g4, :]))
    o_ref[...] = x_ref[...] + _rms(mix, g_ref[...])


def _outproj(ys, x, w, g, layer, tm):
    t = x.shape[0]
    yspec = pl.BlockSpec((tm, GROUP_W), lambda i: (i, 0))
    return pl.pallas_call(
        _outproj_kernel,
        grid=(t // tm,),
        in_specs=[yspec, yspec, yspec, yspec,
                  pl.BlockSpec((tm, D_MODEL), lambda i: (i, 0)),
                  _const_spec((None, D_MODEL, D_MODEL), lambda i: (layer, 0, 0)),
                  _const_spec((1, D_MODEL), lambda i: (0, 0))],
        out_specs=pl.BlockSpec((tm, D_MODEL), lambda i: (i, 0)),
        out_shape=jax.ShapeDtypeStruct((t, D_MODEL), F32),
        compiler_params=_cparams(("parallel",)),
        name="outproj",
    )(*ys, x, w, g)


def _ffn_kernel(x_ref, gpre_ref, wg_ref, wu_ref, wd_ref, gpost_ref, p_ref, wgate_ref, wproj_ref, o_ref, h_ref, acc_ref):
    j = pl.program_id(1)

    @pl.when(j == 0)
    def _():
        h_ref[...] = _rms(x_ref[...], gpre_ref[...]).astype(BF16)
        acc_ref[...] = jnp.zeros_like(acc_ref)

    h = h_ref[...]
    gate = jnp.dot(h, wg_ref[...], preferred_element_type=F32)
    up = jnp.dot(h, wu_ref[...], preferred_element_type=F32)
    act = (_silu(gate) * up).astype(BF16)
    acc_ref[...] += jnp.dot(act, wd_ref[...], preferred_element_type=F32)

    @pl.when(j == pl.num_programs(1) - 1)
    def _():
        acc_ref[...] = x_ref[...] + _rms(acc_ref[...], gpost_ref[...])
        h_ref[...] = acc_ref[...].astype(BF16)
        pb = p_ref[...].astype(BF16)
        for c in range(D_MODEL // PLE_COLS):
            cs = slice(c * PLE_COLS, (c + 1) * PLE_COLS)
            pgate = _sigmoid(jnp.dot(h_ref[...], wgate_ref[:, cs], preferred_element_type=F32))
            ple = jnp.dot(pb, wproj_ref[:, cs], preferred_element_type=F32)
            o_ref[:, cs] = acc_ref[:, cs] + pgate * ple


def _ffn(x, gpre, w_gu, w_down, gpost, p, w_gate, w_proj, layer, tm):
    t = x.shape[0]
    tf = FFN_TF
    nf = D_FF // tf
    return pl.pallas_call(
        _ffn_kernel,
        grid=(t // tm, nf),
        in_specs=[pl.BlockSpec((tm, D_MODEL), lambda i, j: (i, 0)),
                  _const_spec((1, D_MODEL), lambda i, j: (0, 0)),
                  pl.BlockSpec((None, D_MODEL, tf), lambda i, j: (layer, 0, j)),
                  pl.BlockSpec((None, D_MODEL, tf), lambda i, j: (layer, 0, j + nf)),
                  pl.BlockSpec((None, tf, D_MODEL), lambda i, j: (layer, j, 0)),
                  _const_spec((1, D_MODEL), lambda i, j: (0, 0)),
                  pl.BlockSpec((None, tm, PLE_DIM), lambda i, j: (layer, i, 0)),
                  _const_spec((None, D_MODEL, D_MODEL), lambda i, j: (layer, 0, 0)),
                  _const_spec((None, PLE_DIM, D_MODEL), lambda i, j: (layer, 0, 0))],
        out_specs=pl.BlockSpec((tm, D_MODEL), lambda i, j: (i, 0)),
        out_shape=jax.ShapeDtypeStruct((t, D_MODEL), F32),
        scratch_shapes=[pltpu.VMEM((tm, D_MODEL), BF16), pltpu.VMEM((tm, D_MODEL), F32)],
        compiler_params=_cparams(("parallel", "arbitrary")),
        name="ffn",
    )(x, gpre, w_gu, w_gu, w_down, gpost, p, w_gate, w_proj)


def _rwkv_gates(z, z_prev, mu_ref, lw_ref, w0_ref, la_ref, a0_ref, lg_ref, kk_ref, ka_ref, rk_ref):
    zm = z + mu_ref[...] * (z_prev - z)
    g4 = GROUP_W
    r = zm[:, 0:g4]
    k = zm[:, g4:2 * g4]
    v = zm[:, 2 * g4:3 * g4]
    zl = zm[:, LORA_OFF:RWKV_PAD]
    d = functools.partial(jnp.dot, preferred_element_type=F32)
    wl = w0_ref[...] + d(jnp.tanh(zl).astype(BF16), lw_ref[...])
    w_log = -(jnp.maximum(-wl, 0.0) + jnp.log1p(jnp.exp(-jnp.abs(wl)))) - 0.5
    a = _sigmoid(a0_ref[...] + d(zl.astype(BF16), la_ref[...]))
    g = d(_sigmoid(zl).astype(BF16), lg_ref[...])
    kk = k * kk_ref[...]
    kk = kk / jnp.maximum(jnp.sqrt(_group_sums(kk * kk, RWKV_HEAD)), 1e-12)
    k = k * (1.0 + (a - 1.0) * ka_ref[...])
    bonus = _group_sums(r * k * rk_ref[...], RWKV_HEAD) * v
    return r, k, v, -kk, kk * a, w_log, g, bonus


def _rwkv_post(y, bonus, g, gnw_ref, gnb_ref):
    inv_n = 1.0 / RWKV_HEAD
    mean = _group_sums(y, RWKV_HEAD) * inv_n
    yc = y - mean
    var = _group_sums(yc * yc, RWKV_HEAD) * inv_n
    yn = yc * lax.rsqrt(var + RWKV_GN_EPS) * gnw_ref[...] + gnb_ref[...]
    return (yn + bonus) * g


def _rwkv_lanes_gates_kernel(z_ref, sh_ref, mu_ref, lw_ref, w0_ref, la_ref, a0_ref, lg_ref, kk_ref, ka_ref, rk_ref,
                             r_ref, w_ref, k_ref, v_ref, a_ref, b_ref, bonus_ref, g_ref, shout_ref, *, l, nb):
    z = jnp.concatenate([z_ref[t] for t in range(l)], axis=0)
    z_prev = jnp.concatenate([sh_ref[...]] + [z_ref[t] for t in range(l - 1)], axis=0)
    shout_ref[...] = z_ref[l - 1]
    r, k, v, a_vec, b_vec, w_log, g, bonus = _rwkv_gates(
        z, z_prev, mu_ref, lw_ref, w0_ref, la_ref, a0_ref, lg_ref, kk_ref, ka_ref, rk_ref)
    w = jnp.exp(-jnp.exp(w_log))
    for x, ref in ((r, r_ref), (w, w_ref), (k, k_ref), (v, v_ref), (a_vec, a_ref), (b_vec, b_ref),
                   (bonus, bonus_ref), (g, g_ref)):
        for t in range(l):
            ref[t] = x[t * nb:(t + 1) * nb].T


def _rwkv_lanes_step_kernel(r_ref, w_ref, k_ref, v_ref, a_ref, b_ref, bonus_ref, g_ref, gnw_ref, gnb_ref, s0_ref,
                            y_ref, sout_ref, *, l, fill_layer):
    sout = _own_layer(sout_ref, fill_layer)
    vb = 8
    for t in range(l):
        a, w, b, k, r = a_ref[t], w_ref[t], b_ref[t], k_ref[t], r_ref[t]
        src = s0_ref if t == 0 else sout

        def rows(i, carry):
            v0 = pl.multiple_of(i * vb, vb)
            s = src[0, pl.ds(v0, vb)]
            sa = jnp.sum(s * a[None], axis=1)
            vv = v_ref[t, pl.ds(v0, vb), :]
            s = s * w[None] + sa[:, None, :] * b[None] + vv[:, None, :] * k[None]
            sout[0, pl.ds(v0, vb)] = s
            y_ref[t, pl.ds(v0, vb), :] = jnp.sum(s * r[None], axis=1)
            return carry

        lax.fori_loop(0, RWKV_HEAD // vb, rows, 0)
    inv_n = 1.0 / RWKV_HEAD
    for t in range(l):
        y = y_ref[t]
        mean = jnp.sum(y, axis=0, keepdims=True) * inv_n
        yc = y - mean
        var = jnp.sum(yc * yc, axis=0, keepdims=True) * inv_n
        yn = yc * lax.rsqrt(var + RWKV_GN_EPS) * gnw_ref[...] + gnb_ref[...]
        y_ref[t] = (yn + bonus_ref[t]) * g_ref[t]


def _rwkv_lanes(z2, shift_prev, wkv0_t, pw, layer, b, l, prev_stack, out_dtype):
    vec = lambda n: _const_spec((1, n), lambda i: (0, 0))
    mat = lambda r, c: _const_spec((r, c), lambda i: (0, 0))
    zt = z2[:, Z_RWKV:Z_RWKV + RWKV_PAD].reshape(b, l, RWKV_PAD).transpose(1, 0, 2)
    op_shape = jax.ShapeDtypeStruct((l, GROUP_W, b), F32)
    full = lambda shape: _const_spec(shape, lambda i: (0,) * len(shape))
    outs = pl.pallas_call(
        functools.partial(_rwkv_lanes_gates_kernel, l=l, nb=b),
        grid=(1,),
        in_specs=[full((l, b, RWKV_PAD)), full((b, RWKV_PAD)),
                  vec(RWKV_PAD), mat(LORA_W, GROUP_W), vec(GROUP_W), mat(LORA_W, GROUP_W), vec(GROUP_W),
                  mat(LORA_W, GROUP_W), vec(GROUP_W), vec(GROUP_W), vec(GROUP_W)],
        out_specs=[full((l, GROUP_W, b))] * 8 + [full((b, RWKV_PAD))],
        out_shape=[op_shape] * 8 + [jax.ShapeDtypeStruct((b, RWKV_PAD), F32)],
        compiler_params=_cparams(("arbitrary",)),
        name="rwkv7_gates",
    )(zt, shift_prev, pw["mu"], pw["lora_w"], pw["w0"], pw["lora_a"], pw["a0"], pw["lora_g"],
      pw["k_k"], pw["k_a"], pw["r_k"])
    ops, shift_new = outs[:8], outs[8]
    first = prev_stack is None
    tail = (1, RWKV_HEAD, RWKV_HEAD, b)
    op_spec = pl.BlockSpec((l, RWKV_HEAD, b), lambda i: (0, i, 0))
    col_spec = pl.BlockSpec((RWKV_HEAD, 1), lambda i: (i, 0))
    y, s_new = _call_stacked(
        functools.partial(_rwkv_lanes_step_kernel, l=l, fill_layer=layer if first else None),
        list(ops) + [pw["gn_w"].reshape(GROUP_W, 1), pw["gn_b"].reshape(GROUP_W, 1), wkv0_t],
        [op_spec] * 8 + [col_spec, col_spec, _stack_out_spec(tail, layer, False)],
        1, prev_stack,
        grid=(RWKV_HEADS,),
        out_specs=[op_spec, _stack_out_spec(tail, layer, first)],
        out_shape=[op_shape, jax.ShapeDtypeStruct((DEPTH, RWKV_HEADS, RWKV_HEAD, RWKV_HEAD, b), F32)],
        compiler_params=_cparams(("parallel",)),
        name="rwkv7_steps")
    y = y.transpose(2, 0, 1).reshape(b * l, GROUP_W).astype(out_dtype)
    return y, s_new, shift_new[:, :RWKV_COLS]


def _tri_inverse(l_mats, n):
    row = lax.broadcasted_iota(jnp.int32, (n, n), 0)
    col = lax.broadcasted_iota(jnp.int32, (n, n), 1)
    eye = (row == col).astype(F32)
    mm = lambda a, b: jnp.dot(a.astype(BF16), b.astype(BF16), preferred_element_type=F32)
    same = lambda m: (row // m) == (col // m)
    d1 = [jnp.where(same(8), x, 0.0) for x in l_mats]
    d2 = [mm(x, x) for x in d1]
    d4 = [mm(x, x) for x in d2]
    d3 = [mm(x, y) for x, y in zip(d1, d2)]
    t = [eye + x1 + x2 + x3 for x1, x2, x3 in zip(d1, d2, d3)]
    t = [x + mm(x, y) for x, y in zip(t, d4)]
    m = 8
    while m < RWKV_CHUNK:
        e = [jnp.where(same(2 * m) & jnp.logical_not(same(m)), x, 0.0) for x in l_mats]
        te = [mm(x, y) for x, y in zip(t, e)]
        t = [x + mm(y, x) for x, y in zip(t, te)]
        m *= 2
    return t


def _rwkv_chunk_local(chunks):
    c = RWKV_CHUNK
    n = 2 * c
    lo64 = lax.broadcasted_iota(jnp.int32, (c, 128), 1) < RWKV_HEAD
    tri = (lax.broadcasted_iota(jnp.int32, (c, c), 1) <= lax.broadcasted_iota(jnp.int32, (c, c), 0)).astype(BF16)
    stack = lambda x: jnp.concatenate([jnp.where(lo64, x, 0.0), jnp.where(lo64, 0.0, x)], axis=0)
    ops = []
    for r, k, v, a_vec, b_vec, logw in chunks:
        cum = _dot_f32_rhs(tri, logw)
        c_end = cum[c - 1:c]
        e_neg = jnp.exp(-cum)
        e_end = jnp.exp(c_end - cum)
        at = a_vec * jnp.exp(cum - logw)
        rt = r * jnp.exp(cum)
        bt = b_vec * e_neg
        kt = k * e_neg
        bh = b_vec * e_end
        kh = k * e_end
        e_cend = jnp.exp(c_end)
        for p in range(RWKV_PAIRS):
            sl = slice(p * 128, (p + 1) * 128)
            ops.append(dict(xa=stack(at[:, sl]), xr=stack(rt[:, sl]), yb=stack(bt[:, sl]), yk=stack(kt[:, sl]),
                            vs=stack(v[:, sl]).astype(BF16), xbh=stack(bh[:, sl]), xkh=stack(kh[:, sl]),
                            e_cend=e_cend[:, sl]))

    row = lax.broadcasted_iota(jnp.int32, (n, n), 0)
    col = lax.broadcasted_iota(jnp.int32, (n, n), 1)
    same_head = (row // c) == (col // c)
    strict = same_head & (col < row)
    incl = same_head & (col <= row)
    zeros = jnp.zeros((n, n), BF16)
    mm = lambda x, y: jnp.dot(x, y, preferred_element_type=F32)
    nt = lambda x, y: lax.dot_general(x, y, (((1,), (1,)), ((), ())), preferred_element_type=F32)
    tn = lambda x, y: lax.dot_general(x, y, (((0,), (0,)), ((), ())), preferred_element_type=F32)
    unstack = lambda x: x[0:c] + x[c:n]

    gram = [nt(jnp.concatenate([o["xa"], o["xr"]], axis=0).astype(BF16),
               jnp.concatenate([o["yb"], o["yk"]], axis=0).astype(BF16)) for o in ops]
    l_ab = [jnp.where(strict, x[0:n, 0:n], 0.0) for x in gram]
    l_ak = [jnp.where(strict, x[0:n, n:2 * n], 0.0) for x in gram]
    m_rbk = [jnp.concatenate([jnp.where(incl, x[n:2 * n, 0:n], 0.0), jnp.where(incl, x[n:2 * n, n:2 * n], 0.0)],
                             axis=1).astype(BF16) for x in gram]
    w1 = [mm(x.astype(BF16), o["vs"]) for x, o in zip(l_ak, ops)]
    t_inv = _tri_inverse(l_ab, n)
    tu = [mm(t.astype(BF16), jnp.concatenate([o["xa"], w], axis=1).astype(BF16))
          for t, o, w in zip(t_inv, ops, w1)]
    zmat = [jnp.concatenate([x.astype(BF16), jnp.concatenate([zeros, o["vs"]], axis=1)], axis=0)
            for x, o in zip(tu, ops)]
    ry = [mm(m, zm) for m, zm in zip(m_rbk, zmat)]
    ps = [tn(zm, jnp.concatenate([o["xbh"], o["xkh"]], axis=0).astype(BF16))
          for zm, o in zip(zmat, ops)]
    out = [(unstack(o["xr"] + y[:, 0:n]), unstack(y[:, n:2 * n]), o["e_cend"], q[0:n], q[n:2 * n])
           for o, y, q in zip(ops, ry, ps)]
    return [out[ci * RWKV_PAIRS:(ci + 1) * RWKV_PAIRS] for ci in range(len(chunks))]


def _rwkv_chunk_kernel(z_ref, sh_ref, s0_ref, mu_ref, lw_ref, w0_ref, la_ref, a0_ref, lg_ref, kk_ref, ka_ref, rk_ref,
                       gnw_ref, gnb_ref, y_ref, sout_ref, shout_ref, zbuf, s_scr, *, n_tiles, fill_layer):
    c = RWKV_CHUNK
    tile = RWKV_TILE
    zero = jnp.zeros((RWKV_HEAD, RWKV_HEAD), F32)
    for p in range(RWKV_PAIRS):
        s_scr[p] = jnp.concatenate([jnp.concatenate([s0_ref[0, 2 * p], zero], axis=1),
                                    jnp.concatenate([zero, s0_ref[0, 2 * p + 1]], axis=1)], axis=0)
    zbuf[pl.ds(0, 8), :] = jnp.broadcast_to(sh_ref[0], (8, RWKV_PAD))
    nt = lambda x, y: lax.dot_general(x, y, (((1,), (1,)), ((), ())), preferred_element_type=F32)

    def tile_body(ti, carry):
        r0 = pl.multiple_of(ti * tile, tile)
        z = z_ref[0, pl.ds(r0, tile), :]
        zbuf[pl.ds(8, tile), :] = z
        z_prev = zbuf[pl.ds(7, tile), :]
        zbuf[pl.ds(7, 1), :] = zbuf[pl.ds(7 + tile, 1), :]
        r, k, v, a_vec, b_vec, w_log, g, bonus = _rwkv_gates(
            z, z_prev, mu_ref, lw_ref, w0_ref, la_ref, a0_ref, lg_ref, kk_ref, ka_ref, rk_ref)
        logw = -jnp.exp(w_log)
        local = _rwkv_chunk_local([tuple(x[ci * c:(ci + 1) * c] for x in (r, k, v, a_vec, b_vec, logw))
                                   for ci in range(RWKV_TILE_CHUNKS)])
        y_cols = []
        for p in range(RWKV_PAIRS):
            s = s_scr[p]
            ys = []
            for ci in range(RWKV_TILE_CHUNKS):
                rc, yloc, e_cend, trans, sloc = local[ci][p]
                sb = s.astype(BF16)
                ys.append(nt(rc.astype(BF16), sb) + yloc)
                s = s * e_cend + jnp.dot(sb, trans.astype(BF16), preferred_element_type=F32) + sloc
            s_scr[p] = s
            y_cols.append(jnp.concatenate(ys, axis=0))
        y = jnp.concatenate(y_cols, axis=1)
        y_ref[0, pl.ds(r0, tile), :] = _rwkv_post(y, bonus, g, gnw_ref, gnb_ref).astype(y_ref.dtype)
        return carry

    lax.fori_loop(0, n_tiles, tile_body, 0)
    shout_ref[0] = zbuf[pl.ds(7, 1), :]
    sout = _own_layer(sout_ref, fill_layer)
    for p in range(RWKV_PAIRS):
        sout[0, 2 * p] = s_scr[p, 0:RWKV_HEAD, 0:RWKV_HEAD]
        sout[0, 2 * p + 1] = s_scr[p, RWKV_HEAD:2 * RWKV_HEAD, RWKV_HEAD:2 * RWKV_HEAD]


def _rwkv(z2, shift_prev, wkv0, pw, layer, b, l, prev_stack, out_dtype):
    vec = lambda n: _const_spec((1, n), lambda i: (0, 0))
    mat = lambda r, c: _const_spec((r, c), lambda i: (0, 0))
    params = [pw["mu"], pw["lora_w"], pw["w0"], pw["lora_a"], pw["a0"], pw["lora_g"],
              pw["k_k"], pw["k_a"], pw["r_k"], pw["gn_w"], pw["gn_b"]]
    pspecs = [vec(RWKV_PAD), mat(LORA_W, GROUP_W), vec(GROUP_W), mat(LORA_W, GROUP_W), vec(GROUP_W),
              mat(LORA_W, GROUP_W), vec(GROUP_W), vec(GROUP_W), vec(GROUP_W), vec(GROUP_W), vec(GROUP_W)]
    nseq = 1
    z_in = z2.reshape(b, l, Z_COLS)
    shift_prev = shift_prev.reshape(b, 1, RWKV_PAD)
    shspec = pl.BlockSpec((1, 1, RWKV_PAD), lambda i: (i, 0, 0))
    body = functools.partial(_rwkv_chunk_kernel, n_tiles=l // RWKV_TILE)
    zspec = pl.BlockSpec((1, l, RWKV_PAD), lambda i: (i, 0, Z_RWKV // RWKV_PAD))
    yspec = pl.BlockSpec((1, l, GROUP_W), lambda i: (i, 0, 0))
    yshape = jax.ShapeDtypeStruct((b, l, GROUP_W), out_dtype)
    scratch = [pltpu.VMEM((RWKV_TILE + 8, RWKV_PAD), F32), pltpu.VMEM((RWKV_PAIRS, 128, 128), F32)]
    first = prev_stack is None
    tail = (nseq, RWKV_HEADS, RWKV_HEAD, RWKV_HEAD)
    y, s_new, shift_new = _call_stacked(
        functools.partial(body, fill_layer=layer if first else None), [z_in, shift_prev, wkv0] + params,
        [zspec, shspec, _stack_out_spec(tail, layer, False)] + pspecs,
        1, prev_stack,
        grid=(b // nseq,),
        out_specs=[yspec, _stack_out_spec(tail, layer, first), shspec],
        out_shape=[yshape, jax.ShapeDtypeStruct((DEPTH, b, RWKV_HEADS, RWKV_HEAD, RWKV_HEAD), F32),
                   jax.ShapeDtypeStruct(shift_prev.shape, F32)],
        scratch_shapes=scratch,
        compiler_params=_cparams(("parallel",)),
        name="rwkv7")
    return y.reshape(b * l, GROUP_W), s_new, shift_new.reshape(b, RWKV_PAD)[:, :RWKV_COLS]


def _sgu_kernel(z_ref, lnw_ref, lnb_ref, wm_ref, bias_ref, nrm_ref, y_ref, v_ref, *, blk):
    z = z_ref[...]
    zg = 0.5 * z * (1.0 + lax.erf(z * 0.7071067811865476))
    u = zg[:, :GROUP_W]
    v = zg[:, GROUP_W:]
    mu = jnp.mean(v, axis=-1, keepdims=True)
    vc = v - mu
    var = jnp.mean(vc * vc, axis=-1, keepdims=True)
    vn = vc * lax.rsqrt(var + LN_EPS) * lnw_ref[...] + lnb_ref[...]
    v_ref[...] = vn
    row = lax.broadcasted_iota(jnp.int32, (SGU_CHUNK, SGU_CHUNK), 0)
    col = lax.broadcasted_iota(jnp.int32, (SGU_CHUNK, SGU_CHUNK), 1)
    mask = col <= row
    if blk < SGU_CHUNK:
        mask = mask & ((row // blk) == (col // blk))
    vb = vn.astype(BF16)
    wms = [jnp.where(mask, wm_ref[h], 0.0).astype(BF16) for h in range(SGU_HEADS)]
    rows = []
    for c in range(z.shape[0] // SGU_CHUNK):
        rs = slice(c * SGU_CHUNK, (c + 1) * SGU_CHUNK)
        rows.append(jnp.concatenate(
            [jnp.dot(wms[h], vb[rs, h * 128:(h + 1) * 128], preferred_element_type=F32) + bias_ref[:, h:h + 1]
             for h in range(SGU_HEADS)], axis=1))
    s = rows[0] if len(rows) == 1 else jnp.concatenate(rows, axis=0)
    y_ref[...] = _rms(u * s, nrm_ref[...]).astype(y_ref.dtype)


def _sgu(z2, pw, wm, bias, blk, out_dtype):
    t = z2.shape[0]
    tile = SGU_TILE_CHUNKS * SGU_CHUNK
    vec = _const_spec((1, GROUP_W), lambda i: (0, 0))
    return pl.pallas_call(
        functools.partial(_sgu_kernel, blk=blk),
        grid=(t // tile,),
        in_specs=[pl.BlockSpec((tile, SGU_COLS), lambda i: (i, Z_SGU // SGU_COLS)),
                  vec, vec,
                  _const_spec((SGU_HEADS, SGU_CHUNK, SGU_CHUNK), lambda i: (0, 0, 0)),
                  _const_spec((SGU_CHUNK, SGU_HEADS), lambda i: (0, 0)),
                  vec],
        out_specs=[pl.BlockSpec((tile, GROUP_W), lambda i: (i, 0)),
                   pl.BlockSpec((tile, GROUP_W), lambda i: (i, 0))],
        out_shape=[jax.ShapeDtypeStruct((t, GROUP_W), out_dtype), jax.ShapeDtypeStruct((t, GROUP_W), F32)],
        compiler_params=_cparams(("parallel",)),
        name="sgu",
    )(z2, pw["ln_w"], pw["ln_b"], wm, bias, pw["norm"])


def _hgrn_chunks(seqs, lb, nw, chunk, sub):
    nh = HGRN_HEADS
    hs = lambda x, h: x[:, h * HGRN_HEAD:(h + 1) * HGRN_HEAD]
    tri = (lax.broadcasted_iota(jnp.int32, (chunk, chunk), 1)
           <= lax.broadcasted_iota(jnp.int32, (chunk, chunk), 0)).astype(BF16)
    eye = (lax.broadcasted_iota(jnp.int32, (HGRN_HEAD, HGRN_HEAD), 0)
           == lax.broadcasted_iota(jnp.int32, (HGRN_HEAD, HGRN_HEAD), 1)).astype(F32)
    rowid = lax.broadcasted_iota(jnp.int32, (sub, 1), 0)
    d = functools.partial(jnp.dot, preferred_element_type=F32)
    nt = lambda x, y: lax.dot_general(x, y, (((1,), (1,)), ((), ())), preferred_element_type=F32)
    tn = lambda x, y: lax.dot_general(x, y, (((0,), (0,)), ((), ())), preferred_element_type=F32)

    pre = []
    for zq, zf, zi, zg, states in seqs:
        q = _silu(zq)
        fg = lb + (1.0 - lb) * _sigmoid(zf)
        k = 1.0 - fg
        b = _dot_f32_rhs(tri, jnp.log(fg))
        pre.append(dict(q=q, k=k, v=zi, vb=zi.astype(BF16), b=b, qe=(q * jnp.exp(b)).astype(BF16)))
    o = [[d(hs(u["qe"], h), st[h].astype(BF16)) for h in range(nh)] for u, (_, _, _, _, st) in zip(pre, seqs)]
    outs = [[[] for _ in range(nh)] for _ in seqs]
    for i in range(chunk // sub):
        lo_, hi_ = i * sub, (i + 1) * sub
        oi = [[o[n][h][lo_:hi_] for h in range(nh)] for n in range(len(seqs))]
        if i > 0:
            for n, u in enumerate(pre):
                bref = u["b"][lo_ - 1:lo_]
                qt = (u["q"][lo_:hi_] * jnp.exp(u["b"][lo_:hi_] - bref)).astype(BF16)
                kt = (u["k"][:lo_] * jnp.exp(bref - u["b"][:lo_])).astype(BF16)
                att = [nt(hs(qt, h), hs(kt, h)).astype(BF16) for h in range(nh)]
                for h in range(nh):
                    oi[n][h] = oi[n][h] + d(att[h], hs(u["vb"], h)[:lo_])
        tiles = []
        for u in pre:
            qi, bi = u["q"][lo_:hi_], u["b"][lo_:hi_]
            for j in range(sub):
                sj = lo_ + j
                pj = qi * jnp.exp(jnp.minimum(bi - u["b"][sj:sj + 1], 0.0)) * u["k"][sj:sj + 1]
                tiles.extend(hs(pj, h) for h in range(nh))
        att = _group_sums(jnp.concatenate(tiles, axis=0), HGRN_HEAD, terms=1)
        t = 0
        for n, u in enumerate(pre):
            for j in range(sub):
                sj = lo_ + j
                for h in range(nh):
                    a_j = jnp.where(rowid >= j, att[t * sub:(t + 1) * sub], 0.0)
                    oi[n][h] = oi[n][h] + a_j * hs(u["v"], h)[sj:sj + 1]
                    t += 1
        for n in range(len(seqs)):
            for h in range(nh):
                outs[n][h].append(oi[n][h])
    results = []
    for n, (u, (_, _, _, zg, st)) in enumerate(zip(pre, seqs)):
        o_n = jnp.concatenate([outs[n][h][0] if len(outs[n][h]) == 1 else jnp.concatenate(outs[n][h], axis=0)
                               for h in range(nh)], axis=1)
        blast = u["b"][chunk - 1:chunk]
        kd = (u["k"] * jnp.exp(blast - u["b"])).astype(BF16)
        e_last = jnp.exp(blast)
        new_states = []
        for h in range(nh):
            e_col = _group_sums(eye * hs(e_last, h), HGRN_HEAD)
            new_states.append(e_col * st[h] + tn(hs(kd, h), hs(u["vb"], h)))
        ms = _group_sums(o_n * o_n, HGRN_HEAD) * (1.0 / HGRN_HEAD)
        results.append((o_n * lax.rsqrt(ms + NORM_EPS) * nw * _silu(zg), new_states))
    return results


def _hgrn_kernel(zq_ref, zf_ref, zi_ref, zg_ref, lg_ref, nw_ref, s0_ref, y_ref, sout_ref, s_scr,
                 *, layer, nseq, chunk, sub, n_chunks, fill_layer):
    lg = lg_ref[...]
    e = jnp.exp(lg - jnp.max(lg, axis=0, keepdims=True))
    sm = e / jnp.sum(e, axis=0, keepdims=True)
    lb = jnp.sum(sm[0:layer + 1], axis=0, keepdims=True) - sm[0:1]
    nw = nw_ref[...]
    nh = HGRN_HEADS
    sout = _own_layer(sout_ref, fill_layer)
    if n_chunks == 1:
        rows = lambda ref, b: ref[b * chunk:(b + 1) * chunk, :]
        seqs = [(rows(zq_ref, b), rows(zf_ref, b), rows(zi_ref, b), rows(zg_ref, b),
                 [s0_ref[b, h] for h in range(nh)]) for b in range(nseq)]
        ys = []
        for b, (y, states) in enumerate(_hgrn_chunks(seqs, lb, nw, chunk, sub)):
            ys.append(y)
            for h in range(nh):
                sout[b, h] = states[h]
        y_ref[...] = jnp.concatenate(ys, axis=0).astype(y_ref.dtype)
    else:
        s_scr[...] = s0_ref[...]

        def chunk_body(c, carry):
            r0 = pl.multiple_of(c * chunk, chunk)
            rows = lambda ref, b: ref[b, pl.ds(r0, chunk), :]
            seqs = [(rows(zq_ref, b), rows(zf_ref, b), rows(zi_ref, b), rows(zg_ref, b),
                     [s_scr[b, h] for h in range(nh)]) for b in range(nseq)]
            for b, (y, states) in enumerate(_hgrn_chunks(seqs, lb, nw, chunk, sub)):
                y_ref[b, pl.ds(r0, chunk), :] = y.astype(y_ref.dtype)
                for h in range(nh):
                    s_scr[b, h] = states[h]
            return carry

        lax.fori_loop(0, n_chunks, chunk_body, 0)
        sout[...] = s_scr[...]


def _hgrn(z2, s0, lb_logits, norm_w, layer, b, l, prev_stack, out_dtype):
    chunk = min(HGRN_CHUNK, l)
    sub = min(HGRN_SUB, chunk)
    cb = Z_HGRN // GROUP_W
    if l > chunk:
        nseq = 1
        z_in = z2.reshape(b, l, Z_COLS)
        zspec = lambda part: pl.BlockSpec((1, l, GROUP_W), lambda i: (i, 0, cb + part))
        yspec = pl.BlockSpec((1, l, GROUP_W), lambda i: (i, 0, 0))
        yshape = jax.ShapeDtypeStruct((b, l, GROUP_W), out_dtype)
    else:
        nseq = SHORT_SEQS
        z_in = z2
        zspec = lambda part: pl.BlockSpec((nseq * l, GROUP_W), lambda i: (i, cb + part))
        yspec = pl.BlockSpec((nseq * l, GROUP_W), lambda i: (i, 0))
        yshape = jax.ShapeDtypeStruct((b * l, GROUP_W), out_dtype)
    first = prev_stack is None
    tail = (nseq, HGRN_HEADS, HGRN_HEAD, HGRN_HEAD)
    y, s_new = _call_stacked(
        functools.partial(_hgrn_kernel, layer=layer, nseq=nseq, chunk=chunk, sub=sub, n_chunks=l // chunk,
                          fill_layer=layer if first else None),
        [z_in, z_in, z_in, z_in, lb_logits, norm_w, s0],
        [zspec(0), zspec(1), zspec(2), zspec(3),
         _const_spec((DEPTH, GROUP_W), lambda i: (0, 0)),
         _const_spec((1, GROUP_W), lambda i: (0, 0)),
         _stack_out_spec(tail, layer, False)],
        1, prev_stack,
        grid=(b // nseq,),
        out_specs=[yspec, _stack_out_spec(tail, layer, first)],
        out_shape=[yshape, jax.ShapeDtypeStruct((DEPTH, b, HGRN_HEADS, HGRN_HEAD, HGRN_HEAD), F32)],
        scratch_shapes=[pltpu.VMEM((nseq, HGRN_HEADS, HGRN_HEAD, HGRN_HEAD), F32)],
        compiler_params=_cparams(("parallel",)),
        name="hgrn2")
    return y.reshape(b * l, GROUP_W), s_new


def _pool_kernel(z_ref, hist_ref, pw_ref, ps_ref, y_ref, hout_ref, xbuf, s2buf, s4buf, s8buf,
                 *, nseq, tc, n_tiles, start_pos, flat):
    halo = POOL_HALO
    n = tc + halo
    c1 = POOL_CH
    for b in range(nseq):
        xbuf[b, pl.ds(0, 16), :] = jnp.zeros((16, GROUP_W), F32)
        xbuf[b, pl.ds(16, 16), :] = hist_ref[b]

    def tile(r0, b, z):
        xbuf[b, pl.ds(halo, tc), :] = z
        s2buf[b, pl.ds(8, n - 8), :] = xbuf[b, pl.ds(8, n - 8), :] + xbuf[b, pl.ds(7, n - 8), :]
        s4buf[b, pl.ds(16, n - 16), :] = s2buf[b, pl.ds(16, n - 16), c1:] + s2buf[b, pl.ds(14, n - 16), c1:]
        s8buf[b, pl.ds(24, n - 24), :] = s4buf[b, pl.ds(24, n - 24), c1:] + s4buf[b, pl.ds(20, n - 24), c1:]
        s16 = s8buf[b, pl.ds(halo, tc), c1:] + s8buf[b, pl.ds(halo - 8, tc), c1:]
        sums = (s2buf[b, pl.ds(halo, tc), 0:c1], s4buf[b, pl.ds(halo, tc), 0:c1], s8buf[b, pl.ds(halo, tc), 0:c1],
                s16)
        pos = start_pos + r0 + lax.broadcasted_iota(jnp.int32, (tc, 1), 0)
        ys = []
        for gi, win in enumerate(POOL_WINDOWS):
            sl = slice(gi * POOL_CH, (gi + 1) * POOL_CH)
            cnt = jnp.minimum(pos + 1, win).astype(F32)
            dlt = sums[gi] / cnt - z[:, sl]
            ys.append(jnp.dot(dlt.astype(BF16), pw_ref[gi], preferred_element_type=F32) * ps_ref[:, sl])
        if n_tiles > 1:
            xbuf[b, pl.ds(16, 16), :] = xbuf[b, pl.ds(n - 16, 16), :]
        return jnp.concatenate(ys, axis=1)

    if flat:
        y_ref[...] = jnp.concatenate([tile(0, b, z_ref[b * tc:(b + 1) * tc, :]) for b in range(nseq)],
                                     axis=0).astype(y_ref.dtype)
    else:
        def tile_body(ti, carry):
            r0 = pl.multiple_of(ti * tc, tc)
            for b in range(nseq):
                y_ref[b, pl.ds(r0, tc), :] = tile(r0, b, z_ref[b, pl.ds(r0, tc), :]).astype(y_ref.dtype)
            return carry

        lax.fori_loop(0, n_tiles, tile_body, 0)
    for b in range(nseq):
        hout_ref[b] = xbuf[b, pl.ds(n - POOL_HIST, POOL_HIST), :]


def _pool(z2, hist16, pool_w, pool_scale, start_pos, b, l, tc, out_dtype):
    cb = Z_POOL // GROUP_W
    if l >= POOL_HALO:
        nseq = 1
        z_in = z2.reshape(b, l, Z_COLS)
        zspec = pl.BlockSpec((1, l, GROUP_W), lambda i: (i, 0, cb))
        yspec = pl.BlockSpec((1, l, GROUP_W), lambda i: (i, 0, 0))
        yshape = jax.ShapeDtypeStruct((b, l, GROUP_W), out_dtype)
    else:
        nseq = SHORT_SEQS
        z_in = z2
        zspec = pl.BlockSpec((nseq * l, GROUP_W), lambda i: (i, cb))
        yspec = pl.BlockSpec((nseq * l, GROUP_W), lambda i: (i, 0))
        yshape = jax.ShapeDtypeStruct((b * l, GROUP_W), out_dtype)
    n = tc + POOL_HALO
    y, hist_new = pl.pallas_call(
        functools.partial(_pool_kernel, nseq=nseq, tc=tc, n_tiles=l // tc, start_pos=start_pos, flat=nseq > 1),
        grid=(b // nseq,),
        in_specs=[zspec,
                  pl.BlockSpec((nseq, 16, GROUP_W), lambda i: (i, 0, 0)),
                  _const_spec((len(POOL_WINDOWS), POOL_CH, POOL_CH), lambda i: (0, 0, 0)),
                  _const_spec((1, GROUP_W), lambda i: (0, 0))],
        out_specs=[yspec, pl.BlockSpec((nseq, POOL_HIST, GROUP_W), lambda i: (i, 0, 0))],
        out_shape=[yshape, jax.ShapeDtypeStruct((b, POOL_HIST, GROUP_W), F32)],
        scratch_shapes=[pltpu.VMEM((nseq, n, GROUP_W), F32), pltpu.VMEM((nseq, n, GROUP_W), F32),
                        pltpu.VMEM((nseq, n, 3 * POOL_CH), F32), pltpu.VMEM((nseq, n, 2 * POOL_CH), F32)],
        compiler_params=_cparams(("parallel",)),
        name="pool",
    )(z_in, hist16, pool_w, pool_scale)
    return y.reshape(b * l, GROUP_W), hist_new


def _pad_cols(a, n):
    return jnp.pad(a, ((0, 0),) * (a.ndim - 1) + ((0, n - a.shape[-1]),))


W_IN_BLK = 256
W_IN_SRC = 32


def _w_in_kernel(*refs):
    *srcs, o_ref = refs
    x = jnp.concatenate([r[...] for r in srcs], axis=0)
    col = pl.program_id(1) * W_IN_BLK + lax.broadcasted_iota(jnp.int32, (W_IN_BLK, 1), 0)
    x = jnp.where(col < Z_RWKV + RWKV_COLS, x, 0.0)
    o_ref[...] = x.T.astype(BF16)


def _w_in_layout(w_in):
    in_cols = w_in.shape[2]
    c1, c2, c3 = RWKV_COLS, RWKV_COLS + SGU_COLS, RWKV_COLS + SGU_COLS + HGRN_COLS
    per_blk = W_IN_BLK // W_IN_SRC
    last = in_cols // W_IN_SRC - 1

    def src_slab(j, q):
        start = jnp.where(j < Z_SGU // W_IN_BLK, c2 // W_IN_SRC + per_blk * (j - Z_HGRN // W_IN_BLK),
                          jnp.where(j < Z_POOL // W_IN_BLK, c1 // W_IN_SRC + per_blk * (j - Z_SGU // W_IN_BLK),
                                    jnp.where(j < Z_RWKV // W_IN_BLK, c3 // W_IN_SRC + per_blk * (j - Z_POOL // W_IN_BLK),
                                              per_blk * (j - Z_RWKV // W_IN_BLK))))
        return jnp.minimum(start + q, last)

    w_t = jnp.swapaxes(w_in, 1, 2)
    return pl.pallas_call(
        _w_in_kernel,
        grid=(DEPTH, Z_COLS // W_IN_BLK),
        in_specs=[pl.BlockSpec((None, W_IN_SRC, D_MODEL), lambda l, j, q=q: (l, src_slab(j, q), 0))
                  for q in range(per_blk)],
        out_specs=pl.BlockSpec((None, D_MODEL, W_IN_BLK), lambda l, j: (l, 0, j)),
        out_shape=jax.ShapeDtypeStruct((DEPTH, D_MODEL, Z_COLS), BF16),
        compiler_params=_cparams(("parallel", "parallel")),
        name="w_in_layout",
    )(*([w_t] * per_blk))


def _prep_weights(W):
    return dict(w_in=_w_in_layout(W["w_in"]), w_out=W["w_out"].astype(BF16), w_gu=W["ffn_w_gu"].astype(BF16),
                w_down=W["ffn_w_down"].astype(BF16), ple_gate=W["ple_gate"].astype(BF16),
                ple_proj=W["ple_proj"].astype(BF16))


def _prep_layer(i, W):
    row = lambda a: a.reshape(1, -1)
    lora = lambda a, off: jnp.pad(a, ((off, LORA_W - off - a.shape[0]), (0, 0))).astype(BF16)
    rw = dict(
        mu=_pad_cols(row(W["rwkv_mu"][i]), RWKV_PAD),
        lora_w=lora(W["rwkv_w_lora"][i], 0), w0=row(W["rwkv_w0"][i]),
        lora_a=lora(W["rwkv_a_lora"][i], W_LORA), a0=row(W["rwkv_a0"][i]),
        lora_g=lora(W["rwkv_g_lora"][i], W_LORA + A_LORA),
        k_k=row(W["rwkv_k_k"][i]), k_a=row(W["rwkv_k_a"][i]), r_k=row(W["rwkv_r_k"][i]),
        gn_w=row(W["rwkv_gn_w"][i]), gn_b=row(W["rwkv_gn_b"][i]))
    sg = dict(ln_w=row(W["sgu_ln_w"][i]), ln_b=row(W["sgu_ln_b"][i]), norm=row(W["sgu_norm"][i]))
    return dict(
        rw=rw, sg=sg, sgu_w=W["sgu_w"][i], sgu_b=W["sgu_b"][i],
        hgrn_norm=row(W["hgrn_norm"][i]), pool_w=W["pool_w"][i].astype(BF16), pool_scale=row(W["pool_scale"][i]),
        ln_mix_pre=row(W["ln_mix_pre"][i]), ln_mix_post=row(W["ln_mix_post"][i]),
        ln_ffn_pre=row(W["ln_ffn_pre"][i]), ln_ffn_post=row(W["ln_ffn_post"][i]))


def _trunk(x, p, wkv0, shift0, hgrn0, pool0, start_pos, big, layers, lb_logits, cfg):
    b, l, _ = x.shape
    t = b * l
    xf = x.reshape(t, D_MODEL)
    p2 = p.reshape(DEPTH, t, PLE_DIM)
    wkv_new = hgrn_new = None
    shift_l, pool_l, sgu_l = [], [], []
    long_seq = l % RWKV_TILE == 0
    rwkv = _rwkv if long_seq else _rwkv_lanes
    if not long_seq:
        wkv0 = wkv0.transpose(0, 2, 3, 4, 1)
    for i, lw in enumerate(layers):
        z2 = _inproj(xf, lw["ln_mix_pre"], big["w_in"], i, min(cfg["tm_in"], t), cfg["tn_in"])
        y_r, wkv_new, shift_new = rwkv(z2, _pad_cols(shift0[i], RWKV_PAD), wkv0, lw["rw"], i, b, l, wkv_new,
                                       cfg["ydt"])
        if cfg["sgu_blk"] == SGU_CHUNK:
            wm, bias = lw["sgu_w"], lw["sgu_b"].T
        else:
            rep = SGU_CHUNK // l
            wm = jnp.tile(lw["sgu_w"][:, :l, :l], (1, rep, rep))
            bias = jnp.tile(lw["sgu_b"][:, :l].T, (rep, 1))
        y_s, v_rows = _sgu(z2, lw["sg"], wm, bias, cfg["sgu_blk"], cfg["ydt"])
        y_h, hgrn_new = _hgrn(z2, hgrn0, lb_logits, lw["hgrn_norm"], i, b, l, hgrn_new, cfg["ydt"])
        hist16 = jnp.pad(pool0[i], ((0, 0), (1, 0), (0, 0)))
        y_p, pool_new = _pool(z2, hist16, lw["pool_w"], lw["pool_scale"], start_pos, b, l, cfg["tc_pool"],
                              cfg["ydt"])
        xf = _outproj([y_r, y_s, y_h, y_p], xf, big["w_out"], lw["ln_mix_post"], i, cfg["tm"])
        xf = _ffn(xf, lw["ln_ffn_pre"], big["w_gu"], big["w_down"], lw["ln_ffn_post"], p2, big["ple_gate"],
                  big["ple_proj"], i, cfg["tm"])
        shift_l.append(shift_new)
        pool_l.append(pool_new)
        sgu_l.append(v_rows.reshape(b, l, GROUP_W))
    if not long_seq:
        wkv_new = wkv_new.transpose(0, 4, 1, 2, 3)
    return (xf.reshape(b, l, D_MODEL), wkv_new, jnp.stack(shift_l), hgrn_new, jnp.stack(pool_l), jnp.stack(sgu_l))


def _cfg(l):
    if l >= 256:
        return dict(tm=512, tm_in=1024, tn_in=1792, sgu_blk=SGU_CHUNK, tc_pool=256, ydt=BF16)
    return dict(tm=512, tm_in=512, tn_in=1792, sgu_blk=l, tc_pool=l, ydt=BF16)


def kernel(x_prompt, x_sample, state_rwkv_wkv, state_rwkv_shift, state_hgrn, state_pool, p_prompt, p_sample,
           ln_mix_pre, ln_mix_post, ln_ffn_pre, ln_ffn_post, w_in, rwkv_mu, rwkv_w_lora, rwkv_w0, rwkv_a_lora,
           rwkv_a0, rwkv_g_lora, rwkv_k_k, rwkv_k_a, rwkv_r_k, rwkv_gn_w, rwkv_gn_b, sgu_ln_w, sgu_ln_b, sgu_w,
           sgu_b, sgu_norm, hgrn_lb_logits, hgrn_norm, pool_w, pool_scale, w_out, ffn_w_gu, ffn_w_down, ple_gate,
           ple_proj):
    W = dict(ln_mix_pre=ln_mix_pre, ln_mix_post=ln_mix_post, ln_ffn_pre=ln_ffn_pre, ln_ffn_post=ln_ffn_post,
             w_in=w_in, rwkv_mu=rwkv_mu, rwkv_w_lora=rwkv_w_lora, rwkv_w0=rwkv_w0, rwkv_a_lora=rwkv_a_lora,
             rwkv_a0=rwkv_a0, rwkv_g_lora=rwkv_g_lora, rwkv_k_k=rwkv_k_k, rwkv_k_a=rwkv_k_a,
             rwkv_r_k=rwkv_r_k.reshape(DEPTH, GROUP_W), rwkv_gn_w=rwkv_gn_w, rwkv_gn_b=rwkv_gn_b, sgu_ln_w=sgu_ln_w,
             sgu_ln_b=sgu_ln_b, sgu_w=sgu_w, sgu_b=sgu_b, sgu_norm=sgu_norm, hgrn_norm=hgrn_norm, pool_w=pool_w,
             pool_scale=pool_scale, w_out=w_out, ffn_w_gu=ffn_w_gu, ffn_w_down=ffn_w_down, ple_gate=ple_gate,
             ple_proj=ple_proj)
    big = _prep_weights(W)
    layers = [_prep_layer(i, W) for i in range(DEPTH)]
    lb_logits = hgrn_lb_logits.astype(F32)
    bp, lp, _ = x_prompt.shape
    zeros = lambda *s: jnp.zeros((DEPTH, bp) + s, F32)
    y_prompt, wkv_p, shift_p, hgrn_p, pool_p, _ = _trunk(
        x_prompt, p_prompt, zeros(RWKV_HEADS, RWKV_HEAD, RWKV_HEAD), zeros(RWKV_COLS),
        zeros(HGRN_HEADS, HGRN_HEAD, HGRN_HEAD), zeros(POOL_HIST, GROUP_W), 0, big, layers, lb_logits, _cfg(lp))
    y_sample, wkv_s, shift_s, hgrn_s, pool_s, sgu_v_s = _trunk(
        x_sample, p_sample, state_rwkv_wkv, state_rwkv_shift, state_hgrn, state_pool, PAST_LEN, big, layers,
        lb_logits, _cfg(x_sample.shape[1]))
    return (y_prompt, y_sample, wkv_p, shift_p, hgrn_p, pool_p, wkv_s, shift_s, hgrn_s, pool_s, sgu_v_s)
```

```python
import functools

import jax
import jax.numpy as jnp
from jax import lax
from jax.experimental import pallas as pl
from jax.experimental.pallas import tpu as pltpu

F32 = jnp.float32
BF16 = jnp.bfloat16

D_MODEL = 2048
DEPTH = 2
PAST_LEN = 16384
GROUP_W = 512
RWKV_HEAD = 64
RWKV_HEADS = 8
RWKV_PAIRS = RWKV_HEADS // 2
RWKV_CHUNK = 64
RWKV_TILE_CHUNKS = 4
RWKV_TILE = RWKV_TILE_CHUNKS * RWKV_CHUNK
W_LORA, A_LORA, G_LORA = 32, 32, 96
RWKV_GN_EPS = 64e-5
SGU_CHUNK = 128
SGU_HEADS = 4
SGU_TILE_CHUNKS = 4
HGRN_HEADS = 4
HGRN_HEAD = 128
HGRN_CHUNK = 64
HGRN_SUB = 16
POOL_WINDOWS = (2, 4, 8, 16)
POOL_CH = 128
POOL_HIST = 15
POOL_HALO = 32
D_FF = 5632
PLE_DIM = 256
PLE_COLS = 512
FFN_TF = 512
NORM_EPS = 1e-6
LN_EPS = 1e-5
RWKV_COLS = 3 * GROUP_W + W_LORA + A_LORA + G_LORA
RWKV_PAD = 1792
LORA_OFF = 3 * GROUP_W
LORA_W = RWKV_PAD - LORA_OFF
SGU_COLS = 2 * GROUP_W
HGRN_COLS = 4 * GROUP_W
Z_HGRN, Z_SGU, Z_POOL, Z_RWKV = 0, 2048, 3072, 3584
Z_COLS = Z_RWKV + RWKV_PAD
SHORT_SEQS = 8
VMEM_LIMIT = 56 * 1024 * 1024


def _cparams(sem):
    return pltpu.CompilerParams(dimension_semantics=sem, vmem_limit_bytes=VMEM_LIMIT)


def _const_spec(shape, index_map):
    return pl.BlockSpec(shape, index_map, pipeline_mode=pl.Buffered(1))


def _sigmoid(x):
    return jax.nn.sigmoid(x)


def _silu(x):
    return x * jax.nn.sigmoid(x)


def _split3(x):
    hi = x.astype(BF16)
    r = x - hi.astype(F32)
    mid = r.astype(BF16)
    lo = (r - mid.astype(F32)).astype(BF16)
    return hi, mid, lo


def _group_sums(x, width, terms=3):
    rows, cols = x.shape
    nblk = cols // 128
    xs = x if nblk == 1 else jnp.concatenate([x[:, i * 128:(i + 1) * 128] for i in range(nblk)], axis=0)
    ones = ((lax.broadcasted_iota(jnp.int32, (128, 128), 0) // width)
            == (lax.broadcasted_iota(jnp.int32, (128, 128), 1) // width)).astype(BF16)
    s = None
    for part in _split3(xs)[:terms]:
        y = jnp.dot(part, ones, preferred_element_type=F32)
        s = y if s is None else s + y
    return s if nblk == 1 else jnp.concatenate([s[i * rows:(i + 1) * rows] for i in range(nblk)], axis=1)


def _dot_f32_rhs(m, x):
    hi, mid, lo = _split3(x)
    d = functools.partial(jnp.dot, preferred_element_type=F32)
    return d(m, hi) + d(m, mid) + d(m, lo)


def _rms(x, g):
    return x * lax.rsqrt(jnp.mean(x * x, axis=-1, keepdims=True) + NORM_EPS) * g


def _call_stacked(kernel, inputs, in_specs, stack_out, prev_stack, **kw):
    if prev_stack is None:
        return pl.pallas_call(kernel, in_specs=in_specs, **kw)(*inputs)
    n_in = len(inputs)

    def body(*refs):
        kernel(*refs[:n_in], *refs[n_in + 1:])

    return pl.pallas_call(body, in_specs=list(in_specs) + [pl.BlockSpec(memory_space=pl.ANY)],
                          input_output_aliases={n_in: stack_out}, **kw)(*inputs, prev_stack)


def _stack_out_spec(tail, layer, first):
    zeros = (0,) * (len(tail) - 1)
    if first:
        return pl.BlockSpec((DEPTH,) + tail, lambda i: (0, i) + zeros)
    return pl.BlockSpec((None,) + tail, lambda i: (layer, i) + zeros)


def _own_layer(sout_ref, fill_layer):
    if fill_layer is None:
        return sout_ref
    for j in range(DEPTH):
        if j != fill_layer:
            sout_ref[j] = jnp.zeros(sout_ref.shape[1:], sout_ref.dtype)
    return sout_ref.at[fill_layer]


def _inproj_kernel(x_ref, g_ref, w_ref, z_ref, h_ref):
    @pl.when(pl.program_id(1) == 0)
    def _():
        h_ref[...] = _rms(x_ref[...], g_ref[...]).astype(BF16)

    z_ref[...] = jnp.dot(h_ref[...], w_ref[...], preferred_element_type=F32)


def _inproj(x, g, w, layer, tm, tn):
    t = x.shape[0]
    return pl.pallas_call(
        _inproj_kernel,
        grid=(t // tm, Z_COLS // tn),
        in_specs=[pl.BlockSpec((tm, D_MODEL), lambda i, j: (i, 0)),
                  _const_spec((1, D_MODEL), lambda i, j: (0, 0)),
                  pl.BlockSpec((None, D_MODEL, tn), lambda i, j: (layer, 0, j))],
        out_specs=pl.BlockSpec((tm, tn), lambda i, j: (i, j)),
        out_shape=jax.ShapeDtypeStruct((t, Z_COLS), F32),
        scratch_shapes=[pltpu.VMEM((tm, D_MODEL), BF16)],
        compiler_params=_cparams(("parallel", "arbitrary")),
        name="inproj",
    )(x, g, w)


def _outproj_kernel(yr_ref, ys_ref, yh_ref, yp_ref, x_ref, w_ref, g_ref, o_ref):
    d = functools.partial(jnp.dot, preferred_element_type=F32)
    g4 = GROUP_W
    mix = (d(yr_ref[...], w_ref[0:g4, :]) + d(ys_ref[...], w_ref[g4:2 * g4, :])
           + d(yh_ref[...], w_ref[2 * g4:3 * g4, :]) + d(yp_ref[...], w_ref[3 * g4:4 * g4, :]))
    o_ref[...] = x_ref[...] + _rms(mix, g_ref[...])


def _outproj(ys, x, w, g, layer, tm):
    t = x.shape[0]
    yspec = pl.BlockSpec((tm, GROUP_W), lambda i: (i, 0))
    return pl.pallas_call(
        _outproj_kernel,
        grid=(t // tm,),
        in_specs=[yspec, yspec, yspec, yspec,
                  pl.BlockSpec((tm, D_MODEL), lambda i: (i, 0)),
                  _const_spec((None, D_MODEL, D_MODEL), lambda i: (layer, 0, 0)),
                  _const_spec((1, D_MODEL), lambda i: (0, 0))],
        out_specs=pl.BlockSpec((tm, D_MODEL), lambda i: (i, 0)),
        out_shape=jax.ShapeDtypeStruct((t, D_MODEL), F32),
        compiler_params=_cparams(("parallel",)),
        name="outproj",
    )(*ys, x, w, g)


def _ffn_kernel(x_ref, gpre_ref, wg_ref, wu_ref, wd_ref, gpost_ref, p_ref, wgate_ref, wproj_ref, o_ref, h_ref, acc_ref):
    j = pl.program_id(1)

    @pl.when(j == 0)
    def _():
        h_ref[...] = _rms(x_ref[...], gpre_ref[...]).astype(BF16)
        acc_ref[...] = jnp.zeros_like(acc_ref)

    h = h_ref[...]
    gate = jnp.dot(h, wg_ref[...], preferred_element_type=F32)
    up = jnp.dot(h, wu_ref[...], preferred_element_type=F32)
    act = (_silu(gate) * up).astype(BF16)
    acc_ref[...] += jnp.dot(act, wd_ref[...], preferred_element_type=F32)

    @pl.when(j == pl.num_programs(1) - 1)
    def _():
        acc_ref[...] = x_ref[...] + _rms(acc_ref[...], gpost_ref[...])
        h_ref[...] = acc_ref[...].astype(BF16)
        pb = p_ref[...].astype(BF16)
        for c in range(D_MODEL // PLE_COLS):
            cs = slice(c * PLE_COLS, (c + 1) * PLE_COLS)
            pgate = _sigmoid(jnp.dot(h_ref[...], wgate_ref[:, cs], preferred_element_type=F32))
            ple = jnp.dot(pb, wproj_ref[:, cs], preferred_element_type=F32)
            o_ref[:, cs] = acc_ref[:, cs] + pgate * ple


def _ffn(x, gpre, w_gu, w_down, gpost, p, w_gate, w_proj, layer, tm):
    t = x.shape[0]
    tf = FFN_TF
    nf = D_FF // tf
    return pl.pallas_call(
        _ffn_kernel,
        grid=(t // tm, nf),
        in_specs=[pl.BlockSpec((tm, D_MODEL), lambda i, j: (i, 0)),
                  _const_spec((1, D_MODEL), lambda i, j: (0, 0)),
                  pl.BlockSpec((None, D_MODEL, tf), lambda i, j: (layer, 0, j)),
                  pl.BlockSpec((None, D_MODEL, tf), lambda i, j: (layer, 0, j + nf)),
                  pl.BlockSpec((None, tf, D_MODEL), lambda i, j: (layer, j, 0)),
                  _const_spec((1, D_MODEL), lambda i, j: (0, 0)),
                  pl.BlockSpec((None, tm, PLE_DIM), lambda i, j: (layer, i, 0)),
                  _const_spec((None, D_MODEL, D_MODEL), lambda i, j: (layer, 0, 0)),
                  _const_spec((None, PLE_DIM, D_MODEL), lambda i, j: (layer, 0, 0))],
        out_specs=pl.BlockSpec((tm, D_MODEL), lambda i, j: (i, 0)),
        out_shape=jax.ShapeDtypeStruct((t, D_MODEL), F32),
        scratch_shapes=[pltpu.VMEM((tm, D_MODEL), BF16), pltpu.VMEM((tm, D_MODEL), F32)],
        compiler_params=_cparams(("parallel", "arbitrary")),
        name="ffn",
    )(x, gpre, w_gu, w_gu, w_down, gpost, p, w_gate, w_proj)


def _rwkv_gates(z, z_prev, mu_ref, lw_ref, w0_ref, la_ref, a0_ref, lg_ref, kk_ref, ka_ref, rk_ref):
    zm = z + mu_ref[...] * (z_prev - z)
    g4 = GROUP_W
    r = zm[:, 0:g4]
    k = zm[:, g4:2 * g4]
    v = zm[:, 2 * g4:3 * g4]
    zl = zm[:, LORA_OFF:RWKV_PAD]
    d = functools.partial(jnp.dot, preferred_element_type=F32)
    wl = w0_ref[...] + d(jnp.tanh(zl).astype(BF16), lw_ref[...])
    w_log = -(jnp.maximum(-wl, 0.0) + jnp.log1p(jnp.exp(-jnp.abs(wl)))) - 0.5
    a = _sigmoid(a0_ref[...] + d(zl.astype(BF16), la_ref[...]))
    g = d(_sigmoid(zl).astype(BF16), lg_ref[...])
    kk = k * kk_ref[...]
    kk = kk / jnp.maximum(jnp.sqrt(_group_sums(kk * kk, RWKV_HEAD, terms=2)), 1e-12)
    k = k * (1.0 + (a - 1.0) * ka_ref[...])
    bonus = _group_sums(r * k * rk_ref[...], RWKV_HEAD, terms=2) * v
    return r, k, v, -kk, kk * a, w_log, g, bonus


def _rwkv_post(y, bonus, g, gnw_ref, gnb_ref):
    inv_n = 1.0 / RWKV_HEAD
    mean = _group_sums(y, RWKV_HEAD, terms=2) * inv_n
    yc = y - mean
    var = _group_sums(yc * yc, RWKV_HEAD, terms=2) * inv_n
    yn = yc * lax.rsqrt(var + RWKV_GN_EPS) * gnw_ref[...] + gnb_ref[...]
    return (yn + bonus) * g


def _rwkv_lanes_gates_kernel(z_ref, sh_ref, mu_ref, lw_ref, w0_ref, la_ref, a0_ref, lg_ref, kk_ref, ka_ref, rk_ref,
                             r_ref, w_ref, k_ref, v_ref, a_ref, b_ref, bonus_ref, g_ref, shout_ref, *, l, nb):
    z = jnp.concatenate([z_ref[t] for t in range(l)], axis=0)
    z_prev = jnp.concatenate([sh_ref[...]] + [z_ref[t] for t in range(l - 1)], axis=0)
    shout_ref[...] = z_ref[l - 1]
    r, k, v, a_vec, b_vec, w_log, g, bonus = _rwkv_gates(
        z, z_prev, mu_ref, lw_ref, w0_ref, la_ref, a0_ref, lg_ref, kk_ref, ka_ref, rk_ref)
    w = jnp.exp(-jnp.exp(w_log))
    for x, ref in ((r, r_ref), (w, w_ref), (k, k_ref), (v, v_ref), (a_vec, a_ref), (b_vec, b_ref),
                   (bonus, bonus_ref), (g, g_ref)):
        for t in range(l):
            ref[t] = x[t * nb:(t + 1) * nb].T


def _rwkv_lanes_step_kernel(r_ref, w_ref, k_ref, v_ref, a_ref, b_ref, bonus_ref, g_ref, gnw_ref, gnb_ref, s0_ref,
                            y_ref, sout_ref, *, l, fill_layer):
    sout = _own_layer(sout_ref, fill_layer)
    vb = 8
    for t in range(l):
        a, w, b, k, r = a_ref[t], w_ref[t], b_ref[t], k_ref[t], r_ref[t]
        src = s0_ref if t == 0 else sout

        def rows(i, carry):
            v0 = pl.multiple_of(i * vb, vb)
            s = src[0, pl.ds(v0, vb)]
            sa = jnp.sum(s * a[None], axis=1)
            vv = v_ref[t, pl.ds(v0, vb), :]
            s = s * w[None] + sa[:, None, :] * b[None] + vv[:, None, :] * k[None]
            sout[0, pl.ds(v0, vb)] = s
            y_ref[t, pl.ds(v0, vb), :] = jnp.sum(s * r[None], axis=1)
            return carry

        lax.fori_loop(0, RWKV_HEAD // vb, rows, 0)
    inv_n = 1.0 / RWKV_HEAD
    for t in range(l):
        y = y_ref[t]
        mean = jnp.sum(y, axis=0, keepdims=True) * inv_n
        yc = y - mean
        var = jnp.sum(yc * yc, axis=0, keepdims=True) * inv_n
        yn = yc * lax.rsqrt(var + RWKV_GN_EPS) * gnw_ref[...] + gnb_ref[...]
        y_ref[t] = (yn + bonus_ref[t]) * g_ref[t]


def _rwkv_lanes(z2, shift_prev, wkv0_t, pw, layer, b, l, prev_stack, out_dtype):
    vec = lambda n: _const_spec((1, n), lambda i: (0, 0))
    mat = lambda r, c: _const_spec((r, c), lambda i: (0, 0))
    zt = z2[:, Z_RWKV:Z_RWKV + RWKV_PAD].reshape(b, l, RWKV_PAD).transpose(1, 0, 2)
    op_shape = jax.ShapeDtypeStruct((l, GROUP_W, b), F32)
    full = lambda shape: _const_spec(shape, lambda i: (0,) * len(shape))
    outs = pl.pallas_call(
        functools.partial(_rwkv_lanes_gates_kernel, l=l, nb=b),
        grid=(1,),
        in_specs=[full((l, b, RWKV_PAD)), full((b, RWKV_PAD)),
                  vec(RWKV_PAD), mat(LORA_W, GROUP_W), vec(GROUP_W), mat(LORA_W, GROUP_W), vec(GROUP_W),
                  mat(LORA_W, GROUP_W), vec(GROUP_W), vec(GROUP_W), vec(GROUP_W)],
        out_specs=[full((l, GROUP_W, b))] * 8 + [full((b, RWKV_PAD))],
        out_shape=[op_shape] * 8 + [jax.ShapeDtypeStruct((b, RWKV_PAD), F32)],
        compiler_params=_cparams(("arbitrary",)),
        name="rwkv7_gates",
    )(zt, shift_prev, pw["mu"], pw["lora_w"], pw["w0"], pw["lora_a"], pw["a0"], pw["lora_g"],
      pw["k_k"], pw["k_a"], pw["r_k"])
    ops, shift_new = outs[:8], outs[8]
    first = prev_stack is None
    tail = (1, RWKV_HEAD, RWKV_HEAD, b)
    op_spec = pl.BlockSpec((l, RWKV_HEAD, b), lambda i: (0, i, 0))
    col_spec = pl.BlockSpec((RWKV_HEAD, 1), lambda i: (i, 0))
    y, s_new = _call_stacked(
        functools.partial(_rwkv_lanes_step_kernel, l=l, fill_layer=layer if first else None),
        list(ops) + [pw["gn_w"].reshape(GROUP_W, 1), pw["gn_b"].reshape(GROUP_W, 1), wkv0_t],
        [op_spec] * 8 + [col_spec, col_spec, _stack_out_spec(tail, layer, False)],
        1, prev_stack,
        grid=(RWKV_HEADS,),
        out_specs=[op_spec, _stack_out_spec(tail, layer, first)],
        out_shape=[op_shape, jax.ShapeDtypeStruct((DEPTH, RWKV_HEADS, RWKV_HEAD, RWKV_HEAD, b), F32)],
        compiler_params=_cparams(("parallel",)),
        name="rwkv7_steps")
    y = y.transpose(2, 0, 1).reshape(b * l, GROUP_W).astype(out_dtype)
    return y, s_new, shift_new[:, :RWKV_COLS]


def _tri_inverse(l_mats, n):
    row = lax.broadcasted_iota(jnp.int32, (n, n), 0)
    col = lax.broadcasted_iota(jnp.int32, (n, n), 1)
    eye = (row == col).astype(F32)
    mm = lambda a, b: jnp.dot(a.astype(BF16), b.astype(BF16), preferred_element_type=F32)
    same = lambda m: (row // m) == (col // m)
    d1 = [jnp.where(same(8), x, 0.0) for x in l_mats]
    d2 = [mm(x, x) for x in d1]
    d4 = [mm(x, x) for x in d2]
    d3 = [mm(x, y) for x, y in zip(d1, d2)]
    t = [eye + x1 + x2 + x3 for x1, x2, x3 in zip(d1, d2, d3)]
    t = [x + mm(x, y) for x, y in zip(t, d4)]
    m = 8
    while m < RWKV_CHUNK:
        e = [jnp.where(same(2 * m) & jnp.logical_not(same(m)), x, 0.0) for x in l_mats]
        te = [mm(x, y) for x, y in zip(t, e)]
        t = [x + mm(y, x) for x, y in zip(t, te)]
        m *= 2
    return t


def _rwkv_chunk_local(chunks):
    c = RWKV_CHUNK
    n = 2 * c
    lo64 = lax.broadcasted_iota(jnp.int32, (c, 128), 1) < RWKV_HEAD
    tri = (lax.broadcasted_iota(jnp.int32, (c, c), 1) <= lax.broadcasted_iota(jnp.int32, (c, c), 0)).astype(BF16)
    stack = lambda x: jnp.concatenate([jnp.where(lo64, x, 0.0), jnp.where(lo64, 0.0, x)], axis=0)
    ops = []
    for r, k, v, a_vec, b_vec, logw in chunks:
        cum = _dot_f32_rhs(tri, logw)
        c_end = cum[c - 1:c]
        e_neg = jnp.exp(-cum)
        e_end = jnp.exp(c_end - cum)
        at = a_vec * jnp.exp(cum - logw)
        rt = r * jnp.exp(cum)
        bt = b_vec * e_neg
        kt = k * e_neg
        bh = b_vec * e_end
        kh = k * e_end
        e_cend = jnp.exp(c_end)
        for p in range(RWKV_PAIRS):
            sl = slice(p * 128, (p + 1) * 128)
            ops.append(dict(xa=stack(at[:, sl]), xr=stack(rt[:, sl]), yb=stack(bt[:, sl]), yk=stack(kt[:, sl]),
                            vs=stack(v[:, sl]).astype(BF16), xbh=stack(bh[:, sl]), xkh=stack(kh[:, sl]),
                            e_cend=e_cend[:, sl]))

    row = lax.broadcasted_iota(jnp.int32, (n, n), 0)
    col = lax.broadcasted_iota(jnp.int32, (n, n), 1)
    same_head = (row // c) == (col // c)
    strict = same_head & (col < row)
    incl = same_head & (col <= row)
    zeros = jnp.zeros((n, n), BF16)
    mm = lambda x, y: jnp.dot(x, y, preferred_element_type=F32)
    nt = lambda x, y: lax.dot_general(x, y, (((1,), (1,)), ((), ())), preferred_element_type=F32)
    tn = lambda x, y: lax.dot_general(x, y, (((0,), (0,)), ((), ())), preferred_element_type=F32)
    unstack = lambda x: x[0:c] + x[c:n]

    gram = [nt(jnp.concatenate([o["xa"], o["xr"]], axis=0).astype(BF16),
               jnp.concatenate([o["yb"], o["yk"]], axis=0).astype(BF16)) for o in ops]
    l_ab = [jnp.where(strict, x[0:n, 0:n], 0.0) for x in gram]
    l_ak = [jnp.where(strict, x[0:n, n:2 * n], 0.0) for x in gram]
    m_rbk = [jnp.concatenate([jnp.where(incl, x[n:2 * n, 0:n], 0.0), jnp.where(incl, x[n:2 * n, n:2 * n], 0.0)],
                             axis=1).astype(BF16) for x in gram]
    w1 = [mm(x.astype(BF16), o["vs"]) for x, o in zip(l_ak, ops)]
    t_inv = _tri_inverse(l_ab, n)
    tu = [mm(t.astype(BF16), jnp.concatenate([o["xa"], w], axis=1).astype(BF16))
          for t, o, w in zip(t_inv, ops, w1)]
    zmat = [jnp.concatenate([x.astype(BF16), jnp.concatenate([zeros, o["vs"]], axis=1)], axis=0)
            for x, o in zip(tu, ops)]
    ry = [mm(m, zm) for m, zm in zip(m_rbk, zmat)]
    ps = [tn(zm, jnp.concatenate([o["xbh"], o["xkh"]], axis=0).astype(BF16))
          for zm, o in zip(zmat, ops)]
    out = [(unstack(o["xr"] + y[:, 0:n]), unstack(y[:, n:2 * n]), o["e_cend"], q[0:n], q[n:2 * n])
           for o, y, q in zip(ops, ry, ps)]
    return [out[ci * RWKV_PAIRS:(ci + 1) * RWKV_PAIRS] for ci in range(len(chunks))]


def _rwkv_chunk_kernel(z_ref, sh_ref, s0_ref, mu_ref, lw_ref, w0_ref, la_ref, a0_ref, lg_ref, kk_ref, ka_ref, rk_ref,
                       gnw_ref, gnb_ref, y_ref, sout_ref, shout_ref, zbuf, s_scr, *, n_tiles, fill_layer):
    c = RWKV_CHUNK
    tile = RWKV_TILE
    zero = jnp.zeros((RWKV_HEAD, RWKV_HEAD), F32)
    for p in range(RWKV_PAIRS):
        s_scr[p] = jnp.concatenate([jnp.concatenate([s0_ref[0, 2 * p], zero], axis=1),
                                    jnp.concatenate([zero, s0_ref[0, 2 * p + 1]], axis=1)], axis=0)
    zbuf[pl.ds(0, 8), :] = jnp.broadcast_to(sh_ref[0], (8, RWKV_PAD))
    nt = lambda x, y: lax.dot_general(x, y, (((1,), (1,)), ((), ())), preferred_element_type=F32)

    def tile_body(ti, carry):
        r0 = pl.multiple_of(ti * tile, tile)
        z = z_ref[0, pl.ds(r0, tile), :]
        zbuf[pl.ds(8, tile), :] = z
        z_prev = zbuf[pl.ds(7, tile), :]
        zbuf[pl.ds(7, 1), :] = zbuf[pl.ds(7 + tile, 1), :]
        r, k, v, a_vec, b_vec, w_log, g, bonus = _rwkv_gates(
            z, z_prev, mu_ref, lw_ref, w0_ref, la_ref, a0_ref, lg_ref, kk_ref, ka_ref, rk_ref)
        logw = -jnp.exp(w_log)
        local = _rwkv_chunk_local([tuple(x[ci * c:(ci + 1) * c] for x in (r, k, v, a_vec, b_vec, logw))
                                   for ci in range(RWKV_TILE_CHUNKS)])
        y_cols = []
        for p in range(RWKV_PAIRS):
            s = s_scr[p]
            ys = []
            for ci in range(RWKV_TILE_CHUNKS):
                rc, yloc, e_cend, trans, sloc = local[ci][p]
                sb = s.astype(BF16)
                ys.append(nt(rc.astype(BF16), sb) + yloc)
                s = s * e_cend + jnp.dot(sb, trans.astype(BF16), preferred_element_type=F32) + sloc
            s_scr[p] = s
            y_cols.append(jnp.concatenate(ys, axis=0))
        y = jnp.concatenate(y_cols, axis=1)
        y_ref[0, pl.ds(r0, tile), :] = _rwkv_post(y, bonus, g, gnw_ref, gnb_ref).astype(y_ref.dtype)
        return carry

    lax.fori_loop(0, n_tiles, tile_body, 0)
    shout_ref[0] = zbuf[pl.ds(7, 1), :]
    sout = _own_layer(sout_ref, fill_layer)
    for p in range(RWKV_PAIRS):
        sout[0, 2 * p] = s_scr[p, 0:RWKV_HEAD, 0:RWKV_HEAD]
        sout[0, 2 * p + 1] = s_scr[p, RWKV_HEAD:2 * RWKV_HEAD, RWKV_HEAD:2 * RWKV_HEAD]


def _rwkv(z2, shift_prev, wkv0, pw, layer, b, l, prev_stack, out_dtype):
    vec = lambda n: _const_spec((1, n), lambda i: (0, 0))
    mat = lambda r, c: _const_spec((r, c), lambda i: (0, 0))
    params = [pw["mu"], pw["lora_w"], pw["w0"], pw["lora_a"], pw["a0"], pw["lora_g"],
              pw["k_k"], pw["k_a"], pw["r_k"], pw["gn_w"], pw["gn_b"]]
    pspecs = [vec(RWKV_PAD), mat(LORA_W, GROUP_W), vec(GROUP_W), mat(LORA_W, GROUP_W), vec(GROUP_W),
              mat(LORA_W, GROUP_W), vec(GROUP_W), vec(GROUP_W), vec(GROUP_W), vec(GROUP_W), vec(GROUP_W)]
    nseq = 1
    z_in = z2.reshape(b, l, Z_COLS)
    shift_prev = shift_prev.reshape(b, 1, RWKV_PAD)
    shspec = pl.BlockSpec((1, 1, RWKV_PAD), lambda i: (i, 0, 0))
    body = functools.partial(_rwkv_chunk_kernel, n_tiles=l // RWKV_TILE)
    zspec = pl.BlockSpec((1, l, RWKV_PAD), lambda i: (i, 0, Z_RWKV // RWKV_PAD))
    yspec = pl.BlockSpec((1, l, GROUP_W), lambda i: (i, 0, 0))
    yshape = jax.ShapeDtypeStruct((b, l, GROUP_W), out_dtype)
    scratch = [pltpu.VMEM((RWKV_TILE + 8, RWKV_PAD), F32), pltpu.VMEM((RWKV_PAIRS, 128, 128), F32)]
    first = prev_stack is None
    tail = (nseq, RWKV_HEADS, RWKV_HEAD, RWKV_HEAD)
    y, s_new, shift_new = _call_stacked(
        functools.partial(body, fill_layer=layer if first else None), [z_in, shift_prev, wkv0] + params,
        [zspec, shspec, _stack_out_spec(tail, layer, False)] + pspecs,
        1, prev_stack,
        grid=(b // nseq,),
        out_specs=[yspec, _stack_out_spec(tail, layer, first), shspec],
        out_shape=[yshape, jax.ShapeDtypeStruct((DEPTH, b, RWKV_HEADS, RWKV_HEAD, RWKV_HEAD), F32),
                   jax.ShapeDtypeStruct(shift_prev.shape, F32)],
        scratch_shapes=scratch,
        compiler_params=_cparams(("parallel",)),
        name="rwkv7")
    return y.reshape(b * l, GROUP_W), s_new, shift_new.reshape(b, RWKV_PAD)[:, :RWKV_COLS]


def _sgu_kernel(z_ref, lnw_ref, lnb_ref, wm_ref, bias_ref, nrm_ref, y_ref, v_ref, *, blk):
    z = z_ref[...]
    zg = 0.5 * z * (1.0 + lax.erf(z * 0.7071067811865476))
    u = zg[:, :GROUP_W]
    v = zg[:, GROUP_W:]
    mu = jnp.mean(v, axis=-1, keepdims=True)
    vc = v - mu
    var = jnp.mean(vc * vc, axis=-1, keepdims=True)
    vn = vc * lax.rsqrt(var + LN_EPS) * lnw_ref[...] + lnb_ref[...]
    v_ref[...] = vn
    row = lax.broadcasted_iota(jnp.int32, (SGU_CHUNK, SGU_CHUNK), 0)
    col = lax.broadcasted_iota(jnp.int32, (SGU_CHUNK, SGU_CHUNK), 1)
    mask = col <= row
    if blk < SGU_CHUNK:
        mask = mask & ((row // blk) == (col // blk))
    vb = vn.astype(BF16)
    wms = [jnp.where(mask, wm_ref[h], 0.0).astype(BF16) for h in range(SGU_HEADS)]
    rows = []
    for c in range(z.shape[0] // SGU_CHUNK):
        rs = slice(c * SGU_CHUNK, (c + 1) * SGU_CHUNK)
        rows.append(jnp.concatenate(
            [jnp.dot(wms[h], vb[rs, h * 128:(h + 1) * 128], preferred_element_type=F32) + bias_ref[:, h:h + 1]
             for h in range(SGU_HEADS)], axis=1))
    s = rows[0] if len(rows) == 1 else jnp.concatenate(rows, axis=0)
    y_ref[...] = _rms(u * s, nrm_ref[...]).astype(y_ref.dtype)


def _sgu(z2, pw, wm, bias, blk, out_dtype):
    t = z2.shape[0]
    tile = SGU_TILE_CHUNKS * SGU_CHUNK
    vec = _const_spec((1, GROUP_W), lambda i: (0, 0))
    return pl.pallas_call(
        functools.partial(_sgu_kernel, blk=blk),
        grid=(t // tile,),
        in_specs=[pl.BlockSpec((tile, SGU_COLS), lambda i: (i, Z_SGU // SGU_COLS)),
                  vec, vec,
                  _const_spec((SGU_HEADS, SGU_CHUNK, SGU_CHUNK), lambda i: (0, 0, 0)),
                  _const_spec((SGU_CHUNK, SGU_HEADS), lambda i: (0, 0)),
                  vec],
        out_specs=[pl.BlockSpec((tile, GROUP_W), lambda i: (i, 0)),
                   pl.BlockSpec((tile, GROUP_W), lambda i: (i, 0))],
        out_shape=[jax.ShapeDtypeStruct((t, GROUP_W), out_dtype), jax.ShapeDtypeStruct((t, GROUP_W), F32)],
        compiler_params=_cparams(("parallel",)),
        name="sgu",
    )(z2, pw["ln_w"], pw["ln_b"], wm, bias, pw["norm"])


def _hgrn_chunks(seqs, lb, nw, chunk, sub):
    nh = HGRN_HEADS
    hs = lambda x, h: x[:, h * HGRN_HEAD:(h + 1) * HGRN_HEAD]
    tri = (lax.broadcasted_iota(jnp.int32, (chunk, chunk), 1)
           <= lax.broadcasted_iota(jnp.int32, (chunk, chunk), 0)).astype(BF16)
    eye = (lax.broadcasted_iota(jnp.int32, (HGRN_HEAD, HGRN_HEAD), 0)
           == lax.broadcasted_iota(jnp.int32, (HGRN_HEAD, HGRN_HEAD), 1)).astype(F32)
    rowid = lax.broadcasted_iota(jnp.int32, (sub, 1), 0)
    d = functools.partial(jnp.dot, preferred_element_type=F32)
    nt = lambda x, y: lax.dot_general(x, y, (((1,), (1,)), ((), ())), preferred_element_type=F32)
    tn = lambda x, y: lax.dot_general(x, y, (((0,), (0,)), ((), ())), preferred_element_type=F32)

    pre = []
    for zq, zf, zi, zg, states in seqs:
        q = _silu(zq)
        fg = lb + (1.0 - lb) * _sigmoid(zf)
        k = 1.0 - fg
        b = _dot_f32_rhs(tri, jnp.log(fg))
        pre.append(dict(q=q, k=k, v=zi, vb=zi.astype(BF16), b=b, qe=(q * jnp.exp(b)).astype(BF16)))
    o = [[d(hs(u["qe"], h), st[h].astype(BF16)) for h in range(nh)] for u, (_, _, _, _, st) in zip(pre, seqs)]
    outs = [[[] for _ in range(nh)] for _ in seqs]
    for i in range(chunk // sub):
        lo_, hi_ = i * sub, (i + 1) * sub
        oi = [[o[n][h][lo_:hi_] for h in range(nh)] for n in range(len(seqs))]
        if i > 0:
            for n, u in enumerate(pre):
                bref = u["b"][lo_ - 1:lo_]
                qt = (u["q"][lo_:hi_] * jnp.exp(u["b"][lo_:hi_] - bref)).astype(BF16)
                kt = (u["k"][:lo_] * jnp.exp(bref - u["b"][:lo_])).astype(BF16)
                att = [nt(hs(qt, h), hs(kt, h)).astype(BF16) for h in range(nh)]
                for h in range(nh):
                    oi[n][h] = oi[n][h] + d(att[h], hs(u["vb"], h)[:lo_])
        tiles = []
        for u in pre:
            qi, bi = u["q"][lo_:hi_], u["b"][lo_:hi_]
            for j in range(sub):
                sj = lo_ + j
                pj = qi * jnp.exp(bi - u["b"][sj:sj + 1]) * u["k"][sj:sj + 1]
                tiles.extend(hs(pj, h) for h in range(nh))
        att = _group_sums(jnp.concatenate(tiles, axis=0), HGRN_HEAD, terms=1)
        t = 0
        for n, u in enumerate(pre):
            for j in range(sub):
                sj = lo_ + j
                for h in range(nh):
                    a_j = jnp.where(rowid >= j, att[t * sub:(t + 1) * sub], 0.0)
                    oi[n][h] = oi[n][h] + a_j * hs(u["v"], h)[sj:sj + 1]
                    t += 1
        for n in range(len(seqs)):
            for h in range(nh):
                outs[n][h].append(oi[n][h])
    results = []
    for n, (u, (_, _, _, zg, st)) in enumerate(zip(pre, seqs)):
        o_n = jnp.concatenate([outs[n][h][0] if len(outs[n][h]) == 1 else jnp.concatenate(outs[n][h], axis=0)
                               for h in range(nh)], axis=1)
        blast = u["b"][chunk - 1:chunk]
        kd = (u["k"] * jnp.exp(blast - u["b"])).astype(BF16)
        e_last = jnp.exp(blast)
        new_states = []
        for h in range(nh):
            e_col = _group_sums(eye * hs(e_last, h), HGRN_HEAD)
            new_states.append(e_col * st[h] + tn(hs(kd, h), hs(u["vb"], h)))
        ms = _group_sums(o_n * o_n, HGRN_HEAD) * (1.0 / HGRN_HEAD)
        results.append((o_n * lax.rsqrt(ms + NORM_EPS) * nw * _silu(zg), new_states))
    return results


def _hgrn_kernel(zq_ref, zf_ref, zi_ref, zg_ref, lg_ref, nw_ref, s0_ref, y_ref, sout_ref, s_scr,
                 *, layer, nseq, chunk, sub, n_chunks, fill_layer):
    lg = lg_ref[...]
    e = jnp.exp(lg - jnp.max(lg, axis=0, keepdims=True))
    sm = e / jnp.sum(e, axis=0, keepdims=True)
    lb = jnp.sum(sm[0:layer + 1], axis=0, keepdims=True) - sm[0:1]
    nw = nw_ref[...]
    nh = HGRN_HEADS
    sout = _own_layer(sout_ref, fill_layer)
    if n_chunks == 1:
        rows = lambda ref, b: ref[b * chunk:(b + 1) * chunk, :]
        seqs = [(rows(zq_ref, b), rows(zf_ref, b), rows(zi_ref, b), rows(zg_ref, b),
                 [s0_ref[b, h] for h in range(nh)]) for b in range(nseq)]
        ys = []
        for b, (y, states) in enumerate(_hgrn_chunks(seqs, lb, nw, chunk, sub)):
            ys.append(y)
            for h in range(nh):
                sout[b, h] = states[h]
        y_ref[...] = jnp.concatenate(ys, axis=0).astype(y_ref.dtype)
    else:
        s_scr[...] = s0_ref[...]

        def chunk_body(c, carry):
            r0 = pl.multiple_of(c * chunk, chunk)
            rows = lambda ref, b: ref[b, pl.ds(r0, chunk), :]
            seqs = [(rows(zq_ref, b), rows(zf_ref, b), rows(zi_ref, b), rows(zg_ref, b),
                     [s_scr[b, h] for h in range(nh)]) for b in range(nseq)]
            for b, (y, states) in enumerate(_hgrn_chunks(seqs, lb, nw, chunk, sub)):
                y_ref[b, pl.ds(r0, chunk), :] = y.astype(y_ref.dtype)
                for h in range(nh):
                    s_scr[b, h] = states[h]
            return carry

        lax.fori_loop(0, n_chunks, chunk_body, 0)
        sout[...] = s_scr[...]


def _hgrn(z2, s0, lb_logits, norm_w, layer, b, l, prev_stack, out_dtype):
    chunk = min(HGRN_CHUNK, l)
    sub = min(HGRN_SUB, chunk)
    cb = Z_HGRN // GROUP_W
    if l > chunk:
        nseq = 1
        z_in = z2.reshape(b, l, Z_COLS)
        zspec = lambda part: pl.BlockSpec((1, l, GROUP_W), lambda i: (i, 0, cb + part))
        yspec = pl.BlockSpec((1, l, GROUP_W), lambda i: (i, 0, 0))
        yshape = jax.ShapeDtypeStruct((b, l, GROUP_W), out_dtype)
    else:
        nseq = SHORT_SEQS
        z_in = z2
        zspec = lambda part: pl.BlockSpec((nseq * l, GROUP_W), lambda i: (i, cb + part))
        yspec = pl.BlockSpec((nseq * l, GROUP_W), lambda i: (i, 0))
        yshape = jax.ShapeDtypeStruct((b * l, GROUP_W), out_dtype)
    first = prev_stack is None
    tail = (nseq, HGRN_HEADS, HGRN_HEAD, HGRN_HEAD)
    y, s_new = _call_stacked(
        functools.partial(_hgrn_kernel, layer=layer, nseq=nseq, chunk=chunk, sub=sub, n_chunks=l // chunk,
                          fill_layer=layer if first else None),
        [z_in, z_in, z_in, z_in, lb_logits, norm_w, s0],
        [zspec(0), zspec(1), zspec(2), zspec(3),
         _const_spec((DEPTH, GROUP_W), lambda i: (0, 0)),
         _const_spec((1, GROUP_W), lambda i: (0, 0)),
         _stack_out_spec(tail, layer, False)],
        1, prev_stack,
        grid=(b // nseq,),
        out_specs=[yspec, _stack_out_spec(tail, layer, first)],
        out_shape=[yshape, jax.ShapeDtypeStruct((DEPTH, b, HGRN_HEADS, HGRN_HEAD, HGRN_HEAD), F32)],
        scratch_shapes=[pltpu.VMEM((nseq, HGRN_HEADS, HGRN_HEAD, HGRN_HEAD), F32)],
        compiler_params=_cparams(("parallel",)),
        name="hgrn2")
    return y.reshape(b * l, GROUP_W), s_new


def _pool_kernel(z_ref, hist_ref, pw_ref, ps_ref, y_ref, hout_ref, xbuf, s2buf, s4buf, s8buf,
                 *, nseq, tc, n_tiles, start_pos, flat):
    halo = POOL_HALO
    n = tc + halo
    c1 = POOL_CH
    for b in range(nseq):
        xbuf[b, pl.ds(0, 16), :] = jnp.zeros((16, GROUP_W), F32)
        xbuf[b, pl.ds(16, 16), :] = hist_ref[b]

    def tile(r0, b, z):
        xbuf[b, pl.ds(halo, tc), :] = z
        s2buf[b, pl.ds(8, n - 8), :] = xbuf[b, pl.ds(8, n - 8), :] + xbuf[b, pl.ds(7, n - 8), :]
        s4buf[b, pl.ds(16, n - 16), :] = s2buf[b, pl.ds(16, n - 16), c1:] + s2buf[b, pl.ds(14, n - 16), c1:]
        s8buf[b, pl.ds(24, n - 24), :] = s4buf[b, pl.ds(24, n - 24), c1:] + s4buf[b, pl.ds(20, n - 24), c1:]
        s16 = s8buf[b, pl.ds(halo, tc), c1:] + s8buf[b, pl.ds(halo - 8, tc), c1:]
        sums = (s2buf[b, pl.ds(halo, tc), 0:c1], s4buf[b, pl.ds(halo, tc), 0:c1], s8buf[b, pl.ds(halo, tc), 0:c1],
                s16)
        pos = start_pos + r0 + lax.broadcasted_iota(jnp.int32, (tc, 1), 0)
        ys = []
        for gi, win in enumerate(POOL_WINDOWS):
            sl = slice(gi * POOL_CH, (gi + 1) * POOL_CH)
            cnt = jnp.minimum(pos + 1, win).astype(F32)
            dlt = sums[gi] / cnt - z[:, sl]
            ys.append(jnp.dot(dlt.astype(BF16), pw_ref[gi], preferred_element_type=F32) * ps_ref[:, sl])
        if n_tiles > 1:
            xbuf[b, pl.ds(16, 16), :] = xbuf[b, pl.ds(n - 16, 16), :]
        return jnp.concatenate(ys, axis=1)

    if flat:
        y_ref[...] = jnp.concatenate([tile(0, b, z_ref[b * tc:(b + 1) * tc, :]) for b in range(nseq)],
                                     axis=0).astype(y_ref.dtype)
    else:
        def tile_body(ti, carry):
            r0 = pl.multiple_of(ti * tc, tc)
            for b in range(nseq):
                y_ref[b, pl.ds(r0, tc), :] = tile(r0, b, z_ref[b, pl.ds(r0, tc), :]).astype(y_ref.dtype)
            return carry

        lax.fori_loop(0, n_tiles, tile_body, 0)
    for b in range(nseq):
        hout_ref[b] = xbuf[b, pl.ds(n - POOL_HIST, POOL_HIST), :]


def _pool(z2, hist16, pool_w, pool_scale, start_pos, b, l, tc, out_dtype):
    cb = Z_POOL // GROUP_W
    if l >= POOL_HALO:
        nseq = 1
        z_in = z2.reshape(b, l, Z_COLS)
        zspec = pl.BlockSpec((1, l, GROUP_W), lambda i: (i, 0, cb))
        yspec = pl.BlockSpec((1, l, GROUP_W), lambda i: (i, 0, 0))
        yshape = jax.ShapeDtypeStruct((b, l, GROUP_W), out_dtype)
    else:
        nseq = SHORT_SEQS
        z_in = z2
        zspec = pl.BlockSpec((nseq * l, GROUP_W), lambda i: (i, cb))
        yspec = pl.BlockSpec((nseq * l, GROUP_W), lambda i: (i, 0))
        yshape = jax.ShapeDtypeStruct((b * l, GROUP_W), out_dtype)
    n = tc + POOL_HALO
    y, hist_new = pl.pallas_call(
        functools.partial(_pool_kernel, nseq=nseq, tc=tc, n_tiles=l // tc, start_pos=start_pos, flat=nseq > 1),
        grid=(b // nseq,),
        in_specs=[zspec,
                  pl.BlockSpec((nseq, 16, GROUP_W), lambda i: (i, 0, 0)),
                  _const_spec((len(POOL_WINDOWS), POOL_CH, POOL_CH), lambda i: (0, 0, 0)),
                  _const_spec((1, GROUP_W), lambda i: (0, 0))],
        out_specs=[yspec, pl.BlockSpec((nseq, POOL_HIST, GROUP_W), lambda i: (i, 0, 0))],
        out_shape=[yshape, jax.ShapeDtypeStruct((b, POOL_HIST, GROUP_W), F32)],
        scratch_shapes=[pltpu.VMEM((nseq, n, GROUP_W), F32), pltpu.VMEM((nseq, n, GROUP_W), F32),
                        pltpu.VMEM((nseq, n, 3 * POOL_CH), F32), pltpu.VMEM((nseq, n, 2 * POOL_CH), F32)],
        compiler_params=_cparams(("parallel",)),
        name="pool",
    )(z_in, hist16, pool_w, pool_scale)
    return y.reshape(b * l, GROUP_W), hist_new


def _pad_cols(a, n):
    return jnp.pad(a, ((0, 0),) * (a.ndim - 1) + ((0, n - a.shape[-1]),))


W_IN_BLK = 256
W_IN_SRC = 32


def _w_in_kernel(*refs):
    *srcs, o_ref = refs
    x = jnp.concatenate([r[...] for r in srcs], axis=0)
    col = pl.program_id(1) * W_IN_BLK + lax.broadcasted_iota(jnp.int32, (W_IN_BLK, 1), 0)
    x = jnp.where(col < Z_RWKV + RWKV_COLS, x, 0.0)
    o_ref[...] = x.T.astype(BF16)


def _w_in_layout(w_in):
    in_cols = w_in.shape[2]
    c1, c2, c3 = RWKV_COLS, RWKV_COLS + SGU_COLS, RWKV_COLS + SGU_COLS + HGRN_COLS
    per_blk = W_IN_BLK // W_IN_SRC
    last = in_cols // W_IN_SRC - 1

    def src_slab(j, q):
        start = jnp.where(j < Z_SGU // W_IN_BLK, c2 // W_IN_SRC + per_blk * (j - Z_HGRN // W_IN_BLK),
                          jnp.where(j < Z_POOL // W_IN_BLK, c1 // W_IN_SRC + per_blk * (j - Z_SGU // W_IN_BLK),
                                    jnp.where(j < Z_RWKV // W_IN_BLK, c3 // W_IN_SRC + per_blk * (j - Z_POOL // W_IN_BLK),
                                              per_blk * (j - Z_RWKV // W_IN_BLK))))
        return jnp.minimum(start + q, last)

    w_t = jnp.swapaxes(w_in, 1, 2)
    return pl.pallas_call(
        _w_in_kernel,
        grid=(DEPTH, Z_COLS // W_IN_BLK),
        in_specs=[pl.BlockSpec((None, W_IN_SRC, D_MODEL), lambda l, j, q=q: (l, src_slab(j, q), 0))
                  for q in range(per_blk)],
        out_specs=pl.BlockSpec((None, D_MODEL, W_IN_BLK), lambda l, j: (l, 0, j)),
        out_shape=jax.ShapeDtypeStruct((DEPTH, D_MODEL, Z_COLS), BF16),
        compiler_params=_cparams(("parallel", "parallel")),
        name="w_in_layout",
    )(*([w_t] * per_blk))


def _prep_weights(W):
    return dict(w_in=_w_in_layout(W["w_in"]), w_out=W["w_out"].astype(BF16), w_gu=W["ffn_w_gu"].astype(BF16),
                w_down=W["ffn_w_down"].astype(BF16), ple_gate=W["ple_gate"].astype(BF16),
                ple_proj=W["ple_proj"].astype(BF16))


def _prep_layer(i, W):
    row = lambda a: a.reshape(1, -1)
    lora = lambda a, off: jnp.pad(a, ((off, LORA_W - off - a.shape[0]), (0, 0))).astype(BF16)
    rw = dict(
        mu=_pad_cols(row(W["rwkv_mu"][i]), RWKV_PAD),
        lora_w=lora(W["rwkv_w_lora"][i], 0), w0=row(W["rwkv_w0"][i]),
        lora_a=lora(W["rwkv_a_lora"][i], W_LORA), a0=row(W["rwkv_a0"][i]),
        lora_g=lora(W["rwkv_g_lora"][i], W_LORA + A_LORA),
        k_k=row(W["rwkv_k_k"][i]), k_a=row(W["rwkv_k_a"][i]), r_k=row(W["rwkv_r_k"][i]),
        gn_w=row(W["rwkv_gn_w"][i]), gn_b=row(W["rwkv_gn_b"][i]))
    sg = dict(ln_w=row(W["sgu_ln_w"][i]), ln_b=row(W["sgu_ln_b"][i]), norm=row(W["sgu_norm"][i]))
    return dict(
        rw=rw, sg=sg, sgu_w=W["sgu_w"][i], sgu_b=W["sgu_b"][i],
        hgrn_norm=row(W["hgrn_norm"][i]), pool_w=W["pool_w"][i].astype(BF16), pool_scale=row(W["pool_scale"][i]),
        ln_mix_pre=row(W["ln_mix_pre"][i]), ln_mix_post=row(W["ln_mix_post"][i]),
        ln_ffn_pre=row(W["ln_ffn_pre"][i]), ln_ffn_post=row(W["ln_ffn_post"][i]))


def _trunk(x, p, wkv0, shift0, hgrn0, pool0, start_pos, big, layers, lb_logits, cfg):
    b, l, _ = x.shape
    t = b * l
    xf = x.reshape(t, D_MODEL)
    p2 = p.reshape(DEPTH, t, PLE_DIM)
    wkv_new = hgrn_new = None
    shift_l, pool_l, sgu_l = [], [], []
    long_seq = l % RWKV_TILE == 0
    rwkv = _rwkv if long_seq else _rwkv_lanes
    if not long_seq:
        wkv0 = wkv0.transpose(0, 2, 3, 4, 1)
    for i, lw in enumerate(layers):
        z2 = _inproj(xf, lw["ln_mix_pre"], big["w_in"], i, min(cfg["tm_in"], t), cfg["tn_in"])
        y_r, wkv_new, shift_new = rwkv(z2, _pad_cols(shift0[i], RWKV_PAD), wkv0, lw["rw"], i, b, l, wkv_new,
                                       cfg["ydt"])
        if cfg["sgu_blk"] == SGU_CHUNK:
            wm, bias = lw["sgu_w"], lw["sgu_b"].T
        else:
            rep = SGU_CHUNK // l
            wm = jnp.tile(lw["sgu_w"][:, :l, :l], (1, rep, rep))
            bias = jnp.tile(lw["sgu_b"][:, :l].T, (rep, 1))
        y_s, v_rows = _sgu(z2, lw["sg"], wm, bias, cfg["sgu_blk"], cfg["ydt"])
        y_h, hgrn_new = _hgrn(z2, hgrn0, lb_logits, lw["hgrn_norm"], i, b, l, hgrn_new, cfg["ydt"])
        hist16 = jnp.pad(pool0[i], ((0, 0), (1, 0), (0, 0)))
        y_p, pool_new = _pool(z2, hist16, lw["pool_w"], lw["pool_scale"], start_pos, b, l, cfg["tc_pool"],
                              cfg["ydt"])
        xf = _outproj([y_r, y_s, y_h, y_p], xf, big["w_out"], lw["ln_mix_post"], i, cfg["tm"])
        xf = _ffn(xf, lw["ln_ffn_pre"], big["w_gu"], big["w_down"], lw["ln_ffn_post"], p2, big["ple_gate"],
                  big["ple_proj"], i, cfg["tm"])
        shift_l.append(shift_new)
        pool_l.append(pool_new)
        sgu_l.append(v_rows.reshape(b, l, GROUP_W))
    if not long_seq:
        wkv_new = wkv_new.transpose(0, 4, 1, 2, 3)
    return (xf.reshape(b, l, D_MODEL), wkv_new, jnp.stack(shift_l), hgrn_new, jnp.stack(pool_l), jnp.stack(sgu_l))


def _cfg(l):
    if l >= 256:
        return dict(tm=512, tm_in=1024, tn_in=1792, sgu_blk=SGU_CHUNK, tc_pool=256, ydt=BF16)
    return dict(tm=512, tm_in=512, tn_in=1792, sgu_blk=l, tc_pool=l, ydt=BF16)


def kernel(x_prompt, x_sample, state_rwkv_wkv, state_rwkv_shift, state_hgrn, state_pool, p_prompt, p_sample,
           ln_mix_pre, ln_mix_post, ln_ffn_pre, ln_ffn_post, w_in, rwkv_mu, rwkv_w_lora, rwkv_w0, rwkv_a_lora,
           rwkv_a0, rwkv_g_lora, rwkv_k_k, rwkv_k_a, rwkv_r_k, rwkv_gn_w, rwkv_gn_b, sgu_ln_w, sgu_ln_b, sgu_w,
           sgu_b, sgu_norm, hgrn_lb_logits, hgrn_norm, pool_w, pool_scale, w_out, ffn_w_gu, ffn_w_down, ple_gate,
           ple_proj):
    W = dict(ln_mix_pre=ln_mix_pre, ln_mix_post=ln_mix_post, ln_ffn_pre=ln_ffn_pre, ln_ffn_post=ln_ffn_post,
             w_in=w_in, rwkv_mu=rwkv_mu, rwkv_w_lora=rwkv_w_lora, rwkv_w0=rwkv_w0, rwkv_a_lora=rwkv_a_lora,
             rwkv_a0=rwkv_a0, rwkv_g_lora=rwkv_g_lora, rwkv_k_k=rwkv_k_k, rwkv_k_a=rwkv_k_a,
             rwkv_r_k=rwkv_r_k.reshape(DEPTH, GROUP_W), rwkv_gn_w=rwkv_gn_w, rwkv_gn_b=rwkv_gn_b, sgu_ln_w=sgu_ln_w,
             sgu_ln_b=sgu_ln_b, sgu_w=sgu_w, sgu_b=sgu_b, sgu_norm=sgu_norm, hgrn_norm=hgrn_norm, pool_w=pool_w,
             pool_scale=pool_scale, w_out=w_out, ffn_w_gu=ffn_w_gu, ffn_w_down=ffn_w_down, ple_gate=ple_gate,
             ple_proj=ple_proj)
    big = _prep_weights(W)
    layers = [_prep_layer(i, W) for i in range(DEPTH)]
    lb_logits = hgrn_lb_logits.astype(F32)
    bp, lp, _ = x_prompt.shape
    zeros = lambda *s: jnp.zeros((DEPTH, bp) + s, F32)
    y_prompt, wkv_p, shift_p, hgrn_p, pool_p, _ = _trunk(
        x_prompt, p_prompt, zeros(RWKV_HEADS, RWKV_HEAD, RWKV_HEAD), zeros(RWKV_COLS),
        zeros(HGRN_HEADS, HGRN_HEAD, HGRN_HEAD), zeros(POOL_HIST, GROUP_W), 0, big, layers, lb_logits, _cfg(lp))
    y_sample, wkv_s, shift_s, hgrn_s, pool_s, sgu_v_s = _trunk(
        x_sample, p_sample, state_rwkv_wkv, state_rwkv_shift, state_hgrn, state_pool, PAST_LEN, big, layers,
        lb_logits, _cfg(x_sample.shape[1]))
    return (y_prompt, y_sample, wkv_p, shift_p, hgrn_p, pool_p, wkv_s, shift_s, hgrn_s, pool_s, sgu_v_s)
```

```python
import functools

import jax
import jax.numpy as jnp
from jax import lax
from jax.experimental import pallas as pl
from jax.experimental.pallas import tpu as pltpu

F32 = jnp.float32
BF16 = jnp.bfloat16

D_MODEL = 2048
DEPTH = 2
PAST_LEN = 16384
GROUP_W = 512
RWKV_HEAD = 64
RWKV_HEADS = 8
RWKV_PAIRS = RWKV_HEADS // 2
RWKV_CHUNK = 64
RWKV_TILE_CHUNKS = 4
RWKV_TILE = RWKV_TILE_CHUNKS * RWKV_CHUNK
W_LORA, A_LORA, G_LORA = 32, 32, 96
RWKV_GN_EPS = 64e-5
SGU_CHUNK = 128
SGU_HEADS = 4
SGU_TILE_CHUNKS = 4
HGRN_HEADS = 4
HGRN_HEAD = 128
HGRN_CHUNK = 64
HGRN_SUB = 16
POOL_WINDOWS = (2, 4, 8, 16)
POOL_CH = 128
POOL_HIST = 15
POOL_HALO = 32
D_FF = 5632
PLE_DIM = 256
PLE_COLS = 512
FFN_TF = 512
NORM_EPS = 1e-6
LN_EPS = 1e-5
RWKV_COLS = 3 * GROUP_W + W_LORA + A_LORA + G_LORA
RWKV_PAD = 1792
LORA_OFF = 3 * GROUP_W
LORA_W = RWKV_PAD - LORA_OFF
SGU_COLS = 2 * GROUP_W
HGRN_COLS = 4 * GROUP_W
Z_HGRN, Z_SGU, Z_POOL, Z_RWKV = 0, 2048, 3072, 3584
Z_COLS = Z_RWKV + RWKV_PAD
SHORT_SEQS = 8
VMEM_LIMIT = 56 * 1024 * 1024


def _cparams(sem):
    return pltpu.CompilerParams(dimension_semantics=sem, vmem_limit_bytes=VMEM_LIMIT)


def _const_spec(shape, index_map):
    return pl.BlockSpec(shape, index_map, pipeline_mode=pl.Buffered(1))


def _sigmoid(x):
    return jax.nn.sigmoid(x)


def _silu(x):
    return x * jax.nn.sigmoid(x)


def _split3(x):
    hi = x.astype(BF16)
    r = x - hi.astype(F32)
    mid = r.astype(BF16)
    lo = (r - mid.astype(F32)).astype(BF16)
    return hi, mid, lo


def _group_sums(x, width, terms=3):
    rows, cols = x.shape
    nblk = cols // 128
    xs = x if nblk == 1 else jnp.concatenate([x[:, i * 128:(i + 1) * 128] for i in range(nblk)], axis=0)
    ones = ((lax.broadcasted_iota(jnp.int32, (128, 128), 0) // width)
            == (lax.broadcasted_iota(jnp.int32, (128, 128), 1) // width)).astype(BF16)
    s = None
    for part in _split3(xs)[:terms]:
        y = jnp.dot(part, ones, preferred_element_type=F32)
        s = y if s is None else s + y
    return s if nblk == 1 else jnp.concatenate([s[i * rows:(i + 1) * rows] for i in range(nblk)], axis=1)


def _dot_f32_rhs(m, x):
    hi, mid, lo = _split3(x)
    d = functools.partial(jnp.dot, preferred_element_type=F32)
    return d(m, hi) + d(m, mid) + d(m, lo)


def _rms(x, g):
    return x * lax.rsqrt(jnp.mean(x * x, axis=-1, keepdims=True) + NORM_EPS) * g


def _call_stacked(kernel, inputs, in_specs, stack_out, prev_stack, **kw):
    if prev_stack is None:
        return pl.pallas_call(kernel, in_specs=in_specs, **kw)(*inputs)
    n_in = len(inputs)

    def body(*refs):
        kernel(*refs[:n_in], *refs[n_in + 1:])

    return pl.pallas_call(body, in_specs=list(in_specs) + [pl.BlockSpec(memory_space=pl.ANY)],
                          input_output_aliases={n_in: stack_out}, **kw)(*inputs, prev_stack)


def _stack_out_spec(tail, layer, first):
    zeros = (0,) * (len(tail) - 1)
    if first:
        return pl.BlockSpec((DEPTH,) + tail, lambda i: (0, i) + zeros)
    return pl.BlockSpec((None,) + tail, lambda i: (layer, i) + zeros)


def _own_layer(sout_ref, fill_layer):
    if fill_layer is None:
        return sout_ref
    for j in range(DEPTH):
        if j != fill_layer:
            sout_ref[j] = jnp.zeros(sout_ref.shape[1:], sout_ref.dtype)
    return sout_ref.at[fill_layer]


def _inproj_kernel(x_ref, g_ref, w_ref, z_ref, h_ref):
    @pl.when(pl.program_id(1) == 0)
    def _():
        h_ref[...] = _rms(x_ref[...], g_ref[...]).astype(BF16)

    z_ref[...] = jnp.dot(h_ref[...], w_ref[...], preferred_element_type=F32)


def _inproj(x, g, w, layer, tm, tn):
    t = x.shape[0]
    return pl.pallas_call(
        _inproj_kernel,
        grid=(t // tm, Z_COLS // tn),
        in_specs=[pl.BlockSpec((tm, D_MODEL), lambda i, j: (i, 0)),
                  _const_spec((1, D_MODEL), lambda i, j: (0, 0)),
                  pl.BlockSpec((None, D_MODEL, tn), lambda i, j: (layer, 0, j))],
        out_specs=pl.BlockSpec((tm, tn), lambda i, j: (i, j)),
        out_shape=jax.ShapeDtypeStruct((t, Z_COLS), F32),
        scratch_shapes=[pltpu.VMEM((tm, D_MODEL), BF16)],
        compiler_params=_cparams(("parallel", "arbitrary")),
        name="inproj",
    )(x, g, w)


def _outproj_kernel(yr_ref, ys_ref, yh_ref, yp_ref, x_ref, w_ref, g_ref, o_ref):
    d = functools.partial(jnp.dot, preferred_element_type=F32)
    g4 = GROUP_W
    mix = (d(yr_ref[...], w_ref[0:g4, :]) + d(ys_ref[...], w_ref[g4:2 * g4, :])
           + d(yh_ref[...], w_ref[2 * g4:3 * g4, :]) + d(yp_ref[...], w_ref[3 * g4:4 * g4, :]))
    o_ref[...] = x_ref[...] + _rms(mix, g_ref[...])


def _outproj(ys, x, w, g, layer, tm):
    t = x.shape[0]
    yspec = pl.BlockSpec((tm, GROUP_W), lambda i: (i, 0))
    return pl.pallas_call(
        _outproj_kernel,
        grid=(t // tm,),
        in_specs=[yspec, yspec, yspec, yspec,
                  pl.BlockSpec((tm, D_MODEL), lambda i: (i, 0)),
                  _const_spec((None, D_MODEL, D_MODEL), lambda i: (layer, 0, 0)),
                  _const_spec((1, D_MODEL), lambda i: (0, 0))],
        out_specs=pl.BlockSpec((tm, D_MODEL), lambda i: (i, 0)),
        out_shape=jax.ShapeDtypeStruct((t, D_MODEL), F32),
        compiler_params=_cparams(("parallel",)),
        name="outproj",
    )(*ys, x, w, g)


def _ffn_kernel(x_ref, gpre_ref, wg_ref, wu_ref, wd_ref, gpost_ref, p_ref, wgate_ref, wproj_ref, o_ref, h_ref, acc_ref):
    j = pl.program_id(1)

    @pl.when(j == 0)
    def _():
        h_ref[...] = _rms(x_ref[...], gpre_ref[...]).astype(BF16)
        acc_ref[...] = jnp.zeros_like(acc_ref)

    h = h_ref[...]
    gate = jnp.dot(h, wg_ref[...], preferred_element_type=F32)
    up = jnp.dot(h, wu_ref[...], preferred_element_type=F32)
    act = (_silu(gate) * up).astype(BF16)
    acc_ref[...] += jnp.dot(act, wd_ref[...], preferred_element_type=F32)

    @pl.when(j == pl.num_programs(1) - 1)
    def _():
        acc_ref[...] = x_ref[...] + _rms(acc_ref[...], gpost_ref[...])
        h_ref[...] = acc_ref[...].astype(BF16)
        pb = p_ref[...].astype(BF16)
        for c in range(D_MODEL // PLE_COLS):
            cs = slice(c * PLE_COLS, (c + 1) * PLE_COLS)
            pgate = _sigmoid(jnp.dot(h_ref[...], wgate_ref[:, cs], preferred_element_type=F32))
            ple = jnp.dot(pb, wproj_ref[:, cs], preferred_element_type=F32)
            o_ref[:, cs] = acc_ref[:, cs] + pgate * ple


def _ffn(x, gpre, w_gu, w_down, gpost, p, w_gate, w_proj, layer, tm):
    t = x.shape[0]
    tf = FFN_TF
    nf = D_FF // tf
    return pl.pallas_call(
        _ffn_kernel,
        grid=(t // tm, nf),
        in_specs=[pl.BlockSpec((tm, D_MODEL), lambda i, j: (i, 0)),
                  _const_spec((1, D_MODEL), lambda i, j: (0, 0)),
                  pl.BlockSpec((None, D_MODEL, tf), lambda i, j: (layer, 0, j)),
                  pl.BlockSpec((None, D_MODEL, tf), lambda i, j: (layer, 0, j + nf)),
                  pl.BlockSpec((None, tf, D_MODEL), lambda i, j: (layer, j, 0)),
                  _const_spec((1, D_MODEL), lambda i, j: (0, 0)),
                  pl.BlockSpec((None, tm, PLE_DIM), lambda i, j: (layer, i, 0)),
                  _const_spec((None, D_MODEL, D_MODEL), lambda i, j: (layer, 0, 0)),
                  _const_spec((None, PLE_DIM, D_MODEL), lambda i, j: (layer, 0, 0))],
        out_specs=pl.BlockSpec((tm, D_MODEL), lambda i, j: (i, 0)),
        out_shape=jax.ShapeDtypeStruct((t, D_MODEL), F32),
        scratch_shapes=[pltpu.VMEM((tm, D_MODEL), BF16), pltpu.VMEM((tm, D_MODEL), F32)],
        compiler_params=_cparams(("parallel", "arbitrary")),
        name="ffn",
    )(x, gpre, w_gu, w_gu, w_down, gpost, p, w_gate, w_proj)


def _rwkv_gates(z, z_prev, mu_ref, lw_ref, w0_ref, la_ref, a0_ref, lg_ref, kk_ref, ka_ref, rk_ref):
    zm = z + mu_ref[...] * (z_prev - z)
    g4 = GROUP_W
    r = zm[:, 0:g4]
    k = zm[:, g4:2 * g4]
    v = zm[:, 2 * g4:3 * g4]
    zl = zm[:, LORA_OFF:RWKV_PAD]
    d = functools.partial(jnp.dot, preferred_element_type=F32)
    wl = w0_ref[...] + d(jnp.tanh(zl).astype(BF16), lw_ref[...])
    w_log = -(jnp.maximum(-wl, 0.0) + jnp.log1p(jnp.exp(-jnp.abs(wl)))) - 0.5
    a = _sigmoid(a0_ref[...] + d(zl.astype(BF16), la_ref[...]))
    g = d(_sigmoid(zl).astype(BF16), lg_ref[...])
    kk = k * kk_ref[...]
    kk = kk / jnp.maximum(jnp.sqrt(_group_sums(kk * kk, RWKV_HEAD, terms=2)), 1e-12)
    k = k * (1.0 + (a - 1.0) * ka_ref[...])
    bonus = _group_sums(r * k * rk_ref[...], RWKV_HEAD, terms=2) * v
    return r, k, v, -kk, kk * a, w_log, g, bonus


def _rwkv_post(y, bonus, g, gnw_ref, gnb_ref):
    inv_n = 1.0 / RWKV_HEAD
    mean = _group_sums(y, RWKV_HEAD, terms=2) * inv_n
    yc = y - mean
    var = _group_sums(yc * yc, RWKV_HEAD, terms=2) * inv_n
    yn = yc * lax.rsqrt(var + RWKV_GN_EPS) * gnw_ref[...] + gnb_ref[...]
    return (yn + bonus) * g


def _rwkv_lanes_gates_kernel(z_ref, sh_ref, mu_ref, lw_ref, w0_ref, la_ref, a0_ref, lg_ref, kk_ref, ka_ref, rk_ref,
                             r_ref, w_ref, k_ref, v_ref, a_ref, b_ref, bonus_ref, g_ref, shout_ref, *, l, nb):
    z = jnp.concatenate([z_ref[t] for t in range(l)], axis=0)
    z_prev = jnp.concatenate([sh_ref[...]] + [z_ref[t] for t in range(l - 1)], axis=0)
    shout_ref[...] = z_ref[l - 1]
    r, k, v, a_vec, b_vec, w_log, g, bonus = _rwkv_gates(
        z, z_prev, mu_ref, lw_ref, w0_ref, la_ref, a0_ref, lg_ref, kk_ref, ka_ref, rk_ref)
    w = jnp.exp(-jnp.exp(w_log))
    for x, ref in ((r, r_ref), (w, w_ref), (k, k_ref), (v, v_ref), (a_vec, a_ref), (b_vec, b_ref),
                   (bonus, bonus_ref), (g, g_ref)):
        for t in range(l):
            ref[t] = x[t * nb:(t + 1) * nb].T


def _rwkv_lanes_step_kernel(r_ref, w_ref, k_ref, v_ref, a_ref, b_ref, bonus_ref, g_ref, gnw_ref, gnb_ref, s0_ref,
                            y_ref, sout_ref, *, l, fill_layer):
    sout = _own_layer(sout_ref, fill_layer)
    vb = 8
    for t in range(l):
        a, w, b, k, r = a_ref[t], w_ref[t], b_ref[t], k_ref[t], r_ref[t]
        src = s0_ref if t == 0 else sout

        def rows(i, carry):
            v0 = pl.multiple_of(i * vb, vb)
            s = src[0, pl.ds(v0, vb)]
            sa = jnp.sum(s * a[None], axis=1)
            vv = v_ref[t, pl.ds(v0, vb), :]
            s = s * w[None] + sa[:, None, :] * b[None] + vv[:, None, :] * k[None]
            sout[0, pl.ds(v0, vb)] = s
            y_ref[t, pl.ds(v0, vb), :] = jnp.sum(s * r[None], axis=1)
            return carry

        lax.fori_loop(0, RWKV_HEAD // vb, rows, 0)
    inv_n = 1.0 / RWKV_HEAD
    for t in range(l):
        y = y_ref[t]
        mean = jnp.sum(y, axis=0, keepdims=True) * inv_n
        yc = y - mean
        var = jnp.sum(yc * yc, axis=0, keepdims=True) * inv_n
        yn = yc * lax.rsqrt(var + RWKV_GN_EPS) * gnw_ref[...] + gnb_ref[...]
        y_ref[t] = (yn + bonus_ref[t]) * g_ref[t]


def _rwkv_lanes(z2, shift_prev, wkv0_t, pw, layer, b, l, prev_stack, out_dtype):
    vec = lambda n: _const_spec((1, n), lambda i: (0, 0))
    mat = lambda r, c: _const_spec((r, c), lambda i: (0, 0))
    zt = z2[:, Z_RWKV:Z_RWKV + RWKV_PAD].reshape(b, l, RWKV_PAD).transpose(1, 0, 2)
    op_shape = jax.ShapeDtypeStruct((l, GROUP_W, b), F32)
    full = lambda shape: _const_spec(shape, lambda i: (0,) * len(shape))
    outs = pl.pallas_call(
        functools.partial(_rwkv_lanes_gates_kernel, l=l, nb=b),
        grid=(1,),
        in_specs=[full((l, b, RWKV_PAD)), full((b, RWKV_PAD)),
                  vec(RWKV_PAD), mat(LORA_W, GROUP_W), vec(GROUP_W), mat(LORA_W, GROUP_W), vec(GROUP_W),
                  mat(LORA_W, GROUP_W), vec(GROUP_W), vec(GROUP_W), vec(GROUP_W)],
        out_specs=[full((l, GROUP_W, b))] * 8 + [full((b, RWKV_PAD))],
        out_shape=[op_shape] * 8 + [jax.ShapeDtypeStruct((b, RWKV_PAD), F32)],
        compiler_params=_cparams(("arbitrary",)),
        name="rwkv7_gates",
    )(zt, shift_prev, pw["mu"], pw["lora_w"], pw["w0"], pw["lora_a"], pw["a0"], pw["lora_g"],
      pw["k_k"], pw["k_a"], pw["r_k"])
    ops, shift_new = outs[:8], outs[8]
    first = prev_stack is None
    tail = (1, RWKV_HEAD, RWKV_HEAD, b)
    op_spec = pl.BlockSpec((l, RWKV_HEAD, b), lambda i: (0, i, 0))
    col_spec = pl.BlockSpec((RWKV_HEAD, 1), lambda i: (i, 0))
    y, s_new = _call_stacked(
        functools.partial(_rwkv_lanes_step_kernel, l=l, fill_layer=layer if first else None),
        list(ops) + [pw["gn_w"].reshape(GROUP_W, 1), pw["gn_b"].reshape(GROUP_W, 1), wkv0_t],
        [op_spec] * 8 + [col_spec, col_spec, _stack_out_spec(tail, layer, False)],
        1, prev_stack,
        grid=(RWKV_HEADS,),
        out_specs=[op_spec, _stack_out_spec(tail, layer, first)],
        out_shape=[op_shape, jax.ShapeDtypeStruct((DEPTH, RWKV_HEADS, RWKV_HEAD, RWKV_HEAD, b), F32)],
        compiler_params=_cparams(("parallel",)),
        name="rwkv7_steps")
    y = y.transpose(2, 0, 1).reshape(b * l, GROUP_W).astype(out_dtype)
    return y, s_new, shift_new[:, :RWKV_COLS]


def _tri_inverse(l_mats, n):
    row = lax.broadcasted_iota(jnp.int32, (n, n), 0)
    col = lax.broadcasted_iota(jnp.int32, (n, n), 1)
    eye = (row == col).astype(F32)
    mm = lambda a, b: jnp.dot(a.astype(BF16), b.astype(BF16), preferred_element_type=F32)
    same = lambda m: (row // m) == (col // m)
    d1 = [jnp.where(same(8), x, 0.0) for x in l_mats]
    d2 = [mm(x, x) for x in d1]
    d4 = [mm(x, x) for x in d2]
    d3 = [mm(x, y) for x, y in zip(d1, d2)]
    t = [eye + x1 + x2 + x3 for x1, x2, x3 in zip(d1, d2, d3)]
    t = [x + mm(x, y) for x, y in zip(t, d4)]
    m = 8
    while m < RWKV_CHUNK:
        e = [jnp.where(same(2 * m) & jnp.logical_not(same(m)), x, 0.0) for x in l_mats]
        te = [mm(x, y) for x, y in zip(t, e)]
        t = [x + mm(y, x) for x, y in zip(t, te)]
        m *= 2
    return t


def _rwkv_chunk_local(chunks):
    c = RWKV_CHUNK
    n = 2 * c
    lo64 = lax.broadcasted_iota(jnp.int32, (c, 128), 1) < RWKV_HEAD
    tri = (lax.broadcasted_iota(jnp.int32, (c, c), 1) <= lax.broadcasted_iota(jnp.int32, (c, c), 0)).astype(BF16)
    stack = lambda x: jnp.concatenate([jnp.where(lo64, x, 0.0), jnp.where(lo64, 0.0, x)], axis=0)
    ops = []
    for r, k, v, a_vec, b_vec, logw in chunks:
        cum = _dot_f32_rhs(tri, logw)
        c_end = cum[c - 1:c]
        e_neg = jnp.exp(-cum)
        e_end = jnp.exp(c_end - cum)
        at = a_vec * jnp.exp(cum - logw)
        rt = r * jnp.exp(cum)
        bt = b_vec * e_neg
        kt = k * e_neg
        bh = b_vec * e_end
        kh = k * e_end
        e_cend = jnp.exp(c_end)
        for p in range(RWKV_PAIRS):
            sl = slice(p * 128, (p + 1) * 128)
            ops.append(dict(xa=stack(at[:, sl]), xr=stack(rt[:, sl]), yb=stack(bt[:, sl]), yk=stack(kt[:, sl]),
                            vs=stack(v[:, sl]).astype(BF16), xbh=stack(bh[:, sl]), xkh=stack(kh[:, sl]),
                            e_cend=e_cend[:, sl]))

    row = lax.broadcasted_iota(jnp.int32, (n, n), 0)
    col = lax.broadcasted_iota(jnp.int32, (n, n), 1)
    same_head = (row // c) == (col // c)
    strict = same_head & (col < row)
    incl = same_head & (col <= row)
    zeros = jnp.zeros((n, n), BF16)
    mm = lambda x, y: jnp.dot(x, y, preferred_element_type=F32)
    nt = lambda x, y: lax.dot_general(x, y, (((1,), (1,)), ((), ())), preferred_element_type=F32)
    tn = lambda x, y: lax.dot_general(x, y, (((0,), (0,)), ((), ())), preferred_element_type=F32)
    unstack = lambda x: x[0:c] + x[c:n]

    gram = [nt(jnp.concatenate([o["xa"], o["xr"]], axis=0).astype(BF16),
               jnp.concatenate([o["yb"], o["yk"]], axis=0).astype(BF16)) for o in ops]
    l_ab = [jnp.where(strict, x[0:n, 0:n], 0.0) for x in gram]
    l_ak = [jnp.where(strict, x[0:n, n:2 * n], 0.0) for x in gram]
    m_rbk = [jnp.concatenate([jnp.where(incl, x[n:2 * n, 0:n], 0.0), jnp.where(incl, x[n:2 * n, n:2 * n], 0.0)],
                             axis=1).astype(BF16) for x in gram]
    w1 = [mm(x.astype(BF16), o["vs"]) for x, o in zip(l_ak, ops)]
    t_inv = _tri_inverse(l_ab, n)
    tu = [mm(t.astype(BF16), jnp.concatenate([o["xa"], w], axis=1).astype(BF16))
          for t, o, w in zip(t_inv, ops, w1)]
    zmat = [jnp.concatenate([x.astype(BF16), jnp.concatenate([zeros, o["vs"]], axis=1)], axis=0)
            for x, o in zip(tu, ops)]
    ry = [mm(m, zm) for m, zm in zip(m_rbk, zmat)]
    ps = [tn(zm, jnp.concatenate([o["xbh"], o["xkh"]], axis=0).astype(BF16))
          for zm, o in zip(zmat, ops)]
    out = [(unstack(o["xr"] + y[:, 0:n]), unstack(y[:, n:2 * n]), o["e_cend"], q[0:n], q[n:2 * n])
           for o, y, q in zip(ops, ry, ps)]
    return [out[ci * RWKV_PAIRS:(ci + 1) * RWKV_PAIRS] for ci in range(len(chunks))]


def _rwkv_chunk_kernel(z_ref, sh_ref, s0_ref, mu_ref, lw_ref, w0_ref, la_ref, a0_ref, lg_ref, kk_ref, ka_ref, rk_ref,
                       gnw_ref, gnb_ref, y_ref, sout_ref, shout_ref, zbuf, s_scr, *, n_tiles, fill_layer):
    c = RWKV_CHUNK
    tile = RWKV_TILE
    zero = jnp.zeros((RWKV_HEAD, RWKV_HEAD), F32)
    for p in range(RWKV_PAIRS):
        s_scr[p] = jnp.concatenate([jnp.concatenate([s0_ref[0, 2 * p], zero], axis=1),
                                    jnp.concatenate([zero, s0_ref[0, 2 * p + 1]], axis=1)], axis=0)
    zbuf[pl.ds(0, 8), :] = jnp.broadcast_to(sh_ref[0], (8, RWKV_PAD))
    nt = lambda x, y: lax.dot_general(x, y, (((1,), (1,)), ((), ())), preferred_element_type=F32)

    def tile_body(ti, carry):
        r0 = pl.multiple_of(ti * tile, tile)
        z = z_ref[0, pl.ds(r0, tile), :]
        zbuf[pl.ds(8, tile), :] = z
        z_prev = zbuf[pl.ds(7, tile), :]
        zbuf[pl.ds(7, 1), :] = zbuf[pl.ds(7 + tile, 1), :]
        r, k, v, a_vec, b_vec, w_log, g, bonus = _rwkv_gates(
            z, z_prev, mu_ref, lw_ref, w0_ref, la_ref, a0_ref, lg_ref, kk_ref, ka_ref, rk_ref)
        logw = -jnp.exp(w_log)
        local = _rwkv_chunk_local([tuple(x[ci * c:(ci + 1) * c] for x in (r, k, v, a_vec, b_vec, logw))
                                   for ci in range(RWKV_TILE_CHUNKS)])
        y_cols = []
        for p in range(RWKV_PAIRS):
            s = s_scr[p]
            ys = []
            for ci in range(RWKV_TILE_CHUNKS):
                rc, yloc, e_cend, trans, sloc = local[ci][p]
                sb = s.astype(BF16)
                ys.append(nt(rc.astype(BF16), sb) + yloc)
                s = s * e_cend + jnp.dot(sb, trans.astype(BF16), preferred_element_type=F32) + sloc
            s_scr[p] = s
            y_cols.append(jnp.concatenate(ys, axis=0))
        y = jnp.concatenate(y_cols, axis=1)
        y_ref[0, pl.ds(r0, tile), :] = _rwkv_post(y, bonus, g, gnw_ref, gnb_ref).astype(y_ref.dtype)
        return carry

    lax.fori_loop(0, n_tiles, tile_body, 0)
    shout_ref[0] = zbuf[pl.ds(7, 1), :]
    sout = _own_layer(sout_ref, fill_layer)
    for p in range(RWKV_PAIRS):
        sout[0, 2 * p] = s_scr[p, 0:RWKV_HEAD, 0:RWKV_HEAD]
        sout[0, 2 * p + 1] = s_scr[p, RWKV_HEAD:2 * RWKV_HEAD, RWKV_HEAD:2 * RWKV_HEAD]


def _rwkv(z2, shift_prev, wkv0, pw, layer, b, l, prev_stack, out_dtype):
    vec = lambda n: _const_spec((1, n), lambda i: (0, 0))
    mat = lambda r, c: _const_spec((r, c), lambda i: (0, 0))
    params = [pw["mu"], pw["lora_w"], pw["w0"], pw["lora_a"], pw["a0"], pw["lora_g"],
              pw["k_k"], pw["k_a"], pw["r_k"], pw["gn_w"], pw["gn_b"]]
    pspecs = [vec(RWKV_PAD), mat(LORA_W, GROUP_W), vec(GROUP_W), mat(LORA_W, GROUP_W), vec(GROUP_W),
              mat(LORA_W, GROUP_W), vec(GROUP_W), vec(GROUP_W), vec(GROUP_W), vec(GROUP_W), vec(GROUP_W)]
    nseq = 1
    z_in = z2.reshape(b, l, Z_COLS)
    shift_prev = shift_prev.reshape(b, 1, RWKV_PAD)
    shspec = pl.BlockSpec((1, 1, RWKV_PAD), lambda i: (i, 0, 0))
    body = functools.partial(_rwkv_chunk_kernel, n_tiles=l // RWKV_TILE)
    zspec = pl.BlockSpec((1, l, RWKV_PAD), lambda i: (i, 0, Z_RWKV // RWKV_PAD))
    yspec = pl.BlockSpec((1, l, GROUP_W), lambda i: (i, 0, 0))
    yshape = jax.ShapeDtypeStruct((b, l, GROUP_W), out_dtype)
    scratch = [pltpu.VMEM((RWKV_TILE + 8, RWKV_PAD), F32), pltpu.VMEM((RWKV_PAIRS, 128, 128), F32)]
    first = prev_stack is None
    tail = (nseq, RWKV_HEADS, RWKV_HEAD, RWKV_HEAD)
    y, s_new, shift_new = _call_stacked(
        functools.partial(body, fill_layer=layer if first else None), [z_in, shift_prev, wkv0] + params,
        [zspec, shspec, _stack_out_spec(tail, layer, False)] + pspecs,
        1, prev_stack,
        grid=(b // nseq,),
        out_specs=[yspec, _stack_out_spec(tail, layer, first), shspec],
        out_shape=[yshape, jax.ShapeDtypeStruct((DEPTH, b, RWKV_HEADS, RWKV_HEAD, RWKV_HEAD), F32),
                   jax.ShapeDtypeStruct(shift_prev.shape, F32)],
        scratch_shapes=scratch,
        compiler_params=_cparams(("parallel",)),
        name="rwkv7")
    return y.reshape(b * l, GROUP_W), s_new, shift_new.reshape(b, RWKV_PAD)[:, :RWKV_COLS]


def _sgu_kernel(z_ref, lnw_ref, lnb_ref, wm_ref, bias_ref, nrm_ref, y_ref, v_ref, *, blk):
    z = z_ref[...]
    zg = 0.5 * z * (1.0 + lax.erf(z * 0.7071067811865476))
    u = zg[:, :GROUP_W]
    v = zg[:, GROUP_W:]
    mu = jnp.mean(v, axis=-1, keepdims=True)
    vc = v - mu
    var = jnp.mean(vc * vc, axis=-1, keepdims=True)
    vn = vc * lax.rsqrt(var + LN_EPS) * lnw_ref[...] + lnb_ref[...]
    v_ref[...] = vn
    row = lax.broadcasted_iota(jnp.int32, (SGU_CHUNK, SGU_CHUNK), 0)
    col = lax.broadcasted_iota(jnp.int32, (SGU_CHUNK, SGU_CHUNK), 1)
    mask = col <= row
    if blk < SGU_CHUNK:
        mask = mask & ((row // blk) == (col // blk))
    vb = vn.astype(BF16)
    wms = [jnp.where(mask, wm_ref[h], 0.0).astype(BF16) for h in range(SGU_HEADS)]
    rows = []
    for c in range(z.shape[0] // SGU_CHUNK):
        rs = slice(c * SGU_CHUNK, (c + 1) * SGU_CHUNK)
        rows.append(jnp.concatenate(
            [jnp.dot(wms[h], vb[rs, h * 128:(h + 1) * 128], preferred_element_type=F32) + bias_ref[:, h:h + 1]
             for h in range(SGU_HEADS)], axis=1))
    s = rows[0] if len(rows) == 1 else jnp.concatenate(rows, axis=0)
    y_ref[...] = _rms(u * s, nrm_ref[...]).astype(y_ref.dtype)


def _sgu(z2, pw, wm, bias, blk, out_dtype):
    t = z2.shape[0]
    tile = SGU_TILE_CHUNKS * SGU_CHUNK
    vec = _const_spec((1, GROUP_W), lambda i: (0, 0))
    return pl.pallas_call(
        functools.partial(_sgu_kernel, blk=blk),
        grid=(t // tile,),
        in_specs=[pl.BlockSpec((tile, SGU_COLS), lambda i: (i, Z_SGU // SGU_COLS)),
                  vec, vec,
                  _const_spec((SGU_HEADS, SGU_CHUNK, SGU_CHUNK), lambda i: (0, 0, 0)),
                  _const_spec((SGU_CHUNK, SGU_HEADS), lambda i: (0, 0)),
                  vec],
        out_specs=[pl.BlockSpec((tile, GROUP_W), lambda i: (i, 0)),
                   pl.BlockSpec((tile, GROUP_W), lambda i: (i, 0))],
        out_shape=[jax.ShapeDtypeStruct((t, GROUP_W), out_dtype), jax.ShapeDtypeStruct((t, GROUP_W), F32)],
        compiler_params=_cparams(("parallel",)),
        name="sgu",
    )(z2, pw["ln_w"], pw["ln_b"], wm, bias, pw["norm"])


def _hgrn_chunks(seqs, lb, nw, chunk, sub):
    nh = HGRN_HEADS
    hs = lambda x, h: x[:, h * HGRN_HEAD:(h + 1) * HGRN_HEAD]
    tri = (lax.broadcasted_iota(jnp.int32, (chunk, chunk), 1)
           <= lax.broadcasted_iota(jnp.int32, (chunk, chunk), 0)).astype(BF16)
    rowid = lax.broadcasted_iota(jnp.int32, (sub, 1), 0)
    d = functools.partial(jnp.dot, preferred_element_type=F32)
    nt = lambda x, y: lax.dot_general(x, y, (((1,), (1,)), ((), ())), preferred_element_type=F32)
    tn = lambda x, y: lax.dot_general(x, y, (((0,), (0,)), ((), ())), preferred_element_type=F32)

    pre = []
    for zq, zf, zi, zg, states in seqs:
        q = _silu(zq)
        fg = lb + (1.0 - lb) * _sigmoid(zf)
        k = 1.0 - fg
        b = _dot_f32_rhs(tri, jnp.log(fg))
        pre.append(dict(q=q, k=k, v=zi, vb=zi.astype(BF16), b=b, qe=(q * jnp.exp(b)).astype(BF16)))
    o = [[nt(hs(u["qe"], h), st[h].astype(BF16)) for h in range(nh)] for u, (_, _, _, _, st) in zip(pre, seqs)]
    outs = [[[] for _ in range(nh)] for _ in seqs]
    for i in range(chunk // sub):
        lo_, hi_ = i * sub, (i + 1) * sub
        oi = [[o[n][h][lo_:hi_] for h in range(nh)] for n in range(len(seqs))]
        if i > 0:
            for n, u in enumerate(pre):
                bref = u["b"][lo_ - 1:lo_]
                qt = (u["q"][lo_:hi_] * jnp.exp(u["b"][lo_:hi_] - bref)).astype(BF16)
                kt = (u["k"][:lo_] * jnp.exp(bref - u["b"][:lo_])).astype(BF16)
                att = [nt(hs(qt, h), hs(kt, h)).astype(BF16) for h in range(nh)]
                for h in range(nh):
                    oi[n][h] = oi[n][h] + d(att[h], hs(u["vb"], h)[:lo_])
        tiles = []
        for u in pre:
            qi, bi = u["q"][lo_:hi_], u["b"][lo_:hi_]
            for j in range(sub):
                sj = lo_ + j
                pj = qi * jnp.exp(bi - u["b"][sj:sj + 1]) * u["k"][sj:sj + 1]
                tiles.extend(hs(pj, h) for h in range(nh))
        att = _group_sums(jnp.concatenate(tiles, axis=0), HGRN_HEAD, terms=1)
        t = 0
        for n, u in enumerate(pre):
            for j in range(sub):
                sj = lo_ + j
                for h in range(nh):
                    a_j = jnp.where(rowid >= j, att[t * sub:(t + 1) * sub], 0.0)
                    oi[n][h] = oi[n][h] + a_j * hs(u["v"], h)[sj:sj + 1]
                    t += 1
        for n in range(len(seqs)):
            for h in range(nh):
                outs[n][h].append(oi[n][h])
    results = []
    for n, (u, (_, _, _, zg, st)) in enumerate(zip(pre, seqs)):
        o_n = jnp.concatenate([outs[n][h][0] if len(outs[n][h]) == 1 else jnp.concatenate(outs[n][h], axis=0)
                               for h in range(nh)], axis=1)
        blast = u["b"][chunk - 1:chunk]
        kd = (u["k"] * jnp.exp(blast - u["b"])).astype(BF16)
        e_last = jnp.exp(blast)
        new_states = []
        for h in range(nh):
            new_states.append(st[h] * hs(e_last, h) + tn(hs(u["vb"], h), hs(kd, h)))
        ms = _group_sums(o_n * o_n, HGRN_HEAD) * (1.0 / HGRN_HEAD)
        results.append((o_n * lax.rsqrt(ms + NORM_EPS) * nw * _silu(zg), new_states))
    return results


def _hgrn_kernel(zq_ref, zf_ref, zi_ref, zg_ref, lg_ref, nw_ref, s0_ref, y_ref, sout_ref, s_scr,
                 *, layer, nseq, chunk, sub, n_chunks, fill_layer):
    lg = lg_ref[...]
    e = jnp.exp(lg - jnp.max(lg, axis=0, keepdims=True))
    sm = e / jnp.sum(e, axis=0, keepdims=True)
    lb = jnp.sum(sm[0:layer + 1], axis=0, keepdims=True) - sm[0:1]
    nw = nw_ref[...]
    nh = HGRN_HEADS
    sout = _own_layer(sout_ref, fill_layer)
    if n_chunks == 1:
        rows = lambda ref, b: ref[b * chunk:(b + 1) * chunk, :]
        seqs = [(rows(zq_ref, b), rows(zf_ref, b), rows(zi_ref, b), rows(zg_ref, b),
                 [s0_ref[b, h].T for h in range(nh)]) for b in range(nseq)]
        ys = []
        for b, (y, states) in enumerate(_hgrn_chunks(seqs, lb, nw, chunk, sub)):
            ys.append(y)
            for h in range(nh):
                sout[b, h] = states[h].T
        y_ref[...] = jnp.concatenate(ys, axis=0).astype(y_ref.dtype)
    else:
        for b in range(nseq):
            for h in range(nh):
                s_scr[b, h] = s0_ref[b, h].T

        def chunk_body(c, carry):
            r0 = pl.multiple_of(c * chunk, chunk)
            rows = lambda ref, b: ref[b, pl.ds(r0, chunk), :]
            seqs = [(rows(zq_ref, b), rows(zf_ref, b), rows(zi_ref, b), rows(zg_ref, b),
                     [s_scr[b, h] for h in range(nh)]) for b in range(nseq)]
            for b, (y, states) in enumerate(_hgrn_chunks(seqs, lb, nw, chunk, sub)):
                y_ref[b, pl.ds(r0, chunk), :] = y.astype(y_ref.dtype)
                for h in range(nh):
                    s_scr[b, h] = states[h]
            return carry

        lax.fori_loop(0, n_chunks, chunk_body, 0)
        for b in range(nseq):
            for h in range(nh):
                sout[b, h] = s_scr[b, h].T


def _hgrn(z2, s0, lb_logits, norm_w, layer, b, l, prev_stack, out_dtype):
    chunk = min(HGRN_CHUNK, l)
    sub = min(HGRN_SUB, chunk)
    cb = Z_HGRN // GROUP_W
    if l > chunk:
        nseq = 1
        z_in = z2.reshape(b, l, Z_COLS)
        zspec = lambda part: pl.BlockSpec((1, l, GROUP_W), lambda i: (i, 0, cb + part))
        yspec = pl.BlockSpec((1, l, GROUP_W), lambda i: (i, 0, 0))
        yshape = jax.ShapeDtypeStruct((b, l, GROUP_W), out_dtype)
    else:
        nseq = SHORT_SEQS
        z_in = z2
        zspec = lambda part: pl.BlockSpec((nseq * l, GROUP_W), lambda i: (i, cb + part))
        yspec = pl.BlockSpec((nseq * l, GROUP_W), lambda i: (i, 0))
        yshape = jax.ShapeDtypeStruct((b * l, GROUP_W), out_dtype)
    first = prev_stack is None
    tail = (nseq, HGRN_HEADS, HGRN_HEAD, HGRN_HEAD)
    y, s_new = _call_stacked(
        functools.partial(_hgrn_kernel, layer=layer, nseq=nseq, chunk=chunk, sub=sub, n_chunks=l // chunk,
                          fill_layer=layer if first else None),
        [z_in, z_in, z_in, z_in, lb_logits, norm_w, s0],
        [zspec(0), zspec(1), zspec(2), zspec(3),
         _const_spec((DEPTH, GROUP_W), lambda i: (0, 0)),
         _const_spec((1, GROUP_W), lambda i: (0, 0)),
         _stack_out_spec(tail, layer, False)],
        1, prev_stack,
        grid=(b // nseq,),
        out_specs=[yspec, _stack_out_spec(tail, layer, first)],
        out_shape=[yshape, jax.ShapeDtypeStruct((DEPTH, b, HGRN_HEADS, HGRN_HEAD, HGRN_HEAD), F32)],
        scratch_shapes=[pltpu.VMEM((nseq, HGRN_HEADS, HGRN_HEAD, HGRN_HEAD), F32)],
        compiler_params=_cparams(("parallel",)),
        name="hgrn2")
    return y.reshape(b * l, GROUP_W), s_new


def _pool_kernel(z_ref, hist_ref, pw_ref, ps_ref, y_ref, hout_ref, xbuf, s2buf, s4buf, s8buf,
                 *, nseq, tc, n_tiles, start_pos, flat):
    halo = POOL_HALO
    n = tc + halo
    c1 = POOL_CH
    for b in range(nseq):
        xbuf[b, pl.ds(0, 16), :] = jnp.zeros((16, GROUP_W), F32)
        xbuf[b, pl.ds(16, 16), :] = hist_ref[b]

    def tile(r0, b, z):
        xbuf[b, pl.ds(halo, tc), :] = z
        s2buf[b, pl.ds(8, n - 8), :] = xbuf[b, pl.ds(8, n - 8), :] + xbuf[b, pl.ds(7, n - 8), :]
        s4buf[b, pl.ds(16, n - 16), :] = s2buf[b, pl.ds(16, n - 16), c1:] + s2buf[b, pl.ds(14, n - 16), c1:]
        s8buf[b, pl.ds(24, n - 24), :] = s4buf[b, pl.ds(24, n - 24), c1:] + s4buf[b, pl.ds(20, n - 24), c1:]
        s16 = s8buf[b, pl.ds(halo, tc), c1:] + s8buf[b, pl.ds(halo - 8, tc), c1:]
        sums = (s2buf[b, pl.ds(halo, tc), 0:c1], s4buf[b, pl.ds(halo, tc), 0:c1], s8buf[b, pl.ds(halo, tc), 0:c1],
                s16)
        pos = start_pos + r0 + lax.broadcasted_iota(jnp.int32, (tc, 1), 0)
        ys = []
        for gi, win in enumerate(POOL_WINDOWS):
            sl = slice(gi * POOL_CH, (gi + 1) * POOL_CH)
            cnt = jnp.minimum(pos + 1, win).astype(F32)
            dlt = sums[gi] / cnt - z[:, sl]
            ys.append(jnp.dot(dlt.astype(BF16), pw_ref[gi], preferred_element_type=F32) * ps_ref[:, sl])
        if n_tiles > 1:
            xbuf[b, pl.ds(16, 16), :] = xbuf[b, pl.ds(n - 16, 16), :]
        return jnp.concatenate(ys, axis=1)

    if flat:
        y_ref[...] = jnp.concatenate([tile(0, b, z_ref[b * tc:(b + 1) * tc, :]) for b in range(nseq)],
                                     axis=0).astype(y_ref.dtype)
    else:
        def tile_body(ti, carry):
            r0 = pl.multiple_of(ti * tc, tc)
            for b in range(nseq):
                y_ref[b, pl.ds(r0, tc), :] = tile(r0, b, z_ref[b, pl.ds(r0, tc), :]).astype(y_ref.dtype)
            return carry

        lax.fori_loop(0, n_tiles, tile_body, 0)
    for b in range(nseq):
        hout_ref[b] = xbuf[b, pl.ds(n - POOL_HIST, POOL_HIST), :]


def _pool(z2, hist16, pool_w, pool_scale, start_pos, b, l, tc, out_dtype):
    cb = Z_POOL // GROUP_W
    if l >= POOL_HALO:
        nseq = 1
        z_in = z2.reshape(b, l, Z_COLS)
        zspec = pl.BlockSpec((1, l, GROUP_W), lambda i: (i, 0, cb))
        yspec = pl.BlockSpec((1, l, GROUP_W), lambda i: (i, 0, 0))
        yshape = jax.ShapeDtypeStruct((b, l, GROUP_W), out_dtype)
    else:
        nseq = SHORT_SEQS
        z_in = z2
        zspec = pl.BlockSpec((nseq * l, GROUP_W), lambda i: (i, cb))
        yspec = pl.BlockSpec((nseq * l, GROUP_W), lambda i: (i, 0))
        yshape = jax.ShapeDtypeStruct((b * l, GROUP_W), out_dtype)
    n = tc + POOL_HALO
    y, hist_new = pl.pallas_call(
        functools.partial(_pool_kernel, nseq=nseq, tc=tc, n_tiles=l // tc, start_pos=start_pos, flat=nseq > 1),
        grid=(b // nseq,),
        in_specs=[zspec,
                  pl.BlockSpec((nseq, 16, GROUP_W), lambda i: (i, 0, 0)),
                  _const_spec((len(POOL_WINDOWS), POOL_CH, POOL_CH), lambda i: (0, 0, 0)),
                  _const_spec((1, GROUP_W), lambda i: (0, 0))],
        out_specs=[yspec, pl.BlockSpec((nseq, POOL_HIST, GROUP_W), lambda i: (i, 0, 0))],
        out_shape=[yshape, jax.ShapeDtypeStruct((b, POOL_HIST, GROUP_W), F32)],
        scratch_shapes=[pltpu.VMEM((nseq, n, GROUP_W), F32), pltpu.VMEM((nseq, n, GROUP_W), F32),
                        pltpu.VMEM((nseq, n, 3 * POOL_CH), F32), pltpu.VMEM((nseq, n, 2 * POOL_CH), F32)],
        compiler_params=_cparams(("parallel",)),
        name="pool",
    )(z_in, hist16, pool_w, pool_scale)
    return y.reshape(b * l, GROUP_W), hist_new


def _pad_cols(a, n):
    return jnp.pad(a, ((0, 0),) * (a.ndim - 1) + ((0, n - a.shape[-1]),))


W_IN_BLK = 256
W_IN_SRC = 32


def _w_in_kernel(*refs):
    *srcs, o_ref = refs
    x = jnp.concatenate([r[...] for r in srcs], axis=0)
    col = pl.program_id(1) * W_IN_BLK + lax.broadcasted_iota(jnp.int32, (W_IN_BLK, 1), 0)
    x = jnp.where(col < Z_RWKV + RWKV_COLS, x, 0.0)
    o_ref[...] = x.T.astype(BF16)


def _w_in_layout(w_in):
    in_cols = w_in.shape[2]
    c1, c2, c3 = RWKV_COLS, RWKV_COLS + SGU_COLS, RWKV_COLS + SGU_COLS + HGRN_COLS
    per_blk = W_IN_BLK // W_IN_SRC
    last = in_cols // W_IN_SRC - 1

    def src_slab(j, q):
        start = jnp.where(j < Z_SGU // W_IN_BLK, c2 // W_IN_SRC + per_blk * (j - Z_HGRN // W_IN_BLK),
                          jnp.where(j < Z_POOL // W_IN_BLK, c1 // W_IN_SRC + per_blk * (j - Z_SGU // W_IN_BLK),
                                    jnp.where(j < Z_RWKV // W_IN_BLK, c3 // W_IN_SRC + per_blk * (j - Z_POOL // W_IN_BLK),
                                              per_blk * (j - Z_RWKV // W_IN_BLK))))
        return jnp.minimum(start + q, last)

    w_t = jnp.swapaxes(w_in, 1, 2)
    return pl.pallas_call(
        _w_in_kernel,
        grid=(DEPTH, Z_COLS // W_IN_BLK),
        in_specs=[pl.BlockSpec((None, W_IN_SRC, D_MODEL), lambda l, j, q=q: (l, src_slab(j, q), 0))
                  for q in range(per_blk)],
        out_specs=pl.BlockSpec((None, D_MODEL, W_IN_BLK), lambda l, j: (l, 0, j)),
        out_shape=jax.ShapeDtypeStruct((DEPTH, D_MODEL, Z_COLS), BF16),
        compiler_params=_cparams(("parallel", "parallel")),
        name="w_in_layout",
    )(*([w_t] * per_blk))


def _prep_weights(W):
    return dict(w_in=_w_in_layout(W["w_in"]), w_out=W["w_out"].astype(BF16), w_gu=W["ffn_w_gu"].astype(BF16),
                w_down=W["ffn_w_down"].astype(BF16), ple_gate=W["ple_gate"].astype(BF16),
                ple_proj=W["ple_proj"].astype(BF16))


def _prep_layer(i, W):
    row = lambda a: a.reshape(1, -1)
    lora = lambda a, off: jnp.pad(a, ((off, LORA_W - off - a.shape[0]), (0, 0))).astype(BF16)
    rw = dict(
        mu=_pad_cols(row(W["rwkv_mu"][i]), RWKV_PAD),
        lora_w=lora(W["rwkv_w_lora"][i], 0), w0=row(W["rwkv_w0"][i]),
        lora_a=lora(W["rwkv_a_lora"][i], W_LORA), a0=row(W["rwkv_a0"][i]),
        lora_g=lora(W["rwkv_g_lora"][i], W_LORA + A_LORA),
        k_k=row(W["rwkv_k_k"][i]), k_a=row(W["rwkv_k_a"][i]), r_k=row(W["rwkv_r_k"][i]),
        gn_w=row(W["rwkv_gn_w"][i]), gn_b=row(W["rwkv_gn_b"][i]))
    sg = dict(ln_w=row(W["sgu_ln_w"][i]), ln_b=row(W["sgu_ln_b"][i]), norm=row(W["sgu_norm"][i]))
    return dict(
        rw=rw, sg=sg, sgu_w=W["sgu_w"][i], sgu_b=W["sgu_b"][i],
        hgrn_norm=row(W["hgrn_norm"][i]), pool_w=W["pool_w"][i].astype(BF16), pool_scale=row(W["pool_scale"][i]),
        ln_mix_pre=row(W["ln_mix_pre"][i]), ln_mix_post=row(W["ln_mix_post"][i]),
        ln_ffn_pre=row(W["ln_ffn_pre"][i]), ln_ffn_post=row(W["ln_ffn_post"][i]))


def _trunk(x, p, wkv0, shift0, hgrn0, pool0, start_pos, big, layers, lb_logits, cfg):
    b, l, _ = x.shape
    t = b * l
    xf = x.reshape(t, D_MODEL)
    p2 = p.reshape(DEPTH, t, PLE_DIM)
    wkv_new = hgrn_new = None
    shift_l, pool_l, sgu_l = [], [], []
    long_seq = l % RWKV_TILE == 0
    rwkv = _rwkv if long_seq else _rwkv_lanes
    if not long_seq:
        wkv0 = wkv0.transpose(0, 2, 3, 4, 1)
    for i, lw in enumerate(layers):
        z2 = _inproj(xf, lw["ln_mix_pre"], big["w_in"], i, min(cfg["tm_in"], t), cfg["tn_in"])
        y_r, wkv_new, shift_new = rwkv(z2, _pad_cols(shift0[i], RWKV_PAD), wkv0, lw["rw"], i, b, l, wkv_new,
                                       cfg["ydt"])
        if cfg["sgu_blk"] == SGU_CHUNK:
            wm, bias = lw["sgu_w"], lw["sgu_b"].T
        else:
            rep = SGU_CHUNK // l
            wm = jnp.tile(lw["sgu_w"][:, :l, :l], (1, rep, rep))
            bias = jnp.tile(lw["sgu_b"][:, :l].T, (rep, 1))
        y_s, v_rows = _sgu(z2, lw["sg"], wm, bias, cfg["sgu_blk"], cfg["ydt"])
        y_h, hgrn_new = _hgrn(z2, hgrn0, lb_logits, lw["hgrn_norm"], i, b, l, hgrn_new, cfg["ydt"])
        hist16 = jnp.pad(pool0[i], ((0, 0), (1, 0), (0, 0)))
        y_p, pool_new = _pool(z2, hist16, lw["pool_w"], lw["pool_scale"], start_pos, b, l, cfg["tc_pool"],
                              cfg["ydt"])
        xf = _outproj([y_r, y_s, y_h, y_p], xf, big["w_out"], lw["ln_mix_post"], i, cfg["tm"])
        xf = _ffn(xf, lw["ln_ffn_pre"], big["w_gu"], big["w_down"], lw["ln_ffn_post"], p2, big["ple_gate"],
                  big["ple_proj"], i, cfg["tm"])
        shift_l.append(shift_new)
        pool_l.append(pool_new)
        sgu_l.append(v_rows.reshape(b, l, GROUP_W))
    if not long_seq:
        wkv_new = wkv_new.transpose(0, 4, 1, 2, 3)
    return (xf.reshape(b, l, D_MODEL), wkv_new, jnp.stack(shift_l), hgrn_new, jnp.stack(pool_l), jnp.stack(sgu_l))


def _cfg(l):
    if l >= 256:
        return dict(tm=512, tm_in=1024, tn_in=1792, sgu_blk=SGU_CHUNK, tc_pool=256, ydt=BF16)
    return dict(tm=512, tm_in=512, tn_in=1792, sgu_blk=l, tc_pool=l, ydt=BF16)


def kernel(x_prompt, x_sample, state_rwkv_wkv, state_rwkv_shift, state_hgrn, state_pool, p_prompt, p_sample,
           ln_mix_pre, ln_mix_post, ln_ffn_pre, ln_ffn_post, w_in, rwkv_mu, rwkv_w_lora, rwkv_w0, rwkv_a_lora,
           rwkv_a0, rwkv_g_lora, rwkv_k_k, rwkv_k_a, rwkv_r_k, rwkv_gn_w, rwkv_gn_b, sgu_ln_w, sgu_ln_b, sgu_w,
           sgu_b, sgu_norm, hgrn_lb_logits, hgrn_norm, pool_w, pool_scale, w_out, ffn_w_gu, ffn_w_down, ple_gate,
           ple_proj):
    W = dict(ln_mix_pre=ln_mix_pre, ln_mix_post=ln_mix_post, ln_ffn_pre=ln_ffn_pre, ln_ffn_post=ln_ffn_post,
             w_in=w_in, rwkv_mu=rwkv_mu, rwkv_w_lora=rwkv_w_lora, rwkv_w0=rwkv_w0, rwkv_a_lora=rwkv_a_lora,
             rwkv_a0=rwkv_a0, rwkv_g_lora=rwkv_g_lora, rwkv_k_k=rwkv_k_k, rwkv_k_a=rwkv_k_a,
             rwkv_r_k=rwkv_r_k.reshape(DEPTH, GROUP_W), rwkv_gn_w=rwkv_gn_w, rwkv_gn_b=rwkv_gn_b, sgu_ln_w=sgu_ln_w,
             sgu_ln_b=sgu_ln_b, sgu_w=sgu_w, sgu_b=sgu_b, sgu_norm=sgu_norm, hgrn_norm=hgrn_norm, pool_w=pool_w,
             pool_scale=pool_scale, w_out=w_out, ffn_w_gu=ffn_w_gu, ffn_w_down=ffn_w_down, ple_gate=ple_gate,
             ple_proj=ple_proj)
    big = _prep_weights(W)
    layers = [_prep_layer(i, W) for i in range(DEPTH)]
    lb_logits = hgrn_lb_logits.astype(F32)
    bp, lp, _ = x_prompt.shape
    zeros = lambda *s: jnp.zeros((DEPTH, bp) + s, F32)
    y_prompt, wkv_p, shift_p, hgrn_p, pool_p, _ = _trunk(
        x_prompt, p_prompt, zeros(RWKV_HEADS, RWKV_HEAD, RWKV_HEAD), zeros(RWKV_COLS),
        zeros(HGRN_HEADS, HGRN_HEAD, HGRN_HEAD), zeros(POOL_HIST, GROUP_W), 0, big, layers, lb_logits, _cfg(lp))
    y_sample, wkv_s, shift_s, hgrn_s, pool_s, sgu_v_s = _trunk(
        x_sample, p_sample, state_rwkv_wkv, state_rwkv_shift, state_hgrn, state_pool, PAST_LEN, big, layers,
        lb_logits, _cfg(x_sample.shape[1]))
    return (y_prompt, y_sample, wkv_p, shift_p, hgrn_p, pool_p, wkv_s, shift_s, hgrn_s, pool_s, sgu_v_s)
```

```python
import functools

import jax
import jax.numpy as jnp
from jax import lax
from jax.experimental import pallas as pl
from jax.experimental.pallas import tpu as pltpu

F32 = jnp.float32
BF16 = jnp.bfloat16

D_MODEL = 2048
DEPTH = 2
PAST_LEN = 16384
GROUP_W = 512
RWKV_HEAD = 64
RWKV_HEADS = 8
RWKV_PAIRS = RWKV_HEADS // 2
RWKV_CHUNK = 64
RWKV_TILE_CHUNKS = 4
RWKV_TILE = RWKV_TILE_CHUNKS * RWKV_CHUNK
W_LORA, A_LORA, G_LORA = 32, 32, 96
RWKV_GN_EPS = 64e-5
SGU_CHUNK = 128
SGU_HEADS = 4
SGU_TILE_CHUNKS = 4
HGRN_HEADS = 4
HGRN_HEAD = 128
HGRN_CHUNK = 64
HGRN_SUB = 16
POOL_WINDOWS = (2, 4, 8, 16)
POOL_CH = 128
POOL_HIST = 15
POOL_HALO = 32
D_FF = 5632
PLE_DIM = 256
PLE_COLS = 512
FFN_TF = 512
NORM_EPS = 1e-6
LN_EPS = 1e-5
RWKV_COLS = 3 * GROUP_W + W_LORA + A_LORA + G_LORA
RWKV_PAD = 1792
LORA_OFF = 3 * GROUP_W
LORA_W = RWKV_PAD - LORA_OFF
SGU_COLS = 2 * GROUP_W
HGRN_COLS = 4 * GROUP_W
Z_HGRN, Z_SGU, Z_POOL, Z_RWKV = 0, 2048, 3072, 3584
Z_COLS = Z_RWKV + RWKV_PAD
SHORT_SEQS = 8
VMEM_LIMIT = 56 * 1024 * 1024


def _cparams(sem):
    return pltpu.CompilerParams(dimension_semantics=sem, vmem_limit_bytes=VMEM_LIMIT)


def _const_spec(shape, index_map):
    return pl.BlockSpec(shape, index_map, pipeline_mode=pl.Buffered(1))


def _sigmoid(x):
    return jax.nn.sigmoid(x)


def _silu(x):
    return x * jax.nn.sigmoid(x)


def _split3(x):
    hi = x.astype(BF16)
    r = x - hi.astype(F32)
    mid = r.astype(BF16)
    lo = (r - mid.astype(F32)).astype(BF16)
    return hi, mid, lo


def _group_sums(x, width, terms=3):
    rows, cols = x.shape
    nblk = cols // 128
    xs = x if nblk == 1 else jnp.concatenate([x[:, i * 128:(i + 1) * 128] for i in range(nblk)], axis=0)
    ones = ((lax.broadcasted_iota(jnp.int32, (128, 128), 0) // width)
            == (lax.broadcasted_iota(jnp.int32, (128, 128), 1) // width)).astype(BF16)
    s = None
    for part in _split3(xs)[:terms]:
        y = jnp.dot(part, ones, preferred_element_type=F32)
        s = y if s is None else s + y
    return s if nblk == 1 else jnp.concatenate([s[i * rows:(i + 1) * rows] for i in range(nblk)], axis=1)


def _dot_f32_rhs(m, x):
    hi, mid, lo = _split3(x)
    d = functools.partial(jnp.dot, preferred_element_type=F32)
    return d(m, hi) + d(m, mid) + d(m, lo)


def _rms(x, g):
    return x * lax.rsqrt(jnp.mean(x * x, axis=-1, keepdims=True) + NORM_EPS) * g


def _call_stacked(kernel, inputs, in_specs, stack_out, prev_stack, **kw):
    if prev_stack is None:
        return pl.pallas_call(kernel, in_specs=in_specs, **kw)(*inputs)
    n_in = len(inputs)

    def body(*refs):
        kernel(*refs[:n_in], *refs[n_in + 1:])

    return pl.pallas_call(body, in_specs=list(in_specs) + [pl.BlockSpec(memory_space=pl.ANY)],
                          input_output_aliases={n_in: stack_out}, **kw)(*inputs, prev_stack)


def _stack_out_spec(tail, layer, first):
    zeros = (0,) * (len(tail) - 1)
    if first:
        return pl.BlockSpec((DEPTH,) + tail, lambda i: (0, i) + zeros)
    return pl.BlockSpec((None,) + tail, lambda i: (layer, i) + zeros)


def _own_layer(sout_ref, fill_layer):
    if fill_layer is None:
        return sout_ref
    for j in range(DEPTH):
        if j != fill_layer:
            sout_ref[j] = jnp.zeros(sout_ref.shape[1:], sout_ref.dtype)
    return sout_ref.at[fill_layer]


def _inproj_kernel(x_ref, g_ref, w_ref, z_ref, h_ref):
    @pl.when(pl.program_id(1) == 0)
    def _():
        h_ref[...] = _rms(x_ref[...], g_ref[...]).astype(BF16)

    z_ref[...] = jnp.dot(h_ref[...], w_ref[...], preferred_element_type=F32)


def _inproj(x, g, w, layer, tm, tn):
    t = x.shape[0]
    return pl.pallas_call(
        _inproj_kernel,
        grid=(t // tm, Z_COLS // tn),
        in_specs=[pl.BlockSpec((tm, D_MODEL), lambda i, j: (i, 0)),
                  _const_spec((1, D_MODEL), lambda i, j: (0, 0)),
                  pl.BlockSpec((None, D_MODEL, tn), lambda i, j: (layer, 0, j))],
        out_specs=pl.BlockSpec((tm, tn), lambda i, j: (i, j)),
        out_shape=jax.ShapeDtypeStruct((t, Z_COLS), F32),
        scratch_shapes=[pltpu.VMEM((tm, D_MODEL), BF16)],
        compiler_params=_cparams(("parallel", "arbitrary")),
        name="inproj",
    )(x, g, w)


def _outproj_kernel(yr_ref, ys_ref, yh_ref, yp_ref, x_ref, w_ref, g_ref, o_ref):
    d = functools.partial(jnp.dot, preferred_element_type=F32)
    g4 = GROUP_W
    mix = (d(yr_ref[...], w_ref[0:g4, :]) + d(ys_ref[...], w_ref[g4:2 * g4, :])
           + d(yh_ref[...], w_ref[2 * g4:3 * g4, :]) + d(yp_ref[...], w_ref[3 * g4:4 * g4, :]))
    o_ref[...] = x_ref[...] + _rms(mix, g_ref[...])


def _outproj(ys, x, w, g, layer, tm):
    t = x.shape[0]
    yspec = pl.BlockSpec((tm, GROUP_W), lambda i: (i, 0))
    return pl.pallas_call(
        _outproj_kernel,
        grid=(t // tm,),
        in_specs=[yspec, yspec, yspec, yspec,
                  pl.BlockSpec((tm, D_MODEL), lambda i: (i, 0)),
                  _const_spec((None, D_MODEL, D_MODEL), lambda i: (layer, 0, 0)),
                  _const_spec((1, D_MODEL), lambda i: (0, 0))],
        out_specs=pl.BlockSpec((tm, D_MODEL), lambda i: (i, 0)),
        out_shape=jax.ShapeDtypeStruct((t, D_MODEL), F32),
        compiler_params=_cparams(("parallel",)),
        name="outproj",
    )(*ys, x, w, g)


def _ffn_kernel(x_ref, gpre_ref, wg_ref, wu_ref, wd_ref, gpost_ref, p_ref, wgate_ref, wproj_ref, o_ref, h_ref, acc_ref):
    j = pl.program_id(1)

    @pl.when(j == 0)
    def _():
        h_ref[...] = _rms(x_ref[...], gpre_ref[...]).astype(BF16)
        acc_ref[...] = jnp.zeros_like(acc_ref)

    h = h_ref[...]
    gate = jnp.dot(h, wg_ref[...], preferred_element_type=F32)
    up = jnp.dot(h, wu_ref[...], preferred_element_type=F32)
    act = (_silu(gate) * up).astype(BF16)
    acc_ref[...] += jnp.dot(act, wd_ref[...], preferred_element_type=F32)

    @pl.when(j == pl.num_programs(1) - 1)
    def _():
        acc_ref[...] = x_ref[...] + _rms(acc_ref[...], gpost_ref[...])
        h_ref[...] = acc_ref[...].astype(BF16)
        pb = p_ref[...].astype(BF16)
        for c in range(D_MODEL // PLE_COLS):
            cs = slice(c * PLE_COLS, (c + 1) * PLE_COLS)
            pgate = _sigmoid(jnp.dot(h_ref[...], wgate_ref[:, cs], preferred_element_type=F32))
            ple = jnp.dot(pb, wproj_ref[:, cs], preferred_element_type=F32)
            o_ref[:, cs] = acc_ref[:, cs] + pgate * ple


def _ffn(x, gpre, w_gu, w_down, gpost, p, w_gate, w_proj, layer, tm):
    t = x.shape[0]
    tf = FFN_TF
    nf = D_FF // tf
    return pl.pallas_call(
        _ffn_kernel,
        grid=(t // tm, nf),
        in_specs=[pl.BlockSpec((tm, D_MODEL), lambda i, j: (i, 0)),
                  _const_spec((1, D_MODEL), lambda i, j: (0, 0)),
                  pl.BlockSpec((None, D_MODEL, tf), lambda i, j: (layer, 0, j)),
                  pl.BlockSpec((None, D_MODEL, tf), lambda i, j: (layer, 0, j + nf)),
                  pl.BlockSpec((None, tf, D_MODEL), lambda i, j: (layer, j, 0)),
                  _const_spec((1, D_MODEL), lambda i, j: (0, 0)),
                  pl.BlockSpec((None, tm, PLE_DIM), lambda i, j: (layer, i, 0)),
                  _const_spec((None, D_MODEL, D_MODEL), lambda i, j: (layer, 0, 0)),
                  _const_spec((None, PLE_DIM, D_MODEL), lambda i, j: (layer, 0, 0))],
        out_specs=pl.BlockSpec((tm, D_MODEL), lambda i, j: (i, 0)),
        out_shape=jax.ShapeDtypeStruct((t, D_MODEL), F32),
        scratch_shapes=[pltpu.VMEM((tm, D_MODEL), BF16), pltpu.VMEM((tm, D_MODEL), F32)],
        compiler_params=_cparams(("parallel", "arbitrary")),
        name="ffn",
    )(x, gpre, w_gu, w_gu, w_down, gpost, p, w_gate, w_proj)


def _rwkv_gates(z, z_prev, mu_ref, lw_ref, w0_ref, la_ref, a0_ref, lg_ref, kk_ref, ka_ref, rk_ref):
    zm = z + mu_ref[...] * (z_prev - z)
    g4 = GROUP_W
    r = zm[:, 0:g4]
    k = zm[:, g4:2 * g4]
    v = zm[:, 2 * g4:3 * g4]
    zl = zm[:, LORA_OFF:RWKV_PAD]
    d = functools.partial(jnp.dot, preferred_element_type=F32)
    wl = w0_ref[...] + d(jnp.tanh(zl).astype(BF16), lw_ref[...])
    w_log = -(jnp.maximum(-wl, 0.0) + jnp.log1p(jnp.exp(-jnp.abs(wl)))) - 0.5
    a = _sigmoid(a0_ref[...] + d(zl.astype(BF16), la_ref[...]))
    g = d(_sigmoid(zl).astype(BF16), lg_ref[...])
    kk = k * kk_ref[...]
    kk = kk / jnp.maximum(jnp.sqrt(_group_sums(kk * kk, RWKV_HEAD, terms=2)), 1e-12)
    k = k * (1.0 + (a - 1.0) * ka_ref[...])
    bonus = _group_sums(r * k * rk_ref[...], RWKV_HEAD, terms=2) * v
    return r, k, v, -kk, kk * a, w_log, g, bonus


def _rwkv_post(y, bonus, g, gnw_ref, gnb_ref):
    inv_n = 1.0 / RWKV_HEAD
    mean = _group_sums(y, RWKV_HEAD, terms=2) * inv_n
    yc = y - mean
    var = _group_sums(yc * yc, RWKV_HEAD, terms=2) * inv_n
    yn = yc * lax.rsqrt(var + RWKV_GN_EPS) * gnw_ref[...] + gnb_ref[...]
    return (yn + bonus) * g


def _rwkv_lanes_gates_kernel(z_ref, sh_ref, mu_ref, lw_ref, w0_ref, la_ref, a0_ref, lg_ref, kk_ref, ka_ref, rk_ref,
                             r_ref, w_ref, k_ref, v_ref, a_ref, b_ref, bonus_ref, g_ref, shout_ref, *, l, nb):
    z = jnp.concatenate([z_ref[t] for t in range(l)], axis=0)
    z_prev = jnp.concatenate([sh_ref[...]] + [z_ref[t] for t in range(l - 1)], axis=0)
    shout_ref[...] = z_ref[l - 1]
    r, k, v, a_vec, b_vec, w_log, g, bonus = _rwkv_gates(
        z, z_prev, mu_ref, lw_ref, w0_ref, la_ref, a0_ref, lg_ref, kk_ref, ka_ref, rk_ref)
    w = jnp.exp(-jnp.exp(w_log))
    for x, ref in ((r, r_ref), (w, w_ref), (k, k_ref), (v, v_ref), (a_vec, a_ref), (b_vec, b_ref),
                   (bonus, bonus_ref), (g, g_ref)):
        for t in range(l):
            ref[t] = x[t * nb:(t + 1) * nb].T


def _rwkv_lanes_step_kernel(r_ref, w_ref, k_ref, v_ref, a_ref, b_ref, bonus_ref, g_ref, gnw_ref, gnb_ref, s0_ref,
                            y_ref, sout_ref, *, l, fill_layer):
    sout = _own_layer(sout_ref, fill_layer)
    vb = 8
    for t in range(l):
        a, w, b, k, r = a_ref[t], w_ref[t], b_ref[t], k_ref[t], r_ref[t]
        src = s0_ref if t == 0 else sout

        def rows(i, carry):
            v0 = pl.multiple_of(i * vb, vb)
            s = src[0, pl.ds(v0, vb)]
            sa = jnp.sum(s * a[None], axis=1)
            vv = v_ref[t, pl.ds(v0, vb), :]
            s = s * w[None] + sa[:, None, :] * b[None] + vv[:, None, :] * k[None]
            sout[0, pl.ds(v0, vb)] = s
            y_ref[t, pl.ds(v0, vb), :] = jnp.sum(s * r[None], axis=1)
            return carry

        lax.fori_loop(0, RWKV_HEAD // vb, rows, 0)
    inv_n = 1.0 / RWKV_HEAD
    for t in range(l):
        y = y_ref[t]
        mean = jnp.sum(y, axis=0, keepdims=True) * inv_n
        yc = y - mean
        var = jnp.sum(yc * yc, axis=0, keepdims=True) * inv_n
        yn = yc * lax.rsqrt(var + RWKV_GN_EPS) * gnw_ref[...] + gnb_ref[...]
        y_ref[t] = (yn + bonus_ref[t]) * g_ref[t]


def _rwkv_lanes(z2, shift_prev, wkv0_t, pw, layer, b, l, prev_stack, out_dtype):
    vec = lambda n: _const_spec((1, n), lambda i: (0, 0))
    mat = lambda r, c: _const_spec((r, c), lambda i: (0, 0))
    zt = z2[:, Z_RWKV:Z_RWKV + RWKV_PAD].reshape(b, l, RWKV_PAD).transpose(1, 0, 2)
    op_shape = jax.ShapeDtypeStruct((l, GROUP_W, b), F32)
    full = lambda shape: _const_spec(shape, lambda i: (0,) * len(shape))
    outs = pl.pallas_call(
        functools.partial(_rwkv_lanes_gates_kernel, l=l, nb=b),
        grid=(1,),
        in_specs=[full((l, b, RWKV_PAD)), full((b, RWKV_PAD)),
                  vec(RWKV_PAD), mat(LORA_W, GROUP_W), vec(GROUP_W), mat(LORA_W, GROUP_W), vec(GROUP_W),
                  mat(LORA_W, GROUP_W), vec(GROUP_W), vec(GROUP_W), vec(GROUP_W)],
        out_specs=[full((l, GROUP_W, b))] * 8 + [full((b, RWKV_PAD))],
        out_shape=[op_shape] * 8 + [jax.ShapeDtypeStruct((b, RWKV_PAD), F32)],
        compiler_params=_cparams(("arbitrary",)),
        name="rwkv7_gates",
    )(zt, shift_prev, pw["mu"], pw["lora_w"], pw["w0"], pw["lora_a"], pw["a0"], pw["lora_g"],
      pw["k_k"], pw["k_a"], pw["r_k"])
    ops, shift_new = outs[:8], outs[8]
    first = prev_stack is None
    tail = (1, RWKV_HEAD, RWKV_HEAD, b)
    op_spec = pl.BlockSpec((l, RWKV_HEAD, b), lambda i: (0, i, 0))
    col_spec = pl.BlockSpec((RWKV_HEAD, 1), lambda i: (i, 0))
    y, s_new = _call_stacked(
        functools.partial(_rwkv_lanes_step_kernel, l=l, fill_layer=layer if first else None),
        list(ops) + [pw["gn_w"].reshape(GROUP_W, 1), pw["gn_b"].reshape(GROUP_W, 1), wkv0_t],
        [op_spec] * 8 + [col_spec, col_spec, _stack_out_spec(tail, layer, False)],
        1, prev_stack,
        grid=(RWKV_HEADS,),
        out_specs=[op_spec, _stack_out_spec(tail, layer, first)],
        out_shape=[op_shape, jax.ShapeDtypeStruct((DEPTH, RWKV_HEADS, RWKV_HEAD, RWKV_HEAD, b), F32)],
        compiler_params=_cparams(("parallel",)),
        name="rwkv7_steps")
    y = y.transpose(2, 0, 1).reshape(b * l, GROUP_W).astype(out_dtype)
    return y, s_new, shift_new[:, :RWKV_COLS]


def _tri_inverse(l_mats, n):
    row = lax.broadcasted_iota(jnp.int32, (n, n), 0)
    col = lax.broadcasted_iota(jnp.int32, (n, n), 1)
    eye = (row == col).astype(F32)
    mm = lambda a, b: jnp.dot(a.astype(BF16), b.astype(BF16), preferred_element_type=F32)
    same = lambda m: (row // m) == (col // m)
    d1 = [jnp.where(same(8), x, 0.0) for x in l_mats]
    d2 = [mm(x, x) for x in d1]
    d4 = [mm(x, x) for x in d2]
    d3 = [mm(x, y) for x, y in zip(d1, d2)]
    t = [eye + x1 + x2 + x3 for x1, x2, x3 in zip(d1, d2, d3)]
    t = [x + mm(x, y) for x, y in zip(t, d4)]
    m = 8
    while m < RWKV_CHUNK:
        e = [jnp.where(same(2 * m) & jnp.logical_not(same(m)), x, 0.0) for x in l_mats]
        te = [mm(x, y) for x, y in zip(t, e)]
        t = [x + mm(y, x) for x, y in zip(t, te)]
        m *= 2
    return t


def _rwkv_chunk_local(chunks):
    c = RWKV_CHUNK
    n = 2 * c
    lo64 = lax.broadcasted_iota(jnp.int32, (c, 128), 1) < RWKV_HEAD
    tri = (lax.broadcasted_iota(jnp.int32, (c, c), 1) <= lax.broadcasted_iota(jnp.int32, (c, c), 0)).astype(BF16)
    stack = lambda x: jnp.concatenate([jnp.where(lo64, x, 0.0), jnp.where(lo64, 0.0, x)], axis=0)
    ops = []
    for r, k, v, a_vec, b_vec, logw in chunks:
        cum = _dot_f32_rhs(tri, logw)
        c_end = cum[c - 1:c]
        e_neg = jnp.exp(-cum)
        e_end = jnp.exp(c_end - cum)
        at = a_vec * jnp.exp(cum - logw)
        rt = r * jnp.exp(cum)
        bt = b_vec * e_neg
        kt = k * e_neg
        bh = b_vec * e_end
        kh = k * e_end
        e_cend = jnp.exp(c_end)
        for p in range(RWKV_PAIRS):
            sl = slice(p * 128, (p + 1) * 128)
            ops.append(dict(xa=stack(at[:, sl]), xr=stack(rt[:, sl]), yb=stack(bt[:, sl]), yk=stack(kt[:, sl]),
                            vs=stack(v[:, sl]).astype(BF16), xbh=stack(bh[:, sl]), xkh=stack(kh[:, sl]),
                            e_cend=e_cend[:, sl]))

    row = lax.broadcasted_iota(jnp.int32, (n, n), 0)
    col = lax.broadcasted_iota(jnp.int32, (n, n), 1)
    same_head = (row // c) == (col // c)
    strict = same_head & (col < row)
    incl = same_head & (col <= row)
    zeros = jnp.zeros((n, n), BF16)
    mm = lambda x, y: jnp.dot(x, y, preferred_element_type=F32)
    nt = lambda x, y: lax.dot_general(x, y, (((1,), (1,)), ((), ())), preferred_element_type=F32)
    tn = lambda x, y: lax.dot_general(x, y, (((0,), (0,)), ((), ())), preferred_element_type=F32)
    unstack = lambda x: x[0:c] + x[c:n]

    gram = [nt(jnp.concatenate([o["xa"], o["xr"]], axis=0).astype(BF16),
               jnp.concatenate([o["yb"], o["yk"]], axis=0).astype(BF16)) for o in ops]
    l_ab = [jnp.where(strict, x[0:n, 0:n], 0.0) for x in gram]
    l_ak = [jnp.where(strict, x[0:n, n:2 * n], 0.0) for x in gram]
    m_rbk = [jnp.concatenate([jnp.where(incl, x[n:2 * n, 0:n], 0.0), jnp.where(incl, x[n:2 * n, n:2 * n], 0.0)],
                             axis=1).astype(BF16) for x in gram]
    w1 = [mm(x.astype(BF16), o["vs"]) for x, o in zip(l_ak, ops)]
    t_inv = _tri_inverse(l_ab, n)
    tu = [mm(t.astype(BF16), jnp.concatenate([o["xa"], w], axis=1).astype(BF16))
          for t, o, w in zip(t_inv, ops, w1)]
    zmat = [jnp.concatenate([x.astype(BF16), jnp.concatenate([zeros, o["vs"]], axis=1)], axis=0)
            for x, o in zip(tu, ops)]
    ry = [mm(m, zm) for m, zm in zip(m_rbk, zmat)]
    ps = [tn(zm, jnp.concatenate([o["xbh"], o["xkh"]], axis=0).astype(BF16))
          for zm, o in zip(zmat, ops)]
    out = [(unstack(o["xr"] + y[:, 0:n]), unstack(y[:, n:2 * n]), o["e_cend"], q[0:n], q[n:2 * n])
           for o, y, q in zip(ops, ry, ps)]
    return [out[ci * RWKV_PAIRS:(ci + 1) * RWKV_PAIRS] for ci in range(len(chunks))]


def _rwkv_chunk_kernel(z_ref, sh_ref, s0_ref, mu_ref, lw_ref, w0_ref, la_ref, a0_ref, lg_ref, kk_ref, ka_ref, rk_ref,
                       gnw_ref, gnb_ref, y_ref, sout_ref, shout_ref, zbuf, s_scr, *, n_tiles, fill_layer):
    c = RWKV_CHUNK
    tile = RWKV_TILE
    zero = jnp.zeros((RWKV_HEAD, RWKV_HEAD), F32)
    for p in range(RWKV_PAIRS):
        s_scr[p] = jnp.concatenate([jnp.concatenate([s0_ref[0, 2 * p], zero], axis=1),
                                    jnp.concatenate([zero, s0_ref[0, 2 * p + 1]], axis=1)], axis=0)
    zbuf[pl.ds(0, 8), :] = jnp.broadcast_to(sh_ref[0], (8, RWKV_PAD))
    nt = lambda x, y: lax.dot_general(x, y, (((1,), (1,)), ((), ())), preferred_element_type=F32)

    def tile_body(ti, carry):
        r0 = pl.multiple_of(ti * tile, tile)
        z = z_ref[0, pl.ds(r0, tile), :]
        zbuf[pl.ds(8, tile), :] = z
        z_prev = zbuf[pl.ds(7, tile), :]
        zbuf[pl.ds(7, 1), :] = zbuf[pl.ds(7 + tile, 1), :]
        r, k, v, a_vec, b_vec, w_log, g, bonus = _rwkv_gates(
            z, z_prev, mu_ref, lw_ref, w0_ref, la_ref, a0_ref, lg_ref, kk_ref, ka_ref, rk_ref)
        logw = -jnp.exp(w_log)
        local = _rwkv_chunk_local([tuple(x[ci * c:(ci + 1) * c] for x in (r, k, v, a_vec, b_vec, logw))
                                   for ci in range(RWKV_TILE_CHUNKS)])
        y_cols = []
        for p in range(RWKV_PAIRS):
            s = s_scr[p]
            ys = []
            for ci in range(RWKV_TILE_CHUNKS):
                rc, yloc, e_cend, trans, sloc = local[ci][p]
                sb = s.astype(BF16)
                ys.append(nt(rc.astype(BF16), sb) + yloc)
                s = s * e_cend + jnp.dot(sb, trans.astype(BF16), preferred_element_type=F32) + sloc
            s_scr[p] = s
            y_cols.append(jnp.concatenate(ys, axis=0))
        y = jnp.concatenate(y_cols, axis=1)
        y_ref[0, pl.ds(r0, tile), :] = _rwkv_post(y, bonus, g, gnw_ref, gnb_ref).astype(y_ref.dtype)
        return carry

    lax.fori_loop(0, n_tiles, tile_body, 0)
    shout_ref[0] = zbuf[pl.ds(7, 1), :]
    sout = _own_layer(sout_ref, fill_layer)
    for p in range(RWKV_PAIRS):
        sout[0, 2 * p] = s_scr[p, 0:RWKV_HEAD, 0:RWKV_HEAD]
        sout[0, 2 * p + 1] = s_scr[p, RWKV_HEAD:2 * RWKV_HEAD, RWKV_HEAD:2 * RWKV_HEAD]


def _rwkv(z2, shift_prev, wkv0, pw, layer, b, l, prev_stack, out_dtype):
    vec = lambda n: _const_spec((1, n), lambda i: (0, 0))
    mat = lambda r, c: _const_spec((r, c), lambda i: (0, 0))
    params = [pw["mu"], pw["lora_w"], pw["w0"], pw["lora_a"], pw["a0"], pw["lora_g"],
              pw["k_k"], pw["k_a"], pw["r_k"], pw["gn_w"], pw["gn_b"]]
    pspecs = [vec(RWKV_PAD), mat(LORA_W, GROUP_W), vec(GROUP_W), mat(LORA_W, GROUP_W), vec(GROUP_W),
              mat(LORA_W, GROUP_W), vec(GROUP_W), vec(GROUP_W), vec(GROUP_W), vec(GROUP_W), vec(GROUP_W)]
    nseq = 1
    z_in = z2.reshape(b, l, Z_COLS)
    shift_prev = shift_prev.reshape(b, 1, RWKV_PAD)
    shspec = pl.BlockSpec((1, 1, RWKV_PAD), lambda i: (i, 0, 0))
    body = functools.partial(_rwkv_chunk_kernel, n_tiles=l // RWKV_TILE)
    zspec = pl.BlockSpec((1, l, RWKV_PAD), lambda i: (i, 0, Z_RWKV // RWKV_PAD))
    yspec = pl.BlockSpec((1, l, GROUP_W), lambda i: (i, 0, 0))
    yshape = jax.ShapeDtypeStruct((b, l, GROUP_W), out_dtype)
    scratch = [pltpu.VMEM((RWKV_TILE + 8, RWKV_PAD), F32), pltpu.VMEM((RWKV_PAIRS, 128, 128), F32)]
    first = prev_stack is None
    tail = (nseq, RWKV_HEADS, RWKV_HEAD, RWKV_HEAD)
    y, s_new, shift_new = _call_stacked(
        functools.partial(body, fill_layer=layer if first else None), [z_in, shift_prev, wkv0] + params,
        [zspec, shspec, _stack_out_spec(tail, layer, False)] + pspecs,
        1, prev_stack,
        grid=(b // nseq,),
        out_specs=[yspec, _stack_out_spec(tail, layer, first), shspec],
        out_shape=[yshape, jax.ShapeDtypeStruct((DEPTH, b, RWKV_HEADS, RWKV_HEAD, RWKV_HEAD), F32),
                   jax.ShapeDtypeStruct(shift_prev.shape, F32)],
        scratch_shapes=scratch,
        compiler_params=_cparams(("parallel",)),
        name="rwkv7")
    return y.reshape(b * l, GROUP_W), s_new, shift_new.reshape(b, RWKV_PAD)[:, :RWKV_COLS]


def _sgu_kernel(z_ref, lnw_ref, lnb_ref, wm_ref, bias_ref, nrm_ref, y_ref, v_ref, *, blk):
    z = z_ref[...]
    zg = 0.5 * z * (1.0 + lax.erf(z * 0.7071067811865476))
    u = zg[:, :GROUP_W]
    v = zg[:, GROUP_W:]
    mu = jnp.mean(v, axis=-1, keepdims=True)
    vc = v - mu
    var = jnp.mean(vc * vc, axis=-1, keepdims=True)
    vn = vc * lax.rsqrt(var + LN_EPS) * lnw_ref[...] + lnb_ref[...]
    v_ref[...] = vn
    row = lax.broadcasted_iota(jnp.int32, (SGU_CHUNK, SGU_CHUNK), 0)
    col = lax.broadcasted_iota(jnp.int32, (SGU_CHUNK, SGU_CHUNK), 1)
    mask = col <= row
    vb = vn.astype(BF16)
    wms = [wm_ref[h].astype(BF16) for h in range(SGU_HEADS)]
    bias = bias_ref[...]
    if blk < SGU_CHUNK:
        mask = mask & ((row // blk) == (col // blk))
        sel = (col == row % blk).astype(BF16)
        nt = lambda x, y: lax.dot_general(x, y, (((1,), (1,)), ((), ())), preferred_element_type=F32)
        wms = [nt(jnp.dot(sel, w, preferred_element_type=F32).astype(BF16), sel).astype(BF16) for w in wms]
        bias = _dot_f32_rhs(sel, bias)
    wms = [jnp.where(mask, w, jnp.zeros_like(w)) for w in wms]
    rows = []
    for c in range(z.shape[0] // SGU_CHUNK):
        rs = slice(c * SGU_CHUNK, (c + 1) * SGU_CHUNK)
        rows.append(jnp.concatenate(
            [jnp.dot(wms[h], vb[rs, h * 128:(h + 1) * 128], preferred_element_type=F32) + bias[:, h:h + 1]
             for h in range(SGU_HEADS)], axis=1))
    s = rows[0] if len(rows) == 1 else jnp.concatenate(rows, axis=0)
    y_ref[...] = _rms(u * s, nrm_ref[...]).astype(y_ref.dtype)


def _sgu(z2, pw, wm, bias, blk, out_dtype):
    t = z2.shape[0]
    tile = SGU_TILE_CHUNKS * SGU_CHUNK
    vec = _const_spec((1, GROUP_W), lambda i: (0, 0))
    return pl.pallas_call(
        functools.partial(_sgu_kernel, blk=blk),
        grid=(t // tile,),
        in_specs=[pl.BlockSpec((tile, SGU_COLS), lambda i: (i, Z_SGU // SGU_COLS)),
                  vec, vec,
                  _const_spec((SGU_HEADS, SGU_CHUNK, SGU_CHUNK), lambda i: (0, 0, 0)),
                  _const_spec((SGU_CHUNK, SGU_HEADS), lambda i: (0, 0)),
                  vec],
        out_specs=[pl.BlockSpec((tile, GROUP_W), lambda i: (i, 0)),
                   pl.BlockSpec((tile, GROUP_W), lambda i: (i, 0))],
        out_shape=[jax.ShapeDtypeStruct((t, GROUP_W), out_dtype), jax.ShapeDtypeStruct((t, GROUP_W), F32)],
        compiler_params=_cparams(("parallel",)),
        name="sgu",
    )(z2, pw["ln_w"], pw["ln_b"], wm, bias, pw["norm"])


def _hgrn_chunks(seqs, lb, nw, chunk, sub):
    nh = HGRN_HEADS
    hs = lambda x, h: x[:, h * HGRN_HEAD:(h + 1) * HGRN_HEAD]
    tri = (lax.broadcasted_iota(jnp.int32, (chunk, chunk), 1)
           <= lax.broadcasted_iota(jnp.int32, (chunk, chunk), 0)).astype(BF16)
    eye = (lax.broadcasted_iota(jnp.int32, (HGRN_HEAD, HGRN_HEAD), 0)
           == lax.broadcasted_iota(jnp.int32, (HGRN_HEAD, HGRN_HEAD), 1)).astype(F32)
    rowid = lax.broadcasted_iota(jnp.int32, (sub, 1), 0)
    d = functools.partial(jnp.dot, preferred_element_type=F32)
    nt = lambda x, y: lax.dot_general(x, y, (((1,), (1,)), ((), ())), preferred_element_type=F32)
    tn = lambda x, y: lax.dot_general(x, y, (((0,), (0,)), ((), ())), preferred_element_type=F32)

    pre = []
    for zq, zf, zi, zg, states in seqs:
        q = _silu(zq)
        fg = lb + (1.0 - lb) * _sigmoid(zf)
        k = 1.0 - fg
        b = _dot_f32_rhs(tri, jnp.log(fg))
        pre.append(dict(q=q, k=k, v=zi, vb=zi.astype(BF16), b=b, qe=(q * jnp.exp(b)).astype(BF16)))
    o = [[d(hs(u["qe"], h), st[h].astype(BF16)) for h in range(nh)] for u, (_, _, _, _, st) in zip(pre, seqs)]
    outs = [[[] for _ in range(nh)] for _ in seqs]
    for i in range(chunk // sub):
        lo_, hi_ = i * sub, (i + 1) * sub
        oi = [[o[n][h][lo_:hi_] for h in range(nh)] for n in range(len(seqs))]
        if i > 0:
            for n, u in enumerate(pre):
                bref = u["b"][lo_ - 1:lo_]
                qt = (u["q"][lo_:hi_] * jnp.exp(u["b"][lo_:hi_] - bref)).astype(BF16)
                kt = (u["k"][:lo_] * jnp.exp(bref - u["b"][:lo_])).astype(BF16)
                att = [nt(hs(qt, h), hs(kt, h)).astype(BF16) for h in range(nh)]
                for h in range(nh):
                    oi[n][h] = oi[n][h] + d(att[h], hs(u["vb"], h)[:lo_])
        tiles = []
        for u in pre:
            qi, bi = u["q"][lo_:hi_], u["b"][lo_:hi_]
            for j in range(sub):
                sj = lo_ + j
                pj = qi * jnp.exp(bi - u["b"][sj:sj + 1]) * u["k"][sj:sj + 1]
                tiles.extend(hs(pj, h) for h in range(nh))
        att = _group_sums(jnp.concatenate(tiles, axis=0), HGRN_HEAD, terms=1)
        t = 0
        for n, u in enumerate(pre):
            for j in range(sub):
                sj = lo_ + j
                for h in range(nh):
                    a_j = jnp.where(rowid >= j, att[t * sub:(t + 1) * sub], 0.0)
                    oi[n][h] = oi[n][h] + a_j * hs(u["v"], h)[sj:sj + 1]
                    t += 1
        for n in range(len(seqs)):
            for h in range(nh):
                outs[n][h].append(oi[n][h])
    results = []
    for n, (u, (_, _, _, zg, st)) in enumerate(zip(pre, seqs)):
        o_n = jnp.concatenate([outs[n][h][0] if len(outs[n][h]) == 1 else jnp.concatenate(outs[n][h], axis=0)
                               for h in range(nh)], axis=1)
        blast = u["b"][chunk - 1:chunk]
        kd = (u["k"] * jnp.exp(blast - u["b"])).astype(BF16)
        e_last = jnp.exp(blast)
        new_states = []
        for h in range(nh):
            e_col = _group_sums(eye * hs(e_last, h), HGRN_HEAD)
            new_states.append(e_col * st[h] + tn(hs(kd, h), hs(u["vb"], h)))
        ms = _group_sums(o_n * o_n, HGRN_HEAD) * (1.0 / HGRN_HEAD)
        results.append((o_n * lax.rsqrt(ms + NORM_EPS) * nw * _silu(zg), new_states))
    return results


def _hgrn_kernel(zq_ref, zf_ref, zi_ref, zg_ref, lg_ref, nw_ref, s0_ref, y_ref, sout_ref, s_scr,
                 *, layer, nseq, chunk, sub, n_chunks, fill_layer):
    lg = lg_ref[...]
    e = jnp.exp(lg - jnp.max(lg, axis=0, keepdims=True))
    sm = e / jnp.sum(e, axis=0, keepdims=True)
    lb = jnp.sum(sm[0:layer + 1], axis=0, keepdims=True) - sm[0:1]
    nw = nw_ref[...]
    nh = HGRN_HEADS
    sout = _own_layer(sout_ref, fill_layer)
    if n_chunks == 1:
        rows = lambda ref, b: ref[b * chunk:(b + 1) * chunk, :]
        seqs = [(rows(zq_ref, b), rows(zf_ref, b), rows(zi_ref, b), rows(zg_ref, b),
                 [s0_ref[b, h] for h in range(nh)]) for b in range(nseq)]
        ys = []
        for b, (y, states) in enumerate(_hgrn_chunks(seqs, lb, nw, chunk, sub)):
            ys.append(y)
            for h in range(nh):
                sout[b, h] = states[h]
        y_ref[...] = jnp.concatenate(ys, axis=0).astype(y_ref.dtype)
    else:
        s_scr[...] = s0_ref[...]

        def chunk_body(c, carry):
            r0 = pl.multiple_of(c * chunk, chunk)
            rows = lambda ref, b: ref[b, pl.ds(r0, chunk), :]
            seqs = [(rows(zq_ref, b), rows(zf_ref, b), rows(zi_ref, b), rows(zg_ref, b),
                     [s_scr[b, h] for h in range(nh)]) for b in range(nseq)]
            for b, (y, states) in enumerate(_hgrn_chunks(seqs, lb, nw, chunk, sub)):
                y_ref[b, pl.ds(r0, chunk), :] = y.astype(y_ref.dtype)
                for h in range(nh):
                    s_scr[b, h] = states[h]
            return carry

        lax.fori_loop(0, n_chunks, chunk_body, 0)
        sout[...] = s_scr[...]


def _hgrn(z2, s0, lb_logits, norm_w, layer, b, l, prev_stack, out_dtype):
    chunk = min(HGRN_CHUNK, l)
    sub = min(HGRN_SUB, chunk)
    cb = Z_HGRN // GROUP_W
    if l > chunk:
        nseq = 1
        z_in = z2.reshape(b, l, Z_COLS)
        zspec = lambda part: pl.BlockSpec((1, l, GROUP_W), lambda i: (i, 0, cb + part))
        yspec = pl.BlockSpec((1, l, GROUP_W), lambda i: (i, 0, 0))
        yshape = jax.ShapeDtypeStruct((b, l, GROUP_W), out_dtype)
    else:
        nseq = SHORT_SEQS
        z_in = z2
        zspec = lambda part: pl.BlockSpec((nseq * l, GROUP_W), lambda i: (i, cb + part))
        yspec = pl.BlockSpec((nseq * l, GROUP_W), lambda i: (i, 0))
        yshape = jax.ShapeDtypeStruct((b * l, GROUP_W), out_dtype)
    first = prev_stack is None
    tail = (nseq, HGRN_HEADS, HGRN_HEAD, HGRN_HEAD)
    y, s_new = _call_stacked(
        functools.partial(_hgrn_kernel, layer=layer, nseq=nseq, chunk=chunk, sub=sub, n_chunks=l // chunk,
                          fill_layer=layer if first else None),
        [z_in, z_in, z_in, z_in, lb_logits, norm_w, s0],
        [zspec(0), zspec(1), zspec(2), zspec(3),
         _const_spec((DEPTH, GROUP_W), lambda i: (0, 0)),
         _const_spec((1, GROUP_W), lambda i: (0, 0)),
         _stack_out_spec(tail, layer, False)],
        1, prev_stack,
        grid=(b // nseq,),
        out_specs=[yspec, _stack_out_spec(tail, layer, first)],
        out_shape=[yshape, jax.ShapeDtypeStruct((DEPTH, b, HGRN_HEADS, HGRN_HEAD, HGRN_HEAD), F32)],
        scratch_shapes=[pltpu.VMEM((nseq, HGRN_HEADS, HGRN_HEAD, HGRN_HEAD), F32)],
        compiler_params=_cparams(("parallel",)),
        name="hgrn2")
    return y.reshape(b * l, GROUP_W), s_new


def _pool_kernel(z_ref, hist_ref, pw_ref, ps_ref, y_ref, hout_ref, xbuf, s2buf, s4buf, s8buf,
                 *, nseq, tc, n_tiles, start_pos, flat):
    halo = POOL_HALO
    n = tc + halo
    c1 = POOL_CH
    for b in range(nseq):
        xbuf[b, pl.ds(0, 16), :] = jnp.zeros((16, GROUP_W), F32)
        xbuf[b, pl.ds(16, 16), :] = hist_ref[b]

    def tile(r0, b, z):
        xbuf[b, pl.ds(halo, tc), :] = z
        s2buf[b, pl.ds(8, n - 8), :] = xbuf[b, pl.ds(8, n - 8), :] + xbuf[b, pl.ds(7, n - 8), :]
        s4buf[b, pl.ds(16, n - 16), :] = s2buf[b, pl.ds(16, n - 16), c1:] + s2buf[b, pl.ds(14, n - 16), c1:]
        s8buf[b, pl.ds(24, n - 24), :] = s4buf[b, pl.ds(24, n - 24), c1:] + s4buf[b, pl.ds(20, n - 24), c1:]
        s16 = s8buf[b, pl.ds(halo, tc), c1:] + s8buf[b, pl.ds(halo - 8, tc), c1:]
        sums = (s2buf[b, pl.ds(halo, tc), 0:c1], s4buf[b, pl.ds(halo, tc), 0:c1], s8buf[b, pl.ds(halo, tc), 0:c1],
                s16)
        pos = start_pos + r0 + lax.broadcasted_iota(jnp.int32, (tc, 1), 0)
        ys = []
        for gi, win in enumerate(POOL_WINDOWS):
            sl = slice(gi * POOL_CH, (gi + 1) * POOL_CH)
            cnt = jnp.minimum(pos + 1, win).astype(F32)
            dlt = sums[gi] / cnt - z[:, sl]
            ys.append(jnp.dot(dlt.astype(BF16), pw_ref[gi], preferred_element_type=F32) * ps_ref[:, sl])
        if n_tiles > 1:
            xbuf[b, pl.ds(16, 16), :] = xbuf[b, pl.ds(n - 16, 16), :]
        return jnp.concatenate(ys, axis=1)

    if flat:
        y_ref[...] = jnp.concatenate([tile(0, b, z_ref[b * tc:(b + 1) * tc, :]) for b in range(nseq)],
                                     axis=0).astype(y_ref.dtype)
    else:
        def tile_body(ti, carry):
            r0 = pl.multiple_of(ti * tc, tc)
            for b in range(nseq):
                y_ref[b, pl.ds(r0, tc), :] = tile(r0, b, z_ref[b, pl.ds(r0, tc), :]).astype(y_ref.dtype)
            return carry

        lax.fori_loop(0, n_tiles, tile_body, 0)
    for b in range(nseq):
        hout_ref[b] = xbuf[b, pl.ds(n - POOL_HIST, POOL_HIST), :]


def _pool(z2, hist16, pool_w, pool_scale, start_pos, b, l, tc, out_dtype):
    cb = Z_POOL // GROUP_W
    if l >= POOL_HALO:
        nseq = 1
        z_in = z2.reshape(b, l, Z_COLS)
        zspec = pl.BlockSpec((1, l, GROUP_W), lambda i: (i, 0, cb))
        yspec = pl.BlockSpec((1, l, GROUP_W), lambda i: (i, 0, 0))
        yshape = jax.ShapeDtypeStruct((b, l, GROUP_W), out_dtype)
    else:
        nseq = SHORT_SEQS
        z_in = z2
        zspec = pl.BlockSpec((nseq * l, GROUP_W), lambda i: (i, cb))
        yspec = pl.BlockSpec((nseq * l, GROUP_W), lambda i: (i, 0))
        yshape = jax.ShapeDtypeStruct((b * l, GROUP_W), out_dtype)
    n = tc + POOL_HALO
    y, hist_new = pl.pallas_call(
        functools.partial(_pool_kernel, nseq=nseq, tc=tc, n_tiles=l // tc, start_pos=start_pos, flat=nseq > 1),
        grid=(b // nseq,),
        in_specs=[zspec,
                  pl.BlockSpec((nseq, 16, GROUP_W), lambda i: (i, 0, 0)),
                  _const_spec((len(POOL_WINDOWS), POOL_CH, POOL_CH), lambda i: (0, 0, 0)),
                  _const_spec((1, GROUP_W), lambda i: (0, 0))],
        out_specs=[yspec, pl.BlockSpec((nseq, POOL_HIST, GROUP_W), lambda i: (i, 0, 0))],
        out_shape=[yshape, jax.ShapeDtypeStruct((b, POOL_HIST, GROUP_W), F32)],
        scratch_shapes=[pltpu.VMEM((nseq, n, GROUP_W), F32), pltpu.VMEM((nseq, n, GROUP_W), F32),
                        pltpu.VMEM((nseq, n, 3 * POOL_CH), F32), pltpu.VMEM((nseq, n, 2 * POOL_CH), F32)],
        compiler_params=_cparams(("parallel",)),
        name="pool",
    )(z_in, hist16, pool_w, pool_scale)
    return y.reshape(b * l, GROUP_W), hist_new


def _pad_cols(a, n):
    return jnp.pad(a, ((0, 0),) * (a.ndim - 1) + ((0, n - a.shape[-1]),))


W_IN_BLK = 256
W_IN_SRC = 32


def _w_in_kernel(*refs):
    *srcs, o_ref = refs
    x = jnp.concatenate([r[...] for r in srcs], axis=0)
    col = pl.program_id(1) * W_IN_BLK + lax.broadcasted_iota(jnp.int32, (W_IN_BLK, 1), 0)
    x = jnp.where(col < Z_RWKV + RWKV_COLS, x, 0.0)
    o_ref[...] = x.T.astype(BF16)


def _w_in_layout(w_in):
    in_cols = w_in.shape[2]
    c1, c2, c3 = RWKV_COLS, RWKV_COLS + SGU_COLS, RWKV_COLS + SGU_COLS + HGRN_COLS
    per_blk = W_IN_BLK // W_IN_SRC
    last = in_cols // W_IN_SRC - 1

    def src_slab(j, q):
        start = jnp.where(j < Z_SGU // W_IN_BLK, c2 // W_IN_SRC + per_blk * (j - Z_HGRN // W_IN_BLK),
                          jnp.where(j < Z_POOL // W_IN_BLK, c1 // W_IN_SRC + per_blk * (j - Z_SGU // W_IN_BLK),
                                    jnp.where(j < Z_RWKV // W_IN_BLK, c3 // W_IN_SRC + per_blk * (j - Z_POOL // W_IN_BLK),
                                              per_blk * (j - Z_RWKV // W_IN_BLK))))
        return jnp.minimum(start + q, last)

    w_t = jnp.swapaxes(w_in, 1, 2)
    return pl.pallas_call(
        _w_in_kernel,
        grid=(DEPTH, Z_COLS // W_IN_BLK),
        in_specs=[pl.BlockSpec((None, W_IN_SRC, D_MODEL), lambda l, j, q=q: (l, src_slab(j, q), 0))
                  for q in range(per_blk)],
        out_specs=pl.BlockSpec((None, D_MODEL, W_IN_BLK), lambda l, j: (l, 0, j)),
        out_shape=jax.ShapeDtypeStruct((DEPTH, D_MODEL, Z_COLS), BF16),
        compiler_params=_cparams(("parallel", "parallel")),
        name="w_in_layout",
    )(*([w_t] * per_blk))


def _prep_weights(W):
    return dict(w_in=_w_in_layout(W["w_in"]), w_out=W["w_out"].astype(BF16), w_gu=W["ffn_w_gu"].astype(BF16),
                w_down=W["ffn_w_down"].astype(BF16), ple_gate=W["ple_gate"].astype(BF16),
                ple_proj=W["ple_proj"].astype(BF16))


def _prep_layer(i, W):
    row = lambda a: a.reshape(1, -1)
    lora = lambda a, off: jnp.pad(a, ((off, LORA_W - off - a.shape[0]), (0, 0))).astype(BF16)
    rw = dict(
        mu=_pad_cols(row(W["rwkv_mu"][i]), RWKV_PAD),
        lora_w=lora(W["rwkv_w_lora"][i], 0), w0=row(W["rwkv_w0"][i]),
        lora_a=lora(W["rwkv_a_lora"][i], W_LORA), a0=row(W["rwkv_a0"][i]),
        lora_g=lora(W["rwkv_g_lora"][i], W_LORA + A_LORA),
        k_k=row(W["rwkv_k_k"][i]), k_a=row(W["rwkv_k_a"][i]), r_k=row(W["rwkv_r_k"][i]),
        gn_w=row(W["rwkv_gn_w"][i]), gn_b=row(W["rwkv_gn_b"][i]))
    sg = dict(ln_w=row(W["sgu_ln_w"][i]), ln_b=row(W["sgu_ln_b"][i]), norm=row(W["sgu_norm"][i]))
    return dict(
        rw=rw, sg=sg, sgu_w=W["sgu_w"][i], sgu_b=W["sgu_b"][i],
        hgrn_norm=row(W["hgrn_norm"][i]), pool_w=W["pool_w"][i].astype(BF16), pool_scale=row(W["pool_scale"][i]),
        ln_mix_pre=row(W["ln_mix_pre"][i]), ln_mix_post=row(W["ln_mix_post"][i]),
        ln_ffn_pre=row(W["ln_ffn_pre"][i]), ln_ffn_post=row(W["ln_ffn_post"][i]))


def _trunk(x, p, wkv0, shift0, hgrn0, pool0, start_pos, big, layers, lb_logits, cfg):
    b, l, _ = x.shape
    t = b * l
    xf = x.reshape(t, D_MODEL)
    p2 = p.reshape(DEPTH, t, PLE_DIM)
    wkv_new = hgrn_new = None
    shift_l, pool_l, sgu_l = [], [], []
    long_seq = l % RWKV_TILE == 0
    rwkv = _rwkv if long_seq else _rwkv_lanes
    if not long_seq:
        wkv0 = wkv0.transpose(0, 2, 3, 4, 1)
    for i, lw in enumerate(layers):
        z2 = _inproj(xf, lw["ln_mix_pre"], big["w_in"], i, min(cfg["tm_in"], t), cfg["tn_in"])
        y_r, wkv_new, shift_new = rwkv(z2, _pad_cols(shift0[i], RWKV_PAD), wkv0, lw["rw"], i, b, l, wkv_new,
                                       cfg["ydt"])
        y_s, v_rows = _sgu(z2, lw["sg"], lw["sgu_w"], lw["sgu_b"].T, cfg["sgu_blk"], cfg["ydt"])
        y_h, hgrn_new = _hgrn(z2, hgrn0, lb_logits, lw["hgrn_norm"], i, b, l, hgrn_new, cfg["ydt"])
        hist16 = jnp.pad(pool0[i], ((0, 0), (1, 0), (0, 0)))
        y_p, pool_new = _pool(z2, hist16, lw["pool_w"], lw["pool_scale"], start_pos, b, l, cfg["tc_pool"],
                              cfg["ydt"])
        xf = _outproj([y_r, y_s, y_h, y_p], xf, big["w_out"], lw["ln_mix_post"], i, cfg["tm"])
        xf = _ffn(xf, lw["ln_ffn_pre"], big["w_gu"], big["w_down"], lw["ln_ffn_post"], p2, big["ple_gate"],
                  big["ple_proj"], i, cfg["tm"])
        shift_l.append(shift_new)
        pool_l.append(pool_new)
        sgu_l.append(v_rows.reshape(b, l, GROUP_W))
    if not long_seq:
        wkv_new = wkv_new.transpose(0, 4, 1, 2, 3)
    return (xf.reshape(b, l, D_MODEL), wkv_new, jnp.stack(shift_l), hgrn_new, jnp.stack(pool_l), jnp.stack(sgu_l))


def _cfg(l):
    if l >= 256:
        return dict(tm=512, tm_in=1024, tn_in=1792, sgu_blk=SGU_CHUNK, tc_pool=256, ydt=BF16)
    return dict(tm=512, tm_in=512, tn_in=1792, sgu_blk=l, tc_pool=l, ydt=BF16)


def kernel(x_prompt, x_sample, state_rwkv_wkv, state_rwkv_shift, state_hgrn, state_pool, p_prompt, p_sample,
           ln_mix_pre, ln_mix_post, ln_ffn_pre, ln_ffn_post, w_in, rwkv_mu, rwkv_w_lora, rwkv_w0, rwkv_a_lora,
           rwkv_a0, rwkv_g_lora, rwkv_k_k, rwkv_k_a, rwkv_r_k, rwkv_gn_w, rwkv_gn_b, sgu_ln_w, sgu_ln_b, sgu_w,
           sgu_b, sgu_norm, hgrn_lb_logits, hgrn_norm, pool_w, pool_scale, w_out, ffn_w_gu, ffn_w_down, ple_gate,
           ple_proj):
    W = dict(ln_mix_pre=ln_mix_pre, ln_mix_post=ln_mix_post, ln_ffn_pre=ln_ffn_pre, ln_ffn_post=ln_ffn_post,
             w_in=w_in, rwkv_mu=rwkv_mu, rwkv_w_lora=rwkv_w_lora, rwkv_w0=rwkv_w0, rwkv_a_lora=rwkv_a_lora,
             rwkv_a0=rwkv_a0, rwkv_g_lora=rwkv_g_lora, rwkv_k_k=rwkv_k_k, rwkv_k_a=rwkv_k_a,
             rwkv_r_k=rwkv_r_k.reshape(DEPTH, GROUP_W), rwkv_gn_w=rwkv_gn_w, rwkv_gn_b=rwkv_gn_b, sgu_ln_w=sgu_ln_w,
             sgu_ln_b=sgu_ln_b, sgu_w=sgu_w, sgu_b=sgu_b, sgu_norm=sgu_norm, hgrn_norm=hgrn_norm, pool_w=pool_w,
             pool_scale=pool_scale, w_out=w_out, ffn_w_gu=ffn_w_gu, ffn_w_down=ffn_w_down, ple_gate=ple_gate,
             ple_proj=ple_proj)
    big = _prep_weights(W)
    layers = [_prep_layer(i, W) for i in range(DEPTH)]
    lb_logits = hgrn_lb_logits.astype(F32)
    bp, lp, _ = x_prompt.shape
    zeros = lambda *s: jnp.zeros((DEPTH, bp) + s, F32)
    y_prompt, wkv_p, shift_p, hgrn_p, pool_p, _ = _trunk(
        x_prompt, p_prompt, zeros(RWKV_HEADS, RWKV_HEAD, RWKV_HEAD), zeros(RWKV_COLS),
        zeros(HGRN_HEADS, HGRN_HEAD, HGRN_HEAD), zeros(POOL_HIST, GROUP_W), 0, big, layers, lb_logits, _cfg(lp))
    y_sample, wkv_s, shift_s, hgrn_s, pool_s, sgu_v_s = _trunk(
        x_sample, p_sample, state_rwkv_wkv, state_rwkv_shift, state_hgrn, state_pool, PAST_LEN, big, layers,
        lb_logits, _cfg(x_sample.shape[1]))
    return (y_prompt, y_sample, wkv_p, shift_p, hgrn_p, pool_p, wkv_s, shift_s, hgrn_s, pool_s, sgu_v_s)
```
